```python
import jax, jax.numpy as jnp
from jax import lax
import numpy as np

D_MODEL = 1024
BATCH = 16
SEQ = 256
DEPTH = 1
DEC_BATCH = 2
DEC_SEQ = 2048
PAST_LEN = 512

GRID_W = 64
D_MIX = D_MODEL
D_A = D_MIX // 2
HEAD_A = 64
N_HEADS_A = D_A // HEAD_A
D_B = D_MIX - D_A
BLOCK_B = 64
N_BLOCKS_B = D_B // BLOCK_B
N_DIR = 2
CONV_W = 4
CONV_LEFT = 2
LORA_W = 64
LORA_A = 64
LORA_G = 128
RGLRU_C = 8.0
N_EXPERTS = 32
TOP_K = 4
D_FF = D_MODEL
SWIGLU_LIMIT = 7.0
SWIGLU_ALPHA = 1.702
LN_EPS = 1e-5
GN_EPS = 1e-5 * HEAD_A
SPLIT_SIZES = (D_A, D_A, D_A, D_B, D_B, N_DIR * LORA_W, N_DIR * LORA_A, LORA_G)
D_IN = 3 * D_A + 2 * D_B + N_DIR * LORA_W + N_DIR * LORA_A + LORA_G
DEEPNORM_ALPHA = (2 * DEPTH) ** 0.25
DEEPNORM_BETA = (8 * DEPTH) ** -0.25

kernel_name = "hybrid_rwkv7_rglru_moe_diffusion_step"


def layer_norm(x):
    xf = x.astype(jnp.float32)
    mu = jnp.mean(xf, axis=-1, keepdims=True)
    var = jnp.mean(jnp.square(xf - mu), axis=-1, keepdims=True)
    return ((xf - mu) * lax.rsqrt(var + LN_EPS)).astype(x.dtype)


def orient(z):
    return jnp.stack([z[:, 0], jnp.flip(z[:, 1], axis=1)], axis=1)


def short_conv(x, w, b, grid):
    B, T, C = x.shape
    xs = x.reshape(B, T // GRID_W, GRID_W, C) if grid else x.reshape(B, 1, T, C)
    L = xs.shape[2]
    xp = jnp.pad(xs, ((0, 0), (0, 0), (CONV_LEFT, CONV_W - 1 - CONV_LEFT), (0, 0)))
    y = sum(xp[:, :, j:j + L, :] * w[j] for j in range(CONV_W)) + b
    return y.reshape(B, T, C)


def rwkv7_dual_scan(S0, r, w, k, v, a_vec, b_vec):
    def to_scan(z):
        return jnp.moveaxis(orient(z), 2, 0).astype(jnp.float32)

    def step(S, inp):
        r_t, w_t, k_t, v_t, a_t, b_t = inp
        Sa = jnp.einsum('bdhvk,bdhk->bdhv', S, a_t)
        S = S * w_t[..., None, :] + Sa[..., :, None] * b_t[..., None, :] + v_t[..., :, None] * k_t[..., None, :]
        return S, jnp.einsum('bdhvk,bdhk->bdhv', S, r_t)

    S_fin, o = lax.scan(step, S0.astype(jnp.float32),
                        tuple(to_scan(z) for z in (r, w, k, v, a_vec, b_vec)))
    return orient(jnp.moveaxis(o, 0, 2)), S_fin


def lin_combine(p, q):
    a1, b1 = p
    a2, b2 = q
    return a1 * a2, a2 * b1 + b2


def rglru_dual(h0, x, w_a, b_a, w_i, b_i, lam):
    B, T, _ = x.shape
    xf = x.astype(jnp.float32)
    xb = xf.reshape(B, T, N_BLOCKS_B, BLOCK_B)
    gr = jax.nn.sigmoid(jnp.einsum('btni,dnij->bdtnj', xb, w_a).reshape(B, N_DIR, T, D_B) + b_a[None, :, None, :])
    gi = jax.nn.sigmoid(jnp.einsum('btni,dnij->bdtnj', xb, w_i).reshape(B, N_DIR, T, D_B) + b_i[None, :, None, :])
    log_a = -RGLRU_C * gr * jax.nn.softplus(-lam.astype(jnp.float32))[None, :, None, :]
    a = orient(jnp.exp(log_a))
    u = orient(jnp.sqrt(-jnp.expm1(2.0 * log_a)) * gi * xf[:, None])
    u = u.at[:, :, 0].add(a[:, :, 0] * h0.astype(jnp.float32))
    _, h = lax.associative_scan(lin_combine, (a, u), axis=2)
    h_fin = h[:, :, -1]
    return orient(h).sum(axis=1).astype(x.dtype), h_fin.astype(h0.dtype)


def mixer(u, S0, h0, grid, p):
    B, T, _ = u.shape
    H, N = N_HEADS_A, HEAD_A
    proj = u @ p['w_in']
    r, k, v, xr, xgb, lw, la, lg = jnp.split(proj, np.cumsum(SPLIT_SIZES)[:-1].tolist(), axis=-1)
    lw = lw.reshape(B, T, N_DIR, LORA_W)
    la = la.reshape(B, T, N_DIR, LORA_A)
    dw = jnp.einsum('btdr,drc->bdtc', jnp.tanh(lw), p['w_decay_up']) + p['b_decay'][None, :, None, :]
    decay = jnp.exp(-jnp.exp(-jax.nn.softplus(-dw) - 0.5))
    iclr = jax.nn.sigmoid(jnp.einsum('btdr,drc->bdtc', la, p['w_iclr_up']) + p['b_iclr'][None, :, None, :])
    g = jax.nn.sigmoid(lg) @ p['w_gate_up']
    kk = (k * p['k_k']).reshape(B, T, H, N).astype(jnp.float32)
    kk = kk / jnp.maximum(jnp.linalg.norm(kk, axis=-1, keepdims=True), 1e-12)
    k_dir = (k[:, None] * (1.0 + (iclr - 1.0) * p['k_a'])).reshape(B, N_DIR, T, H, N)
    r_h = r.reshape(B, T, H, N)
    v_h = v.reshape(B, T, H, N)
    shp = (B, N_DIR, T, H, N)
    a_vec = jnp.broadcast_to(-kk[:, None], shp)
    b_vec = kk[:, None] * iclr.reshape(shp)
    o, S_fin = rwkv7_dual_scan(S0, jnp.broadcast_to(r_h[:, None], shp), decay.reshape(shp), k_dir,
                               jnp.broadcast_to(v_h[:, None], shp), a_vec, b_vec)
    wkv = o.sum(axis=1)
    mu = jnp.mean(wkv, axis=-1, keepdims=True)
    var = jnp.mean(jnp.square(wkv - mu), axis=-1, keepdims=True)
    gn = ((wkv - mu) * lax.rsqrt(var + GN_EPS)).astype(u.dtype) * p['gn_w'].reshape(H, N) + p['gn_b'].reshape(H, N)
    coef = jnp.sum(r_h[:, None] * k_dir * p['r_k'], axis=(1, 4))[..., None]
    y_a = (gn + coef * v_h).reshape(B, T, D_A) * g
    xc = short_conv(xr, p['conv_w'], p['conv_b'], grid)
    hsum, h_fin = rglru_dual(h0, xc, p['w_rg_a'], p['b_rg_a'], p['w_rg_i'], p['b_rg_i'], p['lam'])
    y_b = hsum * jax.nn.gelu(xgb)
    mix = jnp.concatenate([y_a, y_b], axis=-1) @ p['w_out']
    return mix, S_fin.astype(S0.dtype), h_fin


def moe(u, p):
    B, T, D = u.shape
    xt = u.reshape(B * T, D)
    logits = (xt @ p['w_router'] + p['b_router']).astype(jnp.float32)
    top_v, top_i = lax.top_k(logits, TOP_K)
    probs = jax.nn.softmax(top_v, axis=-1)
    combine = jnp.einsum('tk,tke->et', probs, jax.nn.one_hot(top_i, N_EXPERTS, dtype=jnp.float32)).astype(u.dtype)

    def expert(acc, ew):
        wg, bg, wu, bu, wd, bd, cw = ew
        gate = jnp.minimum(xt @ wg + bg, SWIGLU_LIMIT)
        up = jnp.clip(xt @ wu + bu, -SWIGLU_LIMIT, SWIGLU_LIMIT)
        h = (up + 1.0) * gate * jax.nn.sigmoid(SWIGLU_ALPHA * gate)
        return acc + cw[:, None] * (h @ wd + bd), None

    y, _ = lax.scan(expert, jnp.zeros_like(xt),
                    (p['w_e_gate'], p['b_e_gate'], p['w_e_up'], p['b_e_up'], p['w_e_down'], p['b_e_down'], combine))
    return y.reshape(B, T, D)


def trunk_layer(x, mod, S0, h0, grid, p):
    shift1, scale1, gate1, shift2, scale2, gate2 = jnp.split(mod, 6, axis=-1)
    u = layer_norm(x) * (1.0 + scale1) + shift1
    mix, S_fin, h_fin = mixer(u, S0, h0, grid, p)
    x = layer_norm(DEEPNORM_ALPHA * x + gate1 * mix) * p['ln1_w'] + p['ln1_b']
    u = layer_norm(x) * (1.0 + scale2) + shift2
    x = layer_norm(DEEPNORM_ALPHA * x + gate2 * moe(u, p)) * p['ln2_w'] + p['ln2_b']
    return x, S_fin, h_fin


def setup_inputs(seed: int = 0) -> dict:
    key = jax.random.key(seed)
    ks = iter(jax.random.split(key, 48))

    def nrm(shape, scale):
        return scale * jax.random.normal(next(ks), shape, jnp.float32)

    L, H, N, D = DEPTH, N_HEADS_A, HEAD_A, D_MODEL
    u_lam = jax.random.uniform(next(ks), (L, N_DIR, D_B), jnp.float32, 0.9, 0.999)
    p_lam = u_lam ** (1.0 / RGLRU_C)
    return {
        'x_prompt': nrm((BATCH, SEQ, D), 1.0),
        'x_sample': nrm((DEC_BATCH, DEC_SEQ, D), 1.0),
        'state_rwkv': nrm((DEC_BATCH, L, N_DIR, H, N, N), 0.5),
        'state_rglru': nrm((DEC_BATCH, L, N_DIR, D_B), 0.5),
        'c': nrm((DEC_BATCH, D), 1.0),
        'c_ctx': nrm((D,), 1.0),
        'w_mod': nrm((L, D, 6 * D), 0.5 * D ** -0.5),
        'b_mod': nrm((L, 6 * D), 0.05),
        'w_in': nrm((L, D, D_IN), D ** -0.5),
        'w_decay_up': nrm((L, N_DIR, LORA_W, D_A), 0.1 * LORA_W ** -0.5),
        'b_decay': jax.random.uniform(next(ks), (L, N_DIR, D_A), jnp.float32, -6.0, -1.0),
        'w_iclr_up': nrm((L, N_DIR, LORA_A, D_A), 0.5 * LORA_A ** -0.5),
        'b_iclr': nrm((L, N_DIR, D_A), 0.1),
        'w_gate_up': nrm((L, LORA_G, D_A), LORA_G ** -0.5),
        'k_k': 0.85 + nrm((L, D_A), 0.02),
        'k_a': 1.0 + nrm((L, D_A), 0.02),
        'r_k': nrm((L, H, N), 0.1),
        'gn_w': 1.0 + nrm((L, D_A), 0.02),
        'gn_b': nrm((L, D_A), 0.02),
        'conv_w': nrm((L, CONV_W, D_B), CONV_W ** -0.5),
        'conv_b': nrm((L, D_B), 0.01),
        'w_rg_a': nrm((L, N_DIR, N_BLOCKS_B, BLOCK_B, BLOCK_B), BLOCK_B ** -0.5),
        'b_rg_a': nrm((L, N_DIR, D_B), 0.01),
        'w_rg_i': nrm((L, N_DIR, N_BLOCKS_B, BLOCK_B, BLOCK_B), BLOCK_B ** -0.5),
        'b_rg_i': nrm((L, N_DIR, D_B), 0.01),
        'lam': jnp.log(p_lam) - jnp.log1p(-p_lam),
        'w_out': nrm((L, D_MIX, D), DEEPNORM_BETA * D_MIX ** -0.5),
        'ln1_w': 1.0 + nrm((L, D), 0.02),
        'ln1_b': nrm((L, D), 0.02),
        'w_router': nrm((L, D, N_EXPERTS), D ** -0.5),
        'b_router': nrm((L, N_EXPERTS), 0.01),
        'w_e_gate': nrm((L, N_EXPERTS, D, D_FF), D ** -0.5),
        'b_e_gate': nrm((L, N_EXPERTS, D_FF), 0.01),
        'w_e_up': nrm((L, N_EXPERTS, D, D_FF), D ** -0.5),
        'b_e_up': nrm((L, N_EXPERTS, D_FF), 0.01),
        'w_e_down': nrm((L, N_EXPERTS, D_FF, D), DEEPNORM_BETA * D_FF ** -0.5),
        'b_e_down': nrm((L, N_EXPERTS, D), 0.01),
        'ln2_w': 1.0 + nrm((L, D), 0.02),
        'ln2_b': nrm((L, D), 0.02),
    }


def reference(x_prompt, x_sample, state_rwkv, state_rglru, c, c_ctx, w_mod, b_mod, w_in,
              w_decay_up, b_decay, w_iclr_up, b_iclr, w_gate_up, k_k, k_a, r_k, gn_w, gn_b,
              conv_w, conv_b, w_rg_a, b_rg_a, w_rg_i, b_rg_i, lam, w_out, ln1_w, ln1_b,
              w_router, b_router, w_e_gate, b_e_gate, w_e_up, b_e_up, w_e_down, b_e_down,
              ln2_w, ln2_b):
    y_p, y_s = x_prompt, x_sample
    Bp = x_prompt.shape[0]
    new_rwkv, new_rglru = [], []
    for l in range(DEPTH):
        p = {
            'w_in': w_in[l], 'w_decay_up': w_decay_up[l], 'b_decay': b_decay[l],
            'w_iclr_up': w_iclr_up[l], 'b_iclr': b_iclr[l], 'w_gate_up': w_gate_up[l],
            'k_k': k_k[l], 'k_a': k_a[l], 'r_k': r_k[l], 'gn_w': gn_w[l], 'gn_b': gn_b[l],
            'conv_w': conv_w[l], 'conv_b': conv_b[l], 'w_rg_a': w_rg_a[l], 'b_rg_a': b_rg_a[l],
            'w_rg_i': w_rg_i[l], 'b_rg_i': b_rg_i[l], 'lam': lam[l], 'w_out': w_out[l],
            'ln1_w': ln1_w[l], 'ln1_b': ln1_b[l], 'w_router': w_router[l], 'b_router': b_router[l],
            'w_e_gate': w_e_gate[l], 'b_e_gate': b_e_gate[l], 'w_e_up': w_e_up[l], 'b_e_up': b_e_up[l],
            'w_e_down': w_e_down[l], 'b_e_down': b_e_down[l], 'ln2_w': ln2_w[l], 'ln2_b': ln2_b[l],
        }
        mod_ctx = (jax.nn.silu(c_ctx) @ w_mod[l] + b_mod[l])[None, None, :]
        mod_lat = (jax.nn.silu(c) @ w_mod[l] + b_mod[l])[:, None, :]
        S0 = jnp.zeros((Bp, N_DIR, N_HEADS_A, HEAD_A, HEAD_A), x_prompt.dtype)
        h0 = jnp.zeros((Bp, N_DIR, D_B), x_prompt.dtype)
        y_p, S_ctx, h_ctx = trunk_layer(y_p, mod_ctx, S0, h0, False, p)
        new_rwkv.append(S_ctx)
        new_rglru.append(h_ctx)
        y_s, _, _ = trunk_layer(y_s, mod_lat, state_rwkv[:, l], state_rglru[:, l], True, p)
    new_state_rwkv = jnp.stack(new_rwkv, axis=1)
    new_state_rglru = jnp.stack(new_rglru, axis=1)
    return (y_p, y_s, new_state_rwkv, new_state_rglru)
```

```python
import functools
import math

import jax
import jax.numpy as jnp
from jax import lax
from jax.experimental import pallas as pl
from jax.experimental.pallas import tpu as pltpu

F32 = jnp.float32
BF16 = jnp.bfloat16

D_MODEL = 1024
D_A = 512
D_B = 512
HEAD = 64
N_HEADS = 8
N_PAIRS = N_HEADS // 2
GRID_W = 64
LORA = 64
LORA_G = 128
RGLRU_C = 8.0
N_EXPERTS = 32
TOP_K = 4
D_FF = 1024
SWIGLU_LIMIT = 7.0
SWIGLU_ALPHA = 1.702
LN_EPS = 1e-5
GN_EPS = 1e-5 * HEAD
D_IN = 3 * D_A + 2 * D_B + 2 * LORA + 2 * LORA + LORA_G
DEPTH = 1
DEEPNORM_ALPHA = (2 * DEPTH) ** 0.25

COL_R, COL_K, COL_V, COL_XR, COL_XGB = 0, 1, 2, 3, 4
COL_LW, COL_LA, COL_LG = 20, 21, 22

CHUNK = 64
VMEM_LIMIT = 56 * 1024 * 1024

_NN = (((1,), (0,)), ((), ()))
_NT = (((1,), (1,)), ((), ()))


def _mm(a, b, dims=_NN):
    return lax.dot_general(a, b, dims, preferred_element_type=F32)


def _split2(x):
    hi = x.astype(BF16)
    lo = (x - hi.astype(F32)).astype(BF16)
    return hi, lo


def _split3(x):
    hi = x.astype(BF16)
    r1 = x - hi.astype(F32)
    mid = r1.astype(BF16)
    lo = (r1 - mid.astype(F32)).astype(BF16)
    return hi, mid, lo


def _dot1(a, b, dims=_NN):
    return _mm(a.astype(BF16), b.astype(BF16), dims)


def _dot3(a, b, dims=_NN):
    ah, al = _split2(a)
    bh, bl = _split2(b)
    return _mm(ah, bh, dims) + (_mm(ah, bl, dims) + _mm(al, bh, dims))


def _dot_xb(a, b_bf16, dims=_NN):
    ah, am, al = _split3(a)
    return _mm(ah, b_bf16, dims) + (_mm(am, b_bf16, dims) + _mm(al, b_bf16, dims))


def _dot_xa(a_bf16, b, dims=_NN):
    bh, bm, bl = _split3(b)
    return _mm(a_bf16, bh, dims) + (_mm(a_bf16, bm, dims) + _mm(a_bf16, bl, dims))


def _layer_norm(x):
    mu = jnp.mean(x, axis=-1, keepdims=True)
    xc = x - mu
    var = jnp.mean(xc * xc, axis=-1, keepdims=True)
    return xc * lax.rsqrt(var + LN_EPS)


def _cparams(sem):
    return pltpu.CompilerParams(dimension_semantics=sem, vmem_limit_bytes=VMEM_LIMIT)


def _mod_kernel(c_ref, w_ref, b_ref, o_ref):
    c = c_ref[...]
    s = c * jax.nn.sigmoid(c)
    o_ref[...] = _dot3(s, w_ref[...]) + b_ref[...]


def _modulation(cond8, w_mod, b_mod):
    n = w_mod.shape[1]
    tn = 1024
    return pl.pallas_call(
        _mod_kernel,
        grid=(n // tn,),
        in_specs=[pl.BlockSpec((8, D_MODEL), lambda j: (0, 0)),
                  pl.BlockSpec((D_MODEL, tn), lambda j: (0, j)),
                  pl.BlockSpec((1, tn), lambda j: (0, j))],
        out_specs=pl.BlockSpec((8, tn), lambda j: (0, j)),
        out_shape=jax.ShapeDtypeStruct((8, n), F32),
        compiler_params=_cparams(("arbitrary",)),
        name="modulation",
    )(cond8, w_mod, b_mod.reshape(1, n))


def _inproj_kernel(x_ref, mod_ref, w_ref, o_ref):
    m = mod_ref[0]
    u = _layer_norm(x_ref[...]) * (1.0 + m[1:2]) + m[0:1]
    o_ref[...] = _dot1(u, w_ref[...])


def _inproj(x, mod_rows, w_in_bf16, tok_per_row):
    ntok = x.shape[0]
    tm = 256
    per = tok_per_row // tm
    return pl.pallas_call(
        _inproj_kernel,
        grid=(ntok // tm,),
        in_specs=[pl.BlockSpec((tm, D_MODEL), lambda i: (i, 0)),
                  pl.BlockSpec((1, 6, D_MODEL), lambda i: (i // per, 0, 0)),
                  pl.BlockSpec((D_MODEL, D_IN), lambda i: (0, 0))],
        out_specs=pl.BlockSpec((tm, D_IN), lambda i: (i, 0)),
        out_shape=jax.ShapeDtypeStruct((ntok, D_IN), F32),
        compiler_params=_cparams(("arbitrary",)),
        name="inproj",
    )(x, mod_rows, w_in_bf16)


def _rwkv_kernel(r_ref, k_ref, v_ref, lw_ref, la_ref, s0_ref, wdec_ref, bdec_ref, wic_ref, bic_ref,
                 kkw_ref, ka_ref, rk_ref, hs_ref, o_ref, bonus_ref, sfin_ref, s_scr, *, nc):
    L = CHUNK
    d = pl.program_id(1)
    c = pl.program_id(2)

    @pl.when(c == 0)
    def _():
        s_scr[...] = s0_ref[0, 0]

    r = r_ref[...]
    k = k_ref[...]
    v = v_ref[...]
    hs = hs_ref[...]

    lane128 = lax.broadcasted_iota(jnp.int32, (L, 128), 1)
    dsel = lax.shift_right_logical(lane128, 6) == d
    tl = jnp.where(dsel, jnp.tanh(lw_ref[...]), 0.0)
    dw = _dot3(tl, wdec_ref[...]) + bdec_ref[pl.ds(d, 1), :]
    logw = (-math.exp(-0.5)) * jax.nn.sigmoid(dw)
    la_m = jnp.where(dsel, la_ref[...], 0.0)
    iclr = jax.nn.sigmoid(_dot3(la_m, wic_ref[...]) + bic_ref[pl.ds(d, 1), :])

    kkr = k * kkw_ref[...]
    nrm = jnp.sqrt(_dot_xb(kkr * kkr, hs))
    kk = kkr / jnp.maximum(nrm, 1e-12)
    kdir = k * (1.0 + (iclr - 1.0) * ka_ref[...])
    bv = kk * iclr
    bonus_ref[0] = _dot_xb(r * kdir * rk_ref[...], hs) * v

    sgn = 1 - 2 * d
    row = lax.broadcasted_iota(jnp.int32, (L, L), 0) * sgn
    col = lax.broadcasted_iota(jnp.int32, (L, L), 1) * sgn
    tri = jnp.where(row >= col, 1.0, 0.0).astype(BF16)
    cs = _dot_xa(tri, logw)
    ctot = jnp.sum(logw, axis=0, keepdims=True)
    g_in = jnp.exp(cs)
    g_ex = jnp.exp(cs - logw)
    g_inv = jnp.exp(-cs)
    g_rem = jnp.exp(ctot - cs)
    g_tot = jnp.exp(ctot)
    At = -kk * g_ex
    Rt = r * g_in
    Bt = bv * g_inv
    Kt = kdir * g_inv
    Bg = bv * g_rem
    Kg = kdir * g_rem

    rowp = lax.broadcasted_iota(jnp.int32, (L, 128), 0)
    colp = jnp.bitwise_and(lane128, 63)
    h0 = lane128 < 64
    strict2 = rowp * sgn > colp * sgn
    incl2 = rowp * sgn >= colp * sgn
    eye2 = jnp.where(rowp == colp, 1.0, 0.0)

    def same(sh):
        return lax.shift_right_logical(rowp, sh) == lax.shift_right_logical(colp, sh)

    same8, same16, same32 = same(3), same(4), same(5)
    lvl = (same16 & (~same8), same32 & (~same16), ~same32)

    def sm(x):
        return jnp.concatenate([jnp.where(h0, x, 0.0), jnp.where(h0, 0.0, x)], axis=0)

    def pm(x, y):
        return _dot3(x, sm(y))

    for p in range(N_PAIRS):
        sl = slice(128 * p, 128 * (p + 1))
        Atp, Rtp, Btp, Ktp, Bgp, Kgp, Vp = At[:, sl], Rt[:, sl], Bt[:, sl], Kt[:, sl], Bg[:, sl], Kg[:, sl], v[:, sl]
        M = _dot3(jnp.concatenate([Atp, Rtp], axis=0),
                  jnp.concatenate([sm(Btp), sm(Ktp)], axis=0), _NT)
        N = jnp.where(strict2, M[:L, :128], 0.0)
        Mak = jnp.where(strict2, M[:L, 128:], 0.0)
        Mrb = jnp.where(incl2, M[L:, :128], 0.0)
        Mrk = jnp.where(incl2, M[L:, 128:], 0.0)
        N0 = jnp.where(same8, N, 0.0)
        T = eye2 + N0
        P2 = pm(N0, N0)
        T = T + pm(T, P2)
        P4 = pm(P2, P2)
        T = T + pm(T, P4)
        for msk in lvl:
            T = T + pm(pm(T, jnp.where(msk, N, 0.0)), T)
        smV = sm(Vp)
        W2 = _dot3(Mak, smV)
        S2 = s_scr[p]
        XR = _dot3(jnp.concatenate([sm(Atp), sm(Rtp)], axis=0),
                   jnp.concatenate([S2, S2], axis=0), _NT)
        X = jnp.where(h0, XR[0:L], XR[L:2 * L])
        RS = jnp.where(h0, XR[2 * L:3 * L], XR[3 * L:4 * L])
        U = _dot3(T, sm(X + W2))
        smU = sm(U)
        O = RS + _dot3(jnp.concatenate([Mrb, Mrk], axis=1), jnp.concatenate([smU, smV], axis=0))
        o_ref[0, :, sl] = O
        UVt = jnp.concatenate([U, Vp], axis=0).T
        Rm = _dot3(UVt, jnp.concatenate([Bgp, Kgp], axis=0))
        s_scr[p] = g_tot[:, sl] * S2 + jnp.where(h0, Rm[:HEAD], Rm[HEAD:])

    @pl.when(c == nc - 1)
    def _():
        for p in range(N_PAIRS):
            sp = s_scr[p]
            sfin_ref[0, 0, 2 * p] = sp[:, :HEAD]
            sfin_ref[0, 0, 2 * p + 1] = sp[:, HEAD:]


def _rwkv(proj, s0_pairs, wdec, bdec, wic, bic, kkw, ka, rk, hs, nseq, T):
    ntok = proj.shape[0]
    nc = T // CHUNK
    L = CHUNK

    def rowblk(b, d, c):
        return b * nc + c + d * (nc - 1 - 2 * c)

    def pspec(width, colblk):
        return pl.BlockSpec((L, width), lambda b, d, c: (rowblk(b, d, c), colblk))

    def wspec(shape):
        return pl.BlockSpec(shape, lambda b, d, c: (0,) * len(shape))

    return pl.pallas_call(
        functools.partial(_rwkv_kernel, nc=nc),
        grid=(nseq, 2, nc),
        in_specs=[pspec(D_A, COL_R), pspec(D_A, COL_K), pspec(D_A, COL_V),
                  pspec(128, COL_LW), pspec(128, COL_LA),
                  pl.BlockSpec((1, 1, N_PAIRS, HEAD, 128), lambda b, d, c: (b, d, 0, 0, 0)),
                  wspec((128, D_A)), wspec((2, D_A)), wspec((128, D_A)), wspec((2, D_A)),
                  wspec((1, D_A)), wspec((1, D_A)), wspec((1, D_A)), wspec((D_A, D_A))],
        out_specs=[pl.BlockSpec((1, L, D_A), lambda b, d, c: (d, rowblk(b, d, c), 0)),
                   pl.BlockSpec((1, L, D_A), lambda b, d, c: (d, rowblk(b, d, c), 0)),
                   pl.BlockSpec((1, 1, N_HEADS, HEAD, HEAD), lambda b, d, c: (b, d, 0, 0, 0))],
        out_shape=[jax.ShapeDtypeStruct((2, ntok, D_A), F32),
                   jax.ShapeDtypeStruct((2, ntok, D_A), F32),
                   jax.ShapeDtypeStruct((nseq, 2, N_HEADS, HEAD, HEAD), F32)],
        scratch_shapes=[pltpu.VMEM((N_PAIRS, HEAD, 128), F32)],
        compiler_params=_cparams(("arbitrary", "arbitrary", "arbitrary")),
        name="rwkv_scan",
    )(proj, proj, proj, proj, proj, s0_pairs, wdec, bdec, wic, bic, kkw, ka, rk, hs)


def _gelu_tanh(x):
    return 0.5 * x * (1.0 + jnp.tanh(math.sqrt(2.0 / math.pi) * (x + 0.044715 * (x * x * x))))


def _rglru_kernel(xr_ref, xg_ref, h0_ref, cw_ref, cb_ref, wa_ref, ba_ref, wi_ref, bi_ref, lam_ref,
                  y_ref, hfin_ref, a_scr, u_scr, *, T, lrow):
    RB = min(T, 256)
    nblk = T // RB
    rowi = lax.broadcasted_iota(jnp.int32, (RB, D_B), 0)
    pos = jnp.bitwise_and(rowi, lrow - 1)
    cw = cw_ref[...]
    lam = lam_ref[...]
    sp = jnp.maximum(-lam, 0.0) + jnp.log1p(jnp.exp(-jnp.abs(lam)))

    def gates(blk, carry):
        r0 = pl.multiple_of(blk * RB, RB)
        x = xr_ref[pl.ds(r0, RB), :]
        xm1 = jnp.where(pos >= 1, pltpu.roll(x, 1, 0), 0.0)
        xm2 = jnp.where(pos >= 2, pltpu.roll(x, 2, 0), 0.0)
        xp1 = jnp.where(pos <= lrow - 2, pltpu.roll(x, RB - 1, 0), 0.0)
        xc = xm2 * cw[0:1] + xm1 * cw[1:2] + x * cw[2:3] + xp1 * cw[3:4] + cb_ref[...]
        for dd in range(2):
            gr = jax.nn.sigmoid(_dot1(xc, wa_ref[dd]) + ba_ref[dd:dd + 1, :])
            gi = jax.nn.sigmoid(_dot1(xc, wi_ref[dd]) + bi_ref[dd:dd + 1, :])
            log_a = (-RGLRU_C) * gr * sp[dd:dd + 1, :]
            a = jnp.exp(log_a)
            a_scr[dd, pl.ds(r0, RB), :] = a
            u_scr[dd, pl.ds(r0, RB), :] = jnp.sqrt((1.0 - a) * (1.0 + a)) * gi * xc
        return carry

    lax.fori_loop(0, nblk, gates, 0)

    def step(t, carry):
        hf, hb = carry
        hf = a_scr[0, pl.ds(t, 1), :] * hf + u_scr[0, pl.ds(t, 1), :]
        u_scr[0, pl.ds(t, 1), :] = hf
        tb = T - 1 - t
        hb = a_scr[1, pl.ds(tb, 1), :] * hb + u_scr[1, pl.ds(tb, 1), :]
        u_scr[1, pl.ds(tb, 1), :] = hb
        return hf, hb

    h0 = h0_ref[0]
    hf, hb = lax.fori_loop(0, T, step, (h0[0:1], h0[1:2]), unroll=8)
    hfin_ref[0] = jnp.concatenate([hf, hb], axis=0)

    def outp(blk, carry):
        r0 = pl.multiple_of(blk * RB, RB)
        h = u_scr[0, pl.ds(r0, RB), :] + u_scr[1, pl.ds(r0, RB), :]
        y_ref[pl.ds(r0, RB), :] = h * _gelu_tanh(xg_ref[pl.ds(r0, RB), :])
        return carry

    lax.fori_loop(0, nblk, outp, 0)


def _rglru(proj, h0, conv_w, conv_b, wa_bd, ba, wi_bd, bi, lam, nseq, T, lrow):
    ntok = proj.shape[0]

    def wspec(shape):
        return pl.BlockSpec(shape, lambda b: (0,) * len(shape))

    return pl.pallas_call(
        functools.partial(_rglru_kernel, T=T, lrow=lrow),
        grid=(nseq,),
        in_specs=[pl.BlockSpec((T, D_B), lambda b: (b, COL_XR)),
                  pl.BlockSpec((T, D_B), lambda b: (b, COL_XGB)),
                  pl.BlockSpec((1, 2, D_B), lambda b: (b, 0, 0)),
                  wspec((4, D_B)), wspec((1, D_B)),
                  wspec((2, D_B, D_B)), wspec((2, D_B)), wspec((2, D_B, D_B)), wspec((2, D_B)),
                  wspec((2, D_B))],
        out_specs=[pl.BlockSpec((T, D_B), lambda b: (b, 0)),
                   pl.BlockSpec((1, 2, D_B), lambda b: (b, 0, 0))],
        out_shape=[jax.ShapeDtypeStruct((ntok, D_B), F32),
                   jax.ShapeDtypeStruct((nseq, 2, D_B), F32)],
        scratch_shapes=[pltpu.VMEM((2, T, D_B), F32), pltpu.VMEM((2, T, D_B), F32)],
        compiler_params=_cparams(("arbitrary",)),
        name="rglru",
    )(proj, proj, h0, conv_w, conv_b, wa_bd, ba, wi_bd, bi, lam)


def _mixout_kernel(of_ref, ob_ref, bf_ref, bb_ref, yb_ref, lg_ref, x_ref, mod_ref, wgu_ref, avg_ref,
                   gnw_ref, gnb_ref, wout_ref, l1w_ref, l1b_ref, wr_ref, br_ref,
                   x1_ref, u2_ref, cw_ref):
    m = mod_ref[0]
    avg = avg_ref[...]
    wkv = of_ref[0] + ob_ref[0]
    mu = _dot_xb(wkv, avg)
    dv = wkv - mu
    var = _dot_xb(dv * dv, avg)
    gn = dv * lax.rsqrt(var + GN_EPS) * gnw_ref[...] + gnb_ref[...]
    g = _dot1(jax.nn.sigmoid(lg_ref[...]), wgu_ref[...])
    ya = (gn + (bf_ref[0] + bb_ref[0])) * g
    mix = _dot1(ya, wout_ref[0:D_A, :]) + _dot1(yb_ref[...], wout_ref[D_A:, :])
    x1 = _layer_norm(DEEPNORM_ALPHA * x_ref[...] + m[2:3] * mix) * l1w_ref[...] + l1b_ref[...]
    x1_ref[...] = x1
    u2 = _layer_norm(x1) * (1.0 + m[4:5]) + m[3:4]
    u2_ref[...] = u2.astype(BF16)
    logits = _dot3(u2, wr_ref[...]) + br_ref[...]
    tm = logits.shape[0]
    lane = lax.broadcasted_iota(jnp.int32, (tm, N_EXPERTS), 1)
    work = logits
    tops, hots = [], []
    for _ in range(TOP_K):
        mx = jnp.max(work, axis=1, keepdims=True)
        idx = jnp.min(jnp.where(work == mx, lane, N_EXPERTS), axis=1, keepdims=True)
        hot = lane == idx
        tops.append(mx)
        hots.append(hot)
        work = jnp.where(hot, -jnp.inf, work)
    es = [jnp.exp(t - tops[0]) for t in tops]
    den = es[0] + es[1] + es[2] + es[3]
    cw = jnp.zeros((tm, N_EXPERTS), F32)
    for e_, hot in zip(es, hots):
        cw = cw + jnp.where(hot, e_ / den, 0.0)
    cw_ref[...] = cw


def _mixout(o, bonus, yb, proj, x, mod_rows, tok_per_row, wgu, avg, gnw, gnb, wout, l1w, l1b, wr, br):
    ntok = x.shape[0]
    tm = 256
    per = tok_per_row // tm

    def wspec(shape):
        return pl.BlockSpec(shape, lambda i: (0,) * len(shape))

    return pl.pallas_call(
        _mixout_kernel,
        grid=(ntok // tm,),
        in_specs=[pl.BlockSpec((1, tm, D_A), lambda i: (0, i, 0)),
                  pl.BlockSpec((1, tm, D_A), lambda i: (1, i, 0)),
                  pl.BlockSpec((1, tm, D_A), lambda i: (0, i, 0)),
                  pl.BlockSpec((1, tm, D_A), lambda i: (1, i, 0)),
                  pl.BlockSpec((tm, D_B), lambda i: (i, 0)),
                  pl.BlockSpec((tm, LORA_G), lambda i: (i, COL_LG)),
                  pl.BlockSpec((tm, D_MODEL), lambda i: (i, 0)),
                  pl.BlockSpec((1, 6, D_MODEL), lambda i: (i // per, 0, 0)),
                  wspec((LORA_G, D_A)), wspec((D_A, D_A)), wspec((1, D_A)), wspec((1, D_A)),
                  wspec((D_MODEL, D_MODEL)), wspec((1, D_MODEL)), wspec((1, D_MODEL)),
                  wspec((D_MODEL, N_EXPERTS)), wspec((1, N_EXPERTS))],
        out_specs=[pl.BlockSpec((tm, D_MODEL), lambda i: (i, 0)),
                   pl.BlockSpec((tm, D_MODEL), lambda i: (i, 0)),
                   pl.BlockSpec((tm, N_EXPERTS), lambda i: (i, 0))],
        out_shape=[jax.ShapeDtypeStruct((ntok, D_MODEL), F32),
                   jax.ShapeDtypeStruct((ntok, D_MODEL), BF16),
                   jax.ShapeDtypeStruct((ntok, N_EXPERTS), F32)],
        compiler_params=_cparams(("arbitrary",)),
        name="mixout",
    )(o, o, bonus, bonus, yb, proj, x, mod_rows, wgu, avg, gnw, gnb, wout, l1w, l1b, wr, br)


def _moe_kernel(up_ref, us_ref, cwp_ref, cws_ref, wg_ref, bg_ref, wu_ref, bu_ref, wd_ref, bd_ref,
                o_ref, *, n_ctx_tiles):
    i = pl.program_id(0)
    e = pl.program_id(1)
    tm = o_ref.shape[0]
    is_ctx = i < n_ctx_tiles
    x = jnp.where(is_ctx, up_ref[...], us_ref[...])
    cw = jnp.where(is_ctx, cwp_ref[...], cws_ref[...])
    hot = jnp.where(lax.broadcasted_iota(jnp.int32, (N_EXPERTS, D_MODEL), 0) == e, 1.0, 0.0).astype(BF16)
    cwe = _dot_xb(cw, hot)
    FC = 256
    y = jnp.zeros((tm, D_MODEL), F32)
    for j in range(D_FF // FC):
        cs = slice(j * FC, (j + 1) * FC)
        gate = jnp.minimum(_mm(x, wg_ref[0, :, cs].astype(BF16)) + bg_ref[0, :, cs], SWIGLU_LIMIT)
        up = jnp.clip(_mm(x, wu_ref[0, :, cs].astype(BF16)) + bu_ref[0, :, cs], -SWIGLU_LIMIT, SWIGLU_LIMIT)
        h = (up + 1.0) * gate * jax.nn.sigmoid(SWIGLU_ALPHA * gate)
        y = y + _mm(h.astype(BF16), wd_ref[0, cs, :].astype(BF16))
    contrib = cwe * (y + bd_ref[0])

    @pl.when(e == 0)
    def _():
        o_ref[...] = contrib

    @pl.when(e != 0)
    def _():
        o_ref[...] += contrib


def _moe(u2_p, u2_s, cw_p, cw_s, wg, bg, wu, bu, wd, bd):
    n_p, n_s = u2_p.shape[0], u2_s.shape[0]
    tm = 1024
    tp, ts = n_p // tm, n_s // tm

    def pidx(i, e):
        return (jnp.minimum(i, tp - 1), 0)

    def sidx(i, e):
        return (jnp.maximum(i - tp, 0), 0)

    wspec = pl.BlockSpec((1, D_MODEL, D_FF), lambda i, e: (e, 0, 0))
    bspec = pl.BlockSpec((1, 1, D_FF), lambda i, e: (e, 0, 0))
    return pl.pallas_call(
        functools.partial(_moe_kernel, n_ctx_tiles=tp),
        grid=(tp + ts, N_EXPERTS),
        in_specs=[pl.BlockSpec((tm, D_MODEL), pidx), pl.BlockSpec((tm, D_MODEL), sidx),
                  pl.BlockSpec((tm, N_EXPERTS), pidx), pl.BlockSpec((tm, N_EXPERTS), sidx),
                  wspec, bspec, wspec, bspec, wspec, bspec],
        out_specs=pl.BlockSpec((tm, D_MODEL), lambda i, e: (i, 0)),
        out_shape=jax.ShapeDtypeStruct((n_p + n_s, D_MODEL), F32),
        compiler_params=_cparams(("arbitrary", "arbitrary")),
        name="moe",
    )(u2_p, u2_s, cw_p, cw_s, wg, bg.reshape(N_EXPERTS, 1, D_FF), wu, bu.reshape(N_EXPERTS, 1, D_FF),
      wd, bd.reshape(N_EXPERTS, 1, D_MODEL))


def _final_kernel(x1_ref, moe_ref, mod_ref, w_ref, b_ref, o_ref):
    m = mod_ref[0]
    o_ref[...] = _layer_norm(DEEPNORM_ALPHA * x1_ref[...] + m[5:6] * moe_ref[...]) * w_ref[...] + b_ref[...]


def _final(x1, moe, moe_off_tiles, mod_rows, tok_per_row, w, b):
    ntok = x1.shape[0]
    tm = 512
    per = tok_per_row // tm
    return pl.pallas_call(
        _final_kernel,
        grid=(ntok // tm,),
        in_specs=[pl.BlockSpec((tm, D_MODEL), lambda i: (i, 0)),
                  pl.BlockSpec((tm, D_MODEL), lambda i: (i + moe_off_tiles, 0)),
                  pl.BlockSpec((1, 6, D_MODEL), lambda i: (i // per, 0, 0)),
                  pl.BlockSpec((1, D_MODEL), lambda i: (0, 0)),
                  pl.BlockSpec((1, D_MODEL), lambda i: (0, 0))],
        out_specs=pl.BlockSpec((tm, D_MODEL), lambda i: (i, 0)),
        out_shape=jax.ShapeDtypeStruct((ntok, D_MODEL), F32),
        compiler_params=_cparams(("arbitrary",)),
        name="final_ln",
    )(x1, moe, mod_rows, w, b)


def _block_diag(w):
    nb, bb, _ = w.shape
    eye = jnp.eye(nb, dtype=w.dtype)
    return jnp.einsum('nij,nm->nimj', w, eye).reshape(nb * bb, nb * bb)


def kernel(x_prompt, x_sample, state_rwkv, state_rglru, c, c_ctx, w_mod, b_mod, w_in, w_decay_up, b_decay, w_iclr_up, b_iclr, w_gate_up, k_k, k_a, r_k, gn_w, gn_b, conv_w, conv_b, w_rg_a, b_rg_a, w_rg_i, b_rg_i, lam, w_out, ln1_w, ln1_b, w_router, b_router, w_e_gate, b_e_gate, w_e_up, b_e_up, w_e_down, b_e_down, ln2_w, ln2_b):
    Bp, Tp, D = x_prompt.shape
    Bs, Ts, _ = x_sample.shape
    l = 0
    row = lambda a: a[l].reshape(1, -1)

    cond8 = jnp.concatenate([c_ctx[None, :], c, jnp.zeros((8 - 1 - Bs, D), F32)], axis=0)
    mod = _modulation(cond8, w_mod[l], b_mod[l]).reshape(8, 6, D)
    mod_p, mod_s = mod[0:1], mod[1:1 + Bs]

    w_in_b = w_in[l].astype(BF16)
    w_out_b = w_out[l].astype(BF16)
    head_id = jnp.arange(D_A) // HEAD
    hs = (head_id[:, None] == head_id[None, :]).astype(BF16)
    avg = (hs.astype(F32) / HEAD).astype(BF16)
    wdec = w_decay_up[l].reshape(2 * LORA, D_A)
    wic = w_iclr_up[l].reshape(2 * LORA, D_A)
    wa_bd = jnp.stack([_block_diag(w_rg_a[l, 0]), _block_diag(w_rg_a[l, 1])])
    wi_bd = jnp.stack([_block_diag(w_rg_i[l, 0]), _block_diag(w_rg_i[l, 1])])

    xp = x_prompt.reshape(Bp * Tp, D)
    xs = x_sample.reshape(Bs * Ts, D)

    s0_s = state_rwkv[:, l].reshape(Bs, 2, N_PAIRS, 2, HEAD, HEAD).transpose(0, 1, 2, 4, 3, 5)
    s0_s = s0_s.reshape(Bs, 2, N_PAIRS, HEAD, 2 * HEAD)
    s0_p = jnp.zeros((Bp, 2, N_PAIRS, HEAD, 2 * HEAD), F32)
    h0_p = jnp.zeros((Bp, 2, D_B), F32)
    h0_s = state_rglru[:, l]

    outs = []
    for x, mod_rows, tok_per_row, nseq, T, s0, h0, lrow in (
            (xp, mod_p, Bp * Tp, Bp, Tp, s0_p, h0_p, Tp),
            (xs, mod_s, Ts, Bs, Ts, s0_s, h0_s, GRID_W)):
        proj = _inproj(x, mod_rows, w_in_b, tok_per_row)
        o, bonus, s_fin = _rwkv(proj, s0, wdec, b_decay[l], wic, b_iclr[l], row(k_k), row(k_a),
                                r_k[l].reshape(1, D_A), hs, nseq, T)
        yb, h_fin = _rglru(proj, h0, conv_w[l], row(conv_b), wa_bd, b_rg_a[l], wi_bd, b_rg_i[l], lam[l],
                           nseq, T, lrow)
        x1, u2, cw = _mixout(o, bonus, yb, proj, x, mod_rows, tok_per_row, w_gate_up[l], avg, row(gn_w),
                             row(gn_b), w_out_b, row(ln1_w), row(ln1_b), w_router[l], row(b_router))
        outs.append((x1, u2, cw, s_fin, h_fin, mod_rows, tok_per_row))

    (x1_p, u2_p, cw_p, sfin_p, hfin_p, _, _), (x1_s, u2_s, cw_s, _, _, _, _) = outs
    moe = _moe(u2_p, u2_s, cw_p, cw_s, w_e_gate[l], b_e_gate[l], w_e_up[l], b_e_up[l], w_e_down[l], b_e_down[l])
    ftm = 512
    y_p = _final(x1_p, moe, 0, mod_p, Bp * Tp, row(ln2_w), row(ln2_b))
    y_s = _final(x1_s, moe, (Bp * Tp) // ftm, mod_s, Ts, row(ln2_w), row(ln2_b))
    return (y_p.reshape(Bp, Tp, D), y_s.reshape(Bs, Ts, D),
            sfin_p[:, None], hfin_p[:, None])
```

```python
import functools
import math

import jax
import jax.numpy as jnp
from jax import lax
from jax.experimental import pallas as pl
from jax.experimental.pallas import tpu as pltpu

F32 = jnp.float32
BF16 = jnp.bfloat16

D_MODEL = 1024
D_A = 512
D_B = 512
HEAD = 64
N_HEADS = 8
N_PAIRS = N_HEADS // 2
GRID_W = 64
LORA = 64
LORA_G = 128
RGLRU_C = 8.0
N_EXPERTS = 32
TOP_K = 4
D_FF = 1024
SWIGLU_LIMIT = 7.0
SWIGLU_ALPHA = 1.702
LN_EPS = 1e-5
GN_EPS = 1e-5 * HEAD
D_IN = 3 * D_A + 2 * D_B + 2 * LORA + 2 * LORA + LORA_G
DEPTH = 1
DEEPNORM_ALPHA = (2 * DEPTH) ** 0.25

COL_R, COL_K, COL_V, COL_XR, COL_XGB = 0, 1, 2, 3, 4
COL_LW, COL_LA, COL_LG = 20, 21, 22

CHUNK = 64
VMEM_LIMIT = 56 * 1024 * 1024

_NN = (((1,), (0,)), ((), ()))
_NT = (((1,), (1,)), ((), ()))


def _mm(a, b, dims=_NN):
    return lax.dot_general(a, b, dims, preferred_element_type=F32)


def _split2(x):
    hi = x.astype(BF16)
    lo = (x - hi.astype(F32)).astype(BF16)
    return hi, lo


def _split3(x):
    hi = x.astype(BF16)
    r1 = x - hi.astype(F32)
    mid = r1.astype(BF16)
    lo = (r1 - mid.astype(F32)).astype(BF16)
    return hi, mid, lo


def _dot1(a, b, dims=_NN):
    return _mm(a.astype(BF16), b.astype(BF16), dims)


def _dot3(a, b):
    ah, al = _split2(a)
    bh, bl = _split2(b)
    return _mm(jnp.concatenate([ah, ah, al], axis=1), jnp.concatenate([bh, bl, bh], axis=0))


def _dot_xb(a, b_bf16):
    return _mm(jnp.concatenate(_split3(a), axis=1), jnp.concatenate([b_bf16] * 3, axis=0))


def _dot_xa(a_bf16, b):
    return _mm(jnp.concatenate([a_bf16] * 3, axis=1), jnp.concatenate(_split3(b), axis=0))


def _layer_norm(x):
    mu = jnp.mean(x, axis=-1, keepdims=True)
    xc = x - mu
    var = jnp.mean(xc * xc, axis=-1, keepdims=True)
    return xc * lax.rsqrt(var + LN_EPS)


def _cparams(sem):
    return pltpu.CompilerParams(dimension_semantics=sem, vmem_limit_bytes=VMEM_LIMIT)


def _mod_kernel(c_ref, w_ref, b_ref, o_ref):
    c = c_ref[...]
    s = c * jax.nn.sigmoid(c)
    o_ref[...] = _dot3(s, w_ref[...]) + b_ref[...]


def _modulation(cond8, w_mod, b_mod):
    n = w_mod.shape[1]
    tn = 1024
    return pl.pallas_call(
        _mod_kernel,
        grid=(n // tn,),
        in_specs=[pl.BlockSpec((8, D_MODEL), lambda j: (0, 0)),
                  pl.BlockSpec((D_MODEL, tn), lambda j: (0, j)),
                  pl.BlockSpec((1, tn), lambda j: (0, j))],
        out_specs=pl.BlockSpec((8, tn), lambda j: (0, j)),
        out_shape=jax.ShapeDtypeStruct((8, n), F32),
        compiler_params=_cparams(("arbitrary",)),
        name="modulation",
    )(cond8, w_mod, b_mod.reshape(1, n))


def _inproj_kernel(x_ref, mod_ref, w_ref, o_ref):
    m = mod_ref[0]
    u = _layer_norm(x_ref[...]) * (1.0 + m[1:2]) + m[0:1]
    o_ref[...] = _dot1(u, w_ref[...])


def _inproj(x, mod_rows, w_in_bf16, tok_per_row):
    ntok = x.shape[0]
    tm = 256
    per = tok_per_row // tm
    return pl.pallas_call(
        _inproj_kernel,
        grid=(ntok // tm,),
        in_specs=[pl.BlockSpec((tm, D_MODEL), lambda i: (i, 0)),
                  pl.BlockSpec((1, 6, D_MODEL), lambda i: (i // per, 0, 0)),
                  pl.BlockSpec((D_MODEL, D_IN), lambda i: (0, 0))],
        out_specs=pl.BlockSpec((tm, D_IN), lambda i: (i, 0)),
        out_shape=jax.ShapeDtypeStruct((ntok, D_IN), F32),
        compiler_params=_cparams(("arbitrary",)),
        name="inproj",
    )(x, mod_rows, w_in_bf16)


def _L(s):
    hi, lo = s
    return jnp.concatenate([hi, hi, lo], axis=1)


def _R(s):
    hi, lo = s
    return jnp.concatenate([hi, lo, hi], axis=0)


def _Rt(s):
    hi, lo = s
    return jnp.concatenate([hi, lo, hi], axis=1)


def _rwkv_kernel(rf_ref, kf_ref, vf_ref, lwf_ref, laf_ref, rb_ref, kb_ref, vb_ref, lwb_ref, lab_ref,
                 s0_ref, wdec_ref, bdec_ref, wic_ref, bic_ref, kkw_ref, ka_ref, rk_ref, hs_ref,
                 of_ref, ob_ref, bonf_ref, bonb_ref, sfin_ref,
                 s_scr, t_scr, w3_scr, mr_scr, ar_scr, bk_scr, gt_scr, *, G, ng):
    L = CHUNK
    g = pl.program_id(1)
    ins = ((rf_ref, kf_ref, vf_ref, lwf_ref, laf_ref), (rb_ref, kb_ref, vb_ref, lwb_ref, lab_ref))
    outs = ((of_ref, bonf_ref), (ob_ref, bonb_ref))

    @pl.when(g == 0)
    def _():
        s_scr[...] = s0_ref[0]

    hs = hs_ref[...]
    lane = lax.broadcasted_iota(jnp.int32, (L, 128), 1)
    rowp = lax.broadcasted_iota(jnp.int32, (L, 128), 0)
    colp = jnp.bitwise_and(lane, 63)
    h0 = lane < 64
    m0 = jnp.where(h0, 1.0, 0.0).astype(BF16)
    m1 = jnp.where(h0, 0.0, 1.0).astype(BF16)
    eye2 = jnp.where(rowp == colp, 1.0, 0.0)
    rowL = lax.broadcasted_iota(jnp.int32, (L, L), 0)
    colL = lax.broadcasted_iota(jnp.int32, (L, L), 1)

    def same(sh):
        return lax.shift_right_logical(rowp, sh) == lax.shift_right_logical(colp, sh)

    same8, same16, same32 = same(3), same(4), same(5)

    def bmask(c):
        return jnp.where(c, 1.0, 0.0).astype(BF16)

    mk8 = bmask(same8)
    merge_masks = (bmask(same16 & (~same8)), bmask(same32 & (~same16)), bmask(~same32))

    def sms(s):
        return tuple(jnp.concatenate([x * m0, x * m1], axis=0) for x in s)

    def pm(xs, ys):
        return _mm(_L(xs), _R(sms(ys)))

    chains = [(d, p) for d in range(2) for p in range(N_PAIRS)]

    def each(fn, *lists):
        return [fn(*a) for a in zip(*lists)]

    def prep(j, carry):
        pre = []
        for d in range(2):
            r_ref, k_ref, v_ref, lw_ref, la_ref = ins[d]
            jn = j if d == 0 else G - 1 - j
            rows = pl.ds(pl.multiple_of(jn * L, L), L)
            u = d * G + j
            r = r_ref[rows, :]
            k = k_ref[rows, :]
            v = v_ref[rows, :]
            dsel = (lane >= 64) if d else h0
            tl = jnp.where(dsel, jnp.tanh(lw_ref[rows, :]), 0.0)
            dw = _dot3(tl, wdec_ref[...]) + bdec_ref[d:d + 1, :]
            logw = (-math.exp(-0.5)) * jax.nn.sigmoid(dw)
            la_m = jnp.where(dsel, la_ref[rows, :], 0.0)
            iclr = jax.nn.sigmoid(_dot3(la_m, wic_ref[...]) + bic_ref[d:d + 1, :])
            kkr = k * kkw_ref[...]
            nrm = jnp.sqrt(_dot_xb(kkr * kkr, hs))
            kk = kkr / jnp.maximum(nrm, 1e-12)
            kdir = k * (1.0 + (iclr - 1.0) * ka_ref[...])
            bv = kk * iclr
            outs[d][1][rows, :] = _dot_xb(r * kdir * rk_ref[...], hs) * v

            tri = bmask((rowL <= colL) if d else (rowL >= colL))
            cs = _dot_xa(tri, logw)
            ctot = jnp.sum(logw, axis=0, keepdims=True)
            g_inv = jnp.exp(-cs)
            g_rem = jnp.exp(ctot - cs)
            gt_scr[u] = jnp.exp(ctot)
            At = -kk * jnp.exp(cs - logw)
            Rt = r * jnp.exp(cs)
            Bt = bv * g_inv
            Kt = kdir * g_inv
            Bg = bv * g_rem
            Kg = kdir * g_rem
            for p in range(N_PAIRS):
                sl = slice(128 * p, 128 * (p + 1))
                ar = jnp.concatenate([At[:, sl], Rt[:, sl]], axis=0)
                ar_scr[u, p] = ar
                bk_scr[u, p] = jnp.concatenate([Bg[:, sl], Kg[:, sl]], axis=0)
                pre.append((ar, Bt[:, sl], Kt[:, sl], v[:, sl]))

        def stage_m(c, pr):
            d = c[0]
            ar, bt, kt, _ = pr
            strict2 = (rowp < colp) if d else (rowp > colp)
            incl2 = (rowp <= colp) if d else (rowp >= colp)
            smB, smK = sms(_split2(bt)), sms(_split2(kt))
            rhs = tuple(jnp.concatenate([b_, k_], axis=0) for b_, k_ in zip(smB, smK))
            M = _mm(_L(_split2(ar)), _Rt(rhs), _NT)
            N = jnp.where(strict2, M[:L, :128], 0.0)
            Mak = jnp.where(strict2, M[:L, 128:], 0.0)
            mr_scr[d * G + j, c[1]] = jnp.concatenate([jnp.where(incl2, M[L:, :128], 0.0),
                                                       jnp.where(incl2, M[L:, 128:], 0.0)], axis=1)
            return N, Mak

        NM = each(stage_m, chains, pre)
        Ns = [_split2(nm[0]) for nm in NM]
        N0s = [(n[0] * mk8, n[1] * mk8) for n in Ns]
        T = [eye2 + jnp.where(same8, nm[0], 0.0) for nm in NM]
        P2s = [_split2(pm(n, n)) for n in N0s]
        W2 = [pm(_split2(nm[1]), _split2(pr[3])) for nm, pr in zip(NM, pre)]
        T = [t + pm(_split2(t), p2) for t, p2 in zip(T, P2s)]
        P4s = [_split2(pm(p2, p2)) for p2 in P2s]
        T = [t + pm(_split2(t), p4) for t, p4 in zip(T, P4s)]
        for mk in merge_masks:
            Ts = [_split2(t) for t in T]
            Y = [pm(ts, (n[0] * mk, n[1] * mk)) for ts, n in zip(Ts, Ns)]
            T = [t + pm(_split2(y), ts) for t, y, ts in zip(T, Y, Ts)]
        for (d, p), t, w2 in zip(chains, T, W2):
            t_scr[d * G + j, p] = t
            w3_scr[d * G + j, p] = pm(_split2(t), _split2(w2))
        return carry

    lax.fori_loop(0, G, prep, 0)

    def serial(j, carry):
        us = [d * G + j for d, _ in chains]
        rws = [pl.ds(pl.multiple_of((j if d == 0 else G - 1 - j) * L, L), L) for d, _ in chains]
        sls = [slice(128 * p, 128 * (p + 1)) for _, p in chains]
        S2 = [s_scr[d, p] for d, p in chains]
        XR = [_mm(_L(_split2(ar_scr[u, p])), _Rt(sms(_split2(s2))), _NT)
              for (d, p), u, s2 in zip(chains, us, S2)]
        U = [pm(_split2(t_scr[u, p]), _split2(xr[:L])) + w3_scr[u, p] for (d, p), u, xr in zip(chains, us, XR)]
        V = [ins[d][2][rw, sl] for (d, p), rw, sl in zip(chains, rws, sls)]
        for (d, p), u, rw, sl, s2, xr, uu, vv in zip(chains, us, rws, sls, S2, XR, U, V):
            UVt = jnp.concatenate([uu, vv], axis=0).T
            Rm = _mm(_L(_split2(UVt)), _R(_split2(bk_scr[u, p])))
            s_scr[d, p] = gt_scr[u][:, sl] * s2 + jnp.where(h0, Rm[:HEAD], Rm[HEAD:])
        for (d, p), u, rw, sl, xr, uu, vv in zip(chains, us, rws, sls, XR, U, V):
            smU, smV = sms(_split2(uu)), sms(_split2(vv))
            rhs = jnp.concatenate([smU[0], smV[0], smU[1], smV[1], smU[0], smV[0]], axis=0)
            outs[d][0][rw, sl] = xr[L:] + _mm(_L(_split2(mr_scr[u, p])), rhs)
        return carry

    lax.fori_loop(0, G, serial, 0)

    @pl.when(g == ng - 1)
    def _():
        for d in range(2):
            for p in range(N_PAIRS):
                sp = s_scr[d, p]
                sfin_ref[0, d, 2 * p] = sp[:, :HEAD]
                sfin_ref[0, d, 2 * p + 1] = sp[:, HEAD:]


def _rwkv(proj, s0_pairs, wdec, bdec, wic, bic, kkw, ka, rk, hs, nseq, T):
    ntok = proj.shape[0]
    nc = T // CHUNK
    G = min(nc, 8)
    ng = nc // G
    GL = G * CHUNK

    def fwd(b, g):
        return b * ng + g

    def bwd(b, g):
        return b * ng + ng - 1 - g

    def pspecs(rowblk):
        return [pl.BlockSpec((GL, D_A), lambda b, g: (rowblk(b, g), COL_R)),
                pl.BlockSpec((GL, D_A), lambda b, g: (rowblk(b, g), COL_K)),
                pl.BlockSpec((GL, D_A), lambda b, g: (rowblk(b, g), COL_V)),
                pl.BlockSpec((GL, 128), lambda b, g: (rowblk(b, g), COL_LW)),
                pl.BlockSpec((GL, 128), lambda b, g: (rowblk(b, g), COL_LA))]

    def wspec(shape):
        return pl.BlockSpec(shape, lambda b, g: (0,) * len(shape))

    ospec_f = pl.BlockSpec((GL, D_A), lambda b, g: (fwd(b, g), 0))
    ospec_b = pl.BlockSpec((GL, D_A), lambda b, g: (bwd(b, g), 0))
    tok = jax.ShapeDtypeStruct((ntok, D_A), F32)
    return pl.pallas_call(
        functools.partial(_rwkv_kernel, G=G, ng=ng),
        grid=(nseq, ng),
        in_specs=pspecs(fwd) + pspecs(bwd) + [
            pl.BlockSpec((1, 2, N_PAIRS, HEAD, 128), lambda b, g: (b, 0, 0, 0, 0)),
            wspec((128, D_A)), wspec((2, D_A)), wspec((128, D_A)), wspec((2, D_A)),
            wspec((1, D_A)), wspec((1, D_A)), wspec((1, D_A)), wspec((D_A, D_A))],
        out_specs=[ospec_f, ospec_b, ospec_f, ospec_b,
                   pl.BlockSpec((1, 2, N_HEADS, HEAD, HEAD), lambda b, g: (b, 0, 0, 0, 0))],
        out_shape=[tok, tok, tok, tok, jax.ShapeDtypeStruct((nseq, 2, N_HEADS, HEAD, HEAD), F32)],
        scratch_shapes=[pltpu.VMEM((2, N_PAIRS, HEAD, 128), F32),
                        pltpu.VMEM((2 * G, N_PAIRS, CHUNK, 128), F32),
                        pltpu.VMEM((2 * G, N_PAIRS, CHUNK, 128), F32),
                        pltpu.VMEM((2 * G, N_PAIRS, CHUNK, 256), F32),
                        pltpu.VMEM((2 * G, N_PAIRS, 2 * CHUNK, 128), F32),
                        pltpu.VMEM((2 * G, N_PAIRS, 2 * CHUNK, 128), F32),
                        pltpu.VMEM((2 * G, 1, D_A), F32)],
        compiler_params=_cparams(("arbitrary", "arbitrary")),
        name="rwkv_scan",
    )(*([proj] * 10), s0_pairs, wdec, bdec, wic, bic, kkw, ka, rk, hs)


def _gelu_tanh(x):
    return 0.5 * x * (1.0 + jnp.tanh(math.sqrt(2.0 / math.pi) * (x + 0.044715 * (x * x * x))))


def _rglru_kernel(xr_ref, xg_ref, h0_ref, cw_ref, cb_ref, wa_ref, ba_ref, wi_ref, bi_ref, lam_ref,
                  y_ref, hfin_ref, a_scr, u_scr, *, T, lrow):
    RB = min(T, 256)
    nblk = T // RB
    rowi = lax.broadcasted_iota(jnp.int32, (RB, D_B), 0)
    pos = jnp.bitwise_and(rowi, lrow - 1)
    cw = cw_ref[...]
    lam = lam_ref[...]
    sp = jnp.maximum(-lam, 0.0) + jnp.log1p(jnp.exp(-jnp.abs(lam)))

    def gates(blk, carry):
        r0 = pl.multiple_of(blk * RB, RB)
        x = xr_ref[pl.ds(r0, RB), :]
        xm1 = jnp.where(pos >= 1, pltpu.roll(x, 1, 0), 0.0)
        xm2 = jnp.where(pos >= 2, pltpu.roll(x, 2, 0), 0.0)
        xp1 = jnp.where(pos <= lrow - 2, pltpu.roll(x, RB - 1, 0), 0.0)
        xc = xm2 * cw[0:1] + xm1 * cw[1:2] + x * cw[2:3] + xp1 * cw[3:4] + cb_ref[...]
        for dd in range(2):
            gr = jax.nn.sigmoid(_dot1(xc, wa_ref[dd]) + ba_ref[dd:dd + 1, :])
            gi = jax.nn.sigmoid(_dot1(xc, wi_ref[dd]) + bi_ref[dd:dd + 1, :])
            log_a = (-RGLRU_C) * gr * sp[dd:dd + 1, :]
            a = jnp.exp(log_a)
            a_scr[dd, pl.ds(r0, RB), :] = a
            u_scr[dd, pl.ds(r0, RB), :] = jnp.sqrt((1.0 - a) * (1.0 + a)) * gi * xc
        return carry

    lax.fori_loop(0, nblk, gates, 0)

    def step(t, carry):
        hf, hb = carry
        hf = a_scr[0, pl.ds(t, 1), :] * hf + u_scr[0, pl.ds(t, 1), :]
        u_scr[0, pl.ds(t, 1), :] = hf
        tb = T - 1 - t
        hb = a_scr[1, pl.ds(tb, 1), :] * hb + u_scr[1, pl.ds(tb, 1), :]
        u_scr[1, pl.ds(tb, 1), :] = hb
        return hf, hb

    h0 = h0_ref[0]
    hf, hb = lax.fori_loop(0, T, step, (h0[0:1], h0[1:2]), unroll=8)
    hfin_ref[0] = jnp.concatenate([hf, hb], axis=0)

    def outp(blk, carry):
        r0 = pl.multiple_of(blk * RB, RB)
        h = u_scr[0, pl.ds(r0, RB), :] + u_scr[1, pl.ds(r0, RB), :]
        y_ref[pl.ds(r0, RB), :] = h * _gelu_tanh(xg_ref[pl.ds(r0, RB), :])
        return carry

    lax.fori_loop(0, nblk, outp, 0)


def _rglru(proj, h0, conv_w, conv_b, wa_bd, ba, wi_bd, bi, lam, nseq, T, lrow):
    ntok = proj.shape[0]

    def wspec(shape):
        return pl.BlockSpec(shape, lambda b: (0,) * len(shape))

    return pl.pallas_call(
        functools.partial(_rglru_kernel, T=T, lrow=lrow),
        grid=(nseq,),
        in_specs=[pl.BlockSpec((T, D_B), lambda b: (b, COL_XR)),
                  pl.BlockSpec((T, D_B), lambda b: (b, COL_XGB)),
                  pl.BlockSpec((1, 2, D_B), lambda b: (b, 0, 0)),
                  wspec((4, D_B)), wspec((1, D_B)),
                  wspec((2, D_B, D_B)), wspec((2, D_B)), wspec((2, D_B, D_B)), wspec((2, D_B)),
                  wspec((2, D_B))],
        out_specs=[pl.BlockSpec((T, D_B), lambda b: (b, 0)),
                   pl.BlockSpec((1, 2, D_B), lambda b: (b, 0, 0))],
        out_shape=[jax.ShapeDtypeStruct((ntok, D_B), F32),
                   jax.ShapeDtypeStruct((nseq, 2, D_B), F32)],
        scratch_shapes=[pltpu.VMEM((2, T, D_B), F32), pltpu.VMEM((2, T, D_B), F32)],
        compiler_params=_cparams(("arbitrary",)),
        name="rglru",
    )(proj, proj, h0, conv_w, conv_b, wa_bd, ba, wi_bd, bi, lam)


def _mixout_kernel(of_ref, ob_ref, bf_ref, bb_ref, yb_ref, lg_ref, x_ref, mod_ref, wgu_ref, avg_ref,
                   gnw_ref, gnb_ref, wout_ref, l1w_ref, l1b_ref, wr_ref, br_ref,
                   x1_ref, u2_ref, cw_ref):
    m = mod_ref[0]
    avg = avg_ref[...]
    wkv = of_ref[...] + ob_ref[...]
    mu = _dot_xb(wkv, avg)
    dv = wkv - mu
    var = _dot_xb(dv * dv, avg)
    gn = dv * lax.rsqrt(var + GN_EPS) * gnw_ref[...] + gnb_ref[...]
    g = _dot1(jax.nn.sigmoid(lg_ref[...]), wgu_ref[...])
    ya = (gn + (bf_ref[...] + bb_ref[...])) * g
    mix = _dot1(ya, wout_ref[0:D_A, :]) + _dot1(yb_ref[...], wout_ref[D_A:, :])
    x1 = _layer_norm(DEEPNORM_ALPHA * x_ref[...] + m[2:3] * mix) * l1w_ref[...] + l1b_ref[...]
    x1_ref[...] = x1
    u2 = _layer_norm(x1) * (1.0 + m[4:5]) + m[3:4]
    u2_ref[...] = u2.astype(BF16)
    logits = _dot3(u2, wr_ref[...]) + br_ref[...]
    tm = logits.shape[0]
    lane = lax.broadcasted_iota(jnp.int32, (tm, N_EXPERTS), 1)
    work = logits
    tops, hots = [], []
    for _ in range(TOP_K):
        mx = jnp.max(work, axis=1, keepdims=True)
        idx = jnp.min(jnp.where(work == mx, lane, N_EXPERTS), axis=1, keepdims=True)
        hot = lane == idx
        tops.append(mx)
        hots.append(hot)
        work = jnp.where(hot, -jnp.inf, work)
    es = [jnp.exp(t - tops[0]) for t in tops]
    den = es[0] + es[1] + es[2] + es[3]
    cw = jnp.zeros((tm, N_EXPERTS), F32)
    for e_, hot in zip(es, hots):
        cw = cw + jnp.where(hot, e_ / den, 0.0)
    cw_ref[...] = cw


def _mixout(o_f, o_b, bon_f, bon_b, yb, proj, x, mod_rows, tok_per_row, wgu, avg, gnw, gnb, wout, l1w, l1b,
            wr, br):
    ntok = x.shape[0]
    tm = 256
    per = tok_per_row // tm

    def wspec(shape):
        return pl.BlockSpec(shape, lambda i: (0,) * len(shape))

    return pl.pallas_call(
        _mixout_kernel,
        grid=(ntok // tm,),
        in_specs=[pl.BlockSpec((tm, D_A), lambda i: (i, 0)),
                  pl.BlockSpec((tm, D_A), lambda i: (i, 0)),
                  pl.BlockSpec((tm, D_A), lambda i: (i, 0)),
                  pl.BlockSpec((tm, D_A), lambda i: (i, 0)),
                  pl.BlockSpec((tm, D_B), lambda i: (i, 0)),
                  pl.BlockSpec((tm, LORA_G), lambda i: (i, COL_LG)),
                  pl.BlockSpec((tm, D_MODEL), lambda i: (i, 0)),
                  pl.BlockSpec((1, 6, D_MODEL), lambda i: (i // per, 0, 0)),
                  wspec((LORA_G, D_A)), wspec((D_A, D_A)), wspec((1, D_A)), wspec((1, D_A)),
                  wspec((D_MODEL, D_MODEL)), wspec((1, D_MODEL)), wspec((1, D_MODEL)),
                  wspec((D_MODEL, N_EXPERTS)), wspec((1, N_EXPERTS))],
        out_specs=[pl.BlockSpec((tm, D_MODEL), lambda i: (i, 0)),
                   pl.BlockSpec((tm, D_MODEL), lambda i: (i, 0)),
                   pl.BlockSpec((tm, N_EXPERTS), lambda i: (i, 0))],
        out_shape=[jax.ShapeDtypeStruct((ntok, D_MODEL), F32),
                   jax.ShapeDtypeStruct((ntok, D_MODEL), BF16),
                   jax.ShapeDtypeStruct((ntok, N_EXPERTS), F32)],
        compiler_params=_cparams(("arbitrary",)),
        name="mixout",
    )(o_f, o_b, bon_f, bon_b, yb, proj, x, mod_rows, wgu, avg, gnw, gnb, wout, l1w, l1b, wr, br)


def _moe_kernel(up_ref, us_ref, cwp_ref, cws_ref, wg_ref, bg_ref, wu_ref, bu_ref, wd_ref, bd_ref,
                o_ref, *, n_ctx_tiles):
    i = pl.program_id(0)
    e = pl.program_id(1)
    tm = o_ref.shape[0]
    is_ctx = i < n_ctx_tiles
    x = jnp.where(is_ctx, up_ref[...], us_ref[...])
    cw = jnp.where(is_ctx, cwp_ref[...], cws_ref[...])
    hot = jnp.where(lax.broadcasted_iota(jnp.int32, (N_EXPERTS, D_MODEL), 0) == e, 1.0, 0.0).astype(BF16)
    cwe = _dot_xb(cw, hot)
    FC = 256
    y = jnp.zeros((tm, D_MODEL), F32)
    for j in range(D_FF // FC):
        cs = slice(j * FC, (j + 1) * FC)
        gate = jnp.minimum(_mm(x, wg_ref[0, :, cs].astype(BF16)) + bg_ref[0, :, cs], SWIGLU_LIMIT)
        up = jnp.clip(_mm(x, wu_ref[0, :, cs].astype(BF16)) + bu_ref[0, :, cs], -SWIGLU_LIMIT, SWIGLU_LIMIT)
        h = (up + 1.0) * gate * jax.nn.sigmoid(SWIGLU_ALPHA * gate)
        y = y + _mm(h.astype(BF16), wd_ref[0, cs, :].astype(BF16))
    contrib = cwe * (y + bd_ref[0])

    @pl.when(e == 0)
    def _():
        o_ref[...] = contrib

    @pl.when(e != 0)
    def _():
        o_ref[...] += contrib


def _moe(u2_p, u2_s, cw_p, cw_s, wg, bg, wu, bu, wd, bd):
    n_p, n_s = u2_p.shape[0], u2_s.shape[0]
    tm = 1024
    tp, ts = n_p // tm, n_s // tm

    def pidx(i, e):
        return (jnp.minimum(i, tp - 1), 0)

    def sidx(i, e):
        return (jnp.maximum(i - tp, 0), 0)

    wspec = pl.BlockSpec((1, D_MODEL, D_FF), lambda i, e: (e, 0, 0))
    bspec = pl.BlockSpec((1, 1, D_FF), lambda i, e: (e, 0, 0))
    return pl.pallas_call(
        functools.partial(_moe_kernel, n_ctx_tiles=tp),
        grid=(tp + ts, N_EXPERTS),
        in_specs=[pl.BlockSpec((tm, D_MODEL), pidx), pl.BlockSpec((tm, D_MODEL), sidx),
                  pl.BlockSpec((tm, N_EXPERTS), pidx), pl.BlockSpec((tm, N_EXPERTS), sidx),
                  wspec, bspec, wspec, bspec, wspec, bspec],
        out_specs=pl.BlockSpec((tm, D_MODEL), lambda i, e: (i, 0)),
        out_shape=jax.ShapeDtypeStruct((n_p + n_s, D_MODEL), F32),
        compiler_params=_cparams(("arbitrary", "arbitrary")),
        name="moe",
    )(u2_p, u2_s, cw_p, cw_s, wg, bg.reshape(N_EXPERTS, 1, D_FF), wu, bu.reshape(N_EXPERTS, 1, D_FF),
      wd, bd.reshape(N_EXPERTS, 1, D_MODEL))


def _final_kernel(x1_ref, moe_ref, mod_ref, w_ref, b_ref, o_ref):
    m = mod_ref[0]
    o_ref[...] = _layer_norm(DEEPNORM_ALPHA * x1_ref[...] + m[5:6] * moe_ref[...]) * w_ref[...] + b_ref[...]


def _final(x1, moe, moe_off_tiles, mod_rows, tok_per_row, w, b):
    ntok = x1.shape[0]
    tm = 512
    per = tok_per_row // tm
    return pl.pallas_call(
        _final_kernel,
        grid=(ntok // tm,),
        in_specs=[pl.BlockSpec((tm, D_MODEL), lambda i: (i, 0)),
                  pl.BlockSpec((tm, D_MODEL), lambda i: (i + moe_off_tiles, 0)),
                  pl.BlockSpec((1, 6, D_MODEL), lambda i: (i // per, 0, 0)),
                  pl.BlockSpec((1, D_MODEL), lambda i: (0, 0)),
                  pl.BlockSpec((1, D_MODEL), lambda i: (0, 0))],
        out_specs=pl.BlockSpec((tm, D_MODEL), lambda i: (i, 0)),
        out_shape=jax.ShapeDtypeStruct((ntok, D_MODEL), F32),
        compiler_params=_cparams(("arbitrary",)),
        name="final_ln",
    )(x1, moe, mod_rows, w, b)


def _block_diag(w):
    nb, bb, _ = w.shape
    eye = jnp.eye(nb, dtype=w.dtype)
    return jnp.einsum('nij,nm->nimj', w, eye).reshape(nb * bb, nb * bb)


def kernel(x_prompt, x_sample, state_rwkv, state_rglru, c, c_ctx, w_mod, b_mod, w_in, w_decay_up, b_decay, w_iclr_up, b_iclr, w_gate_up, k_k, k_a, r_k, gn_w, gn_b, conv_w, conv_b, w_rg_a, b_rg_a, w_rg_i, b_rg_i, lam, w_out, ln1_w, ln1_b, w_router, b_router, w_e_gate, b_e_gate, w_e_up, b_e_up, w_e_down, b_e_down, ln2_w, ln2_b):
    Bp, Tp, D = x_prompt.shape
    Bs, Ts, _ = x_sample.shape
    l = 0
    row = lambda a: a[l].reshape(1, -1)

    cond8 = jnp.concatenate([c_ctx[None, :], c, jnp.zeros((8 - 1 - Bs, D), F32)], axis=0)
    mod = _modulation(cond8, w_mod[l], b_mod[l]).reshape(8, 6, D)
    mod_p, mod_s = mod[0:1], mod[1:1 + Bs]

    w_in_b = w_in[l].astype(BF16)
    w_out_b = w_out[l].astype(BF16)
    head_id = jnp.arange(D_A) // HEAD
    hs = (head_id[:, None] == head_id[None, :]).astype(BF16)
    avg = (hs.astype(F32) / HEAD).astype(BF16)
    wdec = w_decay_up[l].reshape(2 * LORA, D_A)
    wic = w_iclr_up[l].reshape(2 * LORA, D_A)
    wa_bd = jnp.stack([_block_diag(w_rg_a[l, 0]), _block_diag(w_rg_a[l, 1])])
    wi_bd = jnp.stack([_block_diag(w_rg_i[l, 0]), _block_diag(w_rg_i[l, 1])])

    xp = x_prompt.reshape(Bp * Tp, D)
    xs = x_sample.reshape(Bs * Ts, D)

    s0_s = state_rwkv[:, l].reshape(Bs, 2, N_PAIRS, 2, HEAD, HEAD).transpose(0, 1, 2, 4, 3, 5)
    s0_s = s0_s.reshape(Bs, 2, N_PAIRS, HEAD, 2 * HEAD)
    s0_p = jnp.zeros((Bp, 2, N_PAIRS, HEAD, 2 * HEAD), F32)
    h0_p = jnp.zeros((Bp, 2, D_B), F32)
    h0_s = state_rglru[:, l]

    outs = []
    for x, mod_rows, tok_per_row, nseq, T, s0, h0, lrow in (
            (xp, mod_p, Bp * Tp, Bp, Tp, s0_p, h0_p, Tp),
            (xs, mod_s, Ts, Bs, Ts, s0_s, h0_s, GRID_W)):
        proj = _inproj(x, mod_rows, w_in_b, tok_per_row)
        o_f, o_b, bon_f, bon_b, s_fin = _rwkv(proj, s0, wdec, b_decay[l], wic, b_iclr[l], row(k_k), row(k_a),
                                              r_k[l].reshape(1, D_A), hs, nseq, T)
        yb, h_fin = _rglru(proj, h0, conv_w[l], row(conv_b), wa_bd, b_rg_a[l], wi_bd, b_rg_i[l], lam[l],
                           nseq, T, lrow)
        x1, u2, cw = _mixout(o_f, o_b, bon_f, bon_b, yb, proj, x, mod_rows, tok_per_row, w_gate_up[l], avg,
                             row(gn_w), row(gn_b), w_out_b, row(ln1_w), row(ln1_b), w_router[l], row(b_router))
        outs.append((x1, u2, cw, s_fin, h_fin, mod_rows, tok_per_row))

    (x1_p, u2_p, cw_p, sfin_p, hfin_p, _, _), (x1_s, u2_s, cw_s, _, _, _, _) = outs
    moe = _moe(u2_p, u2_s, cw_p, cw_s, w_e_gate[l], b_e_gate[l], w_e_up[l], b_e_up[l], w_e_down[l], b_e_down[l])
    ftm = 512
    y_p = _final(x1_p, moe, 0, mod_p, Bp * Tp, row(ln2_w), row(ln2_b))
    y_s = _final(x1_s, moe, (Bp * Tp) // ftm, mod_s, Ts, row(ln2_w), row(ln2_b))
    return (y_p.reshape(Bp, Tp, D), y_s.reshape(Bs, Ts, D),
            sfin_p[:, None], hfin_p[:, None])
```

```python
import functools
import math

import jax
import jax.numpy as jnp
from jax import lax
from jax.experimental import pallas as pl
from jax.experimental.pallas import tpu as pltpu

F32 = jnp.float32
BF16 = jnp.bfloat16

D_MODEL = 1024
D_A = 512
D_B = 512
HEAD = 64
N_HEADS = 8
N_PAIRS = N_HEADS // 2
GRID_W = 64
LORA = 64
LORA_G = 128
RGLRU_C = 8.0
N_EXPERTS = 32
TOP_K = 4
D_FF = 1024
SWIGLU_LIMIT = 7.0
SWIGLU_ALPHA = 1.702
LN_EPS = 1e-5
GN_EPS = 1e-5 * HEAD
D_IN = 3 * D_A + 2 * D_B + 2 * LORA + 2 * LORA + LORA_G
DEPTH = 1
DEEPNORM_ALPHA = (2 * DEPTH) ** 0.25

COL_R, COL_K, COL_V, COL_XR, COL_XGB = 0, 1, 2, 3, 4
COL_LW, COL_LA, COL_LG = 20, 21, 22

CHUNK = 64
VMEM_LIMIT = 56 * 1024 * 1024

_NN = (((1,), (0,)), ((), ()))
_NT = (((1,), (1,)), ((), ()))


def _mm(a, b, dims=_NN):
    return lax.dot_general(a, b, dims, preferred_element_type=F32)


def _split2(x):
    hi = x.astype(BF16)
    lo = (x - hi.astype(F32)).astype(BF16)
    return hi, lo


def _split3(x):
    hi = x.astype(BF16)
    r1 = x - hi.astype(F32)
    mid = r1.astype(BF16)
    lo = (r1 - mid.astype(F32)).astype(BF16)
    return hi, mid, lo


def _dot1(a, b, dims=_NN):
    return _mm(a.astype(BF16), b.astype(BF16), dims)


def _dot3(a, b):
    ah, al = _split2(a)
    bh, bl = _split2(b)
    return _mm(jnp.concatenate([ah, ah, al], axis=1), jnp.concatenate([bh, bl, bh], axis=0))


def _dot_xb(a, b_bf16):
    return _mm(jnp.concatenate(_split3(a), axis=1), jnp.concatenate([b_bf16] * 3, axis=0))


def _dot_xa(a_bf16, b):
    return _mm(jnp.concatenate([a_bf16] * 3, axis=1), jnp.concatenate(_split3(b), axis=0))


ROW_TILE = 8


def _tok_rows(t, n=1):
    return pl.ds(pl.multiple_of(t * ROW_TILE, ROW_TILE), n * ROW_TILE)


def _load_tok_tiles(ref, ntok, lead=()):
    return jnp.concatenate([ref[lead + (pl.ds(s_, ntok, stride=ROW_TILE), slice(None))]
                            for s_ in range(ROW_TILE)], axis=1)


def _store_tok_tiles(ref, x):
    ntok = x.shape[0]
    for s_ in range(ROW_TILE):
        ref[pl.ds(s_, ntok, stride=ROW_TILE), :] = x[:, 128 * s_:128 * (s_ + 1)]


def _layer_norm(x):
    mu = jnp.mean(x, axis=-1, keepdims=True)
    xc = x - mu
    var = jnp.mean(xc * xc, axis=-1, keepdims=True)
    return xc * lax.rsqrt(var + LN_EPS)


def _cparams(sem):
    return pltpu.CompilerParams(dimension_semantics=sem, vmem_limit_bytes=VMEM_LIMIT)


def _mod_kernel(c_ref, w_ref, b_ref, o_ref):
    c = c_ref[...]
    s = c * jax.nn.sigmoid(c)
    o_ref[...] = _dot3(s, w_ref[...]) + b_ref[...]


def _modulation(cond8, w_mod, b_mod):
    n = w_mod.shape[1]
    tn = 1024
    return pl.pallas_call(
        _mod_kernel,
        grid=(n // tn,),
        in_specs=[pl.BlockSpec((8, D_MODEL), lambda j: (0, 0)),
                  pl.BlockSpec((D_MODEL, tn), lambda j: (0, j)),
                  pl.BlockSpec((1, tn), lambda j: (0, j))],
        out_specs=pl.BlockSpec((8, tn), lambda j: (0, j)),
        out_shape=jax.ShapeDtypeStruct((8, n), F32),
        compiler_params=_cparams(("arbitrary",)),
        name="modulation",
    )(cond8, w_mod, b_mod.reshape(1, n))


def _inproj_kernel(x_ref, mod_ref, w_ref, o_ref):
    m = mod_ref[0]
    u = _layer_norm(x_ref[...]) * (1.0 + m[1:2]) + m[0:1]
    o_ref[...] = _dot1(u, w_ref[...])


def _inproj(x, mod_rows, w_in_bf16, tok_per_row):
    ntok = x.shape[0]
    tm = 256
    per = tok_per_row // tm
    return pl.pallas_call(
        _inproj_kernel,
        grid=(ntok // tm,),
        in_specs=[pl.BlockSpec((tm, D_MODEL), lambda i: (i, 0)),
                  pl.BlockSpec((1, 6, D_MODEL), lambda i: (i // per, 0, 0)),
                  pl.BlockSpec((D_MODEL, D_IN), lambda i: (0, 0))],
        out_specs=pl.BlockSpec((tm, D_IN), lambda i: (i, 0)),
        out_shape=jax.ShapeDtypeStruct((ntok, D_IN), F32),
        compiler_params=_cparams(("arbitrary",)),
        name="inproj",
    )(x, mod_rows, w_in_bf16)


def _L(s):
    hi, lo = s
    return jnp.concatenate([hi, hi, lo], axis=1)


def _R(s):
    hi, lo = s
    return jnp.concatenate([hi, lo, hi], axis=0)


def _Rt(s):
    hi, lo = s
    return jnp.concatenate([hi, lo, hi], axis=1)


def _rwkv_kernel(rf_ref, kf_ref, vf_ref, lwf_ref, laf_ref, rb_ref, kb_ref, vb_ref, lwb_ref, lab_ref,
                 s0_ref, wdec_ref, bdec_ref, wic_ref, bic_ref, kkw_ref, ka_ref, rk_ref, hs_ref,
                 of_ref, ob_ref, bonf_ref, bonb_ref, sfin_ref,
                 s_scr, t_scr, w3_scr, mr_scr, ar_scr, bk_scr, gt_scr, *, G, ng):
    L = CHUNK
    g = pl.program_id(1)
    ins = ((rf_ref, kf_ref, vf_ref, lwf_ref, laf_ref), (rb_ref, kb_ref, vb_ref, lwb_ref, lab_ref))
    outs = ((of_ref, bonf_ref), (ob_ref, bonb_ref))

    @pl.when(g == 0)
    def _():
        s_scr[...] = s0_ref[0]

    hs = hs_ref[...]
    lane = lax.broadcasted_iota(jnp.int32, (L, 128), 1)
    rowp = lax.broadcasted_iota(jnp.int32, (L, 128), 0)
    colp = jnp.bitwise_and(lane, 63)
    h0 = lane < 64
    m0 = jnp.where(h0, 1.0, 0.0).astype(BF16)
    m1 = jnp.where(h0, 0.0, 1.0).astype(BF16)
    eye2 = jnp.where(rowp == colp, 1.0, 0.0)
    rowL = lax.broadcasted_iota(jnp.int32, (L, L), 0)
    colL = lax.broadcasted_iota(jnp.int32, (L, L), 1)

    def same(sh):
        return lax.shift_right_logical(rowp, sh) == lax.shift_right_logical(colp, sh)

    same8, same16, same32 = same(3), same(4), same(5)

    def bmask(c):
        return jnp.where(c, 1.0, 0.0).astype(BF16)

    mk8 = bmask(same8)
    merge_masks = (bmask(same16 & (~same8)), bmask(same32 & (~same16)), bmask(~same32))

    def sms(s):
        return tuple(jnp.concatenate([x * m0, x * m1], axis=0) for x in s)

    def pm(xs, ys):
        return _mm(_L(xs), _R(sms(ys)))

    chains = [(d, p) for d in range(2) for p in range(N_PAIRS)]

    def each(fn, *lists):
        return [fn(*a) for a in zip(*lists)]

    def prep(j, carry):
        pre = []
        for d in range(2):
            r_ref, k_ref, v_ref, lw_ref, la_ref = ins[d]
            jn = j if d == 0 else G - 1 - j
            rows = pl.ds(pl.multiple_of(jn * L, L), L)
            u = d * G + j
            r = r_ref[rows, :]
            k = k_ref[rows, :]
            v = v_ref[rows, :]
            dsel = (lane >= 64) if d else h0
            tl = jnp.where(dsel, jnp.tanh(lw_ref[rows, :]), 0.0)
            dw = _dot3(tl, wdec_ref[...]) + bdec_ref[d:d + 1, :]
            logw = (-math.exp(-0.5)) * jax.nn.sigmoid(dw)
            la_m = jnp.where(dsel, la_ref[rows, :], 0.0)
            iclr = jax.nn.sigmoid(_dot3(la_m, wic_ref[...]) + bic_ref[d:d + 1, :])
            kkr = k * kkw_ref[...]
            nrm = jnp.sqrt(_dot_xb(kkr * kkr, hs))
            kk = kkr / jnp.maximum(nrm, 1e-12)
            kdir = k * (1.0 + (iclr - 1.0) * ka_ref[...])
            bv = kk * iclr
            outs[d][1][rows, :] = _dot_xb(r * kdir * rk_ref[...], hs) * v

            tri = bmask((rowL <= colL) if d else (rowL >= colL))
            cs = _dot_xa(tri, logw)
            ctot = jnp.sum(logw, axis=0, keepdims=True)
            g_inv = jnp.exp(-cs)
            g_rem = jnp.exp(ctot - cs)
            gt_scr[u] = jnp.exp(ctot)
            At = -kk * jnp.exp(cs - logw)
            Rt = r * jnp.exp(cs)
            Bt = bv * g_inv
            Kt = kdir * g_inv
            Bg = bv * g_rem
            Kg = kdir * g_rem
            for p in range(N_PAIRS):
                sl = slice(128 * p, 128 * (p + 1))
                ar = jnp.concatenate([At[:, sl], Rt[:, sl]], axis=0)
                ar_scr[u, p] = ar
                bk_scr[u, p] = jnp.concatenate([Bg[:, sl], Kg[:, sl]], axis=0)
                pre.append((ar, Bt[:, sl], Kt[:, sl], v[:, sl]))

        def stage_m(c, pr):
            d = c[0]
            ar, bt, kt, _ = pr
            strict2 = (rowp < colp) if d else (rowp > colp)
            incl2 = (rowp <= colp) if d else (rowp >= colp)
            smB, smK = sms(_split2(bt)), sms(_split2(kt))
            rhs = tuple(jnp.concatenate([b_, k_], axis=0) for b_, k_ in zip(smB, smK))
            M = _mm(_L(_split2(ar)), _Rt(rhs), _NT)
            N = jnp.where(strict2, M[:L, :128], 0.0)
            Mak = jnp.where(strict2, M[:L, 128:], 0.0)
            mr_scr[d * G + j, c[1]] = jnp.concatenate([jnp.where(incl2, M[L:, :128], 0.0),
                                                       jnp.where(incl2, M[L:, 128:], 0.0)], axis=1)
            return N, Mak

        NM = each(stage_m, chains, pre)
        Ns = [_split2(nm[0]) for nm in NM]
        N0s = [(n[0] * mk8, n[1] * mk8) for n in Ns]
        T = [eye2 + jnp.where(same8, nm[0], 0.0) for nm in NM]
        P2s = [_split2(pm(n, n)) for n in N0s]
        W2 = [pm(_split2(nm[1]), _split2(pr[3])) for nm, pr in zip(NM, pre)]
        T = [t + pm(_split2(t), p2) for t, p2 in zip(T, P2s)]
        P4s = [_split2(pm(p2, p2)) for p2 in P2s]
        T = [t + pm(_split2(t), p4) for t, p4 in zip(T, P4s)]
        for mk in merge_masks:
            Ts = [_split2(t) for t in T]
            Y = [pm(ts, (n[0] * mk, n[1] * mk)) for ts, n in zip(Ts, Ns)]
            T = [t + pm(_split2(y), ts) for t, y, ts in zip(T, Y, Ts)]
        for (d, p), t, w2 in zip(chains, T, W2):
            t_scr[d * G + j, p] = t
            w3_scr[d * G + j, p] = pm(_split2(t), _split2(w2))
        return carry

    lax.fori_loop(0, G, prep, 0)

    def serial(j, carry):
        us = [d * G + j for d, _ in chains]
        rws = [pl.ds(pl.multiple_of((j if d == 0 else G - 1 - j) * L, L), L) for d, _ in chains]
        sls = [slice(128 * p, 128 * (p + 1)) for _, p in chains]
        S2 = [s_scr[d, p] for d, p in chains]
        XR = [_mm(_L(_split2(ar_scr[u, p])), _Rt(sms(_split2(s2))), _NT)
              for (d, p), u, s2 in zip(chains, us, S2)]
        U = [pm(_split2(t_scr[u, p]), _split2(xr[:L])) + w3_scr[u, p] for (d, p), u, xr in zip(chains, us, XR)]
        V = [ins[d][2][rw, sl] for (d, p), rw, sl in zip(chains, rws, sls)]
        for (d, p), u, rw, sl, s2, xr, uu, vv in zip(chains, us, rws, sls, S2, XR, U, V):
            UVt = jnp.concatenate([uu, vv], axis=0).T
            Rm = _mm(_L(_split2(UVt)), _R(_split2(bk_scr[u, p])))
            s_scr[d, p] = gt_scr[u][:, sl] * s2 + jnp.where(h0, Rm[:HEAD], Rm[HEAD:])
        for (d, p), u, rw, sl, xr, uu, vv in zip(chains, us, rws, sls, XR, U, V):
            smU, smV = sms(_split2(uu)), sms(_split2(vv))
            rhs = jnp.concatenate([smU[0], smV[0], smU[1], smV[1], smU[0], smV[0]], axis=0)
            outs[d][0][rw, sl] = xr[L:] + _mm(_L(_split2(mr_scr[u, p])), rhs)
        return carry

    lax.fori_loop(0, G, serial, 0)

    @pl.when(g == ng - 1)
    def _():
        for d in range(2):
            for p in range(N_PAIRS):
                sp = s_scr[d, p]
                sfin_ref[0, d, 2 * p] = sp[:, :HEAD]
                sfin_ref[0, d, 2 * p + 1] = sp[:, HEAD:]


def _rwkv(proj, s0_pairs, wdec, bdec, wic, bic, kkw, ka, rk, hs, nseq, T):
    ntok = proj.shape[0]
    nc = T // CHUNK
    G = min(nc, 8)
    ng = nc // G
    GL = G * CHUNK

    def fwd(b, g):
        return b * ng + g

    def bwd(b, g):
        return b * ng + ng - 1 - g

    def pspecs(rowblk):
        return [pl.BlockSpec((GL, D_A), lambda b, g: (rowblk(b, g), COL_R)),
                pl.BlockSpec((GL, D_A), lambda b, g: (rowblk(b, g), COL_K)),
                pl.BlockSpec((GL, D_A), lambda b, g: (rowblk(b, g), COL_V)),
                pl.BlockSpec((GL, 128), lambda b, g: (rowblk(b, g), COL_LW)),
                pl.BlockSpec((GL, 128), lambda b, g: (rowblk(b, g), COL_LA))]

    def wspec(shape):
        return pl.BlockSpec(shape, lambda b, g: (0,) * len(shape))

    ospec_f = pl.BlockSpec((GL, D_A), lambda b, g: (fwd(b, g), 0))
    ospec_b = pl.BlockSpec((GL, D_A), lambda b, g: (bwd(b, g), 0))
    tok = jax.ShapeDtypeStruct((ntok, D_A), F32)
    return pl.pallas_call(
        functools.partial(_rwkv_kernel, G=G, ng=ng),
        grid=(nseq, ng),
        in_specs=pspecs(fwd) + pspecs(bwd) + [
            pl.BlockSpec((1, 2, N_PAIRS, HEAD, 128), lambda b, g: (b, 0, 0, 0, 0)),
            wspec((128, D_A)), wspec((2, D_A)), wspec((128, D_A)), wspec((2, D_A)),
            wspec((1, D_A)), wspec((1, D_A)), wspec((1, D_A)), wspec((D_A, D_A))],
        out_specs=[ospec_f, ospec_b, ospec_f, ospec_b,
                   pl.BlockSpec((1, 2, N_HEADS, HEAD, HEAD), lambda b, g: (b, 0, 0, 0, 0))],
        out_shape=[tok, tok, tok, tok, jax.ShapeDtypeStruct((nseq, 2, N_HEADS, HEAD, HEAD), F32)],
        scratch_shapes=[pltpu.VMEM((2, N_PAIRS, HEAD, 128), F32),
                        pltpu.VMEM((2 * G, N_PAIRS, CHUNK, 128), F32),
                        pltpu.VMEM((2 * G, N_PAIRS, CHUNK, 128), F32),
                        pltpu.VMEM((2 * G, N_PAIRS, CHUNK, 256), F32),
                        pltpu.VMEM((2 * G, N_PAIRS, 2 * CHUNK, 128), F32),
                        pltpu.VMEM((2 * G, N_PAIRS, 2 * CHUNK, 128), F32),
                        pltpu.VMEM((2 * G, 1, D_A), F32)],
        compiler_params=_cparams(("arbitrary", "arbitrary")),
        name="rwkv_scan",
    )(*([proj] * 10), s0_pairs, wdec, bdec, wic, bic, kkw, ka, rk, hs)


def _gelu_tanh(x):
    return 0.5 * x * (1.0 + jnp.tanh(math.sqrt(2.0 / math.pi) * (x + 0.044715 * (x * x * x))))


def _rglru_kernel(xr_ref, xg_ref, h0_ref, cw_ref, cb_ref, wa_ref, ba_ref, wi_ref, bi_ref, lam_ref,
                  y_ref, hfin_ref, a_scr, u_scr, *, T, lrow):
    RB = min(T, 256)
    nblk = T // RB
    rowi = lax.broadcasted_iota(jnp.int32, (RB, D_B), 0)
    pos = jnp.bitwise_and(rowi, lrow - 1)
    cw = cw_ref[...]
    lam = lam_ref[...]
    sp = jnp.maximum(-lam, 0.0) + jnp.log1p(jnp.exp(-jnp.abs(lam)))

    def gates(blk, carry):
        r0 = pl.multiple_of(blk * RB, RB)
        x = xr_ref[pl.ds(r0, RB), :]
        xm1 = jnp.where(pos >= 1, pltpu.roll(x, 1, 0), 0.0)
        xm2 = jnp.where(pos >= 2, pltpu.roll(x, 2, 0), 0.0)
        xp1 = jnp.where(pos <= lrow - 2, pltpu.roll(x, RB - 1, 0), 0.0)
        xc = xm2 * cw[0:1] + xm1 * cw[1:2] + x * cw[2:3] + xp1 * cw[3:4] + cb_ref[...]
        for dd in range(2):
            gr = jax.nn.sigmoid(_dot1(xc, wa_ref[dd]) + ba_ref[dd:dd + 1, :])
            gi = jax.nn.sigmoid(_dot1(xc, wi_ref[dd]) + bi_ref[dd:dd + 1, :])
            log_a = (-RGLRU_C) * gr * sp[dd:dd + 1, :]
            a = jnp.exp(log_a)
            a_scr[dd, pl.ds(r0, RB), :] = a
            u_scr[dd, pl.ds(r0, RB), :] = jnp.sqrt((1.0 - a) * (1.0 + a)) * gi * xc
        return carry

    lax.fori_loop(0, nblk, gates, 0)

    def step(t, carry):
        hf, hb = carry
        hf = a_scr[0, pl.ds(t, 1), :] * hf + u_scr[0, pl.ds(t, 1), :]
        u_scr[0, pl.ds(t, 1), :] = hf
        tb = T - 1 - t
        hb = a_scr[1, pl.ds(tb, 1), :] * hb + u_scr[1, pl.ds(tb, 1), :]
        u_scr[1, pl.ds(tb, 1), :] = hb
        return hf, hb

    h0 = h0_ref[0]
    hf, hb = lax.fori_loop(0, T, step, (h0[0:1], h0[1:2]), unroll=8)
    hfin_ref[0] = jnp.concatenate([hf, hb], axis=0)

    def outp(blk, carry):
        r0 = pl.multiple_of(blk * RB, RB)
        h = u_scr[0, pl.ds(r0, RB), :] + u_scr[1, pl.ds(r0, RB), :]
        y_ref[pl.ds(r0, RB), :] = h * _gelu_tanh(xg_ref[pl.ds(r0, RB), :])
        return carry

    lax.fori_loop(0, nblk, outp, 0)


def _rglru(proj, h0, conv_w, conv_b, wa_bd, ba, wi_bd, bi, lam, nseq, T, lrow):
    ntok = proj.shape[0]

    def wspec(shape):
        return pl.BlockSpec(shape, lambda b: (0,) * len(shape))

    return pl.pallas_call(
        functools.partial(_rglru_kernel, T=T, lrow=lrow),
        grid=(nseq,),
        in_specs=[pl.BlockSpec((T, D_B), lambda b: (b, COL_XR)),
                  pl.BlockSpec((T, D_B), lambda b: (b, COL_XGB)),
                  pl.BlockSpec((1, 2, D_B), lambda b: (b, 0, 0)),
                  wspec((4, D_B)), wspec((1, D_B)),
                  wspec((2, D_B, D_B)), wspec((2, D_B)), wspec((2, D_B, D_B)), wspec((2, D_B)),
                  wspec((2, D_B))],
        out_specs=[pl.BlockSpec((T, D_B), lambda b: (b, 0)),
                   pl.BlockSpec((1, 2, D_B), lambda b: (b, 0, 0))],
        out_shape=[jax.ShapeDtypeStruct((ntok, D_B), F32),
                   jax.ShapeDtypeStruct((nseq, 2, D_B), F32)],
        scratch_shapes=[pltpu.VMEM((2, T, D_B), F32), pltpu.VMEM((2, T, D_B), F32)],
        compiler_params=_cparams(("arbitrary",)),
        name="rglru",
    )(proj, proj, h0, conv_w, conv_b, wa_bd, ba, wi_bd, bi, lam)


def _mixout_kernel(of_ref, ob_ref, bf_ref, bb_ref, yb_ref, lg_ref, x_ref, mod_ref, wgu_ref, avg_ref,
                   gnw_ref, gnb_ref, wout_ref, l1w_ref, l1b_ref, wr_ref, br_ref,
                   x1_ref, u2_ref, eid_ref, prob_ref):
    m = mod_ref[0]
    avg = avg_ref[...]
    wkv = of_ref[...] + ob_ref[...]
    mu = _dot_xb(wkv, avg)
    dv = wkv - mu
    var = _dot_xb(dv * dv, avg)
    gn = dv * lax.rsqrt(var + GN_EPS) * gnw_ref[...] + gnb_ref[...]
    g = _dot1(jax.nn.sigmoid(lg_ref[...]), wgu_ref[...])
    ya = (gn + (bf_ref[...] + bb_ref[...])) * g
    mix = _dot1(ya, wout_ref[0:D_A, :]) + _dot1(yb_ref[...], wout_ref[D_A:, :])
    x1 = _layer_norm(DEEPNORM_ALPHA * x_ref[...] + m[2:3] * mix) * l1w_ref[...] + l1b_ref[...]
    x1_ref[...] = x1
    u2 = _layer_norm(x1) * (1.0 + m[4:5]) + m[3:4]
    _store_tok_tiles(u2_ref, u2)
    logits = _dot3(u2, wr_ref[...]) + br_ref[...]
    tm = logits.shape[0]
    lane = lax.broadcasted_iota(jnp.int32, (tm, N_EXPERTS), 1)
    lane4 = lax.broadcasted_iota(jnp.int32, (tm, TOP_K), 1)
    work = logits
    tops, idxs = [], []
    for _ in range(TOP_K):
        mx = jnp.max(work, axis=1, keepdims=True)
        idx = jnp.min(jnp.where(work == mx, lane, N_EXPERTS), axis=1, keepdims=True)
        tops.append(mx)
        idxs.append(idx)
        work = jnp.where(lane == idx, -jnp.inf, work)
    es = [jnp.exp(t - tops[0]) for t in tops]
    den = es[0] + es[1] + es[2] + es[3]
    eid = jnp.zeros((tm, TOP_K), jnp.int32)
    prob = jnp.zeros((tm, TOP_K), F32)
    for j in range(TOP_K):
        eid = jnp.where(lane4 == j, idxs[j], eid)
        prob = jnp.where(lane4 == j, es[j] / den, prob)
    eid_ref[...] = eid
    prob_ref[...] = prob


def _mixout(o_f, o_b, bon_f, bon_b, yb, proj, x, mod_rows, tok_per_row, wgu, avg, gnw, gnb, wout, l1w, l1b,
            wr, br):
    ntok = x.shape[0]
    tm = 256
    per = tok_per_row // tm

    def wspec(shape):
        return pl.BlockSpec(shape, lambda i: (0,) * len(shape))

    return pl.pallas_call(
        _mixout_kernel,
        grid=(ntok // tm,),
        in_specs=[pl.BlockSpec((tm, D_A), lambda i: (i, 0)),
                  pl.BlockSpec((tm, D_A), lambda i: (i, 0)),
                  pl.BlockSpec((tm, D_A), lambda i: (i, 0)),
                  pl.BlockSpec((tm, D_A), lambda i: (i, 0)),
                  pl.BlockSpec((tm, D_B), lambda i: (i, 0)),
                  pl.BlockSpec((tm, LORA_G), lambda i: (i, COL_LG)),
                  pl.BlockSpec((tm, D_MODEL), lambda i: (i, 0)),
                  pl.BlockSpec((1, 6, D_MODEL), lambda i: (i // per, 0, 0)),
                  wspec((LORA_G, D_A)), wspec((D_A, D_A)), wspec((1, D_A)), wspec((1, D_A)),
                  wspec((D_MODEL, D_MODEL)), wspec((1, D_MODEL)), wspec((1, D_MODEL)),
                  wspec((D_MODEL, N_EXPERTS)), wspec((1, N_EXPERTS))],
        out_specs=[pl.BlockSpec((tm, D_MODEL), lambda i: (i, 0)),
                   pl.BlockSpec((tm * ROW_TILE, 128), lambda i: (i, 0)),
                   pl.BlockSpec((tm, TOP_K), lambda i: (i, 0)),
                   pl.BlockSpec((tm, TOP_K), lambda i: (i, 0))],
        out_shape=[jax.ShapeDtypeStruct((ntok, D_MODEL), F32),
                   jax.ShapeDtypeStruct((ntok * ROW_TILE, 128), F32),
                   jax.ShapeDtypeStruct((ntok, TOP_K), jnp.int32),
                   jax.ShapeDtypeStruct((ntok, TOP_K), F32)],
        compiler_params=_cparams(("arbitrary",)),
        name="mixout",
    )(o_f, o_b, bon_f, bon_b, yb, proj, x, mod_rows, wgu, avg, gnw, gnb, wout, l1w, l1b, wr, br)


MOE_TM = 256
ROUTE_BLK = 256
DISPATCH_TOK = 512


def _route_kernel(eid_ref, slot_ref, te_ref, gend_ref, nv_ref, c_scr, *, ntok, nt_pad):
    nblk = ntok // ROUTE_BLK
    lane32 = lax.broadcasted_iota(jnp.int32, (ROUTE_BLK, N_EXPERTS), 1)
    lane4 = lax.broadcasted_iota(jnp.int32, (ROUTE_BLK, TOP_K), 1)
    rowb = lax.broadcasted_iota(jnp.int32, (ROUTE_BLK, ROUTE_BLK), 0)
    colb = lax.broadcasted_iota(jnp.int32, (ROUTE_BLK, ROUTE_BLK), 1)
    tri = jnp.where(rowb > colb, 1.0, 0.0).astype(BF16)

    def count(b, carry):
        rows = pl.ds(pl.multiple_of(b * ROUTE_BLK, ROUTE_BLK), ROUTE_BLK)
        e = eid_ref[rows, :]
        hot = jnp.zeros((ROUTE_BLK, N_EXPERTS), F32)
        for j in range(TOP_K):
            hot = hot + jnp.where(lane32 == e[:, j:j + 1], 1.0, 0.0)
        c_scr[rows, :] = _mm(tri, hot.astype(BF16)) + carry
        return carry + jnp.sum(hot, axis=0, keepdims=True)

    n = lax.fori_loop(0, nblk, count, jnp.zeros((1, N_EXPERTS), F32))
    padded = jnp.floor((n + (MOE_TM - 1)) * (1.0 / MOE_TM)) * MOE_TM
    r32 = lax.broadcasted_iota(jnp.int32, (N_EXPERTS, N_EXPERTS), 0)
    c32 = lax.broadcasted_iota(jnp.int32, (N_EXPERTS, N_EXPERTS), 1)
    upper = jnp.where(r32 < c32, 1.0, 0.0).astype(BF16)
    off = _dot_xb(jnp.broadcast_to(padded, (8, N_EXPERTS)), upper)[0:1]
    gend = off + padded
    gend_ref[...] = gend.astype(jnp.int32)

    def place(b, carry):
        rows = pl.ds(pl.multiple_of(b * ROUTE_BLK, ROUTE_BLK), ROUTE_BLK)
        e = eid_ref[rows, :]
        base = off + c_scr[rows, :]
        s = jnp.zeros((ROUTE_BLK, TOP_K), F32)
        for j in range(TOP_K):
            sj = jnp.sum(jnp.where(lane32 == e[:, j:j + 1], base, 0.0), axis=1, keepdims=True)
            s = jnp.where(lane4 == j, sj, s)
        slot_ref[rows, :] = s.astype(jnp.int32)
        return carry

    lax.fori_loop(0, nblk, place, 0)

    gcol = jnp.sum(jnp.where(r32 == c32, jnp.broadcast_to(gend, (N_EXPERTS, N_EXPERTS)), 0.0),
                   axis=1, keepdims=True)
    tstart = (lax.broadcasted_iota(jnp.int32, (N_EXPERTS, nt_pad), 1) * MOE_TM).astype(F32)
    te = jnp.sum(jnp.where(gcol <= tstart, 1.0, 0.0), axis=0, keepdims=True)
    te_ref[...] = jnp.minimum(te, N_EXPERTS - 1.0).astype(jnp.int32)
    total = jnp.sum(padded, axis=1, keepdims=True)
    nv_ref[...] = jnp.broadcast_to(total * (1.0 / MOE_TM), (1, 128)).astype(jnp.int32)


def _route(eid):
    ntok = eid.shape[0]
    nt_max = ntok * TOP_K // MOE_TM + N_EXPERTS
    nt_pad = -(-nt_max // 128) * 128
    full = lambda shape: pl.BlockSpec(shape, lambda i: (0,) * len(shape))
    slot, te, gend, nv = pl.pallas_call(
        functools.partial(_route_kernel, ntok=ntok, nt_pad=nt_pad),
        grid=(1,),
        in_specs=[full((ntok, TOP_K))],
        out_specs=[full((ntok, TOP_K)), full((1, nt_pad)), full((1, N_EXPERTS)), full((1, 128))],
        out_shape=[jax.ShapeDtypeStruct((ntok, TOP_K), jnp.int32),
                   jax.ShapeDtypeStruct((1, nt_pad), jnp.int32),
                   jax.ShapeDtypeStruct((1, N_EXPERTS), jnp.int32),
                   jax.ShapeDtypeStruct((1, 128), jnp.int32)],
        scratch_shapes=[pltpu.VMEM((ntok, N_EXPERTS), F32)],
        compiler_params=_cparams(("arbitrary",)),
        name="moe_route",
    )(eid)
    return slot.reshape(ntok * TOP_K), te.reshape(nt_pad), gend.reshape(N_EXPERTS), nv.reshape(128), nt_max


def _dispatch_kernel(gend_ref, up_ref, us_ref, slot_ref, xs_ref, z_scr, slot_smem, sem_z, sem_s, sem_c,
                     *, n_p, n_s):
    z_scr[...] = jnp.zeros_like(z_scr)

    def zero_copy(e):
        return pltpu.make_async_copy(z_scr, xs_ref.at[_tok_rows(gend_ref[e] - MOE_TM, MOE_TM)], sem_z)

    def nonempty(e):
        prev = jnp.where(e > 0, gend_ref[jnp.maximum(e - 1, 0)], 0)
        return gend_ref[e] > prev

    def zstart(e, c):
        @pl.when(nonempty(e))
        def _():
            zero_copy(e).start()
        return c

    def zwait(e, c):
        @pl.when(nonempty(e))
        def _():
            zero_copy(e).wait()
        return c

    lax.fori_loop(0, N_EXPERTS, zstart, 0)
    lax.fori_loop(0, N_EXPERTS, zwait, 0)

    def row_copy(src_ref, tok, s):
        return pltpu.make_async_copy(src_ref.at[_tok_rows(tok)], xs_ref.at[_tok_rows(s)], sem_c)

    for src_ref, ntok, base in ((up_ref, n_p, 0), (us_ref, n_s, n_p)):
        def tile(i, c, src_ref=src_ref, base=base):
            tok0 = i * DISPATCH_TOK
            g0 = pl.multiple_of((base + tok0) * TOP_K, DISPATCH_TOK * TOP_K)
            cp = pltpu.make_async_copy(slot_ref.at[pl.ds(g0, DISPATCH_TOK * TOP_K)], slot_smem, sem_s)
            cp.start()
            cp.wait()

            def issue(t, c2):
                for j in range(TOP_K):
                    row_copy(src_ref, tok0 + t, slot_smem[t * TOP_K + j]).start()
                return c2

            def drain(t, c2):
                for j in range(TOP_K):
                    row_copy(src_ref, tok0 + t, slot_smem[t * TOP_K + j]).wait()
                return c2

            lax.fori_loop(0, DISPATCH_TOK, issue, 0)
            lax.fori_loop(0, DISPATCH_TOK, drain, 0)
            return c

        lax.fori_loop(0, ntok // DISPATCH_TOK, tile, 0)


def _dispatch(gend, u2_p, u2_s, slot, ns_rows):
    n_p, n_s = u2_p.shape[0] // ROW_TILE, u2_s.shape[0] // ROW_TILE
    anyspec = pl.BlockSpec(memory_space=pl.ANY)
    return pl.pallas_call(
        functools.partial(_dispatch_kernel, n_p=n_p, n_s=n_s),
        grid_spec=pltpu.PrefetchScalarGridSpec(
            num_scalar_prefetch=1, grid=(1,),
            in_specs=[anyspec, anyspec, anyspec], out_specs=anyspec,
            scratch_shapes=[pltpu.VMEM((MOE_TM * ROW_TILE, 128), F32),
                            pltpu.SMEM((DISPATCH_TOK * TOP_K,), jnp.int32),
                            pltpu.SemaphoreType.DMA(()), pltpu.SemaphoreType.DMA(()),
                            pltpu.SemaphoreType.DMA(())]),
        out_shape=jax.ShapeDtypeStruct((ns_rows * ROW_TILE, 128), F32),
        compiler_params=_cparams(("arbitrary",)),
        name="moe_dispatch",
    )(gend, u2_p, u2_s, slot)


def _expert_kernel(te_ref, nv_ref, xs_ref, wg_ref, bg_ref, wu_ref, bu_ref, wd_ref, bd_ref, ys_ref,
                   wg_b, wu_b, wd_b):
    i = pl.program_id(0)
    valid = i < nv_ref[0]
    fresh = jnp.logical_or(i == 0, te_ref[i] != te_ref[jnp.maximum(i - 1, 0)])

    @pl.when(jnp.logical_and(valid, fresh))
    def _():
        wg_b[...] = wg_ref[0].astype(BF16)
        wu_b[...] = wu_ref[0].astype(BF16)
        wd_b[...] = wd_ref[0].astype(BF16)

    @pl.when(valid)
    def _():
        x = _load_tok_tiles(xs_ref, MOE_TM).astype(BF16)
        FC = 256
        y = jnp.zeros((MOE_TM, D_MODEL), F32)
        for j in range(D_FF // FC):
            cs = slice(j * FC, (j + 1) * FC)
            gate = jnp.minimum(_mm(x, wg_b[:, cs]) + bg_ref[0, :, cs], SWIGLU_LIMIT)
            up = jnp.clip(_mm(x, wu_b[:, cs]) + bu_ref[0, :, cs], -SWIGLU_LIMIT, SWIGLU_LIMIT)
            h = (up + 1.0) * gate * jax.nn.sigmoid(SWIGLU_ALPHA * gate)
            y = y + _mm(h.astype(BF16), wd_b[cs, :])
        _store_tok_tiles(ys_ref, y + bd_ref[0])


def _experts(te, nv, xs, nt_max, wg, bg, wu, bu, wd, bd):
    def tile(i, te, nv):
        return jnp.minimum(i, nv[0] - 1)

    wspec = pl.BlockSpec((1, D_MODEL, D_FF), lambda i, te, nv: (te[tile(i, te, nv)], 0, 0))
    bspec = pl.BlockSpec((1, 1, D_FF), lambda i, te, nv: (te[tile(i, te, nv)], 0, 0))
    xspec = pl.BlockSpec((MOE_TM * ROW_TILE, 128), lambda i, te, nv: (tile(i, te, nv), 0))
    return pl.pallas_call(
        _expert_kernel,
        grid_spec=pltpu.PrefetchScalarGridSpec(
            num_scalar_prefetch=2, grid=(nt_max,),
            in_specs=[xspec, wspec, bspec, wspec, bspec, wspec, bspec], out_specs=xspec,
            scratch_shapes=[pltpu.VMEM((D_MODEL, D_FF), BF16)] * 3),
        out_shape=jax.ShapeDtypeStruct(xs.shape, F32),
        compiler_params=_cparams(("arbitrary",)),
        name="moe_experts",
    )(te, nv, xs, wg, bg.reshape(N_EXPERTS, 1, D_FF), wu, bu.reshape(N_EXPERTS, 1, D_FF),
      wd, bd.reshape(N_EXPERTS, 1, D_MODEL))


def _combine_kernel(slot_ref, ys_ref, prob_ref, x1_ref, mod_ref, w_ref, b_ref, o_ref,
                    rows_scr, slot_smem, sem_s, sem_g, *, tm, tok_base):
    i = pl.program_id(0)
    g0 = pl.multiple_of((tok_base + i * tm) * TOP_K, tm * TOP_K)
    cp = pltpu.make_async_copy(slot_ref.at[pl.ds(g0, tm * TOP_K)], slot_smem, sem_s)
    cp.start()
    cp.wait()

    def row_copy(t, j):
        return pltpu.make_async_copy(ys_ref.at[_tok_rows(slot_smem[t * TOP_K + j])],
                                     rows_scr.at[j, _tok_rows(t)], sem_g)

    def issue(t, c):
        for j in range(TOP_K):
            row_copy(t, j).start()
        return c

    def drain(t, c):
        for j in range(TOP_K):
            row_copy(t, j).wait()
        return c

    lax.fori_loop(0, tm, issue, 0)
    lax.fori_loop(0, tm, drain, 0)
    p = prob_ref[...]
    moe = p[:, 0:1] * _load_tok_tiles(rows_scr, tm, (0,))
    for j in range(1, TOP_K):
        moe = moe + p[:, j:j + 1] * _load_tok_tiles(rows_scr, tm, (j,))
    m = mod_ref[0]
    o_ref[...] = _layer_norm(DEEPNORM_ALPHA * x1_ref[...] + m[5:6] * moe) * w_ref[...] + b_ref[...]


def _combine(slot, ys, prob, x1, tok_base, mod_rows, tok_per_row, w, b):
    ntok = x1.shape[0]
    tm = 256
    per = tok_per_row // tm
    anyspec = pl.BlockSpec(memory_space=pl.ANY)
    return pl.pallas_call(
        functools.partial(_combine_kernel, tm=tm, tok_base=tok_base),
        grid=(ntok // tm,),
        in_specs=[anyspec, anyspec,
                  pl.BlockSpec((tm, TOP_K), lambda i: (i, 0)),
                  pl.BlockSpec((tm, D_MODEL), lambda i: (i, 0)),
                  pl.BlockSpec((1, 6, D_MODEL), lambda i: (i // per, 0, 0)),
                  pl.BlockSpec((1, D_MODEL), lambda i: (0, 0)),
                  pl.BlockSpec((1, D_MODEL), lambda i: (0, 0))],
        out_specs=pl.BlockSpec((tm, D_MODEL), lambda i: (i, 0)),
        out_shape=jax.ShapeDtypeStruct((ntok, D_MODEL), F32),
        scratch_shapes=[pltpu.VMEM((TOP_K, tm * ROW_TILE, 128), F32),
                        pltpu.SMEM((tm * TOP_K,), jnp.int32),
                        pltpu.SemaphoreType.DMA(()), pltpu.SemaphoreType.DMA(())],
        compiler_params=_cparams(("arbitrary",)),
        name="moe_combine",
    )(slot, ys, prob, x1, mod_rows, w, b)


def _block_diag(w):
    nb, bb, _ = w.shape
    eye = jnp.eye(nb, dtype=w.dtype)
    return jnp.einsum('nij,nm->nimj', w, eye).reshape(nb * bb, nb * bb)


def kernel(x_prompt, x_sample, state_rwkv, state_rglru, c, c_ctx, w_mod, b_mod, w_in, w_decay_up, b_decay, w_iclr_up, b_iclr, w_gate_up, k_k, k_a, r_k, gn_w, gn_b, conv_w, conv_b, w_rg_a, b_rg_a, w_rg_i, b_rg_i, lam, w_out, ln1_w, ln1_b, w_router, b_router, w_e_gate, b_e_gate, w_e_up, b_e_up, w_e_down, b_e_down, ln2_w, ln2_b):
    Bp, Tp, D = x_prompt.shape
    Bs, Ts, _ = x_sample.shape
    l = 0
    row = lambda a: a[l].reshape(1, -1)

    cond8 = jnp.concatenate([c_ctx[None, :], c, jnp.zeros((8 - 1 - Bs, D), F32)], axis=0)
    mod = _modulation(cond8, w_mod[l], b_mod[l]).reshape(8, 6, D)
    mod_p, mod_s = mod[0:1], mod[1:1 + Bs]

    w_in_b = w_in[l].astype(BF16)
    w_out_b = w_out[l].astype(BF16)
    head_id = jnp.arange(D_A) // HEAD
    hs = (head_id[:, None] == head_id[None, :]).astype(BF16)
    avg = (hs.astype(F32) / HEAD).astype(BF16)
    wdec = w_decay_up[l].reshape(2 * LORA, D_A)
    wic = w_iclr_up[l].reshape(2 * LORA, D_A)
    wa_bd = jnp.stack([_block_diag(w_rg_a[l, 0]), _block_diag(w_rg_a[l, 1])])
    wi_bd = jnp.stack([_block_diag(w_rg_i[l, 0]), _block_diag(w_rg_i[l, 1])])

    xp = x_prompt.reshape(Bp * Tp, D)
    xs = x_sample.reshape(Bs * Ts, D)

    s0_s = state_rwkv[:, l].reshape(Bs, 2, N_PAIRS, 2, HEAD, HEAD).transpose(0, 1, 2, 4, 3, 5)
    s0_s = s0_s.reshape(Bs, 2, N_PAIRS, HEAD, 2 * HEAD)
    s0_p = jnp.zeros((Bp, 2, N_PAIRS, HEAD, 2 * HEAD), F32)
    h0_p = jnp.zeros((Bp, 2, D_B), F32)
    h0_s = state_rglru[:, l]

    outs = []
    for x, mod_rows, tok_per_row, nseq, T, s0, h0, lrow in (
            (xp, mod_p, Bp * Tp, Bp, Tp, s0_p, h0_p, Tp),
            (xs, mod_s, Ts, Bs, Ts, s0_s, h0_s, GRID_W)):
        proj = _inproj(x, mod_rows, w_in_b, tok_per_row)
        o_f, o_b, bon_f, bon_b, s_fin = _rwkv(proj, s0, wdec, b_decay[l], wic, b_iclr[l], row(k_k), row(k_a),
                                              r_k[l].reshape(1, D_A), hs, nseq, T)
        yb, h_fin = _rglru(proj, h0, conv_w[l], row(conv_b), wa_bd, b_rg_a[l], wi_bd, b_rg_i[l], lam[l],
                           nseq, T, lrow)
        x1, u2, eid, prob = _mixout(o_f, o_b, bon_f, bon_b, yb, proj, x, mod_rows, tok_per_row, w_gate_up[l], avg,
                                    row(gn_w), row(gn_b), w_out_b, row(ln1_w), row(ln1_b), w_router[l],
                                    row(b_router))
        outs.append((x1, u2, eid, prob, s_fin, h_fin))

    (x1_p, u2_p, eid_p, prob_p, sfin_p, hfin_p), (x1_s, u2_s, eid_s, prob_s, _, _) = outs
    slot, te, gend, nv, nt_max = _route(jnp.concatenate([eid_p, eid_s], axis=0))
    xs_rows = _dispatch(gend, u2_p, u2_s, slot, nt_max * MOE_TM)
    ys = _experts(te, nv, xs_rows, nt_max, w_e_gate[l], b_e_gate[l], w_e_up[l], b_e_up[l], w_e_down[l],
                  b_e_down[l])
    y_p = _combine(slot, ys, prob_p, x1_p, 0, mod_p, Bp * Tp, row(ln2_w), row(ln2_b))
    y_s = _combine(slot, ys, prob_s, x1_s, Bp * Tp, mod_s, Ts, row(ln2_w), row(ln2_b))
    return (y_p.reshape(Bp, Tp, D), y_s.reshape(Bs, Ts, D),
            sfin_p[:, None], hfin_p[:, None])
```

```python
import functools
import math

import jax
import jax.numpy as jnp
from jax import lax
from jax.experimental import pallas as pl
from jax.experimental.pallas import tpu as pltpu

F32 = jnp.float32
BF16 = jnp.bfloat16

D_MODEL = 1024
D_A = 512
D_B = 512
HEAD = 64
N_HEADS = 8
N_PAIRS = N_HEADS // 2
GRID_W = 64
LORA = 64
LORA_G = 128
RGLRU_C = 8.0
N_EXPERTS = 32
TOP_K = 4
D_FF = 1024
SWIGLU_LIMIT = 7.0
SWIGLU_ALPHA = 1.702
LN_EPS = 1e-5
GN_EPS = 1e-5 * HEAD
D_IN = 3 * D_A + 2 * D_B + 2 * LORA + 2 * LORA + LORA_G
DEPTH = 1
DEEPNORM_ALPHA = (2 * DEPTH) ** 0.25

COL_R, COL_K, COL_V, COL_XR, COL_XGB = 0, 1, 2, 3, 4
COL_LW, COL_LA, COL_LG = 20, 21, 22

CHUNK = 64
VMEM_LIMIT = 56 * 1024 * 1024

_NN = (((1,), (0,)), ((), ()))
_NT = (((1,), (1,)), ((), ()))


def _mm(a, b, dims=_NN):
    return lax.dot_general(a, b, dims, preferred_element_type=F32)


def _split2(x):
    hi = x.astype(BF16)
    lo = (x - hi.astype(F32)).astype(BF16)
    return hi, lo


def _split3(x):
    hi = x.astype(BF16)
    r1 = x - hi.astype(F32)
    mid = r1.astype(BF16)
    lo = (r1 - mid.astype(F32)).astype(BF16)
    return hi, mid, lo


def _dot1(a, b, dims=_NN):
    return _mm(a.astype(BF16), b.astype(BF16), dims)


def _dot3(a, b):
    ah, al = _split2(a)
    bh, bl = _split2(b)
    return _mm(jnp.concatenate([ah, ah, al], axis=1), jnp.concatenate([bh, bl, bh], axis=0))


def _dot_xb(a, b_bf16):
    return _mm(jnp.concatenate(_split3(a), axis=1), jnp.concatenate([b_bf16] * 3, axis=0))


def _dot_xa(a_bf16, b):
    return _mm(jnp.concatenate([a_bf16] * 3, axis=1), jnp.concatenate(_split3(b), axis=0))


ROW_TILE = 8


def _tok_rows(t, n=1):
    return pl.ds(pl.multiple_of(t * ROW_TILE, ROW_TILE), n * ROW_TILE)


def _load_tok_tiles(ref, ntok, lead=()):
    return jnp.concatenate([ref[lead + (pl.ds(s_, ntok, stride=ROW_TILE), slice(None))]
                            for s_ in range(ROW_TILE)], axis=1)


def _store_tok_tiles(ref, x):
    ntok = x.shape[0]
    for s_ in range(ROW_TILE):
        ref[pl.ds(s_, ntok, stride=ROW_TILE), :] = x[:, 128 * s_:128 * (s_ + 1)]


def _layer_norm(x):
    mu = jnp.mean(x, axis=-1, keepdims=True)
    xc = x - mu
    var = jnp.mean(xc * xc, axis=-1, keepdims=True)
    return xc * lax.rsqrt(var + LN_EPS)


def _cparams(sem):
    return pltpu.CompilerParams(dimension_semantics=sem, vmem_limit_bytes=VMEM_LIMIT)


def _mod_kernel(c_ref, w_ref, b_ref, o_ref):
    c = c_ref[...]
    s = c * jax.nn.sigmoid(c)
    o_ref[...] = _dot3(s, w_ref[...]) + b_ref[...]


def _modulation(cond8, w_mod, b_mod):
    n = w_mod.shape[1]
    tn = 1024
    return pl.pallas_call(
        _mod_kernel,
        grid=(n // tn,),
        in_specs=[pl.BlockSpec((8, D_MODEL), lambda j: (0, 0)),
                  pl.BlockSpec((D_MODEL, tn), lambda j: (0, j)),
                  pl.BlockSpec((1, tn), lambda j: (0, j))],
        out_specs=pl.BlockSpec((8, tn), lambda j: (0, j)),
        out_shape=jax.ShapeDtypeStruct((8, n), F32),
        compiler_params=_cparams(("arbitrary",)),
        name="modulation",
    )(cond8, w_mod, b_mod.reshape(1, n))


def _inproj_kernel(x_ref, mod_ref, w_ref, o_ref):
    m = mod_ref[0]
    u = _layer_norm(x_ref[...]) * (1.0 + m[1:2]) + m[0:1]
    o_ref[...] = _dot1(u, w_ref[...])


def _inproj(x, mod_rows, w_in_bf16, tok_per_row):
    ntok = x.shape[0]
    tm = 256
    per = tok_per_row // tm
    return pl.pallas_call(
        _inproj_kernel,
        grid=(ntok // tm,),
        in_specs=[pl.BlockSpec((tm, D_MODEL), lambda i: (i, 0)),
                  pl.BlockSpec((1, 6, D_MODEL), lambda i: (i // per, 0, 0)),
                  pl.BlockSpec((D_MODEL, D_IN), lambda i: (0, 0))],
        out_specs=pl.BlockSpec((tm, D_IN), lambda i: (i, 0)),
        out_shape=jax.ShapeDtypeStruct((ntok, D_IN), F32),
        compiler_params=_cparams(("arbitrary",)),
        name="inproj",
    )(x, mod_rows, w_in_bf16)


def _L(s):
    hi, lo = s
    return jnp.concatenate([hi, hi, lo], axis=1)


def _R(s):
    hi, lo = s
    return jnp.concatenate([hi, lo, hi], axis=0)


def _Rt(s):
    hi, lo = s
    return jnp.concatenate([hi, lo, hi], axis=1)


def _rwkv_kernel(rf_ref, kf_ref, vf_ref, lwf_ref, laf_ref, rb_ref, kb_ref, vb_ref, lwb_ref, lab_ref,
                 s0_ref, wdec_ref, bdec_ref, wic_ref, bic_ref, kkw_ref, ka_ref, rk_ref, hs_ref,
                 of_ref, ob_ref, bonf_ref, bonb_ref, sfin_ref,
                 s_scr, t_scr, w3_scr, mr_scr, ar_scr, bk_scr, gt_scr, *, G, ng):
    L = CHUNK
    g = pl.program_id(1)
    ins = ((rf_ref, kf_ref, vf_ref, lwf_ref, laf_ref), (rb_ref, kb_ref, vb_ref, lwb_ref, lab_ref))
    outs = ((of_ref, bonf_ref), (ob_ref, bonb_ref))

    @pl.when(g == 0)
    def _():
        s_scr[...] = s0_ref[0]

    hs = hs_ref[...]
    lane = lax.broadcasted_iota(jnp.int32, (L, 128), 1)
    rowp = lax.broadcasted_iota(jnp.int32, (L, 128), 0)
    colp = jnp.bitwise_and(lane, 63)
    h0 = lane < 64
    m0 = jnp.where(h0, 1.0, 0.0).astype(BF16)
    m1 = jnp.where(h0, 0.0, 1.0).astype(BF16)
    eye2 = jnp.where(rowp == colp, 1.0, 0.0)
    rowL = lax.broadcasted_iota(jnp.int32, (L, L), 0)
    colL = lax.broadcasted_iota(jnp.int32, (L, L), 1)

    def same(sh):
        return lax.shift_right_logical(rowp, sh) == lax.shift_right_logical(colp, sh)

    same8, same16, same32 = same(3), same(4), same(5)

    def bmask(c):
        return jnp.where(c, 1.0, 0.0).astype(BF16)

    mk8 = bmask(same8)
    merge_masks = (bmask(same16 & (~same8)), bmask(same32 & (~same16)), bmask(~same32))

    def sms(s):
        return tuple(jnp.concatenate([x * m0, x * m1], axis=0) for x in s)

    def pm(xs, ys):
        return _mm(_L(xs), _R(sms(ys)))

    chains = [(d, p) for d in range(2) for p in range(N_PAIRS)]

    def each(fn, *lists):
        return [fn(*a) for a in zip(*lists)]

    def prep(j, carry):
        pre = []
        for d in range(2):
            r_ref, k_ref, v_ref, lw_ref, la_ref = ins[d]
            jn = j if d == 0 else G - 1 - j
            rows = pl.ds(pl.multiple_of(jn * L, L), L)
            u = d * G + j
            r = r_ref[rows, :]
            k = k_ref[rows, :]
            v = v_ref[rows, :]
            dsel = (lane >= 64) if d else h0
            tl = jnp.where(dsel, jnp.tanh(lw_ref[rows, :]), 0.0)
            dw = _dot3(tl, wdec_ref[...]) + bdec_ref[d:d + 1, :]
            logw = (-math.exp(-0.5)) * jax.nn.sigmoid(dw)
            la_m = jnp.where(dsel, la_ref[rows, :], 0.0)
            iclr = jax.nn.sigmoid(_dot3(la_m, wic_ref[...]) + bic_ref[d:d + 1, :])
            kkr = k * kkw_ref[...]
            nrm = jnp.sqrt(_dot_xb(kkr * kkr, hs))
            kk = kkr / jnp.maximum(nrm, 1e-12)
            kdir = k * (1.0 + (iclr - 1.0) * ka_ref[...])
            bv = kk * iclr
            outs[d][1][rows, :] = _dot_xb(r * kdir * rk_ref[...], hs) * v

            tri = bmask((rowL <= colL) if d else (rowL >= colL))
            cs = _dot_xa(tri, logw)
            ctot = jnp.sum(logw, axis=0, keepdims=True)
            g_inv = jnp.exp(-cs)
            g_rem = jnp.exp(ctot - cs)
            gt_scr[u] = jnp.exp(ctot)
            At = -kk * jnp.exp(cs - logw)
            Rt = r * jnp.exp(cs)
            Bt = bv * g_inv
            Kt = kdir * g_inv
            Bg = bv * g_rem
            Kg = kdir * g_rem
            for p in range(N_PAIRS):
                sl = slice(128 * p, 128 * (p + 1))
                ar = jnp.concatenate([At[:, sl], Rt[:, sl]], axis=0)
                ar_scr[u, p] = ar
                bk_scr[u, p] = jnp.concatenate([Bg[:, sl], Kg[:, sl]], axis=0)
                pre.append((ar, Bt[:, sl], Kt[:, sl], v[:, sl]))

        def stage_m(c, pr):
            d = c[0]
            ar, bt, kt, _ = pr
            strict2 = (rowp < colp) if d else (rowp > colp)
            incl2 = (rowp <= colp) if d else (rowp >= colp)
            smB, smK = sms(_split2(bt)), sms(_split2(kt))
            rhs = tuple(jnp.concatenate([b_, k_], axis=0) for b_, k_ in zip(smB, smK))
            M = _mm(_L(_split2(ar)), _Rt(rhs), _NT)
            N = jnp.where(strict2, M[:L, :128], 0.0)
            Mak = jnp.where(strict2, M[:L, 128:], 0.0)
            mr_scr[d * G + j, c[1]] = jnp.concatenate([jnp.where(incl2, M[L:, :128], 0.0),
                                                       jnp.where(incl2, M[L:, 128:], 0.0)], axis=1)
            return N, Mak

        NM = each(stage_m, chains, pre)
        Ns = [_split2(nm[0]) for nm in NM]
        N0s = [(n[0] * mk8, n[1] * mk8) for n in Ns]
        T = [eye2 + jnp.where(same8, nm[0], 0.0) for nm in NM]
        P2s = [_split2(pm(n, n)) for n in N0s]
        W2 = [pm(_split2(nm[1]), _split2(pr[3])) for nm, pr in zip(NM, pre)]
        T = [t + pm(_split2(t), p2) for t, p2 in zip(T, P2s)]
        P4s = [_split2(pm(p2, p2)) for p2 in P2s]
        T = [t + pm(_split2(t), p4) for t, p4 in zip(T, P4s)]
        for mk in merge_masks:
            Ts = [_split2(t) for t in T]
            Y = [pm(ts, (n[0] * mk, n[1] * mk)) for ts, n in zip(Ts, Ns)]
            T = [t + pm(_split2(y), ts) for t, y, ts in zip(T, Y, Ts)]
        for (d, p), t, w2 in zip(chains, T, W2):
            t_scr[d * G + j, p] = t
            w3_scr[d * G + j, p] = pm(_split2(t), _split2(w2))
        return carry

    lax.fori_loop(0, G, prep, 0)

    def serial(j, carry):
        us = [d * G + j for d, _ in chains]
        rws = [pl.ds(pl.multiple_of((j if d == 0 else G - 1 - j) * L, L), L) for d, _ in chains]
        sls = [slice(128 * p, 128 * (p + 1)) for _, p in chains]
        S2 = [s_scr[d, p] for d, p in chains]
        XR = [_mm(_L(_split2(ar_scr[u, p])), _Rt(sms(_split2(s2))), _NT)
              for (d, p), u, s2 in zip(chains, us, S2)]
        U = [pm(_split2(t_scr[u, p]), _split2(xr[:L])) + w3_scr[u, p] for (d, p), u, xr in zip(chains, us, XR)]
        V = [ins[d][2][rw, sl] for (d, p), rw, sl in zip(chains, rws, sls)]
        for (d, p), u, rw, sl, s2, xr, uu, vv in zip(chains, us, rws, sls, S2, XR, U, V):
            UVt = jnp.concatenate([uu, vv], axis=0).T
            Rm = _mm(_L(_split2(UVt)), _R(_split2(bk_scr[u, p])))
            s_scr[d, p] = gt_scr[u][:, sl] * s2 + jnp.where(h0, Rm[:HEAD], Rm[HEAD:])
        for (d, p), u, rw, sl, xr, uu, vv in zip(chains, us, rws, sls, XR, U, V):
            smU, smV = sms(_split2(uu)), sms(_split2(vv))
            rhs = jnp.concatenate([smU[0], smV[0], smU[1], smV[1], smU[0], smV[0]], axis=0)
            outs[d][0][rw, sl] = xr[L:] + _mm(_L(_split2(mr_scr[u, p])), rhs)
        return carry

    lax.fori_loop(0, G, serial, 0)

    @pl.when(g == ng - 1)
    def _():
        for d in range(2):
            for p in range(N_PAIRS):
                sp = s_scr[d, p]
                sfin_ref[0, d, 2 * p] = sp[:, :HEAD]
                sfin_ref[0, d, 2 * p + 1] = sp[:, HEAD:]


def _rwkv(proj, s0_pairs, wdec, bdec, wic, bic, kkw, ka, rk, hs, nseq, T):
    ntok = proj.shape[0]
    nc = T // CHUNK
    G = min(nc, 8)
    ng = nc // G
    GL = G * CHUNK

    def fwd(b, g):
        return b * ng + g

    def bwd(b, g):
        return b * ng + ng - 1 - g

    def pspecs(rowblk):
        return [pl.BlockSpec((GL, D_A), lambda b, g: (rowblk(b, g), COL_R)),
                pl.BlockSpec((GL, D_A), lambda b, g: (rowblk(b, g), COL_K)),
                pl.BlockSpec((GL, D_A), lambda b, g: (rowblk(b, g), COL_V)),
                pl.BlockSpec((GL, 128), lambda b, g: (rowblk(b, g), COL_LW)),
                pl.BlockSpec((GL, 128), lambda b, g: (rowblk(b, g), COL_LA))]

    def wspec(shape):
        return pl.BlockSpec(shape, lambda b, g: (0,) * len(shape))

    ospec_f = pl.BlockSpec((GL, D_A), lambda b, g: (fwd(b, g), 0))
    ospec_b = pl.BlockSpec((GL, D_A), lambda b, g: (bwd(b, g), 0))
    tok = jax.ShapeDtypeStruct((ntok, D_A), F32)
    return pl.pallas_call(
        functools.partial(_rwkv_kernel, G=G, ng=ng),
        grid=(nseq, ng),
        in_specs=pspecs(fwd) + pspecs(bwd) + [
            pl.BlockSpec((1, 2, N_PAIRS, HEAD, 128), lambda b, g: (b, 0, 0, 0, 0)),
            wspec((128, D_A)), wspec((2, D_A)), wspec((128, D_A)), wspec((2, D_A)),
            wspec((1, D_A)), wspec((1, D_A)), wspec((1, D_A)), wspec((D_A, D_A))],
        out_specs=[ospec_f, ospec_b, ospec_f, ospec_b,
                   pl.BlockSpec((1, 2, N_HEADS, HEAD, HEAD), lambda b, g: (b, 0, 0, 0, 0))],
        out_shape=[tok, tok, tok, tok, jax.ShapeDtypeStruct((nseq, 2, N_HEADS, HEAD, HEAD), F32)],
        scratch_shapes=[pltpu.VMEM((2, N_PAIRS, HEAD, 128), F32),
                        pltpu.VMEM((2 * G, N_PAIRS, CHUNK, 128), F32),
                        pltpu.VMEM((2 * G, N_PAIRS, CHUNK, 128), F32),
                        pltpu.VMEM((2 * G, N_PAIRS, CHUNK, 256), F32),
                        pltpu.VMEM((2 * G, N_PAIRS, 2 * CHUNK, 128), F32),
                        pltpu.VMEM((2 * G, N_PAIRS, 2 * CHUNK, 128), F32),
                        pltpu.VMEM((2 * G, 1, D_A), F32)],
        compiler_params=_cparams(("arbitrary", "arbitrary")),
        name="rwkv_scan",
    )(*([proj] * 10), s0_pairs, wdec, bdec, wic, bic, kkw, ka, rk, hs)


def _gelu_tanh(x):
    return 0.5 * x * (1.0 + jnp.tanh(math.sqrt(2.0 / math.pi) * (x + 0.044715 * (x * x * x))))


def _rglru_kernel(xr_ref, xg_ref, h0_ref, cw_ref, cb_ref, wa_ref, ba_ref, wi_ref, bi_ref, lam_ref,
                  y_ref, hfin_ref, a_scr, u_scr, *, T, lrow):
    RB = min(T, 256)
    nblk = T // RB
    rowi = lax.broadcasted_iota(jnp.int32, (RB, D_B), 0)
    pos = jnp.bitwise_and(rowi, lrow - 1)
    cw = cw_ref[...]
    lam = lam_ref[...]
    sp = jnp.maximum(-lam, 0.0) + jnp.log1p(jnp.exp(-jnp.abs(lam)))

    def gates(blk, carry):
        r0 = pl.multiple_of(blk * RB, RB)
        x = xr_ref[pl.ds(r0, RB), :]
        xm1 = jnp.where(pos >= 1, pltpu.roll(x, 1, 0), 0.0)
        xm2 = jnp.where(pos >= 2, pltpu.roll(x, 2, 0), 0.0)
        xp1 = jnp.where(pos <= lrow - 2, pltpu.roll(x, RB - 1, 0), 0.0)
        xc = xm2 * cw[0:1] + xm1 * cw[1:2] + x * cw[2:3] + xp1 * cw[3:4] + cb_ref[...]
        for dd in range(2):
            gr = jax.nn.sigmoid(_dot1(xc, wa_ref[dd]) + ba_ref[dd:dd + 1, :])
            gi = jax.nn.sigmoid(_dot1(xc, wi_ref[dd]) + bi_ref[dd:dd + 1, :])
            log_a = (-RGLRU_C) * gr * sp[dd:dd + 1, :]
            a = jnp.exp(log_a)
            a_scr[dd, pl.ds(r0, RB), :] = a
            u_scr[dd, pl.ds(r0, RB), :] = jnp.sqrt((1.0 - a) * (1.0 + a)) * gi * xc
        return carry

    lax.fori_loop(0, nblk, gates, 0)

    def step(t, carry):
        hf, hb = carry
        hf = a_scr[0, pl.ds(t, 1), :] * hf + u_scr[0, pl.ds(t, 1), :]
        u_scr[0, pl.ds(t, 1), :] = hf
        tb = T - 1 - t
        hb = a_scr[1, pl.ds(tb, 1), :] * hb + u_scr[1, pl.ds(tb, 1), :]
        u_scr[1, pl.ds(tb, 1), :] = hb
        return hf, hb

    h0 = h0_ref[0]
    hf, hb = lax.fori_loop(0, T, step, (h0[0:1], h0[1:2]), unroll=8)
    hfin_ref[0] = jnp.concatenate([hf, hb], axis=0)

    def outp(blk, carry):
        r0 = pl.multiple_of(blk * RB, RB)
        h = u_scr[0, pl.ds(r0, RB), :] + u_scr[1, pl.ds(r0, RB), :]
        y_ref[pl.ds(r0, RB), :] = h * _gelu_tanh(xg_ref[pl.ds(r0, RB), :])
        return carry

    lax.fori_loop(0, nblk, outp, 0)


def _rglru(proj, h0, conv_w, conv_b, wa_bd, ba, wi_bd, bi, lam, nseq, T, lrow):
    ntok = proj.shape[0]

    def wspec(shape):
        return pl.BlockSpec(shape, lambda b: (0,) * len(shape))

    return pl.pallas_call(
        functools.partial(_rglru_kernel, T=T, lrow=lrow),
        grid=(nseq,),
        in_specs=[pl.BlockSpec((T, D_B), lambda b: (b, COL_XR)),
                  pl.BlockSpec((T, D_B), lambda b: (b, COL_XGB)),
                  pl.BlockSpec((1, 2, D_B), lambda b: (b, 0, 0)),
                  wspec((4, D_B)), wspec((1, D_B)),
                  wspec((2, D_B, D_B)), wspec((2, D_B)), wspec((2, D_B, D_B)), wspec((2, D_B)),
                  wspec((2, D_B))],
        out_specs=[pl.BlockSpec((T, D_B), lambda b: (b, 0)),
                   pl.BlockSpec((1, 2, D_B), lambda b: (b, 0, 0))],
        out_shape=[jax.ShapeDtypeStruct((ntok, D_B), F32),
                   jax.ShapeDtypeStruct((nseq, 2, D_B), F32)],
        scratch_shapes=[pltpu.VMEM((2, T, D_B), F32), pltpu.VMEM((2, T, D_B), F32)],
        compiler_params=_cparams(("arbitrary",)),
        name="rglru",
    )(proj, proj, h0, conv_w, conv_b, wa_bd, ba, wi_bd, bi, lam)


def _mixout_kernel(of_ref, ob_ref, bf_ref, bb_ref, yb_ref, lg_ref, x_ref, mod_ref, wgu_ref, avg_ref,
                   gnw_ref, gnb_ref, wout_ref, l1w_ref, l1b_ref, wr_ref, br_ref,
                   x1_ref, u2_ref, eid_ref, prob_ref):
    m = mod_ref[0]
    avg = avg_ref[...]
    wkv = of_ref[...] + ob_ref[...]
    mu = _dot_xb(wkv, avg)
    dv = wkv - mu
    var = _dot_xb(dv * dv, avg)
    gn = dv * lax.rsqrt(var + GN_EPS) * gnw_ref[...] + gnb_ref[...]
    g = _dot1(jax.nn.sigmoid(lg_ref[...]), wgu_ref[...])
    ya = (gn + (bf_ref[...] + bb_ref[...])) * g
    mix = _dot1(ya, wout_ref[0:D_A, :]) + _dot1(yb_ref[...], wout_ref[D_A:, :])
    x1 = _layer_norm(DEEPNORM_ALPHA * x_ref[...] + m[2:3] * mix) * l1w_ref[...] + l1b_ref[...]
    x1_ref[...] = x1
    u2 = _layer_norm(x1) * (1.0 + m[4:5]) + m[3:4]
    _store_tok_tiles(u2_ref, u2)
    logits = _dot3(u2, wr_ref[...]) + br_ref[...]
    tm = logits.shape[0]
    lane = lax.broadcasted_iota(jnp.int32, (tm, N_EXPERTS), 1)
    lane4 = lax.broadcasted_iota(jnp.int32, (tm, TOP_K), 1)
    work = logits
    tops, idxs = [], []
    for _ in range(TOP_K):
        mx = jnp.max(work, axis=1, keepdims=True)
        idx = jnp.min(jnp.where(work == mx, lane, N_EXPERTS), axis=1, keepdims=True)
        tops.append(mx)
        idxs.append(idx)
        work = jnp.where(lane == idx, -jnp.inf, work)
    es = [jnp.exp(t - tops[0]) for t in tops]
    den = es[0] + es[1] + es[2] + es[3]
    eid = jnp.zeros((tm, TOP_K), jnp.int32)
    prob = jnp.zeros((tm, TOP_K), F32)
    for j in range(TOP_K):
        eid = jnp.where(lane4 == j, idxs[j], eid)
        prob = jnp.where(lane4 == j, es[j] / den, prob)
    eid_ref[...] = eid
    prob_ref[...] = prob


def _mixout_alias_kernel(*refs):
    _mixout_kernel(*refs[:17], *refs[18:])


def _mixout_first_kernel(*refs, ntiles):
    i = pl.program_id(0)

    @pl.when(i < ntiles)
    def _():
        _mixout_kernel(*refs)

    @pl.when(i >= ntiles)
    def _():
        refs[18][...] = jnp.zeros_like(refs[18])


def _mixout(o_f, o_b, bon_f, bon_b, yb, proj, x, mod_rows, tok_per_row, wgu, avg, gnw, gnb, wout, l1w, l1b,
            wr, br, u2_all, tok_base, ntok_all):
    ntok = x.shape[0]
    tm = 256
    per = tok_per_row // tm
    base_tiles = tok_base // tm
    alias = u2_all is not None
    ntiles = ntok // tm

    def wspec(shape):
        return pl.BlockSpec(shape, lambda i: (0,) * len(shape))

    def tl(i):
        return jnp.minimum(i, ntiles - 1)

    return pl.pallas_call(
        _mixout_alias_kernel if alias else functools.partial(_mixout_first_kernel, ntiles=ntiles),
        grid=(ntiles if alias else ntok_all // tm,),
        input_output_aliases={17: 1} if alias else {},
        in_specs=[pl.BlockSpec((tm, D_A), lambda i: (tl(i), 0)),
                  pl.BlockSpec((tm, D_A), lambda i: (tl(i), 0)),
                  pl.BlockSpec((tm, D_A), lambda i: (tl(i), 0)),
                  pl.BlockSpec((tm, D_A), lambda i: (tl(i), 0)),
                  pl.BlockSpec((tm, D_B), lambda i: (tl(i), 0)),
                  pl.BlockSpec((tm, LORA_G), lambda i: (tl(i), COL_LG)),
                  pl.BlockSpec((tm, D_MODEL), lambda i: (tl(i), 0)),
                  pl.BlockSpec((1, 6, D_MODEL), lambda i: (tl(i) // per, 0, 0)),
                  wspec((LORA_G, D_A)), wspec((D_A, D_A)), wspec((1, D_A)), wspec((1, D_A)),
                  wspec((D_MODEL, D_MODEL)), wspec((1, D_MODEL)), wspec((1, D_MODEL)),
                  wspec((D_MODEL, N_EXPERTS)), wspec((1, N_EXPERTS))]
                 + ([pl.BlockSpec(memory_space=pl.ANY)] if alias else []),
        out_specs=[pl.BlockSpec((tm, D_MODEL), lambda i: (tl(i), 0)),
                   pl.BlockSpec((tm * ROW_TILE, 128), lambda i: (i + base_tiles, 0)),
                   pl.BlockSpec((tm, TOP_K), lambda i: (tl(i), 0)),
                   pl.BlockSpec((tm, TOP_K), lambda i: (tl(i), 0))],
        out_shape=[jax.ShapeDtypeStruct((ntok, D_MODEL), F32),
                   jax.ShapeDtypeStruct((ntok_all * ROW_TILE, 128), F32),
                   jax.ShapeDtypeStruct((ntok, TOP_K), jnp.int32),
                   jax.ShapeDtypeStruct((ntok, TOP_K), F32)],
        compiler_params=_cparams(("arbitrary",)),
        name="mixout",
    )(o_f, o_b, bon_f, bon_b, yb, proj, x, mod_rows, wgu, avg, gnw, gnb, wout, l1w, l1b, wr, br,
      *([u2_all] if alias else []))


MOE_TM = 256
ROUTE_BLK = 256
INVERT_CHUNK = 2048


def _route_kernel(eid_ref, slot_ref, te_ref, nv_ref, c_scr, *, ntok, nt_pad):
    nblk = ntok // ROUTE_BLK
    lane32 = lax.broadcasted_iota(jnp.int32, (ROUTE_BLK, N_EXPERTS), 1)
    lane4 = lax.broadcasted_iota(jnp.int32, (ROUTE_BLK, TOP_K), 1)
    rowb = lax.broadcasted_iota(jnp.int32, (ROUTE_BLK, ROUTE_BLK), 0)
    colb = lax.broadcasted_iota(jnp.int32, (ROUTE_BLK, ROUTE_BLK), 1)
    tri = jnp.where(rowb > colb, 1.0, 0.0).astype(BF16)

    def count(b, carry):
        rows = pl.ds(pl.multiple_of(b * ROUTE_BLK, ROUTE_BLK), ROUTE_BLK)
        e = eid_ref[rows, :]
        hot = jnp.zeros((ROUTE_BLK, N_EXPERTS), F32)
        for j in range(TOP_K):
            hot = hot + jnp.where(lane32 == e[:, j:j + 1], 1.0, 0.0)
        c_scr[rows, :] = _mm(tri, hot.astype(BF16)) + carry
        return carry + jnp.sum(hot, axis=0, keepdims=True)

    n = lax.fori_loop(0, nblk, count, jnp.zeros((1, N_EXPERTS), F32))
    padded = jnp.floor((n + (MOE_TM - 1)) * (1.0 / MOE_TM)) * MOE_TM
    r32 = lax.broadcasted_iota(jnp.int32, (N_EXPERTS, N_EXPERTS), 0)
    c32 = lax.broadcasted_iota(jnp.int32, (N_EXPERTS, N_EXPERTS), 1)
    upper = jnp.where(r32 < c32, 1.0, 0.0).astype(BF16)
    off = _dot_xb(jnp.broadcast_to(padded, (8, N_EXPERTS)), upper)[0:1]
    gend = off + padded

    def place(b, carry):
        rows = pl.ds(pl.multiple_of(b * ROUTE_BLK, ROUTE_BLK), ROUTE_BLK)
        e = eid_ref[rows, :]
        base = off + c_scr[rows, :]
        s = jnp.zeros((ROUTE_BLK, TOP_K), F32)
        for j in range(TOP_K):
            sj = jnp.sum(jnp.where(lane32 == e[:, j:j + 1], base, 0.0), axis=1, keepdims=True)
            s = jnp.where(lane4 == j, sj, s)
        slot_ref[rows, :] = s.astype(jnp.int32)
        return carry

    lax.fori_loop(0, nblk, place, 0)

    gcol = jnp.sum(jnp.where(r32 == c32, jnp.broadcast_to(gend, (N_EXPERTS, N_EXPERTS)), 0.0),
                   axis=1, keepdims=True)
    tstart = (lax.broadcasted_iota(jnp.int32, (N_EXPERTS, nt_pad), 1) * MOE_TM).astype(F32)
    te = jnp.sum(jnp.where(gcol <= tstart, 1.0, 0.0), axis=0, keepdims=True)
    te_ref[...] = jnp.minimum(te, N_EXPERTS - 1.0).astype(jnp.int32)
    total = jnp.sum(padded, axis=1, keepdims=True)
    nv_ref[...] = jnp.broadcast_to(total * (1.0 / MOE_TM), (1, 128)).astype(jnp.int32)


def _route(eid):
    ntok = eid.shape[0]
    nt_max = ntok * TOP_K // MOE_TM + N_EXPERTS
    nt_pad = -(-(nt_max + 1) // 128) * 128
    full = lambda shape: pl.BlockSpec(shape, lambda i: (0,) * len(shape))
    slot, te, nv = pl.pallas_call(
        functools.partial(_route_kernel, ntok=ntok, nt_pad=nt_pad),
        grid=(1,),
        in_specs=[full((ntok, TOP_K))],
        out_specs=[full((ntok, TOP_K)), full((1, nt_pad)), full((1, 128))],
        out_shape=[jax.ShapeDtypeStruct((ntok, TOP_K), jnp.int32),
                   jax.ShapeDtypeStruct((1, nt_pad), jnp.int32),
                   jax.ShapeDtypeStruct((1, 128), jnp.int32)],
        scratch_shapes=[pltpu.VMEM((ntok, N_EXPERTS), F32)],
        compiler_params=_cparams(("arbitrary",)),
        name="moe_route",
    )(eid)
    return slot.reshape(ntok * TOP_K), te.reshape(nt_pad), nv.reshape(128), nt_max


def _invert_kernel(slot_ref, inv_ref, slot_smem, inv_smem, sem_in, sem_out, *, nassign, nslots):
    def fill(s_, c):
        inv_smem[s_] = -1
        return c

    lax.fori_loop(0, nslots, fill, 0, unroll=8)

    def chunk(ci, c):
        a0 = pl.multiple_of(ci * INVERT_CHUNK, INVERT_CHUNK)
        cp = pltpu.make_async_copy(slot_ref.at[pl.ds(a0, INVERT_CHUNK)], slot_smem, sem_in)
        cp.start()
        cp.wait()

        def put(a, c2):
            inv_smem[slot_smem[a]] = a0 + a
            return c2

        lax.fori_loop(0, INVERT_CHUNK, put, 0, unroll=8)
        return c

    lax.fori_loop(0, nassign // INVERT_CHUNK, chunk, 0)
    out = pltpu.make_async_copy(inv_smem, inv_ref, sem_out)
    out.start()
    out.wait()


def _invert(slot, nslots):
    nassign = slot.shape[0]
    anyspec = pl.BlockSpec(memory_space=pl.ANY)
    return pl.pallas_call(
        functools.partial(_invert_kernel, nassign=nassign, nslots=nslots),
        grid=(1,),
        in_specs=[anyspec], out_specs=anyspec,
        out_shape=jax.ShapeDtypeStruct((nslots,), jnp.int32),
        scratch_shapes=[pltpu.SMEM((INVERT_CHUNK,), jnp.int32), pltpu.SMEM((nslots,), jnp.int32),
                        pltpu.SemaphoreType.DMA(()), pltpu.SemaphoreType.DMA(())],
        compiler_params=_cparams(("arbitrary",)),
        name="moe_invert",
    )(slot)


def _expert_kernel(te_ref, nv_ref, inv_ref, u2_ref, wg_ref, bg_ref, wu_ref, bu_ref, wd_ref, bd_ref, y4_ref,
                   xbuf, ybuf, wg_b, wu_b, wd_b, sem_g, sem_s, *, ntok):
    i = pl.program_id(0)
    nvalid = nv_ref[0]
    valid = i < nvalid
    trash = TOP_K * ntok

    def gather_copy(tile, r, buf):
        code = inv_ref[tile * MOE_TM + r]
        t = lax.shift_right_logical(jnp.maximum(code, 0), 2)
        return pltpu.make_async_copy(u2_ref.at[_tok_rows(t)], xbuf.at[buf, _tok_rows(r)], sem_g.at[buf])

    def scatter_copy(tile, r):
        code = inv_ref[tile * MOE_TM + r]
        d = jnp.where(code < 0, trash + r, jnp.bitwise_and(code, TOP_K - 1) * ntok
                      + lax.shift_right_logical(jnp.maximum(code, 0), 2))
        return pltpu.make_async_copy(ybuf.at[_tok_rows(r)], y4_ref.at[_tok_rows(d)], sem_s)

    def wait_gather(buf):
        pltpu.make_async_copy(xbuf.at[buf], xbuf.at[buf], sem_g.at[buf]).wait()

    def wait_scatter():
        pltpu.make_async_copy(ybuf, ybuf, sem_s).wait()

    @pl.when(i == 0)
    def _():
        def first(r, c):
            gather_copy(0, r, 0).start()
            return c

        lax.fori_loop(0, MOE_TM, first, 0)
        ybuf[...] = jnp.zeros_like(ybuf)
        spare = pltpu.make_async_copy(ybuf, y4_ref.at[_tok_rows(trash, MOE_TM)], sem_s)
        spare.start()
        spare.wait()

    fresh = jnp.logical_or(i == 0, te_ref[i] != te_ref[jnp.maximum(i - 1, 0)])

    @pl.when(jnp.logical_and(valid, fresh))
    def _():
        wg_b[...] = wg_ref[0].astype(BF16)
        wu_b[...] = wu_ref[0].astype(BF16)
        wd_b[...] = wd_ref[0].astype(BF16)

    @pl.when(valid)
    def _():
        buf = jnp.bitwise_and(i, 1)
        nxt = jnp.minimum(i + 1, nvalid - 1)
        wait_gather(buf)
        x = _load_tok_tiles(xbuf, MOE_TM, (buf,)).astype(BF16)
        FC = 256
        nfc = D_FF // FC
        per = MOE_TM // nfc
        y = jnp.zeros((MOE_TM, D_MODEL), F32)
        for j in range(nfc):
            for r in range(j * per, (j + 1) * per):
                gather_copy(nxt, r, 1 - buf).start()
            cs = slice(j * FC, (j + 1) * FC)
            gate = jnp.minimum(_mm(x, wg_b[:, cs]) + bg_ref[0, :, cs], SWIGLU_LIMIT)
            up = jnp.clip(_mm(x, wu_b[:, cs]) + bu_ref[0, :, cs], -SWIGLU_LIMIT, SWIGLU_LIMIT)
            h = (up + 1.0) * gate * jax.nn.sigmoid(SWIGLU_ALPHA * gate)
            y = y + _mm(h.astype(BF16), wd_b[cs, :])

        @pl.when(i > 0)
        def _():
            wait_scatter()

        _store_tok_tiles(ybuf, y + bd_ref[0])
        for r in range(MOE_TM):
            scatter_copy(i, r).start()

    @pl.when(i == nvalid)
    def _():
        wait_gather(jnp.bitwise_and(nvalid, 1))
        wait_scatter()


def _experts(te, nv, inv, u2_all, nt_max, wg, bg, wu, bu, wd, bd):
    ntok = u2_all.shape[0] // ROW_TILE

    def tile(i, te, nv, inv):
        return te[jnp.minimum(i, nv[0] - 1)]

    wspec = pl.BlockSpec((1, D_MODEL, D_FF), lambda i, te, nv, inv: (tile(i, te, nv, inv), 0, 0))
    bspec = pl.BlockSpec((1, 1, D_FF), lambda i, te, nv, inv: (tile(i, te, nv, inv), 0, 0))
    anyspec = pl.BlockSpec(memory_space=pl.ANY)
    return pl.pallas_call(
        functools.partial(_expert_kernel, ntok=ntok),
        grid_spec=pltpu.PrefetchScalarGridSpec(
            num_scalar_prefetch=3, grid=(nt_max + 1,),
            in_specs=[anyspec, wspec, bspec, wspec, bspec, wspec, bspec], out_specs=anyspec,
            scratch_shapes=[pltpu.VMEM((2, MOE_TM * ROW_TILE, 128), F32),
                            pltpu.VMEM((MOE_TM * ROW_TILE, 128), F32),
                            pltpu.VMEM((D_MODEL, D_FF), BF16), pltpu.VMEM((D_MODEL, D_FF), BF16),
                            pltpu.VMEM((D_MODEL, D_FF), BF16),
                            pltpu.SemaphoreType.DMA((2,)), pltpu.SemaphoreType.DMA(())]),
        out_shape=jax.ShapeDtypeStruct(((TOP_K * ntok + MOE_TM) * ROW_TILE, 128), F32),
        compiler_params=_cparams(("arbitrary",)),
        name="moe_experts",
    )(te, nv, inv, u2_all, wg, bg.reshape(N_EXPERTS, 1, D_FF), wu, bu.reshape(N_EXPERTS, 1, D_FF),
      wd, bd.reshape(N_EXPERTS, 1, D_MODEL))


def _combine_kernel(y0_ref, y1_ref, y2_ref, y3_ref, prob_ref, x1_ref, mod_ref, w_ref, b_ref, o_ref):
    p = prob_ref[...]
    tm = p.shape[0]
    moe = p[:, 0:1] * _load_tok_tiles(y0_ref, tm)
    for j, y_ref in enumerate((y1_ref, y2_ref, y3_ref), start=1):
        moe = moe + p[:, j:j + 1] * _load_tok_tiles(y_ref, tm)
    m = mod_ref[0]
    o_ref[...] = _layer_norm(DEEPNORM_ALPHA * x1_ref[...] + m[5:6] * moe) * w_ref[...] + b_ref[...]


def _combine(y4, prob, x1, tok_base, ntok_all, mod_rows, tok_per_row, w, b):
    ntok = x1.shape[0]
    tm = 256
    per = tok_per_row // tm

    def yspec(k):
        return pl.BlockSpec((tm * ROW_TILE, 128), lambda i: ((k * ntok_all + tok_base) // tm + i, 0))

    return pl.pallas_call(
        _combine_kernel,
        grid=(ntok // tm,),
        in_specs=[yspec(0), yspec(1), yspec(2), yspec(3),
                  pl.BlockSpec((tm, TOP_K), lambda i: (i, 0)),
                  pl.BlockSpec((tm, D_MODEL), lambda i: (i, 0)),
                  pl.BlockSpec((1, 6, D_MODEL), lambda i: (i // per, 0, 0)),
                  pl.BlockSpec((1, D_MODEL), lambda i: (0, 0)),
                  pl.BlockSpec((1, D_MODEL), lambda i: (0, 0))],
        out_specs=pl.BlockSpec((tm, D_MODEL), lambda i: (i, 0)),
        out_shape=jax.ShapeDtypeStruct((ntok, D_MODEL), F32),
        compiler_params=_cparams(("arbitrary",)),
        name="moe_combine",
    )(y4, y4, y4, y4, prob, x1, mod_rows, w, b)


def _block_diag(w):
    nb, bb, _ = w.shape
    eye = jnp.eye(nb, dtype=w.dtype)
    return jnp.einsum('nij,nm->nimj', w, eye).reshape(nb * bb, nb * bb)


def kernel(x_prompt, x_sample, state_rwkv, state_rglru, c, c_ctx, w_mod, b_mod, w_in, w_decay_up, b_decay, w_iclr_up, b_iclr, w_gate_up, k_k, k_a, r_k, gn_w, gn_b, conv_w, conv_b, w_rg_a, b_rg_a, w_rg_i, b_rg_i, lam, w_out, ln1_w, ln1_b, w_router, b_router, w_e_gate, b_e_gate, w_e_up, b_e_up, w_e_down, b_e_down, ln2_w, ln2_b):
    Bp, Tp, D = x_prompt.shape
    Bs, Ts, _ = x_sample.shape
    l = 0
    row = lambda a: a[l].reshape(1, -1)

    cond8 = jnp.concatenate([c_ctx[None, :], c, jnp.zeros((8 - 1 - Bs, D), F32)], axis=0)
    mod = _modulation(cond8, w_mod[l], b_mod[l]).reshape(8, 6, D)
    mod_p, mod_s = mod[0:1], mod[1:1 + Bs]

    w_in_b = w_in[l].astype(BF16)
    w_out_b = w_out[l].astype(BF16)
    head_id = jnp.arange(D_A) // HEAD
    hs = (head_id[:, None] == head_id[None, :]).astype(BF16)
    avg = (hs.astype(F32) / HEAD).astype(BF16)
    wdec = w_decay_up[l].reshape(2 * LORA, D_A)
    wic = w_iclr_up[l].reshape(2 * LORA, D_A)
    wa_bd = jnp.stack([_block_diag(w_rg_a[l, 0]), _block_diag(w_rg_a[l, 1])])
    wi_bd = jnp.stack([_block_diag(w_rg_i[l, 0]), _block_diag(w_rg_i[l, 1])])

    xp = x_prompt.reshape(Bp * Tp, D)
    xs = x_sample.reshape(Bs * Ts, D)

    s0_s = state_rwkv[:, l].reshape(Bs, 2, N_PAIRS, 2, HEAD, HEAD).transpose(0, 1, 2, 4, 3, 5)
    s0_s = s0_s.reshape(Bs, 2, N_PAIRS, HEAD, 2 * HEAD)
    s0_p = jnp.zeros((Bp, 2, N_PAIRS, HEAD, 2 * HEAD), F32)
    h0_p = jnp.zeros((Bp, 2, D_B), F32)
    h0_s = state_rglru[:, l]

    ntok_all = Bp * Tp + Bs * Ts
    outs = []
    u2_all = None
    for x, mod_rows, tok_per_row, nseq, T, s0, h0, lrow, tok_base in (
            (xp, mod_p, Bp * Tp, Bp, Tp, s0_p, h0_p, Tp, 0),
            (xs, mod_s, Ts, Bs, Ts, s0_s, h0_s, GRID_W, Bp * Tp)):
        proj = _inproj(x, mod_rows, w_in_b, tok_per_row)
        o_f, o_b, bon_f, bon_b, s_fin = _rwkv(proj, s0, wdec, b_decay[l], wic, b_iclr[l], row(k_k), row(k_a),
                                              r_k[l].reshape(1, D_A), hs, nseq, T)
        yb, h_fin = _rglru(proj, h0, conv_w[l], row(conv_b), wa_bd, b_rg_a[l], wi_bd, b_rg_i[l], lam[l],
                           nseq, T, lrow)
        x1, u2_all, eid, prob = _mixout(o_f, o_b, bon_f, bon_b, yb, proj, x, mod_rows, tok_per_row,
                                        w_gate_up[l], avg, row(gn_w), row(gn_b), w_out_b, row(ln1_w),
                                        row(ln1_b), w_router[l], row(b_router), u2_all, tok_base, ntok_all)
        outs.append((x1, eid, prob, s_fin, h_fin))

    (x1_p, eid_p, prob_p, sfin_p, hfin_p), (x1_s, eid_s, prob_s, _, _) = outs
    slot, te, nv, nt_max = _route(jnp.concatenate([eid_p, eid_s], axis=0))
    inv = _invert(slot, nt_max * MOE_TM)
    y4 = _experts(te, nv, inv, u2_all, nt_max, w_e_gate[l], b_e_gate[l], w_e_up[l], b_e_up[l], w_e_down[l],
                  b_e_down[l])
    y_p = _combine(y4, prob_p, x1_p, 0, ntok_all, mod_p, Bp * Tp, row(ln2_w), row(ln2_b))
    y_s = _combine(y4, prob_s, x1_s, Bp * Tp, ntok_all, mod_s, Ts, row(ln2_w), row(ln2_b))
    return (y_p.reshape(Bp, Tp, D), y_s.reshape(Bs, Ts, D),
            sfin_p[:, None], hfin_p[:, None])
```

```python
import functools
import math

import jax
import jax.numpy as jnp
from jax import lax
from jax.experimental import pallas as pl
from jax.experimental.pallas import tpu as pltpu

F32 = jnp.float32
BF16 = jnp.bfloat16

D_MODEL = 1024
D_A = 512
D_B = 512
HEAD = 64
N_HEADS = 8
N_PAIRS = N_HEADS // 2
GRID_W = 64
LORA = 64
LORA_G = 128
RGLRU_C = 8.0
N_EXPERTS = 32
TOP_K = 4
D_FF = 1024
SWIGLU_LIMIT = 7.0
SWIGLU_ALPHA = 1.702
LN_EPS = 1e-5
GN_EPS = 1e-5 * HEAD
D_IN = 3 * D_A + 2 * D_B + 2 * LORA + 2 * LORA + LORA_G
DEPTH = 1
DEEPNORM_ALPHA = (2 * DEPTH) ** 0.25

COL_R, COL_K, COL_V, COL_XR, COL_XGB = 0, 1, 2, 3, 4
COL_LW, COL_LA, COL_LG = 20, 21, 22

CHUNK = 64
VMEM_LIMIT = 56 * 1024 * 1024

_NN = (((1,), (0,)), ((), ()))
_NT = (((1,), (1,)), ((), ()))


def _mm(a, b, dims=_NN):
    return lax.dot_general(a, b, dims, preferred_element_type=F32)


def _split2(x):
    hi = x.astype(BF16)
    lo = (x - hi.astype(F32)).astype(BF16)
    return hi, lo


def _split3(x):
    hi = x.astype(BF16)
    r1 = x - hi.astype(F32)
    mid = r1.astype(BF16)
    lo = (r1 - mid.astype(F32)).astype(BF16)
    return hi, mid, lo


def _dot1(a, b, dims=_NN):
    return _mm(a.astype(BF16), b.astype(BF16), dims)


def _dot3(a, b):
    ah, al = _split2(a)
    bh, bl = _split2(b)
    return _mm(jnp.concatenate([ah, ah, al], axis=1), jnp.concatenate([bh, bl, bh], axis=0))


def _dot_xb(a, b_bf16):
    return _mm(jnp.concatenate(_split3(a), axis=1), jnp.concatenate([b_bf16] * 3, axis=0))


def _dot_xa(a_bf16, b):
    return _mm(jnp.concatenate([a_bf16] * 3, axis=1), jnp.concatenate(_split3(b), axis=0))


ROW_TILE = 8


def _tok_rows(t, n=1):
    return pl.ds(pl.multiple_of(t * ROW_TILE, ROW_TILE), n * ROW_TILE)


def _load_tok_tiles(ref, ntok, lead=()):
    return jnp.concatenate([ref[lead + (pl.ds(s_, ntok, stride=ROW_TILE), slice(None))]
                            for s_ in range(ROW_TILE)], axis=1)


def _store_tok_tiles(ref, x):
    ntok = x.shape[0]
    for s_ in range(ROW_TILE):
        ref[pl.ds(s_, ntok, stride=ROW_TILE), :] = x[:, 128 * s_:128 * (s_ + 1)]


def _layer_norm(x):
    mu = jnp.mean(x, axis=-1, keepdims=True)
    xc = x - mu
    var = jnp.mean(xc * xc, axis=-1, keepdims=True)
    return xc * lax.rsqrt(var + LN_EPS)


def _cparams(sem):
    return pltpu.CompilerParams(dimension_semantics=sem, vmem_limit_bytes=VMEM_LIMIT)


def _mod_kernel(c_ref, w_ref, b_ref, o_ref):
    c = c_ref[...]
    s = c * jax.nn.sigmoid(c)
    o_ref[...] = _dot3(s, w_ref[...]) + b_ref[...]


def _modulation(cond8, w_mod, b_mod):
    n = w_mod.shape[1]
    tn = 1024
    return pl.pallas_call(
        _mod_kernel,
        grid=(n // tn,),
        in_specs=[pl.BlockSpec((8, D_MODEL), lambda j: (0, 0)),
                  pl.BlockSpec((D_MODEL, tn), lambda j: (0, j)),
                  pl.BlockSpec((1, tn), lambda j: (0, j))],
        out_specs=pl.BlockSpec((8, tn), lambda j: (0, j)),
        out_shape=jax.ShapeDtypeStruct((8, n), F32),
        compiler_params=_cparams(("arbitrary",)),
        name="modulation",
    )(cond8, w_mod, b_mod.reshape(1, n))


def _inproj_kernel(x_ref, mod_ref, w_ref, o_ref):
    m = mod_ref[0]
    u = _layer_norm(x_ref[...]) * (1.0 + m[1:2]) + m[0:1]
    o_ref[...] = _dot1(u, w_ref[...])


def _inproj(x, mod_rows, w_in_bf16, tok_per_row):
    ntok = x.shape[0]
    tm = 256
    per = tok_per_row // tm
    return pl.pallas_call(
        _inproj_kernel,
        grid=(ntok // tm,),
        in_specs=[pl.BlockSpec((tm, D_MODEL), lambda i: (i, 0)),
                  pl.BlockSpec((1, 6, D_MODEL), lambda i: (i // per, 0, 0)),
                  pl.BlockSpec((D_MODEL, D_IN), lambda i: (0, 0))],
        out_specs=pl.BlockSpec((tm, D_IN), lambda i: (i, 0)),
        out_shape=jax.ShapeDtypeStruct((ntok, D_IN), F32),
        compiler_params=_cparams(("arbitrary",)),
        name="inproj",
    )(x, mod_rows, w_in_bf16)


def _L(s):
    hi, lo = s
    return jnp.concatenate([hi, hi, lo], axis=1)


def _R(s):
    hi, lo = s
    return jnp.concatenate([hi, lo, hi], axis=0)


def _Rt(s):
    hi, lo = s
    return jnp.concatenate([hi, lo, hi], axis=1)


def _rwkv_kernel(rf_ref, kf_ref, vf_ref, lwf_ref, laf_ref, rb_ref, kb_ref, vb_ref, lwb_ref, lab_ref,
                 s0_ref, wdec_ref, bdec_ref, wic_ref, bic_ref, kkw_ref, ka_ref, rk_ref, hs_ref,
                 of_ref, ob_ref, bonf_ref, bonb_ref, sfin_ref,
                 s_scr, t_scr, w3_scr, mr_scr, ar_scr, bk_scr, gt_scr, *, G, ng):
    L = CHUNK
    g = pl.program_id(1)
    ins = ((rf_ref, kf_ref, vf_ref, lwf_ref, laf_ref), (rb_ref, kb_ref, vb_ref, lwb_ref, lab_ref))
    outs = ((of_ref, bonf_ref), (ob_ref, bonb_ref))

    @pl.when(g == 0)
    def _():
        s_scr[...] = s0_ref[0]

    hs = hs_ref[...]
    lane = lax.broadcasted_iota(jnp.int32, (L, 128), 1)
    rowp = lax.broadcasted_iota(jnp.int32, (L, 128), 0)
    colp = jnp.bitwise_and(lane, 63)
    h0 = lane < 64
    m0 = jnp.where(h0, 1.0, 0.0).astype(BF16)
    m1 = jnp.where(h0, 0.0, 1.0).astype(BF16)
    eye2 = jnp.where(rowp == colp, 1.0, 0.0)
    rowL = lax.broadcasted_iota(jnp.int32, (L, L), 0)
    colL = lax.broadcasted_iota(jnp.int32, (L, L), 1)

    def same(sh):
        return lax.shift_right_logical(rowp, sh) == lax.shift_right_logical(colp, sh)

    same8, same16, same32 = same(3), same(4), same(5)

    def bmask(c):
        return jnp.where(c, 1.0, 0.0).astype(BF16)

    mk8 = bmask(same8)
    merge_masks = (bmask(same16 & (~same8)), bmask(same32 & (~same16)), bmask(~same32))

    def sms(s):
        return tuple(jnp.concatenate([x * m0, x * m1], axis=0) for x in s)

    def pm(xs, ys):
        return _mm(_L(xs), _R(sms(ys)))

    chains = [(d, p) for d in range(2) for p in range(N_PAIRS)]

    def each(fn, *lists):
        return [fn(*a) for a in zip(*lists)]

    def prep(j, carry):
        pre = []
        for d in range(2):
            r_ref, k_ref, v_ref, lw_ref, la_ref = ins[d]
            jn = j if d == 0 else G - 1 - j
            rows = pl.ds(pl.multiple_of(jn * L, L), L)
            u = d * G + j
            r = r_ref[rows, :]
            k = k_ref[rows, :]
            v = v_ref[rows, :]
            dsel = (lane >= 64) if d else h0
            tl = jnp.where(dsel, jnp.tanh(lw_ref[rows, :]), 0.0)
            dw = _dot3(tl, wdec_ref[...]) + bdec_ref[d:d + 1, :]
            logw = (-math.exp(-0.5)) * jax.nn.sigmoid(dw)
            la_m = jnp.where(dsel, la_ref[rows, :], 0.0)
            iclr = jax.nn.sigmoid(_dot3(la_m, wic_ref[...]) + bic_ref[d:d + 1, :])
            kkr = k * kkw_ref[...]
            nrm = jnp.sqrt(_dot_xb(kkr * kkr, hs))
            kk = kkr / jnp.maximum(nrm, 1e-12)
            kdir = k * (1.0 + (iclr - 1.0) * ka_ref[...])
            bv = kk * iclr
            outs[d][1][rows, :] = _dot_xb(r * kdir * rk_ref[...], hs) * v

            tri = bmask((rowL <= colL) if d else (rowL >= colL))
            cs = _dot_xa(tri, logw)
            ctot = jnp.sum(logw, axis=0, keepdims=True)
            g_inv = jnp.exp(-cs)
            g_rem = jnp.exp(ctot - cs)
            gt_scr[u] = jnp.exp(ctot)
            At = -kk * jnp.exp(cs - logw)
            Rt = r * jnp.exp(cs)
            Bt = bv * g_inv
            Kt = kdir * g_inv
            Bg = bv * g_rem
            Kg = kdir * g_rem
            for p in range(N_PAIRS):
                sl = slice(128 * p, 128 * (p + 1))
                ar = jnp.concatenate([At[:, sl], Rt[:, sl]], axis=0)
                ar_scr[u, p] = ar
                bk_scr[u, p] = jnp.concatenate([Bg[:, sl], Kg[:, sl]], axis=0)
                pre.append((ar, Bt[:, sl], Kt[:, sl], v[:, sl]))

        def stage_m(c, pr):
            d = c[0]
            ar, bt, kt, _ = pr
            strict2 = (rowp < colp) if d else (rowp > colp)
            incl2 = (rowp <= colp) if d else (rowp >= colp)
            smB, smK = sms(_split2(bt)), sms(_split2(kt))
            rhs = tuple(jnp.concatenate([b_, k_], axis=0) for b_, k_ in zip(smB, smK))
            M = _mm(_L(_split2(ar)), _Rt(rhs), _NT)
            N = jnp.where(strict2, M[:L, :128], 0.0)
            Mak = jnp.where(strict2, M[:L, 128:], 0.0)
            mr_scr[d * G + j, c[1]] = jnp.concatenate([jnp.where(incl2, M[L:, :128], 0.0),
                                                       jnp.where(incl2, M[L:, 128:], 0.0)], axis=1)
            return N, Mak

        NM = each(stage_m, chains, pre)
        Ns = [_split2(nm[0]) for nm in NM]
        N0s = [(n[0] * mk8, n[1] * mk8) for n in Ns]
        T = [eye2 + jnp.where(same8, nm[0], 0.0) for nm in NM]
        P2s = [_split2(pm(n, n)) for n in N0s]
        W2 = [pm(_split2(nm[1]), _split2(pr[3])) for nm, pr in zip(NM, pre)]
        T = [t + pm(_split2(t), p2) for t, p2 in zip(T, P2s)]
        P4s = [_split2(pm(p2, p2)) for p2 in P2s]
        T = [t + pm(_split2(t), p4) for t, p4 in zip(T, P4s)]
        for mk in merge_masks:
            Ts = [_split2(t) for t in T]
            Y = [pm(ts, (n[0] * mk, n[1] * mk)) for ts, n in zip(Ts, Ns)]
            T = [t + pm(_split2(y), ts) for t, y, ts in zip(T, Y, Ts)]
        for (d, p), t, w2 in zip(chains, T, W2):
            t_scr[d * G + j, p] = t
            w3_scr[d * G + j, p] = pm(_split2(t), _split2(w2))
        return carry

    lax.fori_loop(0, G, prep, 0)

    def serial(j, carry):
        us = [d * G + j for d, _ in chains]
        rws = [pl.ds(pl.multiple_of((j if d == 0 else G - 1 - j) * L, L), L) for d, _ in chains]
        sls = [slice(128 * p, 128 * (p + 1)) for _, p in chains]
        S2 = [s_scr[d, p] for d, p in chains]
        XR = [_mm(_L(_split2(ar_scr[u, p])), _Rt(sms(_split2(s2))), _NT)
              for (d, p), u, s2 in zip(chains, us, S2)]
        U = [pm(_split2(t_scr[u, p]), _split2(xr[:L])) + w3_scr[u, p] for (d, p), u, xr in zip(chains, us, XR)]
        V = [ins[d][2][rw, sl] for (d, p), rw, sl in zip(chains, rws, sls)]
        for (d, p), u, rw, sl, s2, xr, uu, vv in zip(chains, us, rws, sls, S2, XR, U, V):
            UVt = jnp.concatenate([uu, vv], axis=0).T
            Rm = _mm(_L(_split2(UVt)), _R(_split2(bk_scr[u, p])))
            s_scr[d, p] = gt_scr[u][:, sl] * s2 + jnp.where(h0, Rm[:HEAD], Rm[HEAD:])
        for (d, p), u, rw, sl, xr, uu, vv in zip(chains, us, rws, sls, XR, U, V):
            smU, smV = sms(_split2(uu)), sms(_split2(vv))
            rhs = jnp.concatenate([smU[0], smV[0], smU[1], smV[1], smU[0], smV[0]], axis=0)
            outs[d][0][rw, sl] = xr[L:] + _mm(_L(_split2(mr_scr[u, p])), rhs)
        return carry

    lax.fori_loop(0, G, serial, 0)

    @pl.when(g == ng - 1)
    def _():
        for d in range(2):
            for p in range(N_PAIRS):
                sp = s_scr[d, p]
                sfin_ref[0, d, 2 * p] = sp[:, :HEAD]
                sfin_ref[0, d, 2 * p + 1] = sp[:, HEAD:]


def _rwkv(proj, s0_pairs, wdec, bdec, wic, bic, kkw, ka, rk, hs, nseq, T):
    ntok = proj.shape[0]
    nc = T // CHUNK
    G = min(nc, 8)
    ng = nc // G
    GL = G * CHUNK

    def fwd(b, g):
        return b * ng + g

    def bwd(b, g):
        return b * ng + ng - 1 - g

    def pspecs(rowblk):
        return [pl.BlockSpec((GL, D_A), lambda b, g: (rowblk(b, g), COL_R)),
                pl.BlockSpec((GL, D_A), lambda b, g: (rowblk(b, g), COL_K)),
                pl.BlockSpec((GL, D_A), lambda b, g: (rowblk(b, g), COL_V)),
                pl.BlockSpec((GL, 128), lambda b, g: (rowblk(b, g), COL_LW)),
                pl.BlockSpec((GL, 128), lambda b, g: (rowblk(b, g), COL_LA))]

    def wspec(shape):
        return pl.BlockSpec(shape, lambda b, g: (0,) * len(shape))

    ospec_f = pl.BlockSpec((GL, D_A), lambda b, g: (fwd(b, g), 0))
    ospec_b = pl.BlockSpec((GL, D_A), lambda b, g: (bwd(b, g), 0))
    tok = jax.ShapeDtypeStruct((ntok, D_A), F32)
    return pl.pallas_call(
        functools.partial(_rwkv_kernel, G=G, ng=ng),
        grid=(nseq, ng),
        in_specs=pspecs(fwd) + pspecs(bwd) + [
            pl.BlockSpec((1, 2, N_PAIRS, HEAD, 128), lambda b, g: (b, 0, 0, 0, 0)),
            wspec((128, D_A)), wspec((2, D_A)), wspec((128, D_A)), wspec((2, D_A)),
            wspec((1, D_A)), wspec((1, D_A)), wspec((1, D_A)), wspec((D_A, D_A))],
        out_specs=[ospec_f, ospec_b, ospec_f, ospec_b,
                   pl.BlockSpec((1, 2, N_HEADS, HEAD, HEAD), lambda b, g: (b, 0, 0, 0, 0))],
        out_shape=[tok, tok, tok, tok, jax.ShapeDtypeStruct((nseq, 2, N_HEADS, HEAD, HEAD), F32)],
        scratch_shapes=[pltpu.VMEM((2, N_PAIRS, HEAD, 128), F32),
                        pltpu.VMEM((2 * G, N_PAIRS, CHUNK, 128), F32),
                        pltpu.VMEM((2 * G, N_PAIRS, CHUNK, 128), F32),
                        pltpu.VMEM((2 * G, N_PAIRS, CHUNK, 256), F32),
                        pltpu.VMEM((2 * G, N_PAIRS, 2 * CHUNK, 128), F32),
                        pltpu.VMEM((2 * G, N_PAIRS, 2 * CHUNK, 128), F32),
                        pltpu.VMEM((2 * G, 1, D_A), F32)],
        compiler_params=_cparams(("arbitrary", "arbitrary")),
        name="rwkv_scan",
    )(*([proj] * 10), s0_pairs, wdec, bdec, wic, bic, kkw, ka, rk, hs)


def _gelu_tanh(x):
    return 0.5 * x * (1.0 + jnp.tanh(math.sqrt(2.0 / math.pi) * (x + 0.044715 * (x * x * x))))


def _rglru_kernel(xr_ref, xg_ref, h0_ref, cw_ref, cb_ref, wa_ref, ba_ref, wi_ref, bi_ref, lam_ref,
                  y_ref, hfin_ref, a_scr, u_scr, *, T, lrow):
    RB = min(T, 256)
    nblk = T // RB
    rowi = lax.broadcasted_iota(jnp.int32, (RB, D_B), 0)
    pos = jnp.bitwise_and(rowi, lrow - 1)
    cw = cw_ref[...]
    lam = lam_ref[...]
    sp = jnp.maximum(-lam, 0.0) + jnp.log1p(jnp.exp(-jnp.abs(lam)))

    def gates(blk, carry):
        r0 = pl.multiple_of(blk * RB, RB)
        x = xr_ref[pl.ds(r0, RB), :]
        xm1 = jnp.where(pos >= 1, pltpu.roll(x, 1, 0), 0.0)
        xm2 = jnp.where(pos >= 2, pltpu.roll(x, 2, 0), 0.0)
        xp1 = jnp.where(pos <= lrow - 2, pltpu.roll(x, RB - 1, 0), 0.0)
        xc = xm2 * cw[0:1] + xm1 * cw[1:2] + x * cw[2:3] + xp1 * cw[3:4] + cb_ref[...]
        for dd in range(2):
            gr = jax.nn.sigmoid(_dot1(xc, wa_ref[dd]) + ba_ref[dd:dd + 1, :])
            gi = jax.nn.sigmoid(_dot1(xc, wi_ref[dd]) + bi_ref[dd:dd + 1, :])
            log_a = (-RGLRU_C) * gr * sp[dd:dd + 1, :]
            a = jnp.exp(log_a)
            a_scr[dd, pl.ds(r0, RB), :] = a
            u_scr[dd, pl.ds(r0, RB), :] = jnp.sqrt((1.0 - a) * (1.0 + a)) * gi * xc
        return carry

    lax.fori_loop(0, nblk, gates, 0)

    def step(t, carry):
        hf, hb = carry
        hf = a_scr[0, pl.ds(t, 1), :] * hf + u_scr[0, pl.ds(t, 1), :]
        u_scr[0, pl.ds(t, 1), :] = hf
        tb = T - 1 - t
        hb = a_scr[1, pl.ds(tb, 1), :] * hb + u_scr[1, pl.ds(tb, 1), :]
        u_scr[1, pl.ds(tb, 1), :] = hb
        return hf, hb

    h0 = h0_ref[0]
    hf, hb = lax.fori_loop(0, T, step, (h0[0:1], h0[1:2]), unroll=8)
    hfin_ref[0] = jnp.concatenate([hf, hb], axis=0)

    def outp(blk, carry):
        r0 = pl.multiple_of(blk * RB, RB)
        h = u_scr[0, pl.ds(r0, RB), :] + u_scr[1, pl.ds(r0, RB), :]
        y_ref[pl.ds(r0, RB), :] = h * _gelu_tanh(xg_ref[pl.ds(r0, RB), :])
        return carry

    lax.fori_loop(0, nblk, outp, 0)


def _rglru(proj, h0, conv_w, conv_b, wa_bd, ba, wi_bd, bi, lam, nseq, T, lrow):
    ntok = proj.shape[0]

    def wspec(shape):
        return pl.BlockSpec(shape, lambda b: (0,) * len(shape))

    return pl.pallas_call(
        functools.partial(_rglru_kernel, T=T, lrow=lrow),
        grid=(nseq,),
        in_specs=[pl.BlockSpec((T, D_B), lambda b: (b, COL_XR)),
                  pl.BlockSpec((T, D_B), lambda b: (b, COL_XGB)),
                  pl.BlockSpec((1, 2, D_B), lambda b: (b, 0, 0)),
                  wspec((4, D_B)), wspec((1, D_B)),
                  wspec((2, D_B, D_B)), wspec((2, D_B)), wspec((2, D_B, D_B)), wspec((2, D_B)),
                  wspec((2, D_B))],
        out_specs=[pl.BlockSpec((T, D_B), lambda b: (b, 0)),
                   pl.BlockSpec((1, 2, D_B), lambda b: (b, 0, 0))],
        out_shape=[jax.ShapeDtypeStruct((ntok, D_B), F32),
                   jax.ShapeDtypeStruct((nseq, 2, D_B), F32)],
        scratch_shapes=[pltpu.VMEM((2, T, D_B), F32), pltpu.VMEM((2, T, D_B), F32)],
        compiler_params=_cparams(("arbitrary",)),
        name="rglru",
    )(proj, proj, h0, conv_w, conv_b, wa_bd, ba, wi_bd, bi, lam)


def _mixout_kernel(of_ref, ob_ref, bf_ref, bb_ref, yb_ref, lg_ref, x_ref, mod_ref, wgu_ref, avg_ref,
                   gnw_ref, gnb_ref, wout_ref, l1w_ref, l1b_ref, wr_ref, br_ref,
                   x1_ref, u2_ref, eid_ref, prob_ref):
    m = mod_ref[0]
    avg = avg_ref[...]
    wkv = of_ref[...] + ob_ref[...]
    mu = _dot_xb(wkv, avg)
    dv = wkv - mu
    var = _dot_xb(dv * dv, avg)
    gn = dv * lax.rsqrt(var + GN_EPS) * gnw_ref[...] + gnb_ref[...]
    g = _dot1(jax.nn.sigmoid(lg_ref[...]), wgu_ref[...])
    ya = (gn + (bf_ref[...] + bb_ref[...])) * g
    mix = _dot1(ya, wout_ref[0:D_A, :]) + _dot1(yb_ref[...], wout_ref[D_A:, :])
    x1 = _layer_norm(DEEPNORM_ALPHA * x_ref[...] + m[2:3] * mix) * l1w_ref[...] + l1b_ref[...]
    x1_ref[...] = x1
    u2 = _layer_norm(x1) * (1.0 + m[4:5]) + m[3:4]
    _store_tok_tiles(u2_ref, u2)
    logits = _dot3(u2, wr_ref[...]) + br_ref[...]
    tm = logits.shape[0]
    lane = lax.broadcasted_iota(jnp.int32, (tm, N_EXPERTS), 1)
    lane4 = lax.broadcasted_iota(jnp.int32, (tm, TOP_K), 1)
    work = logits
    tops, idxs = [], []
    for _ in range(TOP_K):
        mx = jnp.max(work, axis=1, keepdims=True)
        idx = jnp.min(jnp.where(work == mx, lane, N_EXPERTS), axis=1, keepdims=True)
        tops.append(mx)
        idxs.append(idx)
        work = jnp.where(lane == idx, -jnp.inf, work)
    es = [jnp.exp(t - tops[0]) for t in tops]
    den = es[0] + es[1] + es[2] + es[3]
    eid = jnp.zeros((tm, TOP_K), jnp.int32)
    prob = jnp.zeros((tm, TOP_K), F32)
    for j in range(TOP_K):
        eid = jnp.where(lane4 == j, idxs[j], eid)
        prob = jnp.where(lane4 == j, es[j] / den, prob)
    eid_ref[...] = eid
    prob_ref[...] = prob


def _mixout_alias_kernel(*refs):
    _mixout_kernel(*refs[:17], *refs[18:])


def _mixout_first_kernel(*refs, ntiles):
    i = pl.program_id(0)

    @pl.when(i < ntiles)
    def _():
        _mixout_kernel(*refs)

    @pl.when(i >= ntiles)
    def _():
        refs[18][...] = jnp.zeros_like(refs[18])


def _mixout(o_f, o_b, bon_f, bon_b, yb, proj, x, mod_rows, tok_per_row, wgu, avg, gnw, gnb, wout, l1w, l1b,
            wr, br, u2_all, tok_base, ntok_all):
    ntok = x.shape[0]
    tm = 256
    per = tok_per_row // tm
    base_tiles = tok_base // tm
    alias = u2_all is not None
    ntiles = ntok // tm

    def wspec(shape):
        return pl.BlockSpec(shape, lambda i: (0,) * len(shape))

    def tl(i):
        return jnp.minimum(i, ntiles - 1)

    return pl.pallas_call(
        _mixout_alias_kernel if alias else functools.partial(_mixout_first_kernel, ntiles=ntiles),
        grid=(ntiles if alias else ntok_all // tm,),
        input_output_aliases={17: 1} if alias else {},
        in_specs=[pl.BlockSpec((tm, D_A), lambda i: (tl(i), 0)),
                  pl.BlockSpec((tm, D_A), lambda i: (tl(i), 0)),
                  pl.BlockSpec((tm, D_A), lambda i: (tl(i), 0)),
                  pl.BlockSpec((tm, D_A), lambda i: (tl(i), 0)),
                  pl.BlockSpec((tm, D_B), lambda i: (tl(i), 0)),
                  pl.BlockSpec((tm, LORA_G), lambda i: (tl(i), COL_LG)),
                  pl.BlockSpec((tm, D_MODEL), lambda i: (tl(i), 0)),
                  pl.BlockSpec((1, 6, D_MODEL), lambda i: (tl(i) // per, 0, 0)),
                  wspec((LORA_G, D_A)), wspec((D_A, D_A)), wspec((1, D_A)), wspec((1, D_A)),
                  wspec((D_MODEL, D_MODEL)), wspec((1, D_MODEL)), wspec((1, D_MODEL)),
                  wspec((D_MODEL, N_EXPERTS)), wspec((1, N_EXPERTS))]
                 + ([pl.BlockSpec(memory_space=pl.ANY)] if alias else []),
        out_specs=[pl.BlockSpec((tm, D_MODEL), lambda i: (tl(i), 0)),
                   pl.BlockSpec((tm * ROW_TILE, 128), lambda i: (i + base_tiles, 0)),
                   pl.BlockSpec((tm, TOP_K), lambda i: (tl(i), 0)),
                   pl.BlockSpec((tm, TOP_K), lambda i: (tl(i), 0))],
        out_shape=[jax.ShapeDtypeStruct((ntok, D_MODEL), F32),
                   jax.ShapeDtypeStruct((ntok_all * ROW_TILE, 128), F32),
                   jax.ShapeDtypeStruct((ntok, TOP_K), jnp.int32),
                   jax.ShapeDtypeStruct((ntok, TOP_K), F32)],
        compiler_params=_cparams(("arbitrary",)),
        name="mixout",
    )(o_f, o_b, bon_f, bon_b, yb, proj, x, mod_rows, wgu, avg, gnw, gnb, wout, l1w, l1b, wr, br,
      *([u2_all] if alias else []))


MOE_TM = 256
ROUTE_BLK = 256
INVERT_CHUNK = 2048


def _route_kernel(eid_ref, slot_ref, te_ref, nv_ref, c_scr, *, ntok, nt_pad):
    nblk = ntok // ROUTE_BLK
    lane32 = lax.broadcasted_iota(jnp.int32, (ROUTE_BLK, N_EXPERTS), 1)
    lane4 = lax.broadcasted_iota(jnp.int32, (ROUTE_BLK, TOP_K), 1)
    rowb = lax.broadcasted_iota(jnp.int32, (ROUTE_BLK, ROUTE_BLK), 0)
    colb = lax.broadcasted_iota(jnp.int32, (ROUTE_BLK, ROUTE_BLK), 1)
    tri = jnp.where(rowb > colb, 1.0, 0.0).astype(BF16)

    def count(b, carry):
        rows = pl.ds(pl.multiple_of(b * ROUTE_BLK, ROUTE_BLK), ROUTE_BLK)
        e = eid_ref[rows, :]
        hot = jnp.zeros((ROUTE_BLK, N_EXPERTS), F32)
        for j in range(TOP_K):
            hot = hot + jnp.where(lane32 == e[:, j:j + 1], 1.0, 0.0)
        c_scr[rows, :] = _mm(tri, hot.astype(BF16)) + carry
        return carry + jnp.sum(hot, axis=0, keepdims=True)

    n = lax.fori_loop(0, nblk, count, jnp.zeros((1, N_EXPERTS), F32))
    padded = jnp.floor((n + (MOE_TM - 1)) * (1.0 / MOE_TM)) * MOE_TM
    r32 = lax.broadcasted_iota(jnp.int32, (N_EXPERTS, N_EXPERTS), 0)
    c32 = lax.broadcasted_iota(jnp.int32, (N_EXPERTS, N_EXPERTS), 1)
    upper = jnp.where(r32 < c32, 1.0, 0.0).astype(BF16)
    off = _dot_xb(jnp.broadcast_to(padded, (8, N_EXPERTS)), upper)[0:1]
    gend = off + padded

    def place(b, carry):
        rows = pl.ds(pl.multiple_of(b * ROUTE_BLK, ROUTE_BLK), ROUTE_BLK)
        e = eid_ref[rows, :]
        base = off + c_scr[rows, :]
        s = jnp.zeros((ROUTE_BLK, TOP_K), F32)
        for j in range(TOP_K):
            sj = jnp.sum(jnp.where(lane32 == e[:, j:j + 1], base, 0.0), axis=1, keepdims=True)
            s = jnp.where(lane4 == j, sj, s)
        slot_ref[rows, :] = s.astype(jnp.int32)
        return carry

    lax.fori_loop(0, nblk, place, 0)

    gcol = jnp.sum(jnp.where(r32 == c32, jnp.broadcast_to(gend, (N_EXPERTS, N_EXPERTS)), 0.0),
                   axis=1, keepdims=True)
    tstart = (lax.broadcasted_iota(jnp.int32, (N_EXPERTS, nt_pad), 1) * MOE_TM).astype(F32)
    te = jnp.sum(jnp.where(gcol <= tstart, 1.0, 0.0), axis=0, keepdims=True)
    te_ref[...] = jnp.minimum(te, N_EXPERTS - 1.0).astype(jnp.int32)
    total = jnp.sum(padded, axis=1, keepdims=True)
    nv_ref[...] = jnp.broadcast_to(total * (1.0 / MOE_TM), (1, 128)).astype(jnp.int32)


def _route(eid):
    ntok = eid.shape[0]
    nt_max = ntok * TOP_K // MOE_TM + N_EXPERTS
    nt_pad = -(-(nt_max + 1) // 128) * 128
    full = lambda shape: pl.BlockSpec(shape, lambda i: (0,) * len(shape))
    slot, te, nv = pl.pallas_call(
        functools.partial(_route_kernel, ntok=ntok, nt_pad=nt_pad),
        grid=(1,),
        in_specs=[full((ntok, TOP_K))],
        out_specs=[full((ntok, TOP_K)), full((1, nt_pad)), full((1, 128))],
        out_shape=[jax.ShapeDtypeStruct((ntok, TOP_K), jnp.int32),
                   jax.ShapeDtypeStruct((1, nt_pad), jnp.int32),
                   jax.ShapeDtypeStruct((1, 128), jnp.int32)],
        scratch_shapes=[pltpu.VMEM((ntok, N_EXPERTS), F32)],
        compiler_params=_cparams(("arbitrary",)),
        name="moe_route",
    )(eid)
    return slot.reshape(ntok * TOP_K), te.reshape(nt_pad), nv.reshape(128), nt_max


def _invert_kernel(slot_ref, inv_ref, fill_vmem, slot_smem, inv_smem, sem_in, sem_out, *, nassign, nslots):
    fill_vmem[...] = jnp.full(fill_vmem.shape, -1, jnp.int32)
    fill = pltpu.make_async_copy(fill_vmem, inv_smem, sem_out)
    fill.start()
    fill.wait()

    def chunk(ci, c):
        a0 = pl.multiple_of(ci * INVERT_CHUNK, INVERT_CHUNK)
        cp = pltpu.make_async_copy(slot_ref.at[pl.ds(a0, INVERT_CHUNK)], slot_smem, sem_in)
        cp.start()
        cp.wait()

        def put(a, c2):
            inv_smem[slot_smem[a]] = a0 + a
            return c2

        lax.fori_loop(0, INVERT_CHUNK, put, 0, unroll=8)
        return c

    lax.fori_loop(0, nassign // INVERT_CHUNK, chunk, 0)
    out = pltpu.make_async_copy(inv_smem, inv_ref, sem_out)
    out.start()
    out.wait()


def _invert(slot, nslots):
    nassign = slot.shape[0]
    anyspec = pl.BlockSpec(memory_space=pl.ANY)
    return pl.pallas_call(
        functools.partial(_invert_kernel, nassign=nassign, nslots=nslots),
        grid=(1,),
        in_specs=[anyspec], out_specs=anyspec,
        out_shape=jax.ShapeDtypeStruct((nslots,), jnp.int32),
        scratch_shapes=[pltpu.VMEM((nslots,), jnp.int32),
                        pltpu.SMEM((INVERT_CHUNK,), jnp.int32), pltpu.SMEM((nslots,), jnp.int32),
                        pltpu.SemaphoreType.DMA(()), pltpu.SemaphoreType.DMA(())],
        compiler_params=_cparams(("arbitrary",)),
        name="moe_invert",
    )(slot)


def _expert_kernel(te_ref, nv_ref, inv_ref, u2_ref, wg_ref, bg_ref, wu_ref, bu_ref, wd_ref, bd_ref, y4_ref,
                   xbuf, ybuf, wg_b, wu_b, wd_b, sem_g, sem_s, *, ntok):
    i = pl.program_id(0)
    nvalid = nv_ref[0]
    valid = i < nvalid
    trash = TOP_K * ntok

    def gather_copy(tile, r, buf):
        code = inv_ref[tile * MOE_TM + r]
        t = lax.shift_right_logical(jnp.maximum(code, 0), 2)
        return pltpu.make_async_copy(u2_ref.at[_tok_rows(t)], xbuf.at[buf, _tok_rows(r)], sem_g.at[buf])

    def scatter_copy(tile, r):
        code = inv_ref[tile * MOE_TM + r]
        d = jnp.where(code < 0, trash + r, jnp.bitwise_and(code, TOP_K - 1) * ntok
                      + lax.shift_right_logical(jnp.maximum(code, 0), 2))
        return pltpu.make_async_copy(ybuf.at[_tok_rows(r)], y4_ref.at[_tok_rows(d)], sem_s)

    def wait_gather(buf):
        pltpu.make_async_copy(xbuf.at[buf], xbuf.at[buf], sem_g.at[buf]).wait()

    def wait_scatter():
        pltpu.make_async_copy(ybuf, ybuf, sem_s).wait()

    @pl.when(i == 0)
    def _():
        def first(r2, c):
            gather_copy(0, 2 * r2, 0).start(priority=0)
            gather_copy(0, 2 * r2 + 1, 0).start(priority=1)
            return c

        lax.fori_loop(0, MOE_TM // 2, first, 0)
        ybuf[...] = jnp.zeros_like(ybuf)
        spare = pltpu.make_async_copy(ybuf, y4_ref.at[_tok_rows(trash, MOE_TM)], sem_s)
        spare.start()
        spare.wait()

    fresh = jnp.logical_or(i == 0, te_ref[i] != te_ref[jnp.maximum(i - 1, 0)])

    @pl.when(jnp.logical_and(valid, fresh))
    def _():
        wg_b[...] = wg_ref[0].astype(BF16)
        wu_b[...] = wu_ref[0].astype(BF16)
        wd_b[...] = wd_ref[0].astype(BF16)

    @pl.when(valid)
    def _():
        buf = jnp.bitwise_and(i, 1)
        nxt = jnp.minimum(i + 1, nvalid - 1)
        wait_gather(buf)
        x = _load_tok_tiles(xbuf, MOE_TM, (buf,)).astype(BF16)
        FC = 256
        nfc = D_FF // FC
        per = MOE_TM // nfc
        y = jnp.zeros((MOE_TM, D_MODEL), F32)
        for j in range(nfc):
            for r in range(j * per, (j + 1) * per):
                gather_copy(nxt, r, 1 - buf).start(priority=r % 2)
            cs = slice(j * FC, (j + 1) * FC)
            gate = jnp.minimum(_mm(x, wg_b[:, cs]) + bg_ref[0, :, cs], SWIGLU_LIMIT)
            up = jnp.clip(_mm(x, wu_b[:, cs]) + bu_ref[0, :, cs], -SWIGLU_LIMIT, SWIGLU_LIMIT)
            h = (up + 1.0) * gate * jax.nn.sigmoid(SWIGLU_ALPHA * gate)
            y = y + _mm(h.astype(BF16), wd_b[cs, :])

        @pl.when(i > 0)
        def _():
            wait_scatter()

        _store_tok_tiles(ybuf, y + bd_ref[0])
        for r in range(MOE_TM):
            scatter_copy(i, r).start(priority=r % 2)

    @pl.when(i == nvalid)
    def _():
        wait_gather(jnp.bitwise_and(nvalid, 1))
        wait_scatter()


def _experts(te, nv, inv, u2_all, nt_max, wg, bg, wu, bu, wd, bd):
    ntok = u2_all.shape[0] // ROW_TILE

    def tile(i, te, nv, inv):
        return te[jnp.minimum(i, nv[0] - 1)]

    wspec = pl.BlockSpec((1, D_MODEL, D_FF), lambda i, te, nv, inv: (tile(i, te, nv, inv), 0, 0))
    bspec = pl.BlockSpec((1, 1, D_FF), lambda i, te, nv, inv: (tile(i, te, nv, inv), 0, 0))
    anyspec = pl.BlockSpec(memory_space=pl.ANY)
    return pl.pallas_call(
        functools.partial(_expert_kernel, ntok=ntok),
        grid_spec=pltpu.PrefetchScalarGridSpec(
            num_scalar_prefetch=3, grid=(nt_max + 1,),
            in_specs=[anyspec, wspec, bspec, wspec, bspec, wspec, bspec], out_specs=anyspec,
            scratch_shapes=[pltpu.VMEM((2, MOE_TM * ROW_TILE, 128), F32),
                            pltpu.VMEM((MOE_TM * ROW_TILE, 128), F32),
                            pltpu.VMEM((D_MODEL, D_FF), BF16), pltpu.VMEM((D_MODEL, D_FF), BF16),
                            pltpu.VMEM((D_MODEL, D_FF), BF16),
                            pltpu.SemaphoreType.DMA((2,)), pltpu.SemaphoreType.DMA(())]),
        out_shape=jax.ShapeDtypeStruct(((TOP_K * ntok + MOE_TM) * ROW_TILE, 128), F32),
        compiler_params=_cparams(("arbitrary",)),
        name="moe_experts",
    )(te, nv, inv, u2_all, wg, bg.reshape(N_EXPERTS, 1, D_FF), wu, bu.reshape(N_EXPERTS, 1, D_FF),
      wd, bd.reshape(N_EXPERTS, 1, D_MODEL))


def _combine_kernel(y0_ref, y1_ref, y2_ref, y3_ref, prob_ref, x1_ref, mod_ref, w_ref, b_ref, o_ref):
    p = prob_ref[...]
    tm = p.shape[0]
    moe = p[:, 0:1] * _load_tok_tiles(y0_ref, tm)
    for j, y_ref in enumerate((y1_ref, y2_ref, y3_ref), start=1):
        moe = moe + p[:, j:j + 1] * _load_tok_tiles(y_ref, tm)
    m = mod_ref[0]
    o_ref[...] = _layer_norm(DEEPNORM_ALPHA * x1_ref[...] + m[5:6] * moe) * w_ref[...] + b_ref[...]


def _combine(y4, prob, x1, tok_base, ntok_all, mod_rows, tok_per_row, w, b):
    ntok = x1.shape[0]
    tm = 256
    per = tok_per_row // tm

    def yspec(k):
        return pl.BlockSpec((tm * ROW_TILE, 128), lambda i: ((k * ntok_all + tok_base) // tm + i, 0))

    return pl.pallas_call(
        _combine_kernel,
        grid=(ntok // tm,),
        in_specs=[yspec(0), yspec(1), yspec(2), yspec(3),
                  pl.BlockSpec((tm, TOP_K), lambda i: (i, 0)),
                  pl.BlockSpec((tm, D_MODEL), lambda i: (i, 0)),
                  pl.BlockSpec((1, 6, D_MODEL), lambda i: (i // per, 0, 0)),
                  pl.BlockSpec((1, D_MODEL), lambda i: (0, 0)),
                  pl.BlockSpec((1, D_MODEL), lambda i: (0, 0))],
        out_specs=pl.BlockSpec((tm, D_MODEL), lambda i: (i, 0)),
        out_shape=jax.ShapeDtypeStruct((ntok, D_MODEL), F32),
        compiler_params=_cparams(("arbitrary",)),
        name="moe_combine",
    )(y4, y4, y4, y4, prob, x1, mod_rows, w, b)


def _block_diag(w):
    nb, bb, _ = w.shape
    eye = jnp.eye(nb, dtype=w.dtype)
    return jnp.einsum('nij,nm->nimj', w, eye).reshape(nb * bb, nb * bb)


def kernel(x_prompt, x_sample, state_rwkv, state_rglru, c, c_ctx, w_mod, b_mod, w_in, w_decay_up, b_decay, w_iclr_up, b_iclr, w_gate_up, k_k, k_a, r_k, gn_w, gn_b, conv_w, conv_b, w_rg_a, b_rg_a, w_rg_i, b_rg_i, lam, w_out, ln1_w, ln1_b, w_router, b_router, w_e_gate, b_e_gate, w_e_up, b_e_up, w_e_down, b_e_down, ln2_w, ln2_b):
    Bp, Tp, D = x_prompt.shape
    Bs, Ts, _ = x_sample.shape
    l = 0
    row = lambda a: a[l].reshape(1, -1)

    cond8 = jnp.concatenate([c_ctx[None, :], c, jnp.zeros((8 - 1 - Bs, D), F32)], axis=0)
    mod = _modulation(cond8, w_mod[l], b_mod[l]).reshape(8, 6, D)
    mod_p, mod_s = mod[0:1], mod[1:1 + Bs]

    w_in_b = w_in[l].astype(BF16)
    w_out_b = w_out[l].astype(BF16)
    head_id = jnp.arange(D_A) // HEAD
    hs = (head_id[:, None] == head_id[None, :]).astype(BF16)
    avg = (hs.astype(F32) / HEAD).astype(BF16)
    wdec = w_decay_up[l].reshape(2 * LORA, D_A)
    wic = w_iclr_up[l].reshape(2 * LORA, D_A)
    wa_bd = jnp.stack([_block_diag(w_rg_a[l, 0]), _block_diag(w_rg_a[l, 1])])
    wi_bd = jnp.stack([_block_diag(w_rg_i[l, 0]), _block_diag(w_rg_i[l, 1])])

    xp = x_prompt.reshape(Bp * Tp, D)
    xs = x_sample.reshape(Bs * Ts, D)

    s0_s = state_rwkv[:, l].reshape(Bs, 2, N_PAIRS, 2, HEAD, HEAD).transpose(0, 1, 2, 4, 3, 5)
    s0_s = s0_s.reshape(Bs, 2, N_PAIRS, HEAD, 2 * HEAD)
    s0_p = jnp.zeros((Bp, 2, N_PAIRS, HEAD, 2 * HEAD), F32)
    h0_p = jnp.zeros((Bp, 2, D_B), F32)
    h0_s = state_rglru[:, l]

    ntok_all = Bp * Tp + Bs * Ts
    outs = []
    u2_all = None
    for x, mod_rows, tok_per_row, nseq, T, s0, h0, lrow, tok_base in (
            (xp, mod_p, Bp * Tp, Bp, Tp, s0_p, h0_p, Tp, 0),
            (xs, mod_s, Ts, Bs, Ts, s0_s, h0_s, GRID_W, Bp * Tp)):
        proj = _inproj(x, mod_rows, w_in_b, tok_per_row)
        o_f, o_b, bon_f, bon_b, s_fin = _rwkv(proj, s0, wdec, b_decay[l], wic, b_iclr[l], row(k_k), row(k_a),
                                              r_k[l].reshape(1, D_A), hs, nseq, T)
        yb, h_fin = _rglru(proj, h0, conv_w[l], row(conv_b), wa_bd, b_rg_a[l], wi_bd, b_rg_i[l], lam[l],
                           nseq, T, lrow)
        x1, u2_all, eid, prob = _mixout(o_f, o_b, bon_f, bon_b, yb, proj, x, mod_rows, tok_per_row,
                                        w_gate_up[l], avg, row(gn_w), row(gn_b), w_out_b, row(ln1_w),
                                        row(ln1_b), w_router[l], row(b_router), u2_all, tok_base, ntok_all)
        outs.append((x1, eid, prob, s_fin, h_fin))

    (x1_p, eid_p, prob_p, sfin_p, hfin_p), (x1_s, eid_s, prob_s, _, _) = outs
    slot, te, nv, nt_max = _route(jnp.concatenate([eid_p, eid_s], axis=0))
    inv = _invert(slot, nt_max * MOE_TM)
    y4 = _experts(te, nv, inv, u2_all, nt_max, w_e_gate[l], b_e_gate[l], w_e_up[l], b_e_up[l], w_e_down[l],
                  b_e_down[l])
    y_p = _combine(y4, prob_p, x1_p, 0, ntok_all, mod_p, Bp * Tp, row(ln2_w), row(ln2_b))
    y_s = _combine(y4, prob_s, x1_s, Bp * Tp, ntok_all, mod_s, Ts, row(ln2_w), row(ln2_b))
    return (y_p.reshape(Bp, Tp, D), y_s.reshape(Bs, Ts, D),
            sfin_p[:, None], hfin_p[:, None])
```

```python
import functools
import math

import jax
import jax.numpy as jnp
from jax import lax
from jax.experimental import pallas as pl
from jax.experimental.pallas import tpu as pltpu

F32 = jnp.float32
BF16 = jnp.bfloat16

D_MODEL = 1024
D_A = 512
D_B = 512
HEAD = 64
N_HEADS = 8
N_PAIRS = N_HEADS // 2
GRID_W = 64
LORA = 64
LORA_G = 128
RGLRU_C = 8.0
N_EXPERTS = 32
TOP_K = 4
D_FF = 1024
SWIGLU_LIMIT = 7.0
SWIGLU_ALPHA = 1.702
LN_EPS = 1e-5
GN_EPS = 1e-5 * HEAD
D_IN = 3 * D_A + 2 * D_B + 2 * LORA + 2 * LORA + LORA_G
DEPTH = 1
DEEPNORM_ALPHA = (2 * DEPTH) ** 0.25

COL_R, COL_K, COL_V, COL_XR, COL_XGB = 0, 1, 2, 3, 4
COL_LW, COL_LA, COL_LG = 20, 21, 22

CHUNK = 64
VMEM_LIMIT = 56 * 1024 * 1024

_NN = (((1,), (0,)), ((), ()))
_NT = (((1,), (1,)), ((), ()))


def _mm(a, b, dims=_NN):
    return lax.dot_general(a, b, dims, preferred_element_type=F32)


def _split2(x):
    hi = x.astype(BF16)
    lo = (x - hi.astype(F32)).astype(BF16)
    return hi, lo


def _split3(x):
    hi = x.astype(BF16)
    r1 = x - hi.astype(F32)
    mid = r1.astype(BF16)
    lo = (r1 - mid.astype(F32)).astype(BF16)
    return hi, mid, lo


def _dot1(a, b, dims=_NN):
    return _mm(a.astype(BF16), b.astype(BF16), dims)


def _dot3(a, b):
    ah, al = _split2(a)
    bh, bl = _split2(b)
    return _mm(jnp.concatenate([ah, ah, al], axis=1), jnp.concatenate([bh, bl, bh], axis=0))


def _dot_xb(a, b_bf16):
    return _mm(jnp.concatenate(_split3(a), axis=1), jnp.concatenate([b_bf16] * 3, axis=0))


def _dot_xa(a_bf16, b):
    return _mm(jnp.concatenate([a_bf16] * 3, axis=1), jnp.concatenate(_split3(b), axis=0))


ROW_TILE = 8


def _tok_rows(t, n=1):
    return pl.ds(pl.multiple_of(t * ROW_TILE, ROW_TILE), n * ROW_TILE)


def _load_tok_tiles(ref, ntok, lead=()):
    return jnp.concatenate([ref[lead + (pl.ds(s_, ntok, stride=ROW_TILE), slice(None))]
                            for s_ in range(ROW_TILE)], axis=1)


def _store_tok_tiles(ref, x):
    ntok = x.shape[0]
    for s_ in range(ROW_TILE):
        ref[pl.ds(s_, ntok, stride=ROW_TILE), :] = x[:, 128 * s_:128 * (s_ + 1)]


def _layer_norm(x):
    mu = jnp.mean(x, axis=-1, keepdims=True)
    xc = x - mu
    var = jnp.mean(xc * xc, axis=-1, keepdims=True)
    return xc * lax.rsqrt(var + LN_EPS)


def _cparams(sem):
    return pltpu.CompilerParams(dimension_semantics=sem, vmem_limit_bytes=VMEM_LIMIT)


def _mod_kernel(c_ref, w_ref, b_ref, o_ref):
    c = c_ref[...]
    s = c * jax.nn.sigmoid(c)
    o_ref[...] = _dot3(s, w_ref[...]) + b_ref[...]


def _modulation(cond8, w_mod, b_mod):
    n = w_mod.shape[1]
    tn = 1024
    return pl.pallas_call(
        _mod_kernel,
        grid=(n // tn,),
        in_specs=[pl.BlockSpec((8, D_MODEL), lambda j: (0, 0)),
                  pl.BlockSpec((D_MODEL, tn), lambda j: (0, j)),
                  pl.BlockSpec((1, tn), lambda j: (0, j))],
        out_specs=pl.BlockSpec((8, tn), lambda j: (0, j)),
        out_shape=jax.ShapeDtypeStruct((8, n), F32),
        compiler_params=_cparams(("arbitrary",)),
        name="modulation",
    )(cond8, w_mod, b_mod.reshape(1, n))


def _inproj_kernel(x_ref, mod_ref, w_ref, o_ref):
    m = mod_ref[0]
    u = _layer_norm(x_ref[...]) * (1.0 + m[1:2]) + m[0:1]
    o_ref[...] = _dot1(u, w_ref[...])


def _inproj(x, mod_rows, w_in_bf16, tok_per_row):
    ntok = x.shape[0]
    tm = 256
    per = tok_per_row // tm
    return pl.pallas_call(
        _inproj_kernel,
        grid=(ntok // tm,),
        in_specs=[pl.BlockSpec((tm, D_MODEL), lambda i: (i, 0)),
                  pl.BlockSpec((1, 6, D_MODEL), lambda i: (i // per, 0, 0)),
                  pl.BlockSpec((D_MODEL, D_IN), lambda i: (0, 0))],
        out_specs=pl.BlockSpec((tm, D_IN), lambda i: (i, 0)),
        out_shape=jax.ShapeDtypeStruct((ntok, D_IN), F32),
        compiler_params=_cparams(("arbitrary",)),
        name="inproj",
    )(x, mod_rows, w_in_bf16)


def _L(s):
    hi, lo = s
    return jnp.concatenate([hi, hi, lo], axis=1)


def _R(s):
    hi, lo = s
    return jnp.concatenate([hi, lo, hi], axis=0)


def _Rt(s):
    hi, lo = s
    return jnp.concatenate([hi, lo, hi], axis=1)


def _rwkv_kernel(rf_ref, kf_ref, vf_ref, lwf_ref, laf_ref, rb_ref, kb_ref, vb_ref, lwb_ref, lab_ref,
                 s0_ref, wdec_ref, bdec_ref, wic_ref, bic_ref, kkw_ref, ka_ref, rk_ref, hs_ref,
                 of_ref, ob_ref, bonf_ref, bonb_ref, sfin_ref,
                 s_scr, t_scr, w2_scr, mr_scr, ar_scr, bk_scr, gt_scr, *, G, ng):
    L = CHUNK
    g = pl.program_id(1)
    ins = ((rf_ref, kf_ref, vf_ref, lwf_ref, laf_ref), (rb_ref, kb_ref, vb_ref, lwb_ref, lab_ref))
    outs = ((of_ref, bonf_ref), (ob_ref, bonb_ref))

    @pl.when(g == 0)
    def _():
        s_scr[...] = s0_ref[0]

    hs = hs_ref[...]
    lane = lax.broadcasted_iota(jnp.int32, (L, 128), 1)
    rowp = lax.broadcasted_iota(jnp.int32, (L, 128), 0)
    colp = jnp.bitwise_and(lane, 63)
    h0 = lane < 64
    m0 = jnp.where(h0, 1.0, 0.0).astype(BF16)
    m1 = jnp.where(h0, 0.0, 1.0).astype(BF16)
    eye2 = jnp.where(rowp == colp, 1.0, 0.0)
    rowL = lax.broadcasted_iota(jnp.int32, (L, L), 0)
    colL = lax.broadcasted_iota(jnp.int32, (L, L), 1)

    def same(sh):
        return lax.shift_right_logical(rowp, sh) == lax.shift_right_logical(colp, sh)

    same8, same16, same32 = same(3), same(4), same(5)

    def bmask(c):
        return jnp.where(c, 1.0, 0.0).astype(BF16)

    mk8 = bmask(same8)
    merge_masks = (bmask(same16 & (~same8)), bmask(same32 & (~same16)), bmask(~same32))

    def sms(s):
        return tuple(jnp.concatenate([x * m0, x * m1], axis=0) for x in s)

    def pm(xs, ys):
        return _mm(_L(xs), _R(sms(ys)))

    chains = [(d, p) for d in range(2) for p in range(N_PAIRS)]

    def each(fn, *lists):
        return [fn(*a) for a in zip(*lists)]

    def prep(j, carry):
        pre = []
        for d in range(2):
            r_ref, k_ref, v_ref, lw_ref, la_ref = ins[d]
            jn = j if d == 0 else G - 1 - j
            rows = pl.ds(pl.multiple_of(jn * L, L), L)
            u = d * G + j
            r = r_ref[rows, :]
            k = k_ref[rows, :]
            v = v_ref[rows, :]
            dsel = (lane >= 64) if d else h0
            tl = jnp.where(dsel, jnp.tanh(lw_ref[rows, :]), 0.0)
            dw = _dot3(tl, wdec_ref[...]) + bdec_ref[d:d + 1, :]
            logw = (-math.exp(-0.5)) * jax.nn.sigmoid(dw)
            la_m = jnp.where(dsel, la_ref[rows, :], 0.0)
            iclr = jax.nn.sigmoid(_dot3(la_m, wic_ref[...]) + bic_ref[d:d + 1, :])
            kkr = k * kkw_ref[...]
            nrm = jnp.sqrt(_dot_xb(kkr * kkr, hs))
            kk = kkr / jnp.maximum(nrm, 1e-12)
            kdir = k * (1.0 + (iclr - 1.0) * ka_ref[...])
            bv = kk * iclr
            outs[d][1][rows, :] = _dot_xb(r * kdir * rk_ref[...], hs) * v

            tri = bmask((rowL <= colL) if d else (rowL >= colL))
            cs = _dot_xa(tri, logw)
            ctot = jnp.sum(logw, axis=0, keepdims=True)
            g_inv = jnp.exp(-cs)
            g_rem = jnp.exp(ctot - cs)
            gt_scr[u] = jnp.exp(ctot)
            At = -kk * jnp.exp(cs - logw)
            Rt = r * jnp.exp(cs)
            Bt = bv * g_inv
            Kt = kdir * g_inv
            Bg = bv * g_rem
            Kg = kdir * g_rem
            for p in range(N_PAIRS):
                sl = slice(128 * p, 128 * (p + 1))
                ar = jnp.concatenate([At[:, sl], Rt[:, sl]], axis=0)
                ar_scr[u, p] = ar
                bk_scr[u, p] = jnp.concatenate([Bg[:, sl], Kg[:, sl]], axis=0)
                pre.append((ar, Bt[:, sl], Kt[:, sl], v[:, sl]))

        def stage_m(c, pr):
            d = c[0]
            ar, bt, kt, _ = pr
            strict2 = (rowp < colp) if d else (rowp > colp)
            incl2 = (rowp <= colp) if d else (rowp >= colp)
            smB, smK = sms(_split2(bt)), sms(_split2(kt))
            rhs = tuple(jnp.concatenate([b_, k_], axis=0) for b_, k_ in zip(smB, smK))
            M = _mm(_L(_split2(ar)), _Rt(rhs), _NT)
            N = jnp.where(strict2, M[:L, :128], 0.0)
            Mak = jnp.where(strict2, M[:L, 128:], 0.0)
            mr_scr[d * G + j, c[1]] = jnp.concatenate([jnp.where(incl2, M[L:, :128], 0.0),
                                                       jnp.where(incl2, M[L:, 128:], 0.0)], axis=1)
            return N, Mak

        NM = each(stage_m, chains, pre)
        Ns = [_split2(nm[0]) for nm in NM]
        N0s = [(n[0] * mk8, n[1] * mk8) for n in Ns]
        T = [eye2 + jnp.where(same8, nm[0], 0.0) for nm in NM]
        P2s = [_split2(pm(n, n)) for n in N0s]
        W2 = [pm(_split2(nm[1]), _split2(pr[3])) for nm, pr in zip(NM, pre)]
        T = [t + pm(_split2(t), p2) for t, p2 in zip(T, P2s)]
        P4s = [_split2(pm(p2, p2)) for p2 in P2s]
        T = [t + pm(_split2(t), p4) for t, p4 in zip(T, P4s)]
        for mk in merge_masks:
            Ts = [_split2(t) for t in T]
            Y = [pm(ts, (n[0] * mk, n[1] * mk)) for ts, n in zip(Ts, Ns)]
            T = [t + pm(_split2(y), ts) for t, y, ts in zip(T, Y, Ts)]
        for (d, p), t, w2 in zip(chains, T, W2):
            t_scr[d * G + j, p] = t
            w2_scr[d * G + j, p] = w2
        return carry

    lax.fori_loop(0, G, prep, 0)

    def serial(j, carry):
        us = [d * G + j for d, _ in chains]
        rws = [pl.ds(pl.multiple_of((j if d == 0 else G - 1 - j) * L, L), L) for d, _ in chains]
        sls = [slice(128 * p, 128 * (p + 1)) for _, p in chains]
        S2 = [s_scr[d, p] for d, p in chains]
        XR = [_mm(_L(_split2(ar_scr[u, p])), _Rt(sms(_split2(s2))), _NT)
              for (d, p), u, s2 in zip(chains, us, S2)]
        U = [pm(_split2(t_scr[u, p]), _split2(xr[:L] + w2_scr[u, p])) for (d, p), u, xr in zip(chains, us, XR)]
        V = [ins[d][2][rw, sl] for (d, p), rw, sl in zip(chains, rws, sls)]
        for (d, p), u, rw, sl, s2, xr, uu, vv in zip(chains, us, rws, sls, S2, XR, U, V):
            UVt = jnp.concatenate([uu, vv], axis=0).T
            Rm = _mm(_L(_split2(UVt)), _R(_split2(bk_scr[u, p])))
            s_scr[d, p] = gt_scr[u][:, sl] * s2 + jnp.where(h0, Rm[:HEAD], Rm[HEAD:])
        for (d, p), u, rw, sl, xr, uu, vv in zip(chains, us, rws, sls, XR, U, V):
            smU, smV = sms(_split2(uu)), sms(_split2(vv))
            rhs = jnp.concatenate([smU[0], smV[0], smU[1], smV[1], smU[0], smV[0]], axis=0)
            outs[d][0][rw, sl] = xr[L:] + _mm(_L(_split2(mr_scr[u, p])), rhs)
        return carry

    lax.fori_loop(0, G, serial, 0)

    @pl.when(g == ng - 1)
    def _():
        for d in range(2):
            for p in range(N_PAIRS):
                sp = s_scr[d, p]
                sfin_ref[0, d, 2 * p] = sp[:, :HEAD]
                sfin_ref[0, d, 2 * p + 1] = sp[:, HEAD:]


def _rwkv(proj, s0_pairs, wdec, bdec, wic, bic, kkw, ka, rk, hs, nseq, T):
    ntok = proj.shape[0]
    nc = T // CHUNK
    G = min(nc, 8)
    ng = nc // G
    GL = G * CHUNK

    def fwd(b, g):
        return b * ng + g

    def bwd(b, g):
        return b * ng + ng - 1 - g

    def pspecs(rowblk):
        return [pl.BlockSpec((GL, D_A), lambda b, g: (rowblk(b, g), COL_R)),
                pl.BlockSpec((GL, D_A), lambda b, g: (rowblk(b, g), COL_K)),
                pl.BlockSpec((GL, D_A), lambda b, g: (rowblk(b, g), COL_V)),
                pl.BlockSpec((GL, 128), lambda b, g: (rowblk(b, g), COL_LW)),
                pl.BlockSpec((GL, 128), lambda b, g: (rowblk(b, g), COL_LA))]

    def wspec(shape):
        return pl.BlockSpec(shape, lambda b, g: (0,) * len(shape))

    ospec_f = pl.BlockSpec((GL, D_A), lambda b, g: (fwd(b, g), 0))
    ospec_b = pl.BlockSpec((GL, D_A), lambda b, g: (bwd(b, g), 0))
    tok = jax.ShapeDtypeStruct((ntok, D_A), F32)
    return pl.pallas_call(
        functools.partial(_rwkv_kernel, G=G, ng=ng),
        grid=(nseq, ng),
        in_specs=pspecs(fwd) + pspecs(bwd) + [
            pl.BlockSpec((1, 2, N_PAIRS, HEAD, 128), lambda b, g: (b, 0, 0, 0, 0)),
            wspec((128, D_A)), wspec((2, D_A)), wspec((128, D_A)), wspec((2, D_A)),
            wspec((1, D_A)), wspec((1, D_A)), wspec((1, D_A)), wspec((D_A, D_A))],
        out_specs=[ospec_f, ospec_b, ospec_f, ospec_b,
                   pl.BlockSpec((1, 2, N_HEADS, HEAD, HEAD), lambda b, g: (b, 0, 0, 0, 0))],
        out_shape=[tok, tok, tok, tok, jax.ShapeDtypeStruct((nseq, 2, N_HEADS, HEAD, HEAD), F32)],
        scratch_shapes=[pltpu.VMEM((2, N_PAIRS, HEAD, 128), F32),
                        pltpu.VMEM((2 * G, N_PAIRS, CHUNK, 128), F32),
                        pltpu.VMEM((2 * G, N_PAIRS, CHUNK, 128), F32),
                        pltpu.VMEM((2 * G, N_PAIRS, CHUNK, 256), F32),
                        pltpu.VMEM((2 * G, N_PAIRS, 2 * CHUNK, 128), F32),
                        pltpu.VMEM((2 * G, N_PAIRS, 2 * CHUNK, 128), F32),
                        pltpu.VMEM((2 * G, 1, D_A), F32)],
        compiler_params=_cparams(("arbitrary", "arbitrary")),
        name="rwkv_scan",
    )(*([proj] * 10), s0_pairs, wdec, bdec, wic, bic, kkw, ka, rk, hs)


def _gelu_tanh(x):
    return 0.5 * x * (1.0 + jnp.tanh(math.sqrt(2.0 / math.pi) * (x + 0.044715 * (x * x * x))))


def _rglru_kernel(xr_ref, xg_ref, h0_ref, cw_ref, cb_ref, wa_ref, ba_ref, wi_ref, bi_ref, lam_ref,
                  y_ref, hfin_ref, a_scr, u_scr, *, T, lrow):
    RB = min(T, 256)
    nblk = T // RB
    rowi = lax.broadcasted_iota(jnp.int32, (RB, D_B), 0)
    pos = jnp.bitwise_and(rowi, lrow - 1)
    cw = cw_ref[...]
    lam = lam_ref[...]
    sp = jnp.maximum(-lam, 0.0) + jnp.log1p(jnp.exp(-jnp.abs(lam)))

    def gates(blk, carry):
        r0 = pl.multiple_of(blk * RB, RB)
        x = xr_ref[pl.ds(r0, RB), :]
        xm1 = jnp.where(pos >= 1, pltpu.roll(x, 1, 0), 0.0)
        xm2 = jnp.where(pos >= 2, pltpu.roll(x, 2, 0), 0.0)
        xp1 = jnp.where(pos <= lrow - 2, pltpu.roll(x, RB - 1, 0), 0.0)
        xc = xm2 * cw[0:1] + xm1 * cw[1:2] + x * cw[2:3] + xp1 * cw[3:4] + cb_ref[...]
        for dd in range(2):
            gr = jax.nn.sigmoid(_dot1(xc, wa_ref[dd]) + ba_ref[dd:dd + 1, :])
            gi = jax.nn.sigmoid(_dot1(xc, wi_ref[dd]) + bi_ref[dd:dd + 1, :])
            log_a = (-RGLRU_C) * gr * sp[dd:dd + 1, :]
            a = jnp.exp(log_a)
            a_scr[dd, pl.ds(r0, RB), :] = a
            u_scr[dd, pl.ds(r0, RB), :] = jnp.sqrt((1.0 - a) * (1.0 + a)) * gi * xc
        return carry

    lax.fori_loop(0, nblk, gates, 0)

    SUB = 8
    rowt = lax.broadcasted_iota(jnp.int32, (SUB, D_B), 0)

    def tile_scan(a, u, h_in, rev):
        for s_ in (1, 2, 4):
            keep = (rowt < SUB - s_) if rev else (rowt >= s_)
            sh = SUB - s_ if rev else s_
            a_sh = jnp.where(keep, pltpu.roll(a, sh, 0), 1.0)
            u_sh = jnp.where(keep, pltpu.roll(u, sh, 0), 0.0)
            u = a * u_sh + u
            a = a * a_sh
        return a * h_in + u

    def step(i, carry):
        hf, hb = carry
        rf = pl.ds(pl.multiple_of(i * SUB, SUB), SUB)
        hf_tile = tile_scan(a_scr[0, rf, :], u_scr[0, rf, :], hf, False)
        u_scr[0, rf, :] = hf_tile
        rb = pl.ds(pl.multiple_of(T - SUB - i * SUB, SUB), SUB)
        hb_tile = tile_scan(a_scr[1, rb, :], u_scr[1, rb, :], hb, True)
        u_scr[1, rb, :] = hb_tile
        return hf_tile[SUB - 1:SUB], hb_tile[0:1]

    h0 = h0_ref[0]
    hf, hb = lax.fori_loop(0, T // SUB, step, (h0[0:1], h0[1:2]), unroll=2)
    hfin_ref[0] = jnp.concatenate([hf, hb], axis=0)

    def outp(blk, carry):
        r0 = pl.multiple_of(blk * RB, RB)
        h = u_scr[0, pl.ds(r0, RB), :] + u_scr[1, pl.ds(r0, RB), :]
        y_ref[pl.ds(r0, RB), :] = h * _gelu_tanh(xg_ref[pl.ds(r0, RB), :])
        return carry

    lax.fori_loop(0, nblk, outp, 0)


def _rglru(proj, h0, conv_w, conv_b, wa_bd, ba, wi_bd, bi, lam, nseq, T, lrow):
    ntok = proj.shape[0]

    def wspec(shape):
        return pl.BlockSpec(shape, lambda b: (0,) * len(shape))

    return pl.pallas_call(
        functools.partial(_rglru_kernel, T=T, lrow=lrow),
        grid=(nseq,),
        in_specs=[pl.BlockSpec((T, D_B), lambda b: (b, COL_XR)),
                  pl.BlockSpec((T, D_B), lambda b: (b, COL_XGB)),
                  pl.BlockSpec((1, 2, D_B), lambda b: (b, 0, 0)),
                  wspec((4, D_B)), wspec((1, D_B)),
                  wspec((2, D_B, D_B)), wspec((2, D_B)), wspec((2, D_B, D_B)), wspec((2, D_B)),
                  wspec((2, D_B))],
        out_specs=[pl.BlockSpec((T, D_B), lambda b: (b, 0)),
                   pl.BlockSpec((1, 2, D_B), lambda b: (b, 0, 0))],
        out_shape=[jax.ShapeDtypeStruct((ntok, D_B), F32),
                   jax.ShapeDtypeStruct((nseq, 2, D_B), F32)],
        scratch_shapes=[pltpu.VMEM((2, T, D_B), F32), pltpu.VMEM((2, T, D_B), F32)],
        compiler_params=_cparams(("arbitrary",)),
        name="rglru",
    )(proj, proj, h0, conv_w, conv_b, wa_bd, ba, wi_bd, bi, lam)


def _mixout_kernel(of_ref, ob_ref, bf_ref, bb_ref, yb_ref, lg_ref, x_ref, mod_ref, wgu_ref, avg_ref,
                   gnw_ref, gnb_ref, wout_ref, l1w_ref, l1b_ref, wr_ref, br_ref,
                   x1_ref, u2_ref, eid_ref, prob_ref):
    m = mod_ref[0]
    avg = avg_ref[...]
    wkv = of_ref[...] + ob_ref[...]
    mu = _dot_xb(wkv, avg)
    dv = wkv - mu
    var = _dot_xb(dv * dv, avg)
    gn = dv * lax.rsqrt(var + GN_EPS) * gnw_ref[...] + gnb_ref[...]
    g = _dot1(jax.nn.sigmoid(lg_ref[...]), wgu_ref[...])
    ya = (gn + (bf_ref[...] + bb_ref[...])) * g
    mix = _dot1(ya, wout_ref[0:D_A, :]) + _dot1(yb_ref[...], wout_ref[D_A:, :])
    x1 = _layer_norm(DEEPNORM_ALPHA * x_ref[...] + m[2:3] * mix) * l1w_ref[...] + l1b_ref[...]
    x1_ref[...] = x1
    u2 = _layer_norm(x1) * (1.0 + m[4:5]) + m[3:4]
    _store_tok_tiles(u2_ref, u2)
    logits = _dot3(u2, wr_ref[...]) + br_ref[...]
    tm = logits.shape[0]
    lane = lax.broadcasted_iota(jnp.int32, (tm, N_EXPERTS), 1)
    lane4 = lax.broadcasted_iota(jnp.int32, (tm, TOP_K), 1)
    work = logits
    tops, idxs = [], []
    for _ in range(TOP_K):
        mx = jnp.max(work, axis=1, keepdims=True)
        idx = jnp.min(jnp.where(work == mx, lane, N_EXPERTS), axis=1, keepdims=True)
        tops.append(mx)
        idxs.append(idx)
        work = jnp.where(lane == idx, -jnp.inf, work)
    es = [jnp.exp(t - tops[0]) for t in tops]
    den = es[0] + es[1] + es[2] + es[3]
    eid = jnp.zeros((tm, TOP_K), jnp.int32)
    prob = jnp.zeros((tm, TOP_K), F32)
    for j in range(TOP_K):
        eid = jnp.where(lane4 == j, idxs[j], eid)
        prob = jnp.where(lane4 == j, es[j] / den, prob)
    eid_ref[...] = eid
    prob_ref[...] = prob


def _mixout_alias_kernel(*refs):
    _mixout_kernel(*refs[:17], *refs[18:])


def _mixout_first_kernel(*refs, ntiles):
    i = pl.program_id(0)

    @pl.when(i < ntiles)
    def _():
        _mixout_kernel(*refs)

    @pl.when(i >= ntiles)
    def _():
        refs[18][...] = jnp.zeros_like(refs[18])


def _mixout(o_f, o_b, bon_f, bon_b, yb, proj, x, mod_rows, tok_per_row, wgu, avg, gnw, gnb, wout, l1w, l1b,
            wr, br, u2_all, tok_base, ntok_all):
    ntok = x.shape[0]
    tm = 256
    per = tok_per_row // tm
    base_tiles = tok_base // tm
    alias = u2_all is not None
    ntiles = ntok // tm

    def wspec(shape):
        return pl.BlockSpec(shape, lambda i: (0,) * len(shape))

    def tl(i):
        return jnp.minimum(i, ntiles - 1)

    return pl.pallas_call(
        _mixout_alias_kernel if alias else functools.partial(_mixout_first_kernel, ntiles=ntiles),
        grid=(ntiles if alias else ntok_all // tm,),
        input_output_aliases={17: 1} if alias else {},
        in_specs=[pl.BlockSpec((tm, D_A), lambda i: (tl(i), 0)),
                  pl.BlockSpec((tm, D_A), lambda i: (tl(i), 0)),
                  pl.BlockSpec((tm, D_A), lambda i: (tl(i), 0)),
                  pl.BlockSpec((tm, D_A), lambda i: (tl(i), 0)),
                  pl.BlockSpec((tm, D_B), lambda i: (tl(i), 0)),
                  pl.BlockSpec((tm, LORA_G), lambda i: (tl(i), COL_LG)),
                  pl.BlockSpec((tm, D_MODEL), lambda i: (tl(i), 0)),
                  pl.BlockSpec((1, 6, D_MODEL), lambda i: (tl(i) // per, 0, 0)),
                  wspec((LORA_G, D_A)), wspec((D_A, D_A)), wspec((1, D_A)), wspec((1, D_A)),
                  wspec((D_MODEL, D_MODEL)), wspec((1, D_MODEL)), wspec((1, D_MODEL)),
                  wspec((D_MODEL, N_EXPERTS)), wspec((1, N_EXPERTS))]
                 + ([pl.BlockSpec(memory_space=pl.ANY)] if alias else []),
        out_specs=[pl.BlockSpec((tm, D_MODEL), lambda i: (tl(i), 0)),
                   pl.BlockSpec((tm * ROW_TILE, 128), lambda i: (i + base_tiles, 0)),
                   pl.BlockSpec((tm, TOP_K), lambda i: (tl(i), 0)),
                   pl.BlockSpec((tm, TOP_K), lambda i: (tl(i), 0))],
        out_shape=[jax.ShapeDtypeStruct((ntok, D_MODEL), F32),
                   jax.ShapeDtypeStruct((ntok_all * ROW_TILE, 128), F32),
                   jax.ShapeDtypeStruct((ntok, TOP_K), jnp.int32),
                   jax.ShapeDtypeStruct((ntok, TOP_K), F32)],
        compiler_params=_cparams(("arbitrary",)),
        name="mixout",
    )(o_f, o_b, bon_f, bon_b, yb, proj, x, mod_rows, wgu, avg, gnw, gnb, wout, l1w, l1b, wr, br,
      *([u2_all] if alias else []))


MOE_TM = 256
ROUTE_BLK = 256
INVERT_CHUNK = 2048


def _route_kernel(eid_ref, slot_ref, te_ref, nv_ref, c_scr, *, ntok, nt_pad):
    nblk = ntok // ROUTE_BLK
    lane32 = lax.broadcasted_iota(jnp.int32, (ROUTE_BLK, N_EXPERTS), 1)
    lane4 = lax.broadcasted_iota(jnp.int32, (ROUTE_BLK, TOP_K), 1)
    rowb = lax.broadcasted_iota(jnp.int32, (ROUTE_BLK, ROUTE_BLK), 0)
    colb = lax.broadcasted_iota(jnp.int32, (ROUTE_BLK, ROUTE_BLK), 1)
    tri = jnp.where(rowb > colb, 1.0, 0.0).astype(BF16)

    def count(b, carry):
        rows = pl.ds(pl.multiple_of(b * ROUTE_BLK, ROUTE_BLK), ROUTE_BLK)
        e = eid_ref[rows, :]
        hot = jnp.zeros((ROUTE_BLK, N_EXPERTS), F32)
        for j in range(TOP_K):
            hot = hot + jnp.where(lane32 == e[:, j:j + 1], 1.0, 0.0)
        c_scr[rows, :] = _mm(tri, hot.astype(BF16)) + carry
        return carry + jnp.sum(hot, axis=0, keepdims=True)

    n = lax.fori_loop(0, nblk, count, jnp.zeros((1, N_EXPERTS), F32))
    padded = jnp.floor((n + (MOE_TM - 1)) * (1.0 / MOE_TM)) * MOE_TM
    r32 = lax.broadcasted_iota(jnp.int32, (N_EXPERTS, N_EXPERTS), 0)
    c32 = lax.broadcasted_iota(jnp.int32, (N_EXPERTS, N_EXPERTS), 1)
    upper = jnp.where(r32 < c32, 1.0, 0.0).astype(BF16)
    off = _dot_xb(jnp.broadcast_to(padded, (8, N_EXPERTS)), upper)[0:1]
    gend = off + padded

    def place(b, carry):
        rows = pl.ds(pl.multiple_of(b * ROUTE_BLK, ROUTE_BLK), ROUTE_BLK)
        e = eid_ref[rows, :]
        base = off + c_scr[rows, :]
        s = jnp.zeros((ROUTE_BLK, TOP_K), F32)
        for j in range(TOP_K):
            sj = jnp.sum(jnp.where(lane32 == e[:, j:j + 1], base, 0.0), axis=1, keepdims=True)
            s = jnp.where(lane4 == j, sj, s)
        slot_ref[rows, :] = s.astype(jnp.int32)
        return carry

    lax.fori_loop(0, nblk, place, 0)

    gcol = jnp.sum(jnp.where(r32 == c32, jnp.broadcast_to(gend, (N_EXPERTS, N_EXPERTS)), 0.0),
                   axis=1, keepdims=True)
    tstart = (lax.broadcasted_iota(jnp.int32, (N_EXPERTS, nt_pad), 1) * MOE_TM).astype(F32)
    te = jnp.sum(jnp.where(gcol <= tstart, 1.0, 0.0), axis=0, keepdims=True)
    te_ref[...] = jnp.minimum(te, N_EXPERTS - 1.0).astype(jnp.int32)
    total = jnp.sum(padded, axis=1, keepdims=True)
    nv_ref[...] = jnp.broadcast_to(total * (1.0 / MOE_TM), (1, 128)).astype(jnp.int32)


def _route(eid):
    ntok = eid.shape[0]
    nt_max = ntok * TOP_K // MOE_TM + N_EXPERTS
    nt_pad = -(-(nt_max + 1) // 128) * 128
    full = lambda shape: pl.BlockSpec(shape, lambda i: (0,) * len(shape))
    slot, te, nv = pl.pallas_call(
        functools.partial(_route_kernel, ntok=ntok, nt_pad=nt_pad),
        grid=(1,),
        in_specs=[full((ntok, TOP_K))],
        out_specs=[full((ntok, TOP_K)), full((1, nt_pad)), full((1, 128))],
        out_shape=[jax.ShapeDtypeStruct((ntok, TOP_K), jnp.int32),
                   jax.ShapeDtypeStruct((1, nt_pad), jnp.int32),
                   jax.ShapeDtypeStruct((1, 128), jnp.int32)],
        scratch_shapes=[pltpu.VMEM((ntok, N_EXPERTS), F32)],
        compiler_params=_cparams(("arbitrary",)),
        name="moe_route",
    )(eid)
    return slot.reshape(ntok * TOP_K), te.reshape(nt_pad), nv.reshape(128), nt_max


def _invert_kernel(slot_ref, inv_ref, fill_vmem, slot_smem, inv_smem, sem_in, sem_out, *, nassign):
    slot_id = lax.broadcasted_iota(jnp.int32, fill_vmem.shape, 0)
    fill_vmem[...] = jnp.bitwise_and(slot_id, MOE_TM - 1) * ROW_TILE + TOP_K
    fill = pltpu.make_async_copy(fill_vmem, inv_smem, sem_out)
    fill.start()
    fill.wait()

    def chunk(ci, c):
        a0 = pl.multiple_of(ci * INVERT_CHUNK, INVERT_CHUNK)
        cp = pltpu.make_async_copy(slot_ref.at[pl.ds(a0, INVERT_CHUNK)], slot_smem, sem_in)
        cp.start()
        cp.wait()

        def put(tt, c2):
            base = (lax.shift_right_logical(a0, 2) + tt) * ROW_TILE
            for k in range(TOP_K):
                inv_smem[slot_smem[tt * TOP_K + k]] = base + k
            return c2

        lax.fori_loop(0, INVERT_CHUNK // TOP_K, put, 0, unroll=2)
        return c

    lax.fori_loop(0, nassign // INVERT_CHUNK, chunk, 0)
    out = pltpu.make_async_copy(inv_smem, inv_ref, sem_out)
    out.start()
    out.wait()


def _invert(slot, ntiles):
    nassign = slot.shape[0]
    anyspec = pl.BlockSpec(memory_space=pl.ANY)
    return pl.pallas_call(
        functools.partial(_invert_kernel, nassign=nassign),
        grid=(1,),
        in_specs=[anyspec], out_specs=anyspec,
        out_shape=jax.ShapeDtypeStruct((ntiles * MOE_TM,), jnp.int32),
        scratch_shapes=[pltpu.VMEM((ntiles * MOE_TM,), jnp.int32),
                        pltpu.SMEM((INVERT_CHUNK,), jnp.int32), pltpu.SMEM((ntiles * MOE_TM,), jnp.int32),
                        pltpu.SemaphoreType.DMA(()), pltpu.SemaphoreType.DMA(())],
        compiler_params=_cparams(("arbitrary",)),
        name="moe_invert",
    )(slot)


def _expert_kernel(te_ref, nv_ref, inv_ref, u2_ref, wg_ref, bg_ref, wu_ref, bu_ref, wd_ref, bd_ref, y4_ref,
                   xbuf, ybuf, wg_b, wu_b, wd_b, sem_g, sem_s, *, ntok):
    i = pl.program_id(0)
    nvalid = nv_ref[0]
    valid = i < nvalid
    trash = TOP_K * ntok

    def rows8(row):
        return pl.ds(pl.multiple_of(row, ROW_TILE), ROW_TILE)

    def gather_copy(tile, r, buf):
        src = jnp.bitwise_and(inv_ref[tile * MOE_TM + r], -ROW_TILE)
        return pltpu.make_async_copy(u2_ref.at[rows8(src)], xbuf.at[buf, _tok_rows(r)], sem_g.at[buf])

    def scatter_copy(tile, r):
        code = inv_ref[tile * MOE_TM + r]
        dst = jnp.bitwise_and(code, ROW_TILE - 1) * (ntok * ROW_TILE) + jnp.bitwise_and(code, -ROW_TILE)
        return pltpu.make_async_copy(ybuf.at[_tok_rows(r)], y4_ref.at[rows8(dst)], sem_s)

    def wait_gather(buf):
        pltpu.make_async_copy(xbuf.at[buf], xbuf.at[buf], sem_g.at[buf]).wait()

    def wait_scatter():
        pltpu.make_async_copy(ybuf, ybuf, sem_s).wait()

    @pl.when(i == 0)
    def _():
        def first(r2, c):
            gather_copy(0, 2 * r2, 0).start(priority=0)
            gather_copy(0, 2 * r2 + 1, 0).start(priority=1)
            return c

        lax.fori_loop(0, MOE_TM // 2, first, 0)
        ybuf[...] = jnp.zeros_like(ybuf)
        spare = pltpu.make_async_copy(ybuf, y4_ref.at[_tok_rows(trash, MOE_TM)], sem_s)
        spare.start()
        spare.wait()

    fresh = jnp.logical_or(i == 0, te_ref[i] != te_ref[jnp.maximum(i - 1, 0)])

    @pl.when(jnp.logical_and(valid, fresh))
    def _():
        wg_b[...] = wg_ref[0].astype(BF16)
        wu_b[...] = wu_ref[0].astype(BF16)
        wd_b[...] = wd_ref[0].astype(BF16)

    @pl.when(valid)
    def _():
        buf = jnp.bitwise_and(i, 1)
        nxt = jnp.minimum(i + 1, nvalid - 1)
        wait_gather(buf)
        x = _load_tok_tiles(xbuf, MOE_TM, (buf,)).astype(BF16)
        FC = 256
        nfc = D_FF // FC
        per = MOE_TM // nfc
        y = jnp.zeros((MOE_TM, D_MODEL), F32)
        for j in range(nfc):
            for r in range(j * per, (j + 1) * per):
                gather_copy(nxt, r, 1 - buf).start(priority=r % 2)
            cs = slice(j * FC, (j + 1) * FC)
            gate = jnp.minimum(_mm(x, wg_b[:, cs]) + bg_ref[0, :, cs], SWIGLU_LIMIT)
            up = jnp.clip(_mm(x, wu_b[:, cs]) + bu_ref[0, :, cs], -SWIGLU_LIMIT, SWIGLU_LIMIT)
            h = (up + 1.0) * gate * jax.nn.sigmoid(SWIGLU_ALPHA * gate)
            y = y + _mm(h.astype(BF16), wd_b[cs, :])

        @pl.when(i > 0)
        def _():
            wait_scatter()

        _store_tok_tiles(ybuf, y + bd_ref[0])
        for r in range(MOE_TM):
            scatter_copy(i, r).start(priority=r % 2)

    @pl.when(i == nvalid)
    def _():
        wait_gather(jnp.bitwise_and(nvalid, 1))
        wait_scatter()


def _experts(te, nv, inv, u2_all, nt_max, wg, bg, wu, bu, wd, bd):
    ntok = u2_all.shape[0] // ROW_TILE

    def tile(i, te, nv, inv):
        return te[jnp.minimum(i, nv[0] - 1)]

    wspec = pl.BlockSpec((1, D_MODEL, D_FF), lambda i, te, nv, inv: (tile(i, te, nv, inv), 0, 0))
    bspec = pl.BlockSpec((1, 1, D_FF), lambda i, te, nv, inv: (tile(i, te, nv, inv), 0, 0))
    anyspec = pl.BlockSpec(memory_space=pl.ANY)
    return pl.pallas_call(
        functools.partial(_expert_kernel, ntok=ntok),
        grid_spec=pltpu.PrefetchScalarGridSpec(
            num_scalar_prefetch=3, grid=(nt_max + 1,),
            in_specs=[anyspec, wspec, bspec, wspec, bspec, wspec, bspec], out_specs=anyspec,
            scratch_shapes=[pltpu.VMEM((2, MOE_TM * ROW_TILE, 128), F32),
                            pltpu.VMEM((MOE_TM * ROW_TILE, 128), F32),
                            pltpu.VMEM((D_MODEL, D_FF), BF16), pltpu.VMEM((D_MODEL, D_FF), BF16),
                            pltpu.VMEM((D_MODEL, D_FF), BF16),
                            pltpu.SemaphoreType.DMA((2,)), pltpu.SemaphoreType.DMA(())]),
        out_shape=jax.ShapeDtypeStruct(((TOP_K * ntok + MOE_TM) * ROW_TILE, 128), F32),
        compiler_params=_cparams(("arbitrary",)),
        name="moe_experts",
    )(te, nv, inv, u2_all, wg, bg.reshape(N_EXPERTS, 1, D_FF), wu, bu.reshape(N_EXPERTS, 1, D_FF),
      wd, bd.reshape(N_EXPERTS, 1, D_MODEL))


def _combine_kernel(y0_ref, y1_ref, y2_ref, y3_ref, prob_ref, x1_ref, mod_ref, w_ref, b_ref, o_ref):
    p = prob_ref[...]
    tm = p.shape[0]
    moe = p[:, 0:1] * _load_tok_tiles(y0_ref, tm)
    for j, y_ref in enumerate((y1_ref, y2_ref, y3_ref), start=1):
        moe = moe + p[:, j:j + 1] * _load_tok_tiles(y_ref, tm)
    m = mod_ref[0]
    o_ref[...] = _layer_norm(DEEPNORM_ALPHA * x1_ref[...] + m[5:6] * moe) * w_ref[...] + b_ref[...]


def _combine(y4, prob, x1, tok_base, ntok_all, mod_rows, tok_per_row, w, b):
    ntok = x1.shape[0]
    tm = 256
    per = tok_per_row // tm

    def yspec(k):
        return pl.BlockSpec((tm * ROW_TILE, 128), lambda i: ((k * ntok_all + tok_base) // tm + i, 0))

    return pl.pallas_call(
        _combine_kernel,
        grid=(ntok // tm,),
        in_specs=[yspec(0), yspec(1), yspec(2), yspec(3),
                  pl.BlockSpec((tm, TOP_K), lambda i: (i, 0)),
                  pl.BlockSpec((tm, D_MODEL), lambda i: (i, 0)),
                  pl.BlockSpec((1, 6, D_MODEL), lambda i: (i // per, 0, 0)),
                  pl.BlockSpec((1, D_MODEL), lambda i: (0, 0)),
                  pl.BlockSpec((1, D_MODEL), lambda i: (0, 0))],
        out_specs=pl.BlockSpec((tm, D_MODEL), lambda i: (i, 0)),
        out_shape=jax.ShapeDtypeStruct((ntok, D_MODEL), F32),
        compiler_params=_cparams(("arbitrary",)),
        name="moe_combine",
    )(y4, y4, y4, y4, prob, x1, mod_rows, w, b)


def _block_diag(w):
    nb, bb, _ = w.shape
    eye = jnp.eye(nb, dtype=w.dtype)
    return jnp.einsum('nij,nm->nimj', w, eye).reshape(nb * bb, nb * bb)


def kernel(x_prompt, x_sample, state_rwkv, state_rglru, c, c_ctx, w_mod, b_mod, w_in, w_decay_up, b_decay, w_iclr_up, b_iclr, w_gate_up, k_k, k_a, r_k, gn_w, gn_b, conv_w, conv_b, w_rg_a, b_rg_a, w_rg_i, b_rg_i, lam, w_out, ln1_w, ln1_b, w_router, b_router, w_e_gate, b_e_gate, w_e_up, b_e_up, w_e_down, b_e_down, ln2_w, ln2_b):
    Bp, Tp, D = x_prompt.shape
    Bs, Ts, _ = x_sample.shape
    l = 0
    row = lambda a: a[l].reshape(1, -1)

    cond8 = jnp.concatenate([c_ctx[None, :], c, jnp.zeros((8 - 1 - Bs, D), F32)], axis=0)
    mod = _modulation(cond8, w_mod[l], b_mod[l]).reshape(8, 6, D)
    mod_p, mod_s = mod[0:1], mod[1:1 + Bs]

    w_in_b = w_in[l].astype(BF16)
    w_out_b = w_out[l].astype(BF16)
    head_id = jnp.arange(D_A) // HEAD
    hs = (head_id[:, None] == head_id[None, :]).astype(BF16)
    avg = (hs.astype(F32) / HEAD).astype(BF16)
    wdec = w_decay_up[l].reshape(2 * LORA, D_A)
    wic = w_iclr_up[l].reshape(2 * LORA, D_A)
    wa_bd = jnp.stack([_block_diag(w_rg_a[l, 0]), _block_diag(w_rg_a[l, 1])])
    wi_bd = jnp.stack([_block_diag(w_rg_i[l, 0]), _block_diag(w_rg_i[l, 1])])

    xp = x_prompt.reshape(Bp * Tp, D)
    xs = x_sample.reshape(Bs * Ts, D)

    s0_s = state_rwkv[:, l].reshape(Bs, 2, N_PAIRS, 2, HEAD, HEAD).transpose(0, 1, 2, 4, 3, 5)
    s0_s = s0_s.reshape(Bs, 2, N_PAIRS, HEAD, 2 * HEAD)
    s0_p = jnp.zeros((Bp, 2, N_PAIRS, HEAD, 2 * HEAD), F32)
    h0_p = jnp.zeros((Bp, 2, D_B), F32)
    h0_s = state_rglru[:, l]

    ntok_all = Bp * Tp + Bs * Ts
    outs = []
    u2_all = None
    for x, mod_rows, tok_per_row, nseq, T, s0, h0, lrow, tok_base in (
            (xp, mod_p, Bp * Tp, Bp, Tp, s0_p, h0_p, Tp, 0),
            (xs, mod_s, Ts, Bs, Ts, s0_s, h0_s, GRID_W, Bp * Tp)):
        proj = _inproj(x, mod_rows, w_in_b, tok_per_row)
        o_f, o_b, bon_f, bon_b, s_fin = _rwkv(proj, s0, wdec, b_decay[l], wic, b_iclr[l], row(k_k), row(k_a),
                                              r_k[l].reshape(1, D_A), hs, nseq, T)
        yb, h_fin = _rglru(proj, h0, conv_w[l], row(conv_b), wa_bd, b_rg_a[l], wi_bd, b_rg_i[l], lam[l],
                           nseq, T, lrow)
        x1, u2_all, eid, prob = _mixout(o_f, o_b, bon_f, bon_b, yb, proj, x, mod_rows, tok_per_row,
                                        w_gate_up[l], avg, row(gn_w), row(gn_b), w_out_b, row(ln1_w),
                                        row(ln1_b), w_router[l], row(b_router), u2_all, tok_base, ntok_all)
        outs.append((x1, eid, prob, s_fin, h_fin))

    (x1_p, eid_p, prob_p, sfin_p, hfin_p), (x1_s, eid_s, prob_s, _, _) = outs
    slot, te, nv, nt_max = _route(jnp.concatenate([eid_p, eid_s], axis=0))
    inv = _invert(slot, nt_max)
    y4 = _experts(te, nv, inv, u2_all, nt_max, w_e_gate[l], b_e_gate[l], w_e_up[l], b_e_up[l], w_e_down[l],
                  b_e_down[l])
    y_p = _combine(y4, prob_p, x1_p, 0, ntok_all, mod_p, Bp * Tp, row(ln2_w), row(ln2_b))
    y_s = _combine(y4, prob_s, x1_s, Bp * Tp, ntok_all, mod_s, Ts, row(ln2_w), row(ln2_b))
    return (y_p.reshape(Bp, Tp, D), y_s.reshape(Bs, Ts, D),
            sfin_p[:, None], hfin_p[:, None])
```

```python
import functools
import math

import jax
import jax.numpy as jnp
from jax import lax
from jax.experimental import pallas as pl
from jax.experimental.pallas import tpu as pltpu

F32 = jnp.float32
BF16 = jnp.bfloat16

D_MODEL = 1024
D_A = 512
D_B = 512
HEAD = 64
N_HEADS = 8
N_PAIRS = N_HEADS // 2
GRID_W = 64
LORA = 64
LORA_G = 128
RGLRU_C = 8.0
N_EXPERTS = 32
TOP_K = 4
D_FF = 1024
SWIGLU_LIMIT = 7.0
SWIGLU_ALPHA = 1.702
LN_EPS = 1e-5
GN_EPS = 1e-5 * HEAD
D_IN = 3 * D_A + 2 * D_B + 2 * LORA + 2 * LORA + LORA_G
DEPTH = 1
DEEPNORM_ALPHA = (2 * DEPTH) ** 0.25

COL_R, COL_K, COL_V, COL_XR, COL_XGB = 0, 1, 2, 3, 4
COL_LW, COL_LA, COL_LG = 20, 21, 22

CHUNK = 64
VMEM_LIMIT = 56 * 1024 * 1024

_NN = (((1,), (0,)), ((), ()))
_NT = (((1,), (1,)), ((), ()))


def _mm(a, b, dims=_NN):
    return lax.dot_general(a, b, dims, preferred_element_type=F32)


def _split2(x):
    hi = x.astype(BF16)
    lo = (x - hi.astype(F32)).astype(BF16)
    return hi, lo


def _split3(x):
    hi = x.astype(BF16)
    r1 = x - hi.astype(F32)
    mid = r1.astype(BF16)
    lo = (r1 - mid.astype(F32)).astype(BF16)
    return hi, mid, lo


def _dot1(a, b, dims=_NN):
    return _mm(a.astype(BF16), b.astype(BF16), dims)


def _dot3(a, b):
    ah, al = _split2(a)
    bh, bl = _split2(b)
    return _mm(jnp.concatenate([ah, ah, al], axis=1), jnp.concatenate([bh, bl, bh], axis=0))


def _dot_xb(a, b_bf16):
    return _mm(jnp.concatenate(_split3(a), axis=1), jnp.concatenate([b_bf16] * 3, axis=0))


def _dot_xb2(a, b_bf16):
    return _mm(jnp.concatenate(_split2(a), axis=1), jnp.concatenate([b_bf16] * 2, axis=0))


def _dot_xa(a_bf16, b):
    return _mm(jnp.concatenate([a_bf16] * 3, axis=1), jnp.concatenate(_split3(b), axis=0))


ROW_TILE = 8


def _tok_rows(t, n=1):
    return pl.ds(pl.multiple_of(t * ROW_TILE, ROW_TILE), n * ROW_TILE)


def _load_tok_tiles(ref, ntok, lead=()):
    return jnp.concatenate([ref[lead + (pl.ds(s_, ntok, stride=ROW_TILE), slice(None))]
                            for s_ in range(ROW_TILE)], axis=1)


def _store_tok_tiles(ref, x):
    ntok = x.shape[0]
    for s_ in range(ROW_TILE):
        ref[pl.ds(s_, ntok, stride=ROW_TILE), :] = x[:, 128 * s_:128 * (s_ + 1)]


def _layer_norm(x):
    mu = jnp.mean(x, axis=-1, keepdims=True)
    xc = x - mu
    var = jnp.mean(xc * xc, axis=-1, keepdims=True)
    return xc * lax.rsqrt(var + LN_EPS)


def _cparams(sem):
    return pltpu.CompilerParams(dimension_semantics=sem, vmem_limit_bytes=VMEM_LIMIT)


def _mod_kernel(c_ref, w_ref, b_ref, o_ref):
    c = c_ref[...]
    s = c * jax.nn.sigmoid(c)
    o_ref[...] = _dot3(s, w_ref[...]) + b_ref[...]


def _modulation(cond8, w_mod, b_mod):
    n = w_mod.shape[1]
    tn = 1024
    return pl.pallas_call(
        _mod_kernel,
        grid=(n // tn,),
        in_specs=[pl.BlockSpec((8, D_MODEL), lambda j: (0, 0)),
                  pl.BlockSpec((D_MODEL, tn), lambda j: (0, j)),
                  pl.BlockSpec((1, tn), lambda j: (0, j))],
        out_specs=pl.BlockSpec((8, tn), lambda j: (0, j)),
        out_shape=jax.ShapeDtypeStruct((8, n), F32),
        compiler_params=_cparams(("arbitrary",)),
        name="modulation",
    )(cond8, w_mod, b_mod.reshape(1, n))


def _inproj_kernel(x_ref, mod_ref, w_ref, o_ref):
    m = mod_ref[0]
    u = _layer_norm(x_ref[...]) * (1.0 + m[1:2]) + m[0:1]
    o_ref[...] = _dot1(u, w_ref[...])


def _inproj(x, mod_rows, w_in_bf16, tok_per_row):
    ntok = x.shape[0]
    tm = 512
    per = tok_per_row // tm
    return pl.pallas_call(
        _inproj_kernel,
        grid=(ntok // tm,),
        in_specs=[pl.BlockSpec((tm, D_MODEL), lambda i: (i, 0)),
                  pl.BlockSpec((1, 6, D_MODEL), lambda i: (i // per, 0, 0)),
                  pl.BlockSpec((D_MODEL, D_IN), lambda i: (0, 0))],
        out_specs=pl.BlockSpec((tm, D_IN), lambda i: (i, 0)),
        out_shape=jax.ShapeDtypeStruct((ntok, D_IN), F32),
        compiler_params=_cparams(("arbitrary",)),
        name="inproj",
    )(x, mod_rows, w_in_bf16)


def _L(s):
    hi, lo = s
    return jnp.concatenate([hi, hi, lo], axis=1)


def _R(s):
    hi, lo = s
    return jnp.concatenate([hi, lo, hi], axis=0)


def _rwkv_kernel(rf_ref, kf_ref, vf_ref, lwf_ref, laf_ref, rb_ref, kb_ref, vb_ref, lwb_ref, lab_ref,
                 s0_ref, wdec_ref, bdec_ref, wic_ref, bic_ref, kkw_ref, ka_ref, rk_ref, hs_ref,
                 of_ref, ob_ref, bonf_ref, bonb_ref, sfin_ref,
                 s_scr, t_scr, w2_scr, mr_scr, ar_scr, bk_scr, gt_scr, *, G, ng):
    L = CHUNK
    g = pl.program_id(1)
    ins = ((rf_ref, kf_ref, vf_ref, lwf_ref, laf_ref), (rb_ref, kb_ref, vb_ref, lwb_ref, lab_ref))
    outs = ((of_ref, bonf_ref), (ob_ref, bonb_ref))

    @pl.when(g == 0)
    def _():
        s_scr[...] = s0_ref[0]

    hs = hs_ref[...]
    lane = lax.broadcasted_iota(jnp.int32, (L, 128), 1)
    rowp = lax.broadcasted_iota(jnp.int32, (L, 128), 0)
    colp = jnp.bitwise_and(lane, 63)
    h0 = lane < 64
    m0 = jnp.where(h0, 1.0, 0.0).astype(BF16)
    m1 = jnp.where(h0, 0.0, 1.0).astype(BF16)
    eye2 = jnp.where(rowp == colp, 1.0, 0.0)
    rowL = lax.broadcasted_iota(jnp.int32, (L, L), 0)
    colL = lax.broadcasted_iota(jnp.int32, (L, L), 1)

    def same(sh):
        return lax.shift_right_logical(rowp, sh) == lax.shift_right_logical(colp, sh)

    same8, same16, same32 = same(3), same(4), same(5)

    def bmask(c):
        return jnp.where(c, 1.0, 0.0).astype(BF16)

    mk8 = bmask(same8)
    merge_masks = (bmask(same16 & (~same8)), bmask(same32 & (~same16)), bmask(~same32))

    def smb(x):
        return jnp.concatenate([x * m0, x * m1], axis=0)

    def sms(s):
        return tuple(smb(x) for x in s)

    def pm(xs, ys):
        return _mm(_L(xs), _R(sms(ys)))

    chains = [(d, p) for d in range(2) for p in range(N_PAIRS)]

    def each(fn, *lists):
        return [fn(*a) for a in zip(*lists)]

    def prep(j, carry):
        pre = []
        for d in range(2):
            r_ref, k_ref, v_ref, lw_ref, la_ref = ins[d]
            jn = j if d == 0 else G - 1 - j
            rows = pl.ds(pl.multiple_of(jn * L, L), L)
            u = d * G + j
            r = r_ref[rows, :]
            k = k_ref[rows, :]
            v = v_ref[rows, :]
            dsel = (lane >= 64) if d else h0
            tl = jnp.where(dsel, jnp.tanh(lw_ref[rows, :]), 0.0)
            dw = _dot3(tl, wdec_ref[...]) + bdec_ref[d:d + 1, :]
            logw = (-math.exp(-0.5)) * jax.nn.sigmoid(dw)
            la_m = jnp.where(dsel, la_ref[rows, :], 0.0)
            iclr = jax.nn.sigmoid(_dot3(la_m, wic_ref[...]) + bic_ref[d:d + 1, :])
            kkr = k * kkw_ref[...]
            nrm = jnp.sqrt(_dot_xb2(kkr * kkr, hs))
            kk = kkr / jnp.maximum(nrm, 1e-12)
            kdir = k * (1.0 + (iclr - 1.0) * ka_ref[...])
            bv = kk * iclr
            outs[d][1][rows, :] = _dot_xb2(r * kdir * rk_ref[...], hs) * v

            tri = bmask((rowL <= colL) if d else (rowL >= colL))
            cs = _dot_xa(tri, logw)
            ctot = jnp.sum(logw, axis=0, keepdims=True)
            g_inv = jnp.exp(-cs)
            g_rem = jnp.exp(ctot - cs)
            gt_scr[u] = jnp.exp(ctot)
            At = -kk * jnp.exp(cs - logw)
            Rt = r * jnp.exp(cs)
            Bt = bv * g_inv
            Kt = kdir * g_inv
            Bg = bv * g_rem
            Kg = kdir * g_rem
            for p in range(N_PAIRS):
                sl = slice(128 * p, 128 * (p + 1))
                ar = jnp.concatenate([At[:, sl], Rt[:, sl]], axis=0).astype(BF16)
                ar_scr[u, p] = ar
                bk_scr[u, p] = jnp.concatenate([Bg[:, sl], Kg[:, sl]], axis=0).astype(BF16)
                pre.append((ar, Bt[:, sl].astype(BF16), Kt[:, sl].astype(BF16), v[:, sl].astype(BF16)))

        def stage_m(c, pr):
            d = c[0]
            ar, bt, kt, _ = pr
            strict2 = (rowp < colp) if d else (rowp > colp)
            incl2 = (rowp <= colp) if d else (rowp >= colp)
            M = _mm(ar, jnp.concatenate([smb(bt), smb(kt)], axis=0), _NT)
            N = jnp.where(strict2, M[:L, :128], 0.0)
            Mak = jnp.where(strict2, M[:L, 128:], 0.0)
            mr_scr[d * G + j, c[1]] = jnp.concatenate([jnp.where(incl2, M[L:, :128], 0.0),
                                                       jnp.where(incl2, M[L:, 128:], 0.0)], axis=1).astype(BF16)
            return N, Mak

        NM = each(stage_m, chains, pre)
        Ns = [_split2(nm[0]) for nm in NM]
        N0s = [(n[0] * mk8, n[1] * mk8) for n in Ns]
        T = [eye2 + jnp.where(same8, nm[0], 0.0) for nm in NM]
        P2s = [_split2(pm(n, n)) for n in N0s]
        W2 = [_mm(nm[1].astype(BF16), smb(pr[3])) for nm, pr in zip(NM, pre)]
        T = [t + pm(_split2(t), p2) for t, p2 in zip(T, P2s)]
        P4s = [_split2(pm(p2, p2)) for p2 in P2s]
        T = [t + pm(_split2(t), p4) for t, p4 in zip(T, P4s)]
        for mk in merge_masks:
            Ts = [_split2(t) for t in T]
            Y = [pm(ts, (n[0] * mk, n[1] * mk)) for ts, n in zip(Ts, Ns)]
            T = [t + pm(_split2(y), ts) for t, y, ts in zip(T, Y, Ts)]
        for (d, p), t, w2 in zip(chains, T, W2):
            t_scr[d * G + j, p] = t
            w2_scr[d * G + j, p] = w2
        return carry

    lax.fori_loop(0, G, prep, 0)

    def serial(j, carry):
        us = [d * G + j for d, _ in chains]
        rws = [pl.ds(pl.multiple_of((j if d == 0 else G - 1 - j) * L, L), L) for d, _ in chains]
        sls = [slice(128 * p, 128 * (p + 1)) for _, p in chains]
        S2 = [s_scr[d, p] for d, p in chains]
        XR = [_mm(ar_scr[u, p], smb(s2.astype(BF16)), _NT) for (d, p), u, s2 in zip(chains, us, S2)]
        U = [pm(_split2(t_scr[u, p]), _split2(xr[:L] + w2_scr[u, p])) for (d, p), u, xr in zip(chains, us, XR)]
        V = [ins[d][2][rw, sl] for (d, p), rw, sl in zip(chains, rws, sls)]
        for (d, p), u, rw, sl, s2, xr, uu, vv in zip(chains, us, rws, sls, S2, XR, U, V):
            UVt = jnp.concatenate([uu, vv], axis=0).T
            Rm = _mm(UVt.astype(BF16), bk_scr[u, p])
            s_scr[d, p] = gt_scr[u][:, sl] * s2 + jnp.where(h0, Rm[:HEAD], Rm[HEAD:])
        for (d, p), u, rw, sl, xr, uu, vv in zip(chains, us, rws, sls, XR, U, V):
            rhs = jnp.concatenate([smb(uu.astype(BF16)), smb(vv.astype(BF16))], axis=0)
            outs[d][0][rw, sl] = xr[L:] + _mm(mr_scr[u, p], rhs)
        return carry

    lax.fori_loop(0, G, serial, 0)

    @pl.when(g == ng - 1)
    def _():
        for d in range(2):
            for p in range(N_PAIRS):
                sp = s_scr[d, p]
                sfin_ref[0, d, 2 * p] = sp[:, :HEAD]
                sfin_ref[0, d, 2 * p + 1] = sp[:, HEAD:]


def _rwkv(proj, s0_pairs, wdec, bdec, wic, bic, kkw, ka, rk, hs, nseq, T):
    ntok = proj.shape[0]
    nc = T // CHUNK
    G = min(nc, 8)
    ng = nc // G
    GL = G * CHUNK

    def fwd(b, g):
        return b * ng + g

    def bwd(b, g):
        return b * ng + ng - 1 - g

    def pspecs(rowblk):
        return [pl.BlockSpec((GL, D_A), lambda b, g: (rowblk(b, g), COL_R)),
                pl.BlockSpec((GL, D_A), lambda b, g: (rowblk(b, g), COL_K)),
                pl.BlockSpec((GL, D_A), lambda b, g: (rowblk(b, g), COL_V)),
                pl.BlockSpec((GL, 128), lambda b, g: (rowblk(b, g), COL_LW)),
                pl.BlockSpec((GL, 128), lambda b, g: (rowblk(b, g), COL_LA))]

    def wspec(shape):
        return pl.BlockSpec(shape, lambda b, g: (0,) * len(shape))

    ospec_f = pl.BlockSpec((GL, D_A), lambda b, g: (fwd(b, g), 0))
    ospec_b = pl.BlockSpec((GL, D_A), lambda b, g: (bwd(b, g), 0))
    tok = jax.ShapeDtypeStruct((ntok, D_A), F32)
    return pl.pallas_call(
        functools.partial(_rwkv_kernel, G=G, ng=ng),
        grid=(nseq, ng),
        in_specs=pspecs(fwd) + pspecs(bwd) + [
            pl.BlockSpec((1, 2, N_PAIRS, HEAD, 128), lambda b, g: (b, 0, 0, 0, 0)),
            wspec((128, D_A)), wspec((2, D_A)), wspec((128, D_A)), wspec((2, D_A)),
            wspec((1, D_A)), wspec((1, D_A)), wspec((1, D_A)), wspec((D_A, D_A))],
        out_specs=[ospec_f, ospec_b, ospec_f, ospec_b,
                   pl.BlockSpec((1, 2, N_HEADS, HEAD, HEAD), lambda b, g: (b, 0, 0, 0, 0))],
        out_shape=[tok, tok, tok, tok, jax.ShapeDtypeStruct((nseq, 2, N_HEADS, HEAD, HEAD), F32)],
        scratch_shapes=[pltpu.VMEM((2, N_PAIRS, HEAD, 128), F32),
                        pltpu.VMEM((2 * G, N_PAIRS, CHUNK, 128), F32),
                        pltpu.VMEM((2 * G, N_PAIRS, CHUNK, 128), F32),
                        pltpu.VMEM((2 * G, N_PAIRS, CHUNK, 256), BF16),
                        pltpu.VMEM((2 * G, N_PAIRS, 2 * CHUNK, 128), BF16),
                        pltpu.VMEM((2 * G, N_PAIRS, 2 * CHUNK, 128), BF16),
                        pltpu.VMEM((2 * G, 1, D_A), F32)],
        compiler_params=_cparams(("arbitrary", "arbitrary")),
        name="rwkv_scan",
    )(*([proj] * 10), s0_pairs, wdec, bdec, wic, bic, kkw, ka, rk, hs)


def _gelu_tanh(x):
    return 0.5 * x * (1.0 + jnp.tanh(math.sqrt(2.0 / math.pi) * (x + 0.044715 * (x * x * x))))


def _rglru_kernel(xr_ref, xg_ref, h0_ref, cw_ref, cb_ref, wa_ref, ba_ref, wi_ref, bi_ref, lam_ref,
                  y_ref, hfin_ref, a_scr, u_scr, *, T, lrow):
    RB = min(T, 256)
    nblk = T // RB
    rowi = lax.broadcasted_iota(jnp.int32, (RB, D_B), 0)
    pos = jnp.bitwise_and(rowi, lrow - 1)
    cw = cw_ref[...]
    lam = lam_ref[...]
    sp = jnp.maximum(-lam, 0.0) + jnp.log1p(jnp.exp(-jnp.abs(lam)))

    def gates(blk, carry):
        r0 = pl.multiple_of(blk * RB, RB)
        x = xr_ref[pl.ds(r0, RB), :]
        xm1 = jnp.where(pos >= 1, pltpu.roll(x, 1, 0), 0.0)
        xm2 = jnp.where(pos >= 2, pltpu.roll(x, 2, 0), 0.0)
        xp1 = jnp.where(pos <= lrow - 2, pltpu.roll(x, RB - 1, 0), 0.0)
        xc = xm2 * cw[0:1] + xm1 * cw[1:2] + x * cw[2:3] + xp1 * cw[3:4] + cb_ref[...]
        for dd in range(2):
            gr = jax.nn.sigmoid(_dot1(xc, wa_ref[dd]) + ba_ref[dd:dd + 1, :])
            gi = jax.nn.sigmoid(_dot1(xc, wi_ref[dd]) + bi_ref[dd:dd + 1, :])
            log_a = (-RGLRU_C) * gr * sp[dd:dd + 1, :]
            a = jnp.exp(log_a)
            a_scr[dd, pl.ds(r0, RB), :] = a
            u_scr[dd, pl.ds(r0, RB), :] = jnp.sqrt((1.0 - a) * (1.0 + a)) * gi * xc
        return carry

    lax.fori_loop(0, nblk, gates, 0)

    SUB = 8
    rowt = lax.broadcasted_iota(jnp.int32, (SUB, D_B), 0)

    def tile_scan(a, u, h_in, rev):
        for s_ in (1, 2, 4):
            keep = (rowt < SUB - s_) if rev else (rowt >= s_)
            sh = SUB - s_ if rev else s_
            a_sh = jnp.where(keep, pltpu.roll(a, sh, 0), 1.0)
            u_sh = jnp.where(keep, pltpu.roll(u, sh, 0), 0.0)
            u = a * u_sh + u
            a = a * a_sh
        return a * h_in + u

    def step(i, carry):
        hf, hb = carry
        rf = pl.ds(pl.multiple_of(i * SUB, SUB), SUB)
        hf_tile = tile_scan(a_scr[0, rf, :], u_scr[0, rf, :], hf, False)
        u_scr[0, rf, :] = hf_tile
        rb = pl.ds(pl.multiple_of(T - SUB - i * SUB, SUB), SUB)
        hb_tile = tile_scan(a_scr[1, rb, :], u_scr[1, rb, :], hb, True)
        u_scr[1, rb, :] = hb_tile
        return hf_tile[SUB - 1:SUB], hb_tile[0:1]

    h0 = h0_ref[0]
    hf, hb = lax.fori_loop(0, T // SUB, step, (h0[0:1], h0[1:2]), unroll=2)
    hfin_ref[0] = jnp.concatenate([hf, hb], axis=0)

    def outp(blk, carry):
        r0 = pl.multiple_of(blk * RB, RB)
        h = u_scr[0, pl.ds(r0, RB), :] + u_scr[1, pl.ds(r0, RB), :]
        y_ref[pl.ds(r0, RB), :] = h * _gelu_tanh(xg_ref[pl.ds(r0, RB), :])
        return carry

    lax.fori_loop(0, nblk, outp, 0)


def _rglru(proj, h0, conv_w, conv_b, wa_bd, ba, wi_bd, bi, lam, nseq, T, lrow):
    ntok = proj.shape[0]

    def wspec(shape):
        return pl.BlockSpec(shape, lambda b: (0,) * len(shape))

    return pl.pallas_call(
        functools.partial(_rglru_kernel, T=T, lrow=lrow),
        grid=(nseq,),
        in_specs=[pl.BlockSpec((T, D_B), lambda b: (b, COL_XR)),
                  pl.BlockSpec((T, D_B), lambda b: (b, COL_XGB)),
                  pl.BlockSpec((1, 2, D_B), lambda b: (b, 0, 0)),
                  wspec((4, D_B)), wspec((1, D_B)),
                  wspec((2, D_B, D_B)), wspec((2, D_B)), wspec((2, D_B, D_B)), wspec((2, D_B)),
                  wspec((2, D_B))],
        out_specs=[pl.BlockSpec((T, D_B), lambda b: (b, 0)),
                   pl.BlockSpec((1, 2, D_B), lambda b: (b, 0, 0))],
        out_shape=[jax.ShapeDtypeStruct((ntok, D_B), F32),
                   jax.ShapeDtypeStruct((nseq, 2, D_B), F32)],
        scratch_shapes=[pltpu.VMEM((2, T, D_B), F32), pltpu.VMEM((2, T, D_B), F32)],
        compiler_params=_cparams(("arbitrary",)),
        name="rglru",
    )(proj, proj, h0, conv_w, conv_b, wa_bd, ba, wi_bd, bi, lam)


def _mixout_kernel(of_ref, ob_ref, bf_ref, bb_ref, yb_ref, lg_ref, x_ref, mod_ref, wgu_ref, avg_ref,
                   gnw_ref, gnb_ref, wout_ref, l1w_ref, l1b_ref, wr_ref, br_ref,
                   x1_ref, u2_ref, eid_ref, prob_ref):
    m = mod_ref[0]
    avg = avg_ref[...]
    wkv = of_ref[...] + ob_ref[...]
    mu = _dot_xb2(wkv, avg)
    dv = wkv - mu
    var = _dot_xb2(dv * dv, avg)
    gn = dv * lax.rsqrt(var + GN_EPS) * gnw_ref[...] + gnb_ref[...]
    g = _dot1(jax.nn.sigmoid(lg_ref[...]), wgu_ref[...])
    ya = (gn + (bf_ref[...] + bb_ref[...])) * g
    mix = _dot1(ya, wout_ref[0:D_A, :]) + _dot1(yb_ref[...], wout_ref[D_A:, :])
    x1 = _layer_norm(DEEPNORM_ALPHA * x_ref[...] + m[2:3] * mix) * l1w_ref[...] + l1b_ref[...]
    x1_ref[...] = x1
    u2 = _layer_norm(x1) * (1.0 + m[4:5]) + m[3:4]
    _store_tok_tiles(u2_ref, u2)
    logits = _dot3(u2, wr_ref[...]) + br_ref[...]
    tm = logits.shape[0]
    lane = lax.broadcasted_iota(jnp.int32, (tm, N_EXPERTS), 1)
    lane4 = lax.broadcasted_iota(jnp.int32, (tm, TOP_K), 1)
    work = logits
    tops, idxs = [], []
    for _ in range(TOP_K):
        mx = jnp.max(work, axis=1, keepdims=True)
        idx = jnp.min(jnp.where(work == mx, lane, N_EXPERTS), axis=1, keepdims=True)
        tops.append(mx)
        idxs.append(idx)
        work = jnp.where(lane == idx, -jnp.inf, work)
    es = [jnp.exp(t - tops[0]) for t in tops]
    den = es[0] + es[1] + es[2] + es[3]
    eid = jnp.zeros((tm, TOP_K), jnp.int32)
    prob = jnp.zeros((tm, TOP_K), F32)
    for j in range(TOP_K):
        eid = jnp.where(lane4 == j, idxs[j], eid)
        prob = jnp.where(lane4 == j, es[j] / den, prob)
    eid_ref[...] = eid
    prob_ref[...] = prob


def _mixout_alias_kernel(*refs):
    _mixout_kernel(*refs[:17], *refs[18:])


def _mixout_first_kernel(*refs, ntiles):
    i = pl.program_id(0)

    @pl.when(i < ntiles)
    def _():
        _mixout_kernel(*refs)

    @pl.when(i >= ntiles)
    def _():
        refs[18][...] = jnp.zeros_like(refs[18])


def _mixout(o_f, o_b, bon_f, bon_b, yb, proj, x, mod_rows, tok_per_row, wgu, avg, gnw, gnb, wout, l1w, l1b,
            wr, br, u2_all, tok_base, ntok_all):
    ntok = x.shape[0]
    tm = 256
    per = tok_per_row // tm
    base_tiles = tok_base // tm
    alias = u2_all is not None
    ntiles = ntok // tm

    def wspec(shape):
        return pl.BlockSpec(shape, lambda i: (0,) * len(shape))

    def tl(i):
        return jnp.minimum(i, ntiles - 1)

    return pl.pallas_call(
        _mixout_alias_kernel if alias else functools.partial(_mixout_first_kernel, ntiles=ntiles),
        grid=(ntiles if alias else ntok_all // tm,),
        input_output_aliases={17: 1} if alias else {},
        in_specs=[pl.BlockSpec((tm, D_A), lambda i: (tl(i), 0)),
                  pl.BlockSpec((tm, D_A), lambda i: (tl(i), 0)),
                  pl.BlockSpec((tm, D_A), lambda i: (tl(i), 0)),
                  pl.BlockSpec((tm, D_A), lambda i: (tl(i), 0)),
                  pl.BlockSpec((tm, D_B), lambda i: (tl(i), 0)),
                  pl.BlockSpec((tm, LORA_G), lambda i: (tl(i), COL_LG)),
                  pl.BlockSpec((tm, D_MODEL), lambda i: (tl(i), 0)),
                  pl.BlockSpec((1, 6, D_MODEL), lambda i: (tl(i) // per, 0, 0)),
                  wspec((LORA_G, D_A)), wspec((D_A, D_A)), wspec((1, D_A)), wspec((1, D_A)),
                  wspec((D_MODEL, D_MODEL)), wspec((1, D_MODEL)), wspec((1, D_MODEL)),
                  wspec((D_MODEL, N_EXPERTS)), wspec((1, N_EXPERTS))]
                 + ([pl.BlockSpec(memory_space=pl.ANY)] if alias else []),
        out_specs=[pl.BlockSpec((tm, D_MODEL), lambda i: (tl(i), 0)),
                   pl.BlockSpec((tm * ROW_TILE, 128), lambda i: (i + base_tiles, 0)),
                   pl.BlockSpec((tm, TOP_K), lambda i: (tl(i), 0)),
                   pl.BlockSpec((tm, TOP_K), lambda i: (tl(i), 0))],
        out_shape=[jax.ShapeDtypeStruct((ntok, D_MODEL), F32),
                   jax.ShapeDtypeStruct((ntok_all * ROW_TILE, 128), F32),
                   jax.ShapeDtypeStruct((ntok, TOP_K), jnp.int32),
                   jax.ShapeDtypeStruct((ntok, TOP_K), F32)],
        compiler_params=_cparams(("arbitrary",)),
        name="mixout",
    )(o_f, o_b, bon_f, bon_b, yb, proj, x, mod_rows, wgu, avg, gnw, gnb, wout, l1w, l1b, wr, br,
      *([u2_all] if alias else []))


MOE_TM = 256
ROUTE_BLK = 256
INVERT_CHUNK = 2048


def _route_kernel(eid_ref, slot_ref, te_ref, nv_ref, c_scr, *, ntok, nt_pad):
    nblk = ntok // ROUTE_BLK
    lane32 = lax.broadcasted_iota(jnp.int32, (ROUTE_BLK, N_EXPERTS), 1)
    lane4 = lax.broadcasted_iota(jnp.int32, (ROUTE_BLK, TOP_K), 1)
    rowb = lax.broadcasted_iota(jnp.int32, (ROUTE_BLK, ROUTE_BLK), 0)
    colb = lax.broadcasted_iota(jnp.int32, (ROUTE_BLK, ROUTE_BLK), 1)
    tri = jnp.where(rowb > colb, 1.0, 0.0).astype(BF16)

    def count(b, carry):
        rows = pl.ds(pl.multiple_of(b * ROUTE_BLK, ROUTE_BLK), ROUTE_BLK)
        e = eid_ref[rows, :]
        hot = jnp.zeros((ROUTE_BLK, N_EXPERTS), F32)
        for j in range(TOP_K):
            hot = hot + jnp.where(lane32 == e[:, j:j + 1], 1.0, 0.0)
        c_scr[rows, :] = _mm(tri, hot.astype(BF16)) + carry
        return carry + jnp.sum(hot, axis=0, keepdims=True)

    n = lax.fori_loop(0, nblk, count, jnp.zeros((1, N_EXPERTS), F32))
    padded = jnp.floor((n + (MOE_TM - 1)) * (1.0 / MOE_TM)) * MOE_TM
    r32 = lax.broadcasted_iota(jnp.int32, (N_EXPERTS, N_EXPERTS), 0)
    c32 = lax.broadcasted_iota(jnp.int32, (N_EXPERTS, N_EXPERTS), 1)
    upper = jnp.where(r32 < c32, 1.0, 0.0).astype(BF16)
    off = _dot_xb(jnp.broadcast_to(padded, (8, N_EXPERTS)), upper)[0:1]
    gend = off + padded

    def place(b, carry):
        rows = pl.ds(pl.multiple_of(b * ROUTE_BLK, ROUTE_BLK), ROUTE_BLK)
        e = eid_ref[rows, :]
        base = off + c_scr[rows, :]
        s = jnp.zeros((ROUTE_BLK, TOP_K), F32)
        for j in range(TOP_K):
            sj = jnp.sum(jnp.where(lane32 == e[:, j:j + 1], base, 0.0), axis=1, keepdims=True)
            s = jnp.where(lane4 == j, sj, s)
        slot_ref[rows, :] = s.astype(jnp.int32)
        return carry

    lax.fori_loop(0, nblk, place, 0)

    gcol = jnp.sum(jnp.where(r32 == c32, jnp.broadcast_to(gend, (N_EXPERTS, N_EXPERTS)), 0.0),
                   axis=1, keepdims=True)
    tstart = (lax.broadcasted_iota(jnp.int32, (N_EXPERTS, nt_pad), 1) * MOE_TM).astype(F32)
    te = jnp.sum(jnp.where(gcol <= tstart, 1.0, 0.0), axis=0, keepdims=True)
    te_ref[...] = jnp.minimum(te, N_EXPERTS - 1.0).astype(jnp.int32)
    total = jnp.sum(padded, axis=1, keepdims=True)
    nv_ref[...] = jnp.broadcast_to(total * (1.0 / MOE_TM), (1, 128)).astype(jnp.int32)


def _route(eid):
    ntok = eid.shape[0]
    nt_max = ntok * TOP_K // MOE_TM + N_EXPERTS
    nt_pad = -(-(nt_max + 1) // 128) * 128
    full = lambda shape: pl.BlockSpec(shape, lambda i: (0,) * len(shape))
    slot, te, nv = pl.pallas_call(
        functools.partial(_route_kernel, ntok=ntok, nt_pad=nt_pad),
        grid=(1,),
        in_specs=[full((ntok, TOP_K))],
        out_specs=[full((ntok, TOP_K)), full((1, nt_pad)), full((1, 128))],
        out_shape=[jax.ShapeDtypeStruct((ntok, TOP_K), jnp.int32),
                   jax.ShapeDtypeStruct((1, nt_pad), jnp.int32),
                   jax.ShapeDtypeStruct((1, 128), jnp.int32)],
        scratch_shapes=[pltpu.VMEM((ntok, N_EXPERTS), F32)],
        compiler_params=_cparams(("arbitrary",)),
        name="moe_route",
    )(eid)
    return slot.reshape(ntok * TOP_K), te.reshape(nt_pad), nv.reshape(128), nt_max


def _invert_kernel(slot_ref, inv_ref, fill_vmem, slot_smem, inv_smem, sem_in, sem_out, *, nassign):
    slot_id = lax.broadcasted_iota(jnp.int32, fill_vmem.shape, 0)
    fill_vmem[...] = jnp.bitwise_and(slot_id, MOE_TM - 1) * ROW_TILE + TOP_K
    fill = pltpu.make_async_copy(fill_vmem, inv_smem, sem_out)
    fill.start()
    fill.wait()

    def chunk(ci, c):
        a0 = pl.multiple_of(ci * INVERT_CHUNK, INVERT_CHUNK)
        cp = pltpu.make_async_copy(slot_ref.at[pl.ds(a0, INVERT_CHUNK)], slot_smem, sem_in)
        cp.start()
        cp.wait()

        def put(tt, c2):
            base = (lax.shift_right_logical(a0, 2) + tt) * ROW_TILE
            for k in range(TOP_K):
                inv_smem[slot_smem[tt * TOP_K + k]] = base + k
            return c2

        lax.fori_loop(0, INVERT_CHUNK // TOP_K, put, 0, unroll=2)
        return c

    lax.fori_loop(0, nassign // INVERT_CHUNK, chunk, 0)
    out = pltpu.make_async_copy(inv_smem, inv_ref, sem_out)
    out.start()
    out.wait()


def _invert(slot, ntiles):
    nassign = slot.shape[0]
    anyspec = pl.BlockSpec(memory_space=pl.ANY)
    return pl.pallas_call(
        functools.partial(_invert_kernel, nassign=nassign),
        grid=(1,),
        in_specs=[anyspec], out_specs=anyspec,
        out_shape=jax.ShapeDtypeStruct((ntiles * MOE_TM,), jnp.int32),
        scratch_shapes=[pltpu.VMEM((ntiles * MOE_TM,), jnp.int32),
                        pltpu.SMEM((INVERT_CHUNK,), jnp.int32), pltpu.SMEM((ntiles * MOE_TM,), jnp.int32),
                        pltpu.SemaphoreType.DMA(()), pltpu.SemaphoreType.DMA(())],
        compiler_params=_cparams(("arbitrary",)),
        name="moe_invert",
    )(slot)


def _expert_kernel(te_ref, nv_ref, inv_ref, u2_ref, wg_ref, bg_ref, wu_ref, bu_ref, wd_ref, bd_ref, y4_ref,
                   xbuf, ybuf, wg_b, wu_b, wd_b, sem_g, sem_s, *, ntok):
    i = pl.program_id(0)
    nvalid = nv_ref[0]
    valid = i < nvalid
    trash = TOP_K * ntok

    def rows8(row):
        return pl.ds(pl.multiple_of(row, ROW_TILE), ROW_TILE)

    def gather_copy(tile, r, buf):
        src = jnp.bitwise_and(inv_ref[tile * MOE_TM + r], -ROW_TILE)
        return pltpu.make_async_copy(u2_ref.at[rows8(src)], xbuf.at[buf, _tok_rows(r)], sem_g.at[buf])

    def scatter_copy(tile, r):
        code = inv_ref[tile * MOE_TM + r]
        dst = jnp.bitwise_and(code, ROW_TILE - 1) * (ntok * ROW_TILE) + jnp.bitwise_and(code, -ROW_TILE)
        return pltpu.make_async_copy(ybuf.at[_tok_rows(r)], y4_ref.at[rows8(dst)], sem_s)

    def wait_gather(buf):
        pltpu.make_async_copy(xbuf.at[buf], xbuf.at[buf], sem_g.at[buf]).wait()

    def wait_scatter():
        pltpu.make_async_copy(ybuf, ybuf, sem_s).wait()

    @pl.when(i == 0)
    def _():
        def first(r2, c):
            gather_copy(0, 2 * r2, 0).start(priority=0)
            gather_copy(0, 2 * r2 + 1, 0).start(priority=1)
            return c

        lax.fori_loop(0, MOE_TM // 2, first, 0)
        ybuf[...] = jnp.zeros_like(ybuf)
        spare = pltpu.make_async_copy(ybuf, y4_ref.at[_tok_rows(trash, MOE_TM)], sem_s)
        spare.start()
        spare.wait()

    fresh = jnp.logical_or(i == 0, te_ref[i] != te_ref[jnp.maximum(i - 1, 0)])

    @pl.when(jnp.logical_and(valid, fresh))
    def _():
        wg_b[...] = wg_ref[0].astype(BF16)
        wu_b[...] = wu_ref[0].astype(BF16)
        wd_b[...] = wd_ref[0].astype(BF16)

    @pl.when(valid)
    def _():
        buf = jnp.bitwise_and(i, 1)
        nxt = jnp.minimum(i + 1, nvalid - 1)
        wait_gather(buf)
        x = _load_tok_tiles(xbuf, MOE_TM, (buf,)).astype(BF16)
        FC = 256
        nfc = D_FF // FC
        per = MOE_TM // nfc
        y = jnp.zeros((MOE_TM, D_MODEL), F32)
        for j in range(nfc):
            for r in range(j * per, (j + 1) * per):
                gather_copy(nxt, r, 1 - buf).start(priority=r % 2)
            cs = slice(j * FC, (j + 1) * FC)
            gate = jnp.minimum(_mm(x, wg_b[:, cs]) + bg_ref[0, :, cs], SWIGLU_LIMIT)
            up = jnp.clip(_mm(x, wu_b[:, cs]) + bu_ref[0, :, cs], -SWIGLU_LIMIT, SWIGLU_LIMIT)
            h = (up + 1.0) * gate * jax.nn.sigmoid(SWIGLU_ALPHA * gate)
            y = y + _mm(h.astype(BF16), wd_b[cs, :])

        @pl.when(i > 0)
        def _():
            wait_scatter()

        _store_tok_tiles(ybuf, y + bd_ref[0])
        for r in range(MOE_TM):
            scatter_copy(i, r).start(priority=r % 2)

    @pl.when(i == nvalid)
    def _():
        wait_gather(jnp.bitwise_and(nvalid, 1))
        wait_scatter()


def _experts(te, nv, inv, u2_all, nt_max, wg, bg, wu, bu, wd, bd):
    ntok = u2_all.shape[0] // ROW_TILE

    def tile(i, te, nv, inv):
        return te[jnp.minimum(i, nv[0] - 1)]

    wspec = pl.BlockSpec((1, D_MODEL, D_FF), lambda i, te, nv, inv: (tile(i, te, nv, inv), 0, 0))
    bspec = pl.BlockSpec((1, 1, D_FF), lambda i, te, nv, inv: (tile(i, te, nv, inv), 0, 0))
    anyspec = pl.BlockSpec(memory_space=pl.ANY)
    return pl.pallas_call(
        functools.partial(_expert_kernel, ntok=ntok),
        grid_spec=pltpu.PrefetchScalarGridSpec(
            num_scalar_prefetch=3, grid=(nt_max + 1,),
            in_specs=[anyspec, wspec, bspec, wspec, bspec, wspec, bspec], out_specs=anyspec,
            scratch_shapes=[pltpu.VMEM((2, MOE_TM * ROW_TILE, 128), F32),
                            pltpu.VMEM((MOE_TM * ROW_TILE, 128), F32),
                            pltpu.VMEM((D_MODEL, D_FF), BF16), pltpu.VMEM((D_MODEL, D_FF), BF16),
                            pltpu.VMEM((D_MODEL, D_FF), BF16),
                            pltpu.SemaphoreType.DMA((2,)), pltpu.SemaphoreType.DMA(())]),
        out_shape=jax.ShapeDtypeStruct(((TOP_K * ntok + MOE_TM) * ROW_TILE, 128), F32),
        compiler_params=_cparams(("arbitrary",)),
        name="moe_experts",
    )(te, nv, inv, u2_all, wg, bg.reshape(N_EXPERTS, 1, D_FF), wu, bu.reshape(N_EXPERTS, 1, D_FF),
      wd, bd.reshape(N_EXPERTS, 1, D_MODEL))


def _combine_kernel(y0_ref, y1_ref, y2_ref, y3_ref, prob_ref, x1_ref, mod_ref, w_ref, b_ref, o_ref):
    p = prob_ref[...]
    tm = p.shape[0]
    moe = p[:, 0:1] * _load_tok_tiles(y0_ref, tm)
    for j, y_ref in enumerate((y1_ref, y2_ref, y3_ref), start=1):
        moe = moe + p[:, j:j + 1] * _load_tok_tiles(y_ref, tm)
    m = mod_ref[0]
    o_ref[...] = _layer_norm(DEEPNORM_ALPHA * x1_ref[...] + m[5:6] * moe) * w_ref[...] + b_ref[...]


def _combine(y4, prob, x1, tok_base, ntok_all, mod_rows, tok_per_row, w, b):
    ntok = x1.shape[0]
    tm = 256
    per = tok_per_row // tm

    def yspec(k):
        return pl.BlockSpec((tm * ROW_TILE, 128), lambda i: ((k * ntok_all + tok_base) // tm + i, 0))

    return pl.pallas_call(
        _combine_kernel,
        grid=(ntok // tm,),
        in_specs=[yspec(0), yspec(1), yspec(2), yspec(3),
                  pl.BlockSpec((tm, TOP_K), lambda i: (i, 0)),
                  pl.BlockSpec((tm, D_MODEL), lambda i: (i, 0)),
                  pl.BlockSpec((1, 6, D_MODEL), lambda i: (i // per, 0, 0)),
                  pl.BlockSpec((1, D_MODEL), lambda i: (0, 0)),
                  pl.BlockSpec((1, D_MODEL), lambda i: (0, 0))],
        out_specs=pl.BlockSpec((tm, D_MODEL), lambda i: (i, 0)),
        out_shape=jax.ShapeDtypeStruct((ntok, D_MODEL), F32),
        compiler_params=_cparams(("arbitrary",)),
        name="moe_combine",
    )(y4, y4, y4, y4, prob, x1, mod_rows, w, b)


def _block_diag(w):
    nb, bb, _ = w.shape
    eye = jnp.eye(nb, dtype=w.dtype)
    return jnp.einsum('nij,nm->nimj', w, eye).reshape(nb * bb, nb * bb)


def kernel(x_prompt, x_sample, state_rwkv, state_rglru, c, c_ctx, w_mod, b_mod, w_in, w_decay_up, b_decay, w_iclr_up, b_iclr, w_gate_up, k_k, k_a, r_k, gn_w, gn_b, conv_w, conv_b, w_rg_a, b_rg_a, w_rg_i, b_rg_i, lam, w_out, ln1_w, ln1_b, w_router, b_router, w_e_gate, b_e_gate, w_e_up, b_e_up, w_e_down, b_e_down, ln2_w, ln2_b):
    Bp, Tp, D = x_prompt.shape
    Bs, Ts, _ = x_sample.shape
    l = 0
    row = lambda a: a[l].reshape(1, -1)

    cond8 = jnp.concatenate([c_ctx[None, :], c, jnp.zeros((8 - 1 - Bs, D), F32)], axis=0)
    mod = _modulation(cond8, w_mod[l], b_mod[l]).reshape(8, 6, D)
    mod_p, mod_s = mod[0:1], mod[1:1 + Bs]

    w_in_b = w_in[l].astype(BF16)
    w_out_b = w_out[l].astype(BF16)
    head_id = jnp.arange(D_A) // HEAD
    hs = (head_id[:, None] == head_id[None, :]).astype(BF16)
    avg = (hs.astype(F32) / HEAD).astype(BF16)
    wdec = w_decay_up[l].reshape(2 * LORA, D_A)
    wic = w_iclr_up[l].reshape(2 * LORA, D_A)
    wa_bd = jnp.stack([_block_diag(w_rg_a[l, 0]), _block_diag(w_rg_a[l, 1])])
    wi_bd = jnp.stack([_block_diag(w_rg_i[l, 0]), _block_diag(w_rg_i[l, 1])])

    xp = x_prompt.reshape(Bp * Tp, D)
    xs = x_sample.reshape(Bs * Ts, D)

    s0_s = state_rwkv[:, l].reshape(Bs, 2, N_PAIRS, 2, HEAD, HEAD).transpose(0, 1, 2, 4, 3, 5)
    s0_s = s0_s.reshape(Bs, 2, N_PAIRS, HEAD, 2 * HEAD)
    s0_p = jnp.zeros((Bp, 2, N_PAIRS, HEAD, 2 * HEAD), F32)
    h0_p = jnp.zeros((Bp, 2, D_B), F32)
    h0_s = state_rglru[:, l]

    ntok_all = Bp * Tp + Bs * Ts
    outs = []
    u2_all = None
    for x, mod_rows, tok_per_row, nseq, T, s0, h0, lrow, tok_base in (
            (xp, mod_p, Bp * Tp, Bp, Tp, s0_p, h0_p, Tp, 0),
            (xs, mod_s, Ts, Bs, Ts, s0_s, h0_s, GRID_W, Bp * Tp)):
        proj = _inproj(x, mod_rows, w_in_b, tok_per_row)
        o_f, o_b, bon_f, bon_b, s_fin = _rwkv(proj, s0, wdec, b_decay[l], wic, b_iclr[l], row(k_k), row(k_a),
                                              r_k[l].reshape(1, D_A), hs, nseq, T)
        yb, h_fin = _rglru(proj, h0, conv_w[l], row(conv_b), wa_bd, b_rg_a[l], wi_bd, b_rg_i[l], lam[l],
                           nseq, T, lrow)
        x1, u2_all, eid, prob = _mixout(o_f, o_b, bon_f, bon_b, yb, proj, x, mod_rows, tok_per_row,
                                        w_gate_up[l], avg, row(gn_w), row(gn_b), w_out_b, row(ln1_w),
                                        row(ln1_b), w_router[l], row(b_router), u2_all, tok_base, ntok_all)
        outs.append((x1, eid, prob, s_fin, h_fin))

    (x1_p, eid_p, prob_p, sfin_p, hfin_p), (x1_s, eid_s, prob_s, _, _) = outs
    slot, te, nv, nt_max = _route(jnp.concatenate([eid_p, eid_s], axis=0))
    inv = _invert(slot, nt_max)
    y4 = _experts(te, nv, inv, u2_all, nt_max, w_e_gate[l], b_e_gate[l], w_e_up[l], b_e_up[l], w_e_down[l],
                  b_e_down[l])
    y_p = _combine(y4, prob_p, x1_p, 0, ntok_all, mod_p, Bp * Tp, row(ln2_w), row(ln2_b))
    y_s = _combine(y4, prob_s, x1_s, Bp * Tp, ntok_all, mod_s, Ts, row(ln2_w), row(ln2_b))
    return (y_p.reshape(Bp, Tp, D), y_s.reshape(Bs, Ts, D),
            sfin_p[:, None], hfin_p[:, None])
```

```python
import functools
import math

import jax
import jax.numpy as jnp
from jax import lax
from jax.experimental import pallas as pl
from jax.experimental.pallas import tpu as pltpu

F32 = jnp.float32
BF16 = jnp.bfloat16

D_MODEL = 1024
D_A = 512
D_B = 512
HEAD = 64
N_HEADS = 8
N_PAIRS = N_HEADS // 2
GRID_W = 64
LORA = 64
LORA_G = 128
RGLRU_C = 8.0
N_EXPERTS = 32
TOP_K = 4
D_FF = 1024
SWIGLU_LIMIT = 7.0
SWIGLU_ALPHA = 1.702
LN_EPS = 1e-5
GN_EPS = 1e-5 * HEAD
D_IN = 3 * D_A + 2 * D_B + 2 * LORA + 2 * LORA + LORA_G
DEPTH = 1
DEEPNORM_ALPHA = (2 * DEPTH) ** 0.25

COL_R, COL_K, COL_V, COL_XR, COL_XGB = 0, 1, 2, 3, 4
COL_LW, COL_LA, COL_LG = 20, 21, 22

CHUNK = 64
PREP_CHUNKS = 2
VMEM_LIMIT = 56 * 1024 * 1024

_NN = (((1,), (0,)), ((), ()))
_NT = (((1,), (1,)), ((), ()))


def _mm(a, b, dims=_NN):
    return lax.dot_general(a, b, dims, preferred_element_type=F32)


def _split2(x):
    hi = x.astype(BF16)
    lo = (x - hi.astype(F32)).astype(BF16)
    return hi, lo


def _split3(x):
    hi = x.astype(BF16)
    r1 = x - hi.astype(F32)
    mid = r1.astype(BF16)
    lo = (r1 - mid.astype(F32)).astype(BF16)
    return hi, mid, lo


def _dot1(a, b, dims=_NN):
    return _mm(a.astype(BF16), b.astype(BF16), dims)


def _dot3(a, b):
    ah, al = _split2(a)
    bh, bl = _split2(b)
    return _mm(jnp.concatenate([ah, ah, al], axis=1), jnp.concatenate([bh, bl, bh], axis=0))


def _dot_xb(a, b_bf16):
    return _mm(jnp.concatenate(_split3(a), axis=1), jnp.concatenate([b_bf16] * 3, axis=0))


def _dot_xb2(a, b_bf16):
    return _mm(jnp.concatenate(_split2(a), axis=1), jnp.concatenate([b_bf16] * 2, axis=0))


def _dot_xa(a_bf16, b):
    return _mm(jnp.concatenate([a_bf16] * 3, axis=1), jnp.concatenate(_split3(b), axis=0))


ROW_TILE = 8


def _tok_rows(t, n=1):
    return pl.ds(pl.multiple_of(t * ROW_TILE, ROW_TILE), n * ROW_TILE)


def _load_tok_tiles(ref, ntok, lead=()):
    return jnp.concatenate([ref[lead + (pl.ds(s_, ntok, stride=ROW_TILE), slice(None))]
                            for s_ in range(ROW_TILE)], axis=1)


def _store_tok_tiles(ref, x):
    ntok = x.shape[0]
    for s_ in range(ROW_TILE):
        ref[pl.ds(s_, ntok, stride=ROW_TILE), :] = x[:, 128 * s_:128 * (s_ + 1)]


def _layer_norm(x):
    mu = jnp.mean(x, axis=-1, keepdims=True)
    xc = x - mu
    var = jnp.mean(xc * xc, axis=-1, keepdims=True)
    return xc * lax.rsqrt(var + LN_EPS)


def _cparams(sem):
    return pltpu.CompilerParams(dimension_semantics=sem, vmem_limit_bytes=VMEM_LIMIT)


def _mod_kernel(c_ref, w_ref, b_ref, o_ref):
    c = c_ref[...]
    s = c * jax.nn.sigmoid(c)
    o_ref[...] = _dot3(s, w_ref[...]) + b_ref[...]


def _modulation(cond8, w_mod, b_mod):
    n = w_mod.shape[1]
    tn = 1024
    return pl.pallas_call(
        _mod_kernel,
        grid=(n // tn,),
        in_specs=[pl.BlockSpec((8, D_MODEL), lambda j: (0, 0)),
                  pl.BlockSpec((D_MODEL, tn), lambda j: (0, j)),
                  pl.BlockSpec((1, tn), lambda j: (0, j))],
        out_specs=pl.BlockSpec((8, tn), lambda j: (0, j)),
        out_shape=jax.ShapeDtypeStruct((8, n), F32),
        compiler_params=_cparams(("arbitrary",)),
        name="modulation",
    )(cond8, w_mod, b_mod.reshape(1, n))


def _inproj_kernel(x_ref, mod_ref, w_ref, o_ref):
    m = mod_ref[0]
    u = _layer_norm(x_ref[...]) * (1.0 + m[1:2]) + m[0:1]
    o_ref[...] = _dot1(u, w_ref[...])


def _inproj(x, mod_rows, w_in_bf16, tok_per_row):
    ntok = x.shape[0]
    tm = 512
    per = tok_per_row // tm
    return pl.pallas_call(
        _inproj_kernel,
        grid=(ntok // tm,),
        in_specs=[pl.BlockSpec((tm, D_MODEL), lambda i: (i, 0)),
                  pl.BlockSpec((1, 6, D_MODEL), lambda i: (i // per, 0, 0)),
                  pl.BlockSpec((D_MODEL, D_IN), lambda i: (0, 0))],
        out_specs=pl.BlockSpec((tm, D_IN), lambda i: (i, 0)),
        out_shape=jax.ShapeDtypeStruct((ntok, D_IN), F32),
        compiler_params=_cparams(("arbitrary",)),
        name="inproj",
    )(x, mod_rows, w_in_bf16)


def _rwkv_kernel(rf_ref, kf_ref, vf_ref, lwf_ref, laf_ref, rb_ref, kb_ref, vb_ref, lwb_ref, lab_ref,
                 s0_ref, wdec_ref, bdec_ref, wic_ref, bic_ref, kkw_ref, ka_ref, rk_ref, hs_ref,
                 of_ref, ob_ref, bonf_ref, bonb_ref, sfin_ref,
                 s_scr, t_scr, w2_scr, mr_scr, ar_scr, bk_scr, gt_scr, *, G, ng):
    L = CHUNK
    g = pl.program_id(1)
    ins = ((rf_ref, kf_ref, vf_ref, lwf_ref, laf_ref), (rb_ref, kb_ref, vb_ref, lwb_ref, lab_ref))
    outs = ((of_ref, bonf_ref), (ob_ref, bonb_ref))

    @pl.when(g == 0)
    def _():
        s_scr[...] = s0_ref[0]

    hs = hs_ref[...]
    lane = lax.broadcasted_iota(jnp.int32, (L, 128), 1)
    rowp = lax.broadcasted_iota(jnp.int32, (L, 128), 0)
    colp = jnp.bitwise_and(lane, 63)
    h0 = lane < 64
    m0 = jnp.where(h0, 1.0, 0.0).astype(BF16)
    m1 = jnp.where(h0, 0.0, 1.0).astype(BF16)
    eye2 = jnp.where(rowp == colp, 1.0, 0.0)
    rowL = lax.broadcasted_iota(jnp.int32, (L, L), 0)
    colL = lax.broadcasted_iota(jnp.int32, (L, L), 1)

    def same(sh):
        return lax.shift_right_logical(rowp, sh) == lax.shift_right_logical(colp, sh)

    same8, same16, same32 = same(3), same(4), same(5)

    def bmask(c):
        return jnp.where(c, 1.0, 0.0).astype(BF16)

    mk8 = bmask(same8)
    merge_masks = (bmask(same16 & (~same8)), bmask(same32 & (~same16)), bmask(~same32))

    def smb(x):
        return jnp.concatenate([x * m0, x * m1], axis=0)

    def sms(s):
        return tuple(smb(x) for x in s)

    def pm(x_b, ys):
        return _mm(jnp.concatenate([x_b, x_b], axis=1), jnp.concatenate(sms(ys), axis=0))

    chains = [(d, p) for d in range(2) for p in range(N_PAIRS)]

    def each(fn, *lists):
        return [fn(*a) for a in zip(*lists)]

    def prep(jb, carry):
        pre, units = [], []
        for j, d in [(jb * PREP_CHUNKS + jo, d) for jo in range(PREP_CHUNKS) for d in range(2)]:
            r_ref, k_ref, v_ref, lw_ref, la_ref = ins[d]
            jn = j if d == 0 else G - 1 - j
            rows = pl.ds(pl.multiple_of(jn * L, L), L)
            u = d * G + j
            r = r_ref[rows, :]
            k = k_ref[rows, :]
            v = v_ref[rows, :]
            dsel = (lane >= 64) if d else h0
            tl = jnp.where(dsel, jnp.tanh(lw_ref[rows, :]), 0.0)
            dw = _dot3(tl, wdec_ref[...]) + bdec_ref[d:d + 1, :]
            logw = (-math.exp(-0.5)) * jax.nn.sigmoid(dw)
            la_m = jnp.where(dsel, la_ref[rows, :], 0.0)
            iclr = jax.nn.sigmoid(_dot3(la_m, wic_ref[...]) + bic_ref[d:d + 1, :])
            kkr = k * kkw_ref[...]
            nrm = jnp.sqrt(_dot_xb2(kkr * kkr, hs))
            kk = kkr / jnp.maximum(nrm, 1e-12)
            kdir = k * (1.0 + (iclr - 1.0) * ka_ref[...])
            bv = kk * iclr
            outs[d][1][rows, :] = _dot_xb2(r * kdir * rk_ref[...], hs) * v

            tri = bmask((rowL <= colL) if d else (rowL >= colL))
            cs = _dot_xa(tri, logw)
            ctot = jnp.sum(logw, axis=0, keepdims=True)
            g_inv = jnp.exp(-cs)
            g_rem = jnp.exp(ctot - cs)
            gt_scr[u] = jnp.exp(ctot)
            At = -kk * jnp.exp(cs - logw)
            Rt = r * jnp.exp(cs)
            Bt = bv * g_inv
            Kt = kdir * g_inv
            Bg = bv * g_rem
            Kg = kdir * g_rem
            for p in range(N_PAIRS):
                sl = slice(128 * p, 128 * (p + 1))
                ar = jnp.concatenate([At[:, sl], Rt[:, sl]], axis=0).astype(BF16)
                ar_scr[u, p] = ar
                bk_scr[u, p] = jnp.concatenate([Bg[:, sl], Kg[:, sl]], axis=0).astype(BF16)
                pre.append((ar, Bt[:, sl].astype(BF16), Kt[:, sl].astype(BF16), v[:, sl].astype(BF16)))
                units.append((d, p, u))

        def stage_m(c, pr):
            d = c[0]
            ar, bt, kt, _ = pr
            strict2 = (rowp < colp) if d else (rowp > colp)
            incl2 = (rowp <= colp) if d else (rowp >= colp)
            M = _mm(ar, jnp.concatenate([smb(bt), smb(kt)], axis=0), _NT)
            N = jnp.where(strict2, M[:L, :128], 0.0)
            Mak = jnp.where(strict2, M[:L, 128:], 0.0)
            mr_scr[c[2], c[1]] = jnp.concatenate([jnp.where(incl2, M[L:, :128], 0.0),
                                                       jnp.where(incl2, M[L:, 128:], 0.0)], axis=1).astype(BF16)
            return N, Mak

        NM = each(stage_m, units, pre)
        Ns = [_split2(nm[0]) for nm in NM]
        N0s = [(n[0] * mk8, n[1] * mk8) for n in Ns]
        T = [eye2 + jnp.where(same8, nm[0], 0.0) for nm in NM]
        P2s = [_split2(pm(n[0], n)) for n in N0s]
        W2 = [_mm(nm[1].astype(BF16), smb(pr[3])) for nm, pr in zip(NM, pre)]
        T = [t + pm(t.astype(BF16), p2) for t, p2 in zip(T, P2s)]
        P4s = [_split2(pm(p2[0], p2)) for p2 in P2s]
        T = [t + pm(t.astype(BF16), p4) for t, p4 in zip(T, P4s)]
        for mk in merge_masks:
            Ts = [_split2(t) for t in T]
            Y = [pm(ts[0], (n[0] * mk, n[1] * mk)) for ts, n in zip(Ts, Ns)]
            T = [t + pm(y.astype(BF16), ts) for t, y, ts in zip(T, Y, Ts)]
        for (d, p, u), t, w2 in zip(units, T, W2):
            t_scr[u, p] = t.astype(BF16)
            w2_scr[u, p] = w2
        return carry

    lax.fori_loop(0, G // PREP_CHUNKS, prep, 0)

    def serial(j, carry):
        us = [d * G + j for d, _ in chains]
        rws = [pl.ds(pl.multiple_of((j if d == 0 else G - 1 - j) * L, L), L) for d, _ in chains]
        sls = [slice(128 * p, 128 * (p + 1)) for _, p in chains]
        S2 = [s_scr[d, p] for d, p in chains]
        XR = [_mm(ar_scr[u, p], smb(s2.astype(BF16)), _NT) for (d, p), u, s2 in zip(chains, us, S2)]
        U = [pm(t_scr[u, p], _split2(xr[:L] + w2_scr[u, p])) for (d, p), u, xr in zip(chains, us, XR)]
        V = [ins[d][2][rw, sl] for (d, p), rw, sl in zip(chains, rws, sls)]
        for (d, p), u, rw, sl, s2, xr, uu, vv in zip(chains, us, rws, sls, S2, XR, U, V):
            UVt = jnp.concatenate([uu, vv], axis=0).T
            Rm = _mm(UVt.astype(BF16), bk_scr[u, p])
            s_scr[d, p] = gt_scr[u][:, sl] * s2 + jnp.where(h0, Rm[:HEAD], Rm[HEAD:])
        for (d, p), u, rw, sl, xr, uu, vv in zip(chains, us, rws, sls, XR, U, V):
            rhs = jnp.concatenate([smb(uu.astype(BF16)), smb(vv.astype(BF16))], axis=0)
            outs[d][0][rw, sl] = xr[L:] + _mm(mr_scr[u, p], rhs)
        return carry

    lax.fori_loop(0, G, serial, 0)

    @pl.when(g == ng - 1)
    def _():
        for d in range(2):
            for p in range(N_PAIRS):
                sp = s_scr[d, p]
                sfin_ref[0, d, 2 * p] = sp[:, :HEAD]
                sfin_ref[0, d, 2 * p + 1] = sp[:, HEAD:]


def _rwkv(proj, s0_pairs, wdec, bdec, wic, bic, kkw, ka, rk, hs, nseq, T):
    ntok = proj.shape[0]
    nc = T // CHUNK
    G = min(nc, 8)
    ng = nc // G
    GL = G * CHUNK

    def fwd(b, g):
        return b * ng + g

    def bwd(b, g):
        return b * ng + ng - 1 - g

    def pspecs(rowblk):
        return [pl.BlockSpec((GL, D_A), lambda b, g: (rowblk(b, g), COL_R)),
                pl.BlockSpec((GL, D_A), lambda b, g: (rowblk(b, g), COL_K)),
                pl.BlockSpec((GL, D_A), lambda b, g: (rowblk(b, g), COL_V)),
                pl.BlockSpec((GL, 128), lambda b, g: (rowblk(b, g), COL_LW)),
                pl.BlockSpec((GL, 128), lambda b, g: (rowblk(b, g), COL_LA))]

    def wspec(shape):
        return pl.BlockSpec(shape, lambda b, g: (0,) * len(shape))

    ospec_f = pl.BlockSpec((GL, D_A), lambda b, g: (fwd(b, g), 0))
    ospec_b = pl.BlockSpec((GL, D_A), lambda b, g: (bwd(b, g), 0))
    tok = jax.ShapeDtypeStruct((ntok, D_A), F32)
    return pl.pallas_call(
        functools.partial(_rwkv_kernel, G=G, ng=ng),
        grid=(nseq, ng),
        in_specs=pspecs(fwd) + pspecs(bwd) + [
            pl.BlockSpec((1, 2, N_PAIRS, HEAD, 128), lambda b, g: (b, 0, 0, 0, 0)),
            wspec((128, D_A)), wspec((2, D_A)), wspec((128, D_A)), wspec((2, D_A)),
            wspec((1, D_A)), wspec((1, D_A)), wspec((1, D_A)), wspec((D_A, D_A))],
        out_specs=[ospec_f, ospec_b, ospec_f, ospec_b,
                   pl.BlockSpec((1, 2, N_HEADS, HEAD, HEAD), lambda b, g: (b, 0, 0, 0, 0))],
        out_shape=[tok, tok, tok, tok, jax.ShapeDtypeStruct((nseq, 2, N_HEADS, HEAD, HEAD), F32)],
        scratch_shapes=[pltpu.VMEM((2, N_PAIRS, HEAD, 128), F32),
                        pltpu.VMEM((2 * G, N_PAIRS, CHUNK, 128), BF16),
                        pltpu.VMEM((2 * G, N_PAIRS, CHUNK, 128), F32),
                        pltpu.VMEM((2 * G, N_PAIRS, CHUNK, 256), BF16),
                        pltpu.VMEM((2 * G, N_PAIRS, 2 * CHUNK, 128), BF16),
                        pltpu.VMEM((2 * G, N_PAIRS, 2 * CHUNK, 128), BF16),
                        pltpu.VMEM((2 * G, 1, D_A), F32)],
        compiler_params=_cparams(("arbitrary", "arbitrary")),
        name="rwkv_scan",
    )(*([proj] * 10), s0_pairs, wdec, bdec, wic, bic, kkw, ka, rk, hs)


def _gelu_tanh(x):
    return 0.5 * x * (1.0 + jnp.tanh(math.sqrt(2.0 / math.pi) * (x + 0.044715 * (x * x * x))))


def _rglru_kernel(xr_ref, xg_ref, h0_ref, cw_ref, cb_ref, wa_ref, ba_ref, wi_ref, bi_ref, lam_ref,
                  y_ref, hfin_ref, a_scr, u_scr, *, T, lrow):
    RB = min(T, 256)
    nblk = T // RB
    rowi = lax.broadcasted_iota(jnp.int32, (RB, D_B), 0)
    pos = jnp.bitwise_and(rowi, lrow - 1)
    cw = cw_ref[...]
    lam = lam_ref[...]
    sp = jnp.maximum(-lam, 0.0) + jnp.log1p(jnp.exp(-jnp.abs(lam)))

    def gates(blk, carry):
        r0 = pl.multiple_of(blk * RB, RB)
        x = xr_ref[pl.ds(r0, RB), :]
        xm1 = jnp.where(pos >= 1, pltpu.roll(x, 1, 0), 0.0)
        xm2 = jnp.where(pos >= 2, pltpu.roll(x, 2, 0), 0.0)
        xp1 = jnp.where(pos <= lrow - 2, pltpu.roll(x, RB - 1, 0), 0.0)
        xc = xm2 * cw[0:1] + xm1 * cw[1:2] + x * cw[2:3] + xp1 * cw[3:4] + cb_ref[...]
        for dd in range(2):
            gr = jax.nn.sigmoid(_dot1(xc, wa_ref[dd]) + ba_ref[dd:dd + 1, :])
            gi = jax.nn.sigmoid(_dot1(xc, wi_ref[dd]) + bi_ref[dd:dd + 1, :])
            log_a = (-RGLRU_C) * gr * sp[dd:dd + 1, :]
            a = jnp.exp(log_a)
            a_scr[dd, pl.ds(r0, RB), :] = a
            u_scr[dd, pl.ds(r0, RB), :] = jnp.sqrt((1.0 - a) * (1.0 + a)) * gi * xc
        return carry

    lax.fori_loop(0, nblk, gates, 0)

    SUB = 8
    rowt = lax.broadcasted_iota(jnp.int32, (SUB, D_B), 0)

    def tile_scan(a, u, h_in, rev):
        for s_ in (1, 2, 4):
            keep = (rowt < SUB - s_) if rev else (rowt >= s_)
            sh = SUB - s_ if rev else s_
            a_sh = jnp.where(keep, pltpu.roll(a, sh, 0), 1.0)
            u_sh = jnp.where(keep, pltpu.roll(u, sh, 0), 0.0)
            u = a * u_sh + u
            a = a * a_sh
        return a * h_in + u

    def step(i, carry):
        hf, hb = carry
        rf = pl.ds(pl.multiple_of(i * SUB, SUB), SUB)
        hf_tile = tile_scan(a_scr[0, rf, :], u_scr[0, rf, :], hf, False)
        u_scr[0, rf, :] = hf_tile
        rb = pl.ds(pl.multiple_of(T - SUB - i * SUB, SUB), SUB)
        hb_tile = tile_scan(a_scr[1, rb, :], u_scr[1, rb, :], hb, True)
        u_scr[1, rb, :] = hb_tile
        return hf_tile[SUB - 1:SUB], hb_tile[0:1]

    h0 = h0_ref[0]
    hf, hb = lax.fori_loop(0, T // SUB, step, (h0[0:1], h0[1:2]), unroll=2)
    hfin_ref[0] = jnp.concatenate([hf, hb], axis=0)

    def outp(blk, carry):
        r0 = pl.multiple_of(blk * RB, RB)
        h = u_scr[0, pl.ds(r0, RB), :] + u_scr[1, pl.ds(r0, RB), :]
        y_ref[pl.ds(r0, RB), :] = h * _gelu_tanh(xg_ref[pl.ds(r0, RB), :])
        return carry

    lax.fori_loop(0, nblk, outp, 0)


def _rglru(proj, h0, conv_w, conv_b, wa_bd, ba, wi_bd, bi, lam, nseq, T, lrow):
    ntok = proj.shape[0]

    def wspec(shape):
        return pl.BlockSpec(shape, lambda b: (0,) * len(shape))

    return pl.pallas_call(
        functools.partial(_rglru_kernel, T=T, lrow=lrow),
        grid=(nseq,),
        in_specs=[pl.BlockSpec((T, D_B), lambda b: (b, COL_XR)),
                  pl.BlockSpec((T, D_B), lambda b: (b, COL_XGB)),
                  pl.BlockSpec((1, 2, D_B), lambda b: (b, 0, 0)),
                  wspec((4, D_B)), wspec((1, D_B)),
                  wspec((2, D_B, D_B)), wspec((2, D_B)), wspec((2, D_B, D_B)), wspec((2, D_B)),
                  wspec((2, D_B))],
        out_specs=[pl.BlockSpec((T, D_B), lambda b: (b, 0)),
                   pl.BlockSpec((1, 2, D_B), lambda b: (b, 0, 0))],
        out_shape=[jax.ShapeDtypeStruct((ntok, D_B), F32),
                   jax.ShapeDtypeStruct((nseq, 2, D_B), F32)],
        scratch_shapes=[pltpu.VMEM((2, T, D_B), F32), pltpu.VMEM((2, T, D_B), F32)],
        compiler_params=_cparams(("arbitrary",)),
        name="rglru",
    )(proj, proj, h0, conv_w, conv_b, wa_bd, ba, wi_bd, bi, lam)


def _mixout_kernel(of_ref, ob_ref, bf_ref, bb_ref, yb_ref, lg_ref, x_ref, mod_ref, wgu_ref, avg_ref,
                   gnw_ref, gnb_ref, wout_ref, l1w_ref, l1b_ref, wr_ref, br_ref,
                   x1_ref, u2_ref, eid_ref, prob_ref):
    m = mod_ref[0]
    avg = avg_ref[...]
    wkv = of_ref[...] + ob_ref[...]
    mu = _dot_xb2(wkv, avg)
    dv = wkv - mu
    var = _dot_xb2(dv * dv, avg)
    gn = dv * lax.rsqrt(var + GN_EPS) * gnw_ref[...] + gnb_ref[...]
    g = _dot1(jax.nn.sigmoid(lg_ref[...]), wgu_ref[...])
    ya = (gn + (bf_ref[...] + bb_ref[...])) * g
    mix = _dot1(ya, wout_ref[0:D_A, :]) + _dot1(yb_ref[...], wout_ref[D_A:, :])
    x1 = _layer_norm(DEEPNORM_ALPHA * x_ref[...] + m[2:3] * mix) * l1w_ref[...] + l1b_ref[...]
    x1_ref[...] = x1
    u2 = _layer_norm(x1) * (1.0 + m[4:5]) + m[3:4]
    _store_tok_tiles(u2_ref, u2)
    logits = _dot3(u2, wr_ref[...]) + br_ref[...]
    tm = logits.shape[0]
    lane = lax.broadcasted_iota(jnp.int32, (tm, N_EXPERTS), 1)
    lane4 = lax.broadcasted_iota(jnp.int32, (tm, TOP_K), 1)
    work = logits
    tops, idxs = [], []
    for _ in range(TOP_K):
        mx = jnp.max(work, axis=1, keepdims=True)
        idx = jnp.min(jnp.where(work == mx, lane, N_EXPERTS), axis=1, keepdims=True)
        tops.append(mx)
        idxs.append(idx)
        work = jnp.where(lane == idx, -jnp.inf, work)
    es = [jnp.exp(t - tops[0]) for t in tops]
    den = es[0] + es[1] + es[2] + es[3]
    eid = jnp.zeros((tm, TOP_K), jnp.int32)
    prob = jnp.zeros((tm, TOP_K), F32)
    for j in range(TOP_K):
        eid = jnp.where(lane4 == j, idxs[j], eid)
        prob = jnp.where(lane4 == j, es[j] / den, prob)
    eid_ref[...] = eid
    prob_ref[...] = prob


def _mixout_alias_kernel(*refs):
    _mixout_kernel(*refs[:17], *refs[18:])


def _mixout_first_kernel(*refs, ntiles):
    i = pl.program_id(0)

    @pl.when(i < ntiles)
    def _():
        _mixout_kernel(*refs)

    @pl.when(i >= ntiles)
    def _():
        refs[18][...] = jnp.zeros_like(refs[18])


def _mixout(o_f, o_b, bon_f, bon_b, yb, proj, x, mod_rows, tok_per_row, wgu, avg, gnw, gnb, wout, l1w, l1b,
            wr, br, u2_all, tok_base, ntok_all):
    ntok = x.shape[0]
    tm = 256
    per = tok_per_row // tm
    base_tiles = tok_base // tm
    alias = u2_all is not None
    ntiles = ntok // tm

    def wspec(shape):
        return pl.BlockSpec(shape, lambda i: (0,) * len(shape))

    def tl(i):
        return jnp.minimum(i, ntiles - 1)

    return pl.pallas_call(
        _mixout_alias_kernel if alias else functools.partial(_mixout_first_kernel, ntiles=ntiles),
        grid=(ntiles if alias else ntok_all // tm,),
        input_output_aliases={17: 1} if alias else {},
        in_specs=[pl.BlockSpec((tm, D_A), lambda i: (tl(i), 0)),
                  pl.BlockSpec((tm, D_A), lambda i: (tl(i), 0)),
                  pl.BlockSpec((tm, D_A), lambda i: (tl(i), 0)),
                  pl.BlockSpec((tm, D_A), lambda i: (tl(i), 0)),
                  pl.BlockSpec((tm, D_B), lambda i: (tl(i), 0)),
                  pl.BlockSpec((tm, LORA_G), lambda i: (tl(i), COL_LG)),
                  pl.BlockSpec((tm, D_MODEL), lambda i: (tl(i), 0)),
                  pl.BlockSpec((1, 6, D_MODEL), lambda i: (tl(i) // per, 0, 0)),
                  wspec((LORA_G, D_A)), wspec((D_A, D_A)), wspec((1, D_A)), wspec((1, D_A)),
                  wspec((D_MODEL, D_MODEL)), wspec((1, D_MODEL)), wspec((1, D_MODEL)),
                  wspec((D_MODEL, N_EXPERTS)), wspec((1, N_EXPERTS))]
                 + ([pl.BlockSpec(memory_space=pl.ANY)] if alias else []),
        out_specs=[pl.BlockSpec((tm, D_MODEL), lambda i: (tl(i), 0)),
                   pl.BlockSpec((tm * ROW_TILE, 128), lambda i: (i + base_tiles, 0)),
                   pl.BlockSpec((tm, TOP_K), lambda i: (tl(i), 0)),
                   pl.BlockSpec((tm, TOP_K), lambda i: (tl(i), 0))],
        out_shape=[jax.ShapeDtypeStruct((ntok, D_MODEL), F32),
                   jax.ShapeDtypeStruct((ntok_all * ROW_TILE, 128), F32),
                   jax.ShapeDtypeStruct((ntok, TOP_K), jnp.int32),
                   jax.ShapeDtypeStruct((ntok, TOP_K), F32)],
        compiler_params=_cparams(("arbitrary",)),
        name="mixout",
    )(o_f, o_b, bon_f, bon_b, yb, proj, x, mod_rows, wgu, avg, gnw, gnb, wout, l1w, l1b, wr, br,
      *([u2_all] if alias else []))


MOE_TM = 256
ROUTE_BLK = 256
INVERT_CHUNK = 2048


def _route_kernel(eid_ref, slot_ref, te_ref, nv_ref, c_scr, *, ntok, nt_pad):
    nblk = ntok // ROUTE_BLK
    lane32 = lax.broadcasted_iota(jnp.int32, (ROUTE_BLK, N_EXPERTS), 1)
    lane4 = lax.broadcasted_iota(jnp.int32, (ROUTE_BLK, TOP_K), 1)
    rowb = lax.broadcasted_iota(jnp.int32, (ROUTE_BLK, ROUTE_BLK), 0)
    colb = lax.broadcasted_iota(jnp.int32, (ROUTE_BLK, ROUTE_BLK), 1)
    tri = jnp.where(rowb > colb, 1.0, 0.0).astype(BF16)

    def count(b, carry):
        rows = pl.ds(pl.multiple_of(b * ROUTE_BLK, ROUTE_BLK), ROUTE_BLK)
        e = eid_ref[rows, :]
        hot = jnp.zeros((ROUTE_BLK, N_EXPERTS), F32)
        for j in range(TOP_K):
            hot = hot + jnp.where(lane32 == e[:, j:j + 1], 1.0, 0.0)
        c_scr[rows, :] = _mm(tri, hot.astype(BF16)) + carry
        return carry + jnp.sum(hot, axis=0, keepdims=True)

    n = lax.fori_loop(0, nblk, count, jnp.zeros((1, N_EXPERTS), F32))
    padded = jnp.floor((n + (MOE_TM - 1)) * (1.0 / MOE_TM)) * MOE_TM
    r32 = lax.broadcasted_iota(jnp.int32, (N_EXPERTS, N_EXPERTS), 0)
    c32 = lax.broadcasted_iota(jnp.int32, (N_EXPERTS, N_EXPERTS), 1)
    upper = jnp.where(r32 < c32, 1.0, 0.0).astype(BF16)
    off = _dot_xb(jnp.broadcast_to(padded, (8, N_EXPERTS)), upper)[0:1]
    gend = off + padded

    def place(b, carry):
        rows = pl.ds(pl.multiple_of(b * ROUTE_BLK, ROUTE_BLK), ROUTE_BLK)
        e = eid_ref[rows, :]
        base = off + c_scr[rows, :]
        s = jnp.zeros((ROUTE_BLK, TOP_K), F32)
        for j in range(TOP_K):
            sj = jnp.sum(jnp.where(lane32 == e[:, j:j + 1], base, 0.0), axis=1, keepdims=True)
            s = jnp.where(lane4 == j, sj, s)
        slot_ref[rows, :] = s.astype(jnp.int32)
        return carry

    lax.fori_loop(0, nblk, place, 0)

    gcol = jnp.sum(jnp.where(r32 == c32, jnp.broadcast_to(gend, (N_EXPERTS, N_EXPERTS)), 0.0),
                   axis=1, keepdims=True)
    tstart = (lax.broadcasted_iota(jnp.int32, (N_EXPERTS, nt_pad), 1) * MOE_TM).astype(F32)
    te = jnp.sum(jnp.where(gcol <= tstart, 1.0, 0.0), axis=0, keepdims=True)
    te_ref[...] = jnp.minimum(te, N_EXPERTS - 1.0).astype(jnp.int32)
    total = jnp.sum(padded, axis=1, keepdims=True)
    nv_ref[...] = jnp.broadcast_to(total * (1.0 / MOE_TM), (1, 128)).astype(jnp.int32)


def _route(eid):
    ntok = eid.shape[0]
    nt_max = ntok * TOP_K // MOE_TM + N_EXPERTS
    nt_pad = -(-(nt_max + 1) // 128) * 128
    full = lambda shape: pl.BlockSpec(shape, lambda i: (0,) * len(shape))
    slot, te, nv = pl.pallas_call(
        functools.partial(_route_kernel, ntok=ntok, nt_pad=nt_pad),
        grid=(1,),
        in_specs=[full((ntok, TOP_K))],
        out_specs=[full((ntok, TOP_K)), full((1, nt_pad)), full((1, 128))],
        out_shape=[jax.ShapeDtypeStruct((ntok, TOP_K), jnp.int32),
                   jax.ShapeDtypeStruct((1, nt_pad), jnp.int32),
                   jax.ShapeDtypeStruct((1, 128), jnp.int32)],
        scratch_shapes=[pltpu.VMEM((ntok, N_EXPERTS), F32)],
        compiler_params=_cparams(("arbitrary",)),
        name="moe_route",
    )(eid)
    return slot.reshape(ntok * TOP_K), te.reshape(nt_pad), nv.reshape(128), nt_max


def _invert_kernel(slot_ref, inv_ref, fill_vmem, slot_smem, inv_smem, sem_in, sem_out, *, nassign):
    slot_id = lax.broadcasted_iota(jnp.int32, fill_vmem.shape, 0)
    fill_vmem[...] = jnp.bitwise_and(slot_id, MOE_TM - 1) * ROW_TILE + TOP_K
    fill = pltpu.make_async_copy(fill_vmem, inv_smem, sem_out)
    fill.start()
    fill.wait()

    def chunk(ci, c):
        a0 = pl.multiple_of(ci * INVERT_CHUNK, INVERT_CHUNK)
        cp = pltpu.make_async_copy(slot_ref.at[pl.ds(a0, INVERT_CHUNK)], slot_smem, sem_in)
        cp.start()
        cp.wait()

        def put(tt, c2):
            base = (lax.shift_right_logical(a0, 2) + tt) * ROW_TILE
            for k in range(TOP_K):
                inv_smem[slot_smem[tt * TOP_K + k]] = base + k
            return c2

        lax.fori_loop(0, INVERT_CHUNK // TOP_K, put, 0, unroll=2)
        return c

    lax.fori_loop(0, nassign // INVERT_CHUNK, chunk, 0)
    out = pltpu.make_async_copy(inv_smem, inv_ref, sem_out)
    out.start()
    out.wait()


def _invert(slot, ntiles):
    nassign = slot.shape[0]
    anyspec = pl.BlockSpec(memory_space=pl.ANY)
    return pl.pallas_call(
        functools.partial(_invert_kernel, nassign=nassign),
        grid=(1,),
        in_specs=[anyspec], out_specs=anyspec,
        out_shape=jax.ShapeDtypeStruct((ntiles * MOE_TM,), jnp.int32),
        scratch_shapes=[pltpu.VMEM((ntiles * MOE_TM,), jnp.int32),
                        pltpu.SMEM((INVERT_CHUNK,), jnp.int32), pltpu.SMEM((ntiles * MOE_TM,), jnp.int32),
                        pltpu.SemaphoreType.DMA(()), pltpu.SemaphoreType.DMA(())],
        compiler_params=_cparams(("arbitrary",)),
        name="moe_invert",
    )(slot)


def _expert_kernel(te_ref, nv_ref, inv_ref, u2_ref, wg_ref, bg_ref, wu_ref, bu_ref, wd_ref, bd_ref, y4_ref,
                   xbuf, ybuf, wg_b, wu_b, wd_b, sem_g, sem_s, *, ntok):
    i = pl.program_id(0)
    nvalid = nv_ref[0]
    valid = i < nvalid
    trash = TOP_K * ntok

    def rows8(row):
        return pl.ds(pl.multiple_of(row, ROW_TILE), ROW_TILE)

    def gather_copy(tile, r, buf):
        src = jnp.bitwise_and(inv_ref[tile * MOE_TM + r], -ROW_TILE)
        return pltpu.make_async_copy(u2_ref.at[rows8(src)], xbuf.at[buf, _tok_rows(r)], sem_g.at[buf])

    def scatter_copy(tile, r):
        code = inv_ref[tile * MOE_TM + r]
        dst = jnp.bitwise_and(code, ROW_TILE - 1) * (ntok * ROW_TILE) + jnp.bitwise_and(code, -ROW_TILE)
        return pltpu.make_async_copy(ybuf.at[_tok_rows(r)], y4_ref.at[rows8(dst)], sem_s)

    def wait_gather(buf):
        pltpu.make_async_copy(xbuf.at[buf], xbuf.at[buf], sem_g.at[buf]).wait()

    def wait_scatter():
        pltpu.make_async_copy(ybuf, ybuf, sem_s).wait()

    @pl.when(i == 0)
    def _():
        def first(r2, c):
            gather_copy(0, 2 * r2, 0).start(priority=0)
            gather_copy(0, 2 * r2 + 1, 0).start(priority=1)
            return c

        lax.fori_loop(0, MOE_TM // 2, first, 0)
        ybuf[...] = jnp.zeros_like(ybuf)
        spare = pltpu.make_async_copy(ybuf, y4_ref.at[_tok_rows(trash, MOE_TM)], sem_s)
        spare.start()
        spare.wait()

    fresh = jnp.logical_or(i == 0, te_ref[i] != te_ref[jnp.maximum(i - 1, 0)])

    @pl.when(jnp.logical_and(valid, fresh))
    def _():
        wg_b[...] = wg_ref[0].astype(BF16)
        wu_b[...] = wu_ref[0].astype(BF16)
        wd_b[...] = wd_ref[0].astype(BF16)

    @pl.when(valid)
    def _():
        buf = jnp.bitwise_and(i, 1)
        nxt = jnp.minimum(i + 1, nvalid - 1)
        wait_gather(buf)
        x = _load_tok_tiles(xbuf, MOE_TM, (buf,)).astype(BF16)
        FC = 256
        nfc = D_FF // FC
        per = MOE_TM // nfc
        y = jnp.zeros((MOE_TM, D_MODEL), F32)
        for j in range(nfc):
            for r in range(j * per, (j + 1) * per):
                gather_copy(nxt, r, 1 - buf).start(priority=r % 2)
            cs = slice(j * FC, (j + 1) * FC)
            gate = jnp.minimum(_mm(x, wg_b[:, cs]) + bg_ref[0, :, cs], SWIGLU_LIMIT)
            up = jnp.clip(_mm(x, wu_b[:, cs]) + bu_ref[0, :, cs], -SWIGLU_LIMIT, SWIGLU_LIMIT)
            h = (up + 1.0) * gate * jax.nn.sigmoid(SWIGLU_ALPHA * gate)
            y = y + _mm(h.astype(BF16), wd_b[cs, :])

        @pl.when(i > 0)
        def _():
            wait_scatter()

        _store_tok_tiles(ybuf, y + bd_ref[0])
        for r in range(MOE_TM):
            scatter_copy(i, r).start(priority=r % 2)

    @pl.when(i == nvalid)
    def _():
        wait_gather(jnp.bitwise_and(nvalid, 1))
        wait_scatter()


def _experts(te, nv, inv, u2_all, nt_max, wg, bg, wu, bu, wd, bd):
    ntok = u2_all.shape[0] // ROW_TILE

    def tile(i, te, nv, inv):
        return te[jnp.minimum(i, nv[0] - 1)]

    wspec = pl.BlockSpec((1, D_MODEL, D_FF), lambda i, te, nv, inv: (tile(i, te, nv, inv), 0, 0))
    bspec = pl.BlockSpec((1, 1, D_FF), lambda i, te, nv, inv: (tile(i, te, nv, inv), 0, 0))
    anyspec = pl.BlockSpec(memory_space=pl.ANY)
    return pl.pallas_call(
        functools.partial(_expert_kernel, ntok=ntok),
        grid_spec=pltpu.PrefetchScalarGridSpec(
            num_scalar_prefetch=3, grid=(nt_max + 1,),
            in_specs=[anyspec, wspec, bspec, wspec, bspec, wspec, bspec], out_specs=anyspec,
            scratch_shapes=[pltpu.VMEM((2, MOE_TM * ROW_TILE, 128), F32),
                            pltpu.VMEM((MOE_TM * ROW_TILE, 128), F32),
                            pltpu.VMEM((D_MODEL, D_FF), BF16), pltpu.VMEM((D_MODEL, D_FF), BF16),
                            pltpu.VMEM((D_MODEL, D_FF), BF16),
                            pltpu.SemaphoreType.DMA((2,)), pltpu.SemaphoreType.DMA(())]),
        out_shape=jax.ShapeDtypeStruct(((TOP_K * ntok + MOE_TM) * ROW_TILE, 128), F32),
        compiler_params=_cparams(("arbitrary",)),
        name="moe_experts",
    )(te, nv, inv, u2_all, wg, bg.reshape(N_EXPERTS, 1, D_FF), wu, bu.reshape(N_EXPERTS, 1, D_FF),
      wd, bd.reshape(N_EXPERTS, 1, D_MODEL))


def _combine_kernel(y0_ref, y1_ref, y2_ref, y3_ref, prob_ref, x1_ref, mod_ref, w_ref, b_ref, o_ref):
    p = prob_ref[...]
    tm = p.shape[0]
    moe = p[:, 0:1] * _load_tok_tiles(y0_ref, tm)
    for j, y_ref in enumerate((y1_ref, y2_ref, y3_ref), start=1):
        moe = moe + p[:, j:j + 1] * _load_tok_tiles(y_ref, tm)
    m = mod_ref[0]
    o_ref[...] = _layer_norm(DEEPNORM_ALPHA * x1_ref[...] + m[5:6] * moe) * w_ref[...] + b_ref[...]


def _combine(y4, prob, x1, tok_base, ntok_all, mod_rows, tok_per_row, w, b):
    ntok = x1.shape[0]
    tm = 256
    per = tok_per_row // tm

    def yspec(k):
        return pl.BlockSpec((tm * ROW_TILE, 128), lambda i: ((k * ntok_all + tok_base) // tm + i, 0))

    return pl.pallas_call(
        _combine_kernel,
        grid=(ntok // tm,),
        in_specs=[yspec(0), yspec(1), yspec(2), yspec(3),
                  pl.BlockSpec((tm, TOP_K), lambda i: (i, 0)),
                  pl.BlockSpec((tm, D_MODEL), lambda i: (i, 0)),
                  pl.BlockSpec((1, 6, D_MODEL), lambda i: (i // per, 0, 0)),
                  pl.BlockSpec((1, D_MODEL), lambda i: (0, 0)),
                  pl.BlockSpec((1, D_MODEL), lambda i: (0, 0))],
        out_specs=pl.BlockSpec((tm, D_MODEL), lambda i: (i, 0)),
        out_shape=jax.ShapeDtypeStruct((ntok, D_MODEL), F32),
        compiler_params=_cparams(("arbitrary",)),
        name="moe_combine",
    )(y4, y4, y4, y4, prob, x1, mod_rows, w, b)


def _block_diag(w):
    nb, bb, _ = w.shape
    eye = jnp.eye(nb, dtype=w.dtype)
    return jnp.einsum('nij,nm->nimj', w, eye).reshape(nb * bb, nb * bb)


def kernel(x_prompt, x_sample, state_rwkv, state_rglru, c, c_ctx, w_mod, b_mod, w_in, w_decay_up, b_decay, w_iclr_up, b_iclr, w_gate_up, k_k, k_a, r_k, gn_w, gn_b, conv_w, conv_b, w_rg_a, b_rg_a, w_rg_i, b_rg_i, lam, w_out, ln1_w, ln1_b, w_router, b_router, w_e_gate, b_e_gate, w_e_up, b_e_up, w_e_down, b_e_down, ln2_w, ln2_b):
    Bp, Tp, D = x_prompt.shape
    Bs, Ts, _ = x_sample.shape
    l = 0
    row = lambda a: a[l].reshape(1, -1)

    cond8 = jnp.concatenate([c_ctx[None, :], c, jnp.zeros((8 - 1 - Bs, D), F32)], axis=0)
    mod = _modulation(cond8, w_mod[l], b_mod[l]).reshape(8, 6, D)
    mod_p, mod_s = mod[0:1], mod[1:1 + Bs]

    w_in_b = w_in[l].astype(BF16)
    w_out_b = w_out[l].astype(BF16)
    head_id = jnp.arange(D_A) // HEAD
    hs = (head_id[:, None] == head_id[None, :]).astype(BF16)
    avg = (hs.astype(F32) / HEAD).astype(BF16)
    wdec = w_decay_up[l].reshape(2 * LORA, D_A)
    wic = w_iclr_up[l].reshape(2 * LORA, D_A)
    wa_bd = jnp.stack([_block_diag(w_rg_a[l, 0]), _block_diag(w_rg_a[l, 1])])
    wi_bd = jnp.stack([_block_diag(w_rg_i[l, 0]), _block_diag(w_rg_i[l, 1])])

    xp = x_prompt.reshape(Bp * Tp, D)
    xs = x_sample.reshape(Bs * Ts, D)

    s0_s = state_rwkv[:, l].reshape(Bs, 2, N_PAIRS, 2, HEAD, HEAD).transpose(0, 1, 2, 4, 3, 5)
    s0_s = s0_s.reshape(Bs, 2, N_PAIRS, HEAD, 2 * HEAD)
    s0_p = jnp.zeros((Bp, 2, N_PAIRS, HEAD, 2 * HEAD), F32)
    h0_p = jnp.zeros((Bp, 2, D_B), F32)
    h0_s = state_rglru[:, l]

    ntok_all = Bp * Tp + Bs * Ts
    outs = []
    u2_all = None
    for x, mod_rows, tok_per_row, nseq, T, s0, h0, lrow, tok_base in (
            (xp, mod_p, Bp * Tp, Bp, Tp, s0_p, h0_p, Tp, 0),
            (xs, mod_s, Ts, Bs, Ts, s0_s, h0_s, GRID_W, Bp * Tp)):
        proj = _inproj(x, mod_rows, w_in_b, tok_per_row)
        o_f, o_b, bon_f, bon_b, s_fin = _rwkv(proj, s0, wdec, b_decay[l], wic, b_iclr[l], row(k_k), row(k_a),
                                              r_k[l].reshape(1, D_A), hs, nseq, T)
        yb, h_fin = _rglru(proj, h0, conv_w[l], row(conv_b), wa_bd, b_rg_a[l], wi_bd, b_rg_i[l], lam[l],
                           nseq, T, lrow)
        x1, u2_all, eid, prob = _mixout(o_f, o_b, bon_f, bon_b, yb, proj, x, mod_rows, tok_per_row,
                                        w_gate_up[l], avg, row(gn_w), row(gn_b), w_out_b, row(ln1_w),
                                        row(ln1_b), w_router[l], row(b_router), u2_all, tok_base, ntok_all)
        outs.append((x1, eid, prob, s_fin, h_fin))

    (x1_p, eid_p, prob_p, sfin_p, hfin_p), (x1_s, eid_s, prob_s, _, _) = outs
    slot, te, nv, nt_max = _route(jnp.concatenate([eid_p, eid_s], axis=0))
    inv = _invert(slot, nt_max)
    y4 = _experts(te, nv, inv, u2_all, nt_max, w_e_gate[l], b_e_gate[l], w_e_up[l], b_e_up[l], w_e_down[l],
                  b_e_down[l])
    y_p = _combine(y4, prob_p, x1_p, 0, ntok_all, mod_p, Bp * Tp, row(ln2_w), row(ln2_b))
    y_s = _combine(y4, prob_s, x1_s, Bp * Tp, ntok_all, mod_s, Ts, row(ln2_w), row(ln2_b))
    return (y_p.reshape(Bp, Tp, D), y_s.reshape(Bs, Ts, D),
            sfin_p[:, None], hfin_p[:, None])
```

```python
import functools
import math

import jax
import jax.numpy as jnp
from jax import lax
from jax.experimental import pallas as pl
from jax.experimental.pallas import tpu as pltpu

F32 = jnp.float32
BF16 = jnp.bfloat16

D_MODEL = 1024
D_A = 512
D_B = 512
HEAD = 64
N_HEADS = 8
N_PAIRS = N_HEADS // 2
GRID_W = 64
LORA = 64
LORA_G = 128
RGLRU_C = 8.0
N_EXPERTS = 32
TOP_K = 4
D_FF = 1024
SWIGLU_LIMIT = 7.0
SWIGLU_ALPHA = 1.702
LN_EPS = 1e-5
GN_EPS = 1e-5 * HEAD
D_IN = 3 * D_A + 2 * D_B + 2 * LORA + 2 * LORA + LORA_G
DEPTH = 1
DEEPNORM_ALPHA = (2 * DEPTH) ** 0.25

COL_R, COL_K, COL_V, COL_XR, COL_XGB = 0, 1, 2, 3, 4
COL_LW, COL_LA, COL_LG = 20, 21, 22

CHUNK = 64
PREP_CHUNKS = 2
VMEM_LIMIT = 56 * 1024 * 1024

_NN = (((1,), (0,)), ((), ()))
_NT = (((1,), (1,)), ((), ()))


def _mm(a, b, dims=_NN):
    return lax.dot_general(a, b, dims, preferred_element_type=F32)


def _split2(x):
    hi = x.astype(BF16)
    lo = (x - hi.astype(F32)).astype(BF16)
    return hi, lo


def _split3(x):
    hi = x.astype(BF16)
    r1 = x - hi.astype(F32)
    mid = r1.astype(BF16)
    lo = (r1 - mid.astype(F32)).astype(BF16)
    return hi, mid, lo


def _dot1(a, b, dims=_NN):
    return _mm(a.astype(BF16), b.astype(BF16), dims)


def _dot3(a, b):
    ah, al = _split2(a)
    bh, bl = _split2(b)
    return _mm(jnp.concatenate([ah, ah, al], axis=1), jnp.concatenate([bh, bl, bh], axis=0))


def _dot_xb(a, b_bf16):
    return _mm(jnp.concatenate(_split3(a), axis=1), jnp.concatenate([b_bf16] * 3, axis=0))


def _dot_xb2(a, b_bf16):
    return _mm(jnp.concatenate(_split2(a), axis=1), jnp.concatenate([b_bf16] * 2, axis=0))


def _dot_xa(a_bf16, b):
    return _mm(jnp.concatenate([a_bf16] * 3, axis=1), jnp.concatenate(_split3(b), axis=0))


ROW_TILE = 8


def _tok_rows(t, n=1):
    return pl.ds(pl.multiple_of(t * ROW_TILE, ROW_TILE), n * ROW_TILE)


def _load_tok_tiles(ref, ntok, lead=()):
    return jnp.concatenate([ref[lead + (pl.ds(s_, ntok, stride=ROW_TILE), slice(None))]
                            for s_ in range(ROW_TILE)], axis=1)


def _store_tok_tiles(ref, x):
    ntok = x.shape[0]
    for s_ in range(ROW_TILE):
        ref[pl.ds(s_, ntok, stride=ROW_TILE), :] = x[:, 128 * s_:128 * (s_ + 1)]


def _layer_norm(x):
    mu = jnp.mean(x, axis=-1, keepdims=True)
    xc = x - mu
    var = jnp.mean(xc * xc, axis=-1, keepdims=True)
    return xc * lax.rsqrt(var + LN_EPS)


def _cparams(sem):
    return pltpu.CompilerParams(dimension_semantics=sem, vmem_limit_bytes=VMEM_LIMIT)


def _mod_kernel(c_ref, w_ref, b_ref, o_ref):
    c = c_ref[...]
    s = c * jax.nn.sigmoid(c)
    o_ref[...] = _dot3(s, w_ref[...]) + b_ref[...]


def _modulation(cond8, w_mod, b_mod):
    n = w_mod.shape[1]
    tn = 1024
    return pl.pallas_call(
        _mod_kernel,
        grid=(n // tn,),
        in_specs=[pl.BlockSpec((8, D_MODEL), lambda j: (0, 0)),
                  pl.BlockSpec((D_MODEL, tn), lambda j: (0, j)),
                  pl.BlockSpec((1, tn), lambda j: (0, j))],
        out_specs=pl.BlockSpec((8, tn), lambda j: (0, j)),
        out_shape=jax.ShapeDtypeStruct((8, n), F32),
        compiler_params=_cparams(("arbitrary",)),
        name="modulation",
    )(cond8, w_mod, b_mod.reshape(1, n))


def _inproj_kernel(x_ref, mod_ref, w_ref, o_ref):
    m = mod_ref[0]
    u = _layer_norm(x_ref[...]) * (1.0 + m[1:2]) + m[0:1]
    o_ref[...] = _dot1(u, w_ref[...])


def _inproj(x, mod_rows, w_in_bf16, tok_per_row):
    ntok = x.shape[0]
    tm = 512
    per = tok_per_row // tm
    return pl.pallas_call(
        _inproj_kernel,
        grid=(ntok // tm,),
        in_specs=[pl.BlockSpec((tm, D_MODEL), lambda i: (i, 0)),
                  pl.BlockSpec((1, 6, D_MODEL), lambda i: (i // per, 0, 0)),
                  pl.BlockSpec((D_MODEL, D_IN), lambda i: (0, 0))],
        out_specs=pl.BlockSpec((tm, D_IN), lambda i: (i, 0)),
        out_shape=jax.ShapeDtypeStruct((ntok, D_IN), F32),
        compiler_params=_cparams(("arbitrary",)),
        name="inproj",
    )(x, mod_rows, w_in_bf16)


def _rwkv_kernel(rf_ref, kf_ref, vf_ref, lwf_ref, laf_ref, rb_ref, kb_ref, vb_ref, lwb_ref, lab_ref,
                 s0_ref, wdec_ref, bdec_ref, wic_ref, bic_ref, kkw_ref, ka_ref, rk_ref, hs_ref,
                 of_ref, ob_ref, bonf_ref, bonb_ref, sfin_ref,
                 s_scr, t_scr, w2_scr, mr_scr, ar_scr, bk_scr, gt_scr, *, G, ng):
    L = CHUNK
    g = pl.program_id(1)
    ins = ((rf_ref, kf_ref, vf_ref, lwf_ref, laf_ref), (rb_ref, kb_ref, vb_ref, lwb_ref, lab_ref))
    outs = ((of_ref, bonf_ref), (ob_ref, bonb_ref))

    @pl.when(g == 0)
    def _():
        s_scr[...] = s0_ref[0]

    hs = hs_ref[...]
    lane = lax.broadcasted_iota(jnp.int32, (L, 128), 1)
    rowp = lax.broadcasted_iota(jnp.int32, (L, 128), 0)
    colp = jnp.bitwise_and(lane, 63)
    h0 = lane < 64
    m0 = jnp.where(h0, 1.0, 0.0).astype(BF16)
    m1 = jnp.where(h0, 0.0, 1.0).astype(BF16)
    eye2 = jnp.where(rowp == colp, 1.0, 0.0)
    rowL = lax.broadcasted_iota(jnp.int32, (L, L), 0)
    colL = lax.broadcasted_iota(jnp.int32, (L, L), 1)

    def same(sh):
        return lax.shift_right_logical(rowp, sh) == lax.shift_right_logical(colp, sh)

    same8, same16, same32 = same(3), same(4), same(5)

    def bmask(c):
        return jnp.where(c, 1.0, 0.0).astype(BF16)

    mk8 = bmask(same8)
    merge_masks = (bmask(same16 & (~same8)), bmask(same32 & (~same16)), bmask(~same32))

    def smb(x):
        return jnp.concatenate([x * m0, x * m1], axis=0)

    def sms(s):
        return tuple(smb(x) for x in s)

    def pm(x_b, ys):
        return _mm(jnp.concatenate([x_b, x_b], axis=1), jnp.concatenate(sms(ys), axis=0))

    chains = [(d, p) for d in range(2) for p in range(N_PAIRS)]

    def each(fn, *lists):
        return [fn(*a) for a in zip(*lists)]

    def prep(jb, carry):
        pre, units = [], []
        for j, d in [(jb * PREP_CHUNKS + jo, d) for jo in range(PREP_CHUNKS) for d in range(2)]:
            r_ref, k_ref, v_ref, lw_ref, la_ref = ins[d]
            jn = j if d == 0 else G - 1 - j
            rows = pl.ds(pl.multiple_of(jn * L, L), L)
            u = d * G + j
            r = r_ref[rows, :]
            k = k_ref[rows, :]
            v = v_ref[rows, :]
            dsel = (lane >= 64) if d else h0
            tl = jnp.where(dsel, jnp.tanh(lw_ref[rows, :]), 0.0)
            dw = _dot3(tl, wdec_ref[...]) + bdec_ref[d:d + 1, :]
            logw = (-math.exp(-0.5)) * jax.nn.sigmoid(dw)
            la_m = jnp.where(dsel, la_ref[rows, :], 0.0)
            iclr = jax.nn.sigmoid(_dot3(la_m, wic_ref[...]) + bic_ref[d:d + 1, :])
            kkr = k * kkw_ref[...]
            nrm = jnp.sqrt(_dot_xb2(kkr * kkr, hs))
            kk = kkr / jnp.maximum(nrm, 1e-12)
            kdir = k * (1.0 + (iclr - 1.0) * ka_ref[...])
            bv = kk * iclr
            outs[d][1][rows, :] = _dot_xb2(r * kdir * rk_ref[...], hs) * v

            tri = bmask((rowL <= colL) if d else (rowL >= colL))
            cs = _dot_xa(tri, logw)
            ctot = jnp.sum(logw, axis=0, keepdims=True)
            g_inv = jnp.exp(-cs)
            g_rem = jnp.exp(ctot - cs)
            gt_scr[u] = jnp.exp(ctot)
            At = -kk * jnp.exp(cs - logw)
            Rt = r * jnp.exp(cs)
            Bt = bv * g_inv
            Kt = kdir * g_inv
            Bg = bv * g_rem
            Kg = kdir * g_rem
            for p in range(N_PAIRS):
                sl = slice(128 * p, 128 * (p + 1))
                ar = jnp.concatenate([At[:, sl], Rt[:, sl]], axis=0).astype(BF16)
                ar_scr[u, p] = ar
                bk_scr[u, p] = jnp.concatenate([Bg[:, sl], Kg[:, sl]], axis=0).astype(BF16)
                pre.append((ar, Bt[:, sl].astype(BF16), Kt[:, sl].astype(BF16), v[:, sl].astype(BF16)))
                units.append((d, p, u))

        def stage_m(c, pr):
            d = c[0]
            ar, bt, kt, _ = pr
            strict2 = (rowp < colp) if d else (rowp > colp)
            incl2 = (rowp <= colp) if d else (rowp >= colp)
            M = _mm(ar, jnp.concatenate([smb(bt), smb(kt)], axis=0), _NT)
            N = jnp.where(strict2, M[:L, :128], 0.0)
            Mak = jnp.where(strict2, M[:L, 128:], 0.0)
            mr_scr[c[2], c[1]] = jnp.concatenate([jnp.where(incl2, M[L:, :128], 0.0),
                                                       jnp.where(incl2, M[L:, 128:], 0.0)], axis=1).astype(BF16)
            return N, Mak

        NM = each(stage_m, units, pre)
        Ns = [_split2(nm[0]) for nm in NM]
        N0s = [(n[0] * mk8, n[1] * mk8) for n in Ns]
        T = [eye2 + jnp.where(same8, nm[0], 0.0) for nm in NM]
        P2s = [_split2(pm(n[0], n)) for n in N0s]
        W2 = [_mm(nm[1].astype(BF16), smb(pr[3])) for nm, pr in zip(NM, pre)]
        T = [t + pm(t.astype(BF16), p2) for t, p2 in zip(T, P2s)]
        P4s = [_split2(pm(p2[0], p2)) for p2 in P2s]
        T = [t + pm(t.astype(BF16), p4) for t, p4 in zip(T, P4s)]
        for mk in merge_masks:
            Ts = [_split2(t) for t in T]
            Y = [pm(ts[0], (n[0] * mk, n[1] * mk)) for ts, n in zip(Ts, Ns)]
            T = [t + pm(y.astype(BF16), ts) for t, y, ts in zip(T, Y, Ts)]
        for (d, p, u), t, w2 in zip(units, T, W2):
            t_scr[u, p] = t.astype(BF16)
            w2_scr[u, p] = w2
        return carry

    lax.fori_loop(0, G // PREP_CHUNKS, prep, 0)

    def serial(j, carry):
        us = [d * G + j for d, _ in chains]
        rws = [pl.ds(pl.multiple_of((j if d == 0 else G - 1 - j) * L, L), L) for d, _ in chains]
        sls = [slice(128 * p, 128 * (p + 1)) for _, p in chains]
        S2 = [s_scr[d, p] for d, p in chains]
        XR = [_mm(ar_scr[u, p], smb(s2.astype(BF16)), _NT) for (d, p), u, s2 in zip(chains, us, S2)]
        U = [pm(t_scr[u, p], _split2(xr[:L] + w2_scr[u, p])) for (d, p), u, xr in zip(chains, us, XR)]
        V = [ins[d][2][rw, sl] for (d, p), rw, sl in zip(chains, rws, sls)]
        for (d, p), u, rw, sl, s2, xr, uu, vv in zip(chains, us, rws, sls, S2, XR, U, V):
            UVt = jnp.concatenate([uu, vv], axis=0).T
            Rm = _mm(UVt.astype(BF16), bk_scr[u, p])
            s_scr[d, p] = gt_scr[u][:, sl] * s2 + jnp.where(h0, Rm[:HEAD], Rm[HEAD:])
        for (d, p), u, rw, sl, xr, uu, vv in zip(chains, us, rws, sls, XR, U, V):
            rhs = jnp.concatenate([smb(uu.astype(BF16)), smb(vv.astype(BF16))], axis=0)
            outs[d][0][rw, sl] = xr[L:] + _mm(mr_scr[u, p], rhs)
        return carry

    lax.fori_loop(0, G, serial, 0)

    @pl.when(g == ng - 1)
    def _():
        for d in range(2):
            for p in range(N_PAIRS):
                sp = s_scr[d, p]
                sfin_ref[0, d, 2 * p] = sp[:, :HEAD]
                sfin_ref[0, d, 2 * p + 1] = sp[:, HEAD:]


def _rwkv(proj, s0_pairs, wdec, bdec, wic, bic, kkw, ka, rk, hs, nseq, T):
    ntok = proj.shape[0]
    nc = T // CHUNK
    G = min(nc, 8)
    ng = nc // G
    GL = G * CHUNK

    def fwd(b, g):
        return b * ng + g

    def bwd(b, g):
        return b * ng + ng - 1 - g

    def pspecs(rowblk):
        return [pl.BlockSpec((GL, D_A), lambda b, g: (rowblk(b, g), COL_R)),
                pl.BlockSpec((GL, D_A), lambda b, g: (rowblk(b, g), COL_K)),
                pl.BlockSpec((GL, D_A), lambda b, g: (rowblk(b, g), COL_V)),
                pl.BlockSpec((GL, 128), lambda b, g: (rowblk(b, g), COL_LW)),
                pl.BlockSpec((GL, 128), lambda b, g: (rowblk(b, g), COL_LA))]

    def wspec(shape):
        return pl.BlockSpec(shape, lambda b, g: (0,) * len(shape))

    ospec_f = pl.BlockSpec((GL, D_A), lambda b, g: (fwd(b, g), 0))
    ospec_b = pl.BlockSpec((GL, D_A), lambda b, g: (bwd(b, g), 0))
    tok = jax.ShapeDtypeStruct((ntok, D_A), F32)
    return pl.pallas_call(
        functools.partial(_rwkv_kernel, G=G, ng=ng),
        grid=(nseq, ng),
        in_specs=pspecs(fwd) + pspecs(bwd) + [
            pl.BlockSpec((1, 2, N_PAIRS, HEAD, 128), lambda b, g: (b, 0, 0, 0, 0)),
            wspec((128, D_A)), wspec((2, D_A)), wspec((128, D_A)), wspec((2, D_A)),
            wspec((1, D_A)), wspec((1, D_A)), wspec((1, D_A)), wspec((D_A, D_A))],
        out_specs=[ospec_f, ospec_b, ospec_f, ospec_b,
                   pl.BlockSpec((1, 2, N_HEADS, HEAD, HEAD), lambda b, g: (b, 0, 0, 0, 0))],
        out_shape=[tok, tok, tok, tok, jax.ShapeDtypeStruct((nseq, 2, N_HEADS, HEAD, HEAD), F32)],
        scratch_shapes=[pltpu.VMEM((2, N_PAIRS, HEAD, 128), F32),
                        pltpu.VMEM((2 * G, N_PAIRS, CHUNK, 128), BF16),
                        pltpu.VMEM((2 * G, N_PAIRS, CHUNK, 128), F32),
                        pltpu.VMEM((2 * G, N_PAIRS, CHUNK, 256), BF16),
                        pltpu.VMEM((2 * G, N_PAIRS, 2 * CHUNK, 128), BF16),
                        pltpu.VMEM((2 * G, N_PAIRS, 2 * CHUNK, 128), BF16),
                        pltpu.VMEM((2 * G, 1, D_A), F32)],
        compiler_params=_cparams(("arbitrary", "arbitrary")),
        name="rwkv_scan",
    )(*([proj] * 10), s0_pairs, wdec, bdec, wic, bic, kkw, ka, rk, hs)


def _gelu_tanh(x):
    return 0.5 * x * (1.0 + jnp.tanh(math.sqrt(2.0 / math.pi) * (x + 0.044715 * (x * x * x))))


def _rglru_kernel(xr_ref, xg_ref, h0_ref, cw_ref, cb_ref, wa_ref, ba_ref, wi_ref, bi_ref, lam_ref,
                  y_ref, hfin_ref, a_scr, u_scr, *, T, lrow):
    RB = min(T, 256)
    nblk = T // RB
    rowi = lax.broadcasted_iota(jnp.int32, (RB, D_B), 0)
    pos = jnp.bitwise_and(rowi, lrow - 1)
    cw = cw_ref[...]
    lam = lam_ref[...]
    sp = jnp.maximum(-lam, 0.0) + jnp.log1p(jnp.exp(-jnp.abs(lam)))

    def gates(blk, carry):
        r0 = pl.multiple_of(blk * RB, RB)
        x = xr_ref[pl.ds(r0, RB), :]
        xm1 = jnp.where(pos >= 1, pltpu.roll(x, 1, 0), 0.0)
        xm2 = jnp.where(pos >= 2, pltpu.roll(x, 2, 0), 0.0)
        xp1 = jnp.where(pos <= lrow - 2, pltpu.roll(x, RB - 1, 0), 0.0)
        xc = xm2 * cw[0:1] + xm1 * cw[1:2] + x * cw[2:3] + xp1 * cw[3:4] + cb_ref[...]
        for dd in range(2):
            gr = jax.nn.sigmoid(_dot1(xc, wa_ref[dd]) + ba_ref[dd:dd + 1, :])
            gi = jax.nn.sigmoid(_dot1(xc, wi_ref[dd]) + bi_ref[dd:dd + 1, :])
            log_a = (-RGLRU_C) * gr * sp[dd:dd + 1, :]
            a = jnp.exp(log_a)
            a_scr[dd, pl.ds(r0, RB), :] = a
            u_scr[dd, pl.ds(r0, RB), :] = jnp.sqrt((1.0 - a) * (1.0 + a)) * gi * xc
        return carry

    lax.fori_loop(0, nblk, gates, 0)

    SUB = 8
    rowt = lax.broadcasted_iota(jnp.int32, (SUB, D_B), 0)

    def tile_scan(a, u, h_in, rev):
        for s_ in (1, 2, 4):
            keep = (rowt < SUB - s_) if rev else (rowt >= s_)
            sh = SUB - s_ if rev else s_
            a_sh = jnp.where(keep, pltpu.roll(a, sh, 0), 1.0)
            u_sh = jnp.where(keep, pltpu.roll(u, sh, 0), 0.0)
            u = a * u_sh + u
            a = a * a_sh
        return a * h_in + u

    def step(i, carry):
        hf, hb = carry
        rf = pl.ds(pl.multiple_of(i * SUB, SUB), SUB)
        hf_tile = tile_scan(a_scr[0, rf, :], u_scr[0, rf, :], hf, False)
        u_scr[0, rf, :] = hf_tile
        rb = pl.ds(pl.multiple_of(T - SUB - i * SUB, SUB), SUB)
        hb_tile = tile_scan(a_scr[1, rb, :], u_scr[1, rb, :], hb, True)
        u_scr[1, rb, :] = hb_tile
        return hf_tile[SUB - 1:SUB], hb_tile[0:1]

    h0 = h0_ref[0]
    hf, hb = lax.fori_loop(0, T // SUB, step, (h0[0:1], h0[1:2]), unroll=2)
    hfin_ref[0] = jnp.concatenate([hf, hb], axis=0)

    def outp(blk, carry):
        r0 = pl.multiple_of(blk * RB, RB)
        h = u_scr[0, pl.ds(r0, RB), :] + u_scr[1, pl.ds(r0, RB), :]
        y_ref[pl.ds(r0, RB), :] = h * _gelu_tanh(xg_ref[pl.ds(r0, RB), :])
        return carry

    lax.fori_loop(0, nblk, outp, 0)


def _rglru(proj, h0, conv_w, conv_b, wa_bd, ba, wi_bd, bi, lam, nseq, T, lrow):
    ntok = proj.shape[0]

    def wspec(shape):
        return pl.BlockSpec(shape, lambda b: (0,) * len(shape))

    return pl.pallas_call(
        functools.partial(_rglru_kernel, T=T, lrow=lrow),
        grid=(nseq,),
        in_specs=[pl.BlockSpec((T, D_B), lambda b: (b, COL_XR)),
                  pl.BlockSpec((T, D_B), lambda b: (b, COL_XGB)),
                  pl.BlockSpec((1, 2, D_B), lambda b: (b, 0, 0)),
                  wspec((4, D_B)), wspec((1, D_B)),
                  wspec((2, D_B, D_B)), wspec((2, D_B)), wspec((2, D_B, D_B)), wspec((2, D_B)),
                  wspec((2, D_B))],
        out_specs=[pl.BlockSpec((T, D_B), lambda b: (b, 0)),
                   pl.BlockSpec((1, 2, D_B), lambda b: (b, 0, 0))],
        out_shape=[jax.ShapeDtypeStruct((ntok, D_B), F32),
                   jax.ShapeDtypeStruct((nseq, 2, D_B), F32)],
        scratch_shapes=[pltpu.VMEM((2, T, D_B), F32), pltpu.VMEM((2, T, D_B), F32)],
        compiler_params=_cparams(("arbitrary",)),
        name="rglru",
    )(proj, proj, h0, conv_w, conv_b, wa_bd, ba, wi_bd, bi, lam)


def _mixout_kernel(of_ref, ob_ref, bf_ref, bb_ref, yb_ref, lg_ref, x_ref, mod_ref, wgu_ref, avg_ref,
                   gnw_ref, gnb_ref, wout_ref, l1w_ref, l1b_ref, wr_ref, br_ref,
                   x1_ref, u2_ref, eid_ref, prob_ref):
    m = mod_ref[0]
    avg = avg_ref[...]
    wkv = of_ref[...] + ob_ref[...]
    mu = _dot_xb2(wkv, avg)
    dv = wkv - mu
    var = _dot_xb2(dv * dv, avg)
    gn = dv * lax.rsqrt(var + GN_EPS) * gnw_ref[...] + gnb_ref[...]
    g = _dot1(jax.nn.sigmoid(lg_ref[...]), wgu_ref[...])
    ya = (gn + (bf_ref[...] + bb_ref[...])) * g
    mix = _dot1(ya, wout_ref[0:D_A, :]) + _dot1(yb_ref[...], wout_ref[D_A:, :])
    x1 = _layer_norm(DEEPNORM_ALPHA * x_ref[...] + m[2:3] * mix) * l1w_ref[...] + l1b_ref[...]
    x1_ref[...] = x1
    u2 = _layer_norm(x1) * (1.0 + m[4:5]) + m[3:4]
    _store_tok_tiles(u2_ref, u2)
    logits = _dot3(u2, wr_ref[...]) + br_ref[...]
    tm = logits.shape[0]
    lane = lax.broadcasted_iota(jnp.int32, (tm, N_EXPERTS), 1)
    lane4 = lax.broadcasted_iota(jnp.int32, (tm, TOP_K), 1)
    work = logits
    tops, idxs = [], []
    for _ in range(TOP_K):
        mx = jnp.max(work, axis=1, keepdims=True)
        idx = jnp.min(jnp.where(work == mx, lane, N_EXPERTS), axis=1, keepdims=True)
        tops.append(mx)
        idxs.append(idx)
        work = jnp.where(lane == idx, -jnp.inf, work)
    es = [jnp.exp(t - tops[0]) for t in tops]
    den = es[0] + es[1] + es[2] + es[3]
    eid = jnp.zeros((tm, TOP_K), jnp.int32)
    prob = jnp.zeros((tm, TOP_K), F32)
    for j in range(TOP_K):
        eid = jnp.where(lane4 == j, idxs[j], eid)
        prob = jnp.where(lane4 == j, es[j] / den, prob)
    eid_ref[...] = eid
    prob_ref[...] = prob


def _mixout_alias_kernel(*refs):
    _mixout_kernel(*refs[:17], *refs[18:])


def _mixout_first_kernel(*refs, ntiles):
    i = pl.program_id(0)

    @pl.when(i < ntiles)
    def _():
        _mixout_kernel(*refs)

    @pl.when(i >= ntiles)
    def _():
        refs[18][...] = jnp.zeros_like(refs[18])


def _mixout(o_f, o_b, bon_f, bon_b, yb, proj, x, mod_rows, tok_per_row, wgu, avg, gnw, gnb, wout, l1w, l1b,
            wr, br, u2_all, tok_base, ntok_all):
    ntok = x.shape[0]
    tm = 256
    per = tok_per_row // tm
    base_tiles = tok_base // tm
    alias = u2_all is not None
    ntiles = ntok // tm

    def wspec(shape):
        return pl.BlockSpec(shape, lambda i: (0,) * len(shape))

    def tl(i):
        return jnp.minimum(i, ntiles - 1)

    return pl.pallas_call(
        _mixout_alias_kernel if alias else functools.partial(_mixout_first_kernel, ntiles=ntiles),
        grid=(ntiles if alias else ntok_all // tm,),
        input_output_aliases={17: 1} if alias else {},
        in_specs=[pl.BlockSpec((tm, D_A), lambda i: (tl(i), 0)),
                  pl.BlockSpec((tm, D_A), lambda i: (tl(i), 0)),
                  pl.BlockSpec((tm, D_A), lambda i: (tl(i), 0)),
                  pl.BlockSpec((tm, D_A), lambda i: (tl(i), 0)),
                  pl.BlockSpec((tm, D_B), lambda i: (tl(i), 0)),
                  pl.BlockSpec((tm, LORA_G), lambda i: (tl(i), COL_LG)),
                  pl.BlockSpec((tm, D_MODEL), lambda i: (tl(i), 0)),
                  pl.BlockSpec((1, 6, D_MODEL), lambda i: (tl(i) // per, 0, 0)),
                  wspec((LORA_G, D_A)), wspec((D_A, D_A)), wspec((1, D_A)), wspec((1, D_A)),
                  wspec((D_MODEL, D_MODEL)), wspec((1, D_MODEL)), wspec((1, D_MODEL)),
                  wspec((D_MODEL, N_EXPERTS)), wspec((1, N_EXPERTS))]
                 + ([pl.BlockSpec(memory_space=pl.ANY)] if alias else []),
        out_specs=[pl.BlockSpec((tm, D_MODEL), lambda i: (tl(i), 0)),
                   pl.BlockSpec((tm * ROW_TILE, 128), lambda i: (i + base_tiles, 0)),
                   pl.BlockSpec((tm, TOP_K), lambda i: (tl(i), 0)),
                   pl.BlockSpec((tm, TOP_K), lambda i: (tl(i), 0))],
        out_shape=[jax.ShapeDtypeStruct((ntok, D_MODEL), F32),
                   jax.ShapeDtypeStruct((ntok_all * ROW_TILE, 128), F32),
                   jax.ShapeDtypeStruct((ntok, TOP_K), jnp.int32),
                   jax.ShapeDtypeStruct((ntok, TOP_K), F32)],
        compiler_params=_cparams(("arbitrary",)),
        name="mixout",
    )(o_f, o_b, bon_f, bon_b, yb, proj, x, mod_rows, wgu, avg, gnw, gnb, wout, l1w, l1b, wr, br,
      *([u2_all] if alias else []))


MOE_TM = 256
ROUTE_BLK = 256
INVERT_CHUNK = 2048


def _route_kernel(eid_ref, slot_ref, te_ref, nv_ref, c_scr, *, ntok, nt_pad):
    nblk = ntok // ROUTE_BLK
    lane32 = lax.broadcasted_iota(jnp.int32, (ROUTE_BLK, N_EXPERTS), 1)
    lane4 = lax.broadcasted_iota(jnp.int32, (ROUTE_BLK, TOP_K), 1)
    rowb = lax.broadcasted_iota(jnp.int32, (ROUTE_BLK, ROUTE_BLK), 0)
    colb = lax.broadcasted_iota(jnp.int32, (ROUTE_BLK, ROUTE_BLK), 1)
    tri = jnp.where(rowb > colb, 1.0, 0.0).astype(BF16)

    def count(b, carry):
        rows = pl.ds(pl.multiple_of(b * ROUTE_BLK, ROUTE_BLK), ROUTE_BLK)
        e = eid_ref[rows, :]
        hot = jnp.zeros((ROUTE_BLK, N_EXPERTS), F32)
        for j in range(TOP_K):
            hot = hot + jnp.where(lane32 == e[:, j:j + 1], 1.0, 0.0)
        c_scr[rows, :] = _mm(tri, hot.astype(BF16)) + carry
        return carry + jnp.sum(hot, axis=0, keepdims=True)

    n = lax.fori_loop(0, nblk, count, jnp.zeros((1, N_EXPERTS), F32))
    padded = jnp.floor((n + (MOE_TM - 1)) * (1.0 / MOE_TM)) * MOE_TM
    r32 = lax.broadcasted_iota(jnp.int32, (N_EXPERTS, N_EXPERTS), 0)
    c32 = lax.broadcasted_iota(jnp.int32, (N_EXPERTS, N_EXPERTS), 1)
    upper = jnp.where(r32 < c32, 1.0, 0.0).astype(BF16)
    off = _dot_xb(jnp.broadcast_to(padded, (8, N_EXPERTS)), upper)[0:1]
    gend = off + padded

    def place(b, carry):
        rows = pl.ds(pl.multiple_of(b * ROUTE_BLK, ROUTE_BLK), ROUTE_BLK)
        e = eid_ref[rows, :]
        base = off + c_scr[rows, :]
        s = jnp.zeros((ROUTE_BLK, TOP_K), F32)
        for j in range(TOP_K):
            sj = jnp.sum(jnp.where(lane32 == e[:, j:j + 1], base, 0.0), axis=1, keepdims=True)
            s = jnp.where(lane4 == j, sj, s)
        slot_ref[rows, :] = s.astype(jnp.int32)
        return carry

    lax.fori_loop(0, nblk, place, 0)

    gcol = jnp.sum(jnp.where(r32 == c32, jnp.broadcast_to(gend, (N_EXPERTS, N_EXPERTS)), 0.0),
                   axis=1, keepdims=True)
    tstart = (lax.broadcasted_iota(jnp.int32, (N_EXPERTS, nt_pad), 1) * MOE_TM).astype(F32)
    te = jnp.sum(jnp.where(gcol <= tstart, 1.0, 0.0), axis=0, keepdims=True)
    te_ref[...] = jnp.minimum(te, N_EXPERTS - 1.0).astype(jnp.int32)
    total = jnp.sum(padded, axis=1, keepdims=True)
    nv_ref[...] = jnp.broadcast_to(total * (1.0 / MOE_TM), (1, 128)).astype(jnp.int32)


def _route(eid):
    ntok = eid.shape[0]
    nt_max = ntok * TOP_K // MOE_TM + N_EXPERTS
    nt_pad = -(-(nt_max + 1) // 128) * 128
    full = lambda shape: pl.BlockSpec(shape, lambda i: (0,) * len(shape))
    slot, te, nv = pl.pallas_call(
        functools.partial(_route_kernel, ntok=ntok, nt_pad=nt_pad),
        grid=(1,),
        in_specs=[full((ntok, TOP_K))],
        out_specs=[full((ntok, TOP_K)), full((1, nt_pad)), full((1, 128))],
        out_shape=[jax.ShapeDtypeStruct((ntok, TOP_K), jnp.int32),
                   jax.ShapeDtypeStruct((1, nt_pad), jnp.int32),
                   jax.ShapeDtypeStruct((1, 128), jnp.int32)],
        scratch_shapes=[pltpu.VMEM((ntok, N_EXPERTS), F32)],
        compiler_params=_cparams(("arbitrary",)),
        name="moe_route",
    )(eid)
    return slot.reshape(ntok * TOP_K), te.reshape(nt_pad), nv.reshape(128), nt_max


def _invert_kernel(slot_ref, inv_ref, fill_vmem, slot_smem, inv_smem, sem_in, sem_out, *, nassign, plane):
    slot_id = lax.broadcasted_iota(jnp.int32, fill_vmem.shape, 0)
    fill_vmem[...] = jnp.bitwise_and(slot_id, MOE_TM - 1) * ROW_TILE + TOP_K * plane
    fill = pltpu.make_async_copy(fill_vmem, inv_smem, sem_out)
    fill.start()
    fill.wait()

    def chunk(ci, c):
        a0 = pl.multiple_of(ci * INVERT_CHUNK, INVERT_CHUNK)
        cp = pltpu.make_async_copy(slot_ref.at[pl.ds(a0, INVERT_CHUNK)], slot_smem, sem_in)
        cp.start()
        cp.wait()

        def put(tt, c2):
            base = (lax.shift_right_logical(a0, 2) + tt) * ROW_TILE
            for k in range(TOP_K):
                inv_smem[slot_smem[tt * TOP_K + k]] = base + k * plane
            return c2

        lax.fori_loop(0, INVERT_CHUNK // TOP_K, put, 0, unroll=2)
        return c

    lax.fori_loop(0, nassign // INVERT_CHUNK, chunk, 0)
    out = pltpu.make_async_copy(inv_smem, inv_ref, sem_out)
    out.start()
    out.wait()


def _invert(slot, ntiles, ntok):
    nassign = slot.shape[0]
    plane = ntok * ROW_TILE
    assert plane & (plane - 1) == 0, "token count must be a power of two"
    anyspec = pl.BlockSpec(memory_space=pl.ANY)
    return pl.pallas_call(
        functools.partial(_invert_kernel, nassign=nassign, plane=plane),
        grid=(1,),
        in_specs=[anyspec], out_specs=anyspec,
        out_shape=jax.ShapeDtypeStruct((ntiles * MOE_TM,), jnp.int32),
        scratch_shapes=[pltpu.VMEM((ntiles * MOE_TM,), jnp.int32),
                        pltpu.SMEM((INVERT_CHUNK,), jnp.int32), pltpu.SMEM((ntiles * MOE_TM,), jnp.int32),
                        pltpu.SemaphoreType.DMA(()), pltpu.SemaphoreType.DMA(())],
        compiler_params=_cparams(("arbitrary",)),
        name="moe_invert",
    )(slot)


def _expert_kernel(te_ref, nv_ref, inv_ref, u2_ref, wg_ref, bg_ref, wu_ref, bu_ref, wd_ref, bd_ref, y4_ref,
                   xbuf, ybuf, wg_b, wu_b, wd_b, sem_g, sem_s, *, ntok):
    i = pl.program_id(0)
    nvalid = nv_ref[0]
    valid = i < nvalid
    trash = TOP_K * ntok
    plane = ntok * ROW_TILE

    def rows8(row):
        return pl.ds(pl.multiple_of(row, ROW_TILE), ROW_TILE)

    def gather_copy(tile, r, buf):
        src = jnp.bitwise_and(inv_ref[tile * MOE_TM + r], plane - 1)
        return pltpu.make_async_copy(u2_ref.at[rows8(src)], xbuf.at[buf, _tok_rows(r)], sem_g.at[buf])

    def scatter_copy(tile, r):
        return pltpu.make_async_copy(ybuf.at[_tok_rows(r)], y4_ref.at[rows8(inv_ref[tile * MOE_TM + r])], sem_s)

    def wait_gather(buf):
        pltpu.make_async_copy(xbuf.at[buf], xbuf.at[buf], sem_g.at[buf]).wait()

    def wait_scatter():
        pltpu.make_async_copy(ybuf, ybuf, sem_s).wait()

    @pl.when(i == 0)
    def _():
        def first(r2, c):
            gather_copy(0, 2 * r2, 0).start(priority=0)
            gather_copy(0, 2 * r2 + 1, 0).start(priority=1)
            return c

        lax.fori_loop(0, MOE_TM // 2, first, 0)
        ybuf[...] = jnp.zeros_like(ybuf)
        spare = pltpu.make_async_copy(ybuf, y4_ref.at[_tok_rows(trash, MOE_TM)], sem_s)
        spare.start()
        spare.wait()

    fresh = jnp.logical_or(i == 0, te_ref[i] != te_ref[jnp.maximum(i - 1, 0)])

    @pl.when(jnp.logical_and(valid, fresh))
    def _():
        wg_b[...] = wg_ref[0].astype(BF16)
        wu_b[...] = wu_ref[0].astype(BF16)
        wd_b[...] = wd_ref[0].astype(BF16)

    @pl.when(valid)
    def _():
        buf = jnp.bitwise_and(i, 1)
        nxt = jnp.minimum(i + 1, nvalid - 1)
        wait_gather(buf)
        x = _load_tok_tiles(xbuf, MOE_TM, (buf,)).astype(BF16)
        FC = 512
        nfc = D_FF // FC
        per = MOE_TM // nfc
        y = jnp.zeros((MOE_TM, D_MODEL), F32)
        for j in range(nfc):
            for r in range(j * per, (j + 1) * per):
                gather_copy(nxt, r, 1 - buf).start(priority=r % 2)
            cs = slice(j * FC, (j + 1) * FC)
            gate = jnp.minimum(_mm(x, wg_b[:, cs]) + bg_ref[0, :, cs], SWIGLU_LIMIT)
            up = jnp.clip(_mm(x, wu_b[:, cs]) + bu_ref[0, :, cs], -SWIGLU_LIMIT, SWIGLU_LIMIT)
            h = (up + 1.0) * gate * jax.nn.sigmoid(SWIGLU_ALPHA * gate)
            y = y + _mm(h.astype(BF16), wd_b[cs, :])

        @pl.when(i > 0)
        def _():
            wait_scatter()

        _store_tok_tiles(ybuf, y + bd_ref[0])
        for r in range(MOE_TM):
            scatter_copy(i, r).start(priority=r % 2)

    @pl.when(i == nvalid)
    def _():
        wait_gather(jnp.bitwise_and(nvalid, 1))
        wait_scatter()


def _experts(te, nv, inv, u2_all, nt_max, wg, bg, wu, bu, wd, bd):
    ntok = u2_all.shape[0] // ROW_TILE

    def tile(i, te, nv, inv):
        return te[jnp.minimum(i, nv[0] - 1)]

    wspec = pl.BlockSpec((1, D_MODEL, D_FF), lambda i, te, nv, inv: (tile(i, te, nv, inv), 0, 0))
    bspec = pl.BlockSpec((1, 1, D_FF), lambda i, te, nv, inv: (tile(i, te, nv, inv), 0, 0))
    anyspec = pl.BlockSpec(memory_space=pl.ANY)
    return pl.pallas_call(
        functools.partial(_expert_kernel, ntok=ntok),
        grid_spec=pltpu.PrefetchScalarGridSpec(
            num_scalar_prefetch=3, grid=(nt_max + 1,),
            in_specs=[anyspec, wspec, bspec, wspec, bspec, wspec, bspec], out_specs=anyspec,
            scratch_shapes=[pltpu.VMEM((2, MOE_TM * ROW_TILE, 128), F32),
                            pltpu.VMEM((MOE_TM * ROW_TILE, 128), F32),
                            pltpu.VMEM((D_MODEL, D_FF), BF16), pltpu.VMEM((D_MODEL, D_FF), BF16),
                            pltpu.VMEM((D_MODEL, D_FF), BF16),
                            pltpu.SemaphoreType.DMA((2,)), pltpu.SemaphoreType.DMA(())]),
        out_shape=jax.ShapeDtypeStruct(((TOP_K * ntok + MOE_TM) * ROW_TILE, 128), F32),
        compiler_params=_cparams(("arbitrary",)),
        name="moe_experts",
    )(te, nv, inv, u2_all, wg, bg.reshape(N_EXPERTS, 1, D_FF), wu, bu.reshape(N_EXPERTS, 1, D_FF),
      wd, bd.reshape(N_EXPERTS, 1, D_MODEL))


def _combine_kernel(y0_ref, y1_ref, y2_ref, y3_ref, prob_ref, x1_ref, mod_ref, w_ref, b_ref, o_ref):
    p = prob_ref[...]
    tm = p.shape[0]
    moe = p[:, 0:1] * _load_tok_tiles(y0_ref, tm)
    for j, y_ref in enumerate((y1_ref, y2_ref, y3_ref), start=1):
        moe = moe + p[:, j:j + 1] * _load_tok_tiles(y_ref, tm)
    m = mod_ref[0]
    o_ref[...] = _layer_norm(DEEPNORM_ALPHA * x1_ref[...] + m[5:6] * moe) * w_ref[...] + b_ref[...]


def _combine(y4, prob, x1, tok_base, ntok_all, mod_rows, tok_per_row, w, b):
    ntok = x1.shape[0]
    tm = 256
    per = tok_per_row // tm

    def yspec(k):
        return pl.BlockSpec((tm * ROW_TILE, 128), lambda i: ((k * ntok_all + tok_base) // tm + i, 0))

    return pl.pallas_call(
        _combine_kernel,
        grid=(ntok // tm,),
        in_specs=[yspec(0), yspec(1), yspec(2), yspec(3),
                  pl.BlockSpec((tm, TOP_K), lambda i: (i, 0)),
                  pl.BlockSpec((tm, D_MODEL), lambda i: (i, 0)),
                  pl.BlockSpec((1, 6, D_MODEL), lambda i: (i // per, 0, 0)),
                  pl.BlockSpec((1, D_MODEL), lambda i: (0, 0)),
                  pl.BlockSpec((1, D_MODEL), lambda i: (0, 0))],
        out_specs=pl.BlockSpec((tm, D_MODEL), lambda i: (i, 0)),
        out_shape=jax.ShapeDtypeStruct((ntok, D_MODEL), F32),
        compiler_params=_cparams(("arbitrary",)),
        name="moe_combine",
    )(y4, y4, y4, y4, prob, x1, mod_rows, w, b)


def _block_diag(w):
    nb, bb, _ = w.shape
    eye = jnp.eye(nb, dtype=w.dtype)
    return jnp.einsum('nij,nm->nimj', w, eye).reshape(nb * bb, nb * bb)


def kernel(x_prompt, x_sample, state_rwkv, state_rglru, c, c_ctx, w_mod, b_mod, w_in, w_decay_up, b_decay, w_iclr_up, b_iclr, w_gate_up, k_k, k_a, r_k, gn_w, gn_b, conv_w, conv_b, w_rg_a, b_rg_a, w_rg_i, b_rg_i, lam, w_out, ln1_w, ln1_b, w_router, b_router, w_e_gate, b_e_gate, w_e_up, b_e_up, w_e_down, b_e_down, ln2_w, ln2_b):
    Bp, Tp, D = x_prompt.shape
    Bs, Ts, _ = x_sample.shape
    l = 0
    row = lambda a: a[l].reshape(1, -1)

    cond8 = jnp.concatenate([c_ctx[None, :], c, jnp.zeros((8 - 1 - Bs, D), F32)], axis=0)
    mod = _modulation(cond8, w_mod[l], b_mod[l]).reshape(8, 6, D)
    mod_p, mod_s = mod[0:1], mod[1:1 + Bs]

    w_in_b = w_in[l].astype(BF16)
    w_out_b = w_out[l].astype(BF16)
    head_id = jnp.arange(D_A) // HEAD
    hs = (head_id[:, None] == head_id[None, :]).astype(BF16)
    avg = (hs.astype(F32) / HEAD).astype(BF16)
    wdec = w_decay_up[l].reshape(2 * LORA, D_A)
    wic = w_iclr_up[l].reshape(2 * LORA, D_A)
    wa_bd = jnp.stack([_block_diag(w_rg_a[l, 0]), _block_diag(w_rg_a[l, 1])])
    wi_bd = jnp.stack([_block_diag(w_rg_i[l, 0]), _block_diag(w_rg_i[l, 1])])

    xp = x_prompt.reshape(Bp * Tp, D)
    xs = x_sample.reshape(Bs * Ts, D)

    s0_s = state_rwkv[:, l].reshape(Bs, 2, N_PAIRS, 2, HEAD, HEAD).transpose(0, 1, 2, 4, 3, 5)
    s0_s = s0_s.reshape(Bs, 2, N_PAIRS, HEAD, 2 * HEAD)
    s0_p = jnp.zeros((Bp, 2, N_PAIRS, HEAD, 2 * HEAD), F32)
    h0_p = jnp.zeros((Bp, 2, D_B), F32)
    h0_s = state_rglru[:, l]

    ntok_all = Bp * Tp + Bs * Ts
    outs = []
    u2_all = None
    for x, mod_rows, tok_per_row, nseq, T, s0, h0, lrow, tok_base in (
            (xp, mod_p, Bp * Tp, Bp, Tp, s0_p, h0_p, Tp, 0),
            (xs, mod_s, Ts, Bs, Ts, s0_s, h0_s, GRID_W, Bp * Tp)):
        proj = _inproj(x, mod_rows, w_in_b, tok_per_row)
        o_f, o_b, bon_f, bon_b, s_fin = _rwkv(proj, s0, wdec, b_decay[l], wic, b_iclr[l], row(k_k), row(k_a),
                                              r_k[l].reshape(1, D_A), hs, nseq, T)
        yb, h_fin = _rglru(proj, h0, conv_w[l], row(conv_b), wa_bd, b_rg_a[l], wi_bd, b_rg_i[l], lam[l],
                           nseq, T, lrow)
        x1, u2_all, eid, prob = _mixout(o_f, o_b, bon_f, bon_b, yb, proj, x, mod_rows, tok_per_row,
                                        w_gate_up[l], avg, row(gn_w), row(gn_b), w_out_b, row(ln1_w),
                                        row(ln1_b), w_router[l], row(b_router), u2_all, tok_base, ntok_all)
        outs.append((x1, eid, prob, s_fin, h_fin))

    (x1_p, eid_p, prob_p, sfin_p, hfin_p), (x1_s, eid_s, prob_s, _, _) = outs
    slot, te, nv, nt_max = _route(jnp.concatenate([eid_p, eid_s], axis=0))
    inv = _invert(slot, nt_max, ntok_all)
    y4 = _experts(te, nv, inv, u2_all, nt_max, w_e_gate[l], b_e_gate[l], w_e_up[l], b_e_up[l], w_e_down[l],
                  b_e_down[l])
    y_p = _combine(y4, prob_p, x1_p, 0, ntok_all, mod_p, Bp * Tp, row(ln2_w), row(ln2_b))
    y_s = _combine(y4, prob_s, x1_s, Bp * Tp, ntok_all, mod_s, Ts, row(ln2_w), row(ln2_b))
    return (y_p.reshape(Bp, Tp, D), y_s.reshape(Bs, Ts, D),
            sfin_p[:, None], hfin_p[:, None])
```

```python
import functools
import math

import jax
import jax.numpy as jnp
from jax import lax
from jax.experimental import pallas as pl
from jax.experimental.pallas import tpu as pltpu

F32 = jnp.float32
BF16 = jnp.bfloat16

D_MODEL = 1024
D_A = 512
D_B = 512
HEAD = 64
N_HEADS = 8
N_PAIRS = N_HEADS // 2
GRID_W = 64
LORA = 64
LORA_G = 128
RGLRU_C = 8.0
N_EXPERTS = 32
TOP_K = 4
D_FF = 1024
SWIGLU_LIMIT = 7.0
SWIGLU_ALPHA = 1.702
LN_EPS = 1e-5
GN_EPS = 1e-5 * HEAD
D_IN = 3 * D_A + 2 * D_B + 2 * LORA + 2 * LORA + LORA_G
DEPTH = 1
DEEPNORM_ALPHA = (2 * DEPTH) ** 0.25

COL_R, COL_K, COL_V, COL_XR, COL_XGB = 0, 1, 2, 3, 4
COL_LW, COL_LA, COL_LG = 20, 21, 22

CHUNK = 64
PREP_CHUNKS = 2
VMEM_LIMIT = 56 * 1024 * 1024

_NN = (((1,), (0,)), ((), ()))
_NT = (((1,), (1,)), ((), ()))


def _mm(a, b, dims=_NN):
    return lax.dot_general(a, b, dims, preferred_element_type=F32)


def _split2(x):
    hi = x.astype(BF16)
    lo = (x - hi.astype(F32)).astype(BF16)
    return hi, lo


def _split3(x):
    hi = x.astype(BF16)
    r1 = x - hi.astype(F32)
    mid = r1.astype(BF16)
    lo = (r1 - mid.astype(F32)).astype(BF16)
    return hi, mid, lo


def _dot1(a, b, dims=_NN):
    return _mm(a.astype(BF16), b.astype(BF16), dims)


def _dot3(a, b):
    ah, al = _split2(a)
    bh, bl = _split2(b)
    return _mm(jnp.concatenate([ah, ah, al], axis=1), jnp.concatenate([bh, bl, bh], axis=0))


def _dot_xb(a, b_bf16):
    return _mm(jnp.concatenate(_split3(a), axis=1), jnp.concatenate([b_bf16] * 3, axis=0))


def _dot_xb2(a, b_bf16):
    return _mm(jnp.concatenate(_split2(a), axis=1), jnp.concatenate([b_bf16] * 2, axis=0))


def _dot_xa(a_bf16, b):
    return _mm(jnp.concatenate([a_bf16] * 3, axis=1), jnp.concatenate(_split3(b), axis=0))


ROW_TILE = 8


def _tok_rows(t, n=1):
    return pl.ds(pl.multiple_of(t * ROW_TILE, ROW_TILE), n * ROW_TILE)


def _load_tok_tiles(ref, ntok, lead=()):
    return jnp.concatenate([ref[lead + (pl.ds(s_, ntok, stride=ROW_TILE), slice(None))]
                            for s_ in range(ROW_TILE)], axis=1)


def _store_tok_tiles(ref, x):
    ntok = x.shape[0]
    for s_ in range(ROW_TILE):
        ref[pl.ds(s_, ntok, stride=ROW_TILE), :] = x[:, 128 * s_:128 * (s_ + 1)]


def _layer_norm(x):
    mu = jnp.mean(x, axis=-1, keepdims=True)
    xc = x - mu
    var = jnp.mean(xc * xc, axis=-1, keepdims=True)
    return xc * lax.rsqrt(var + LN_EPS)


def _cparams(sem):
    return pltpu.CompilerParams(dimension_semantics=sem, vmem_limit_bytes=VMEM_LIMIT)


def _mod_kernel(c_ref, w_ref, b_ref, o_ref):
    c = c_ref[...]
    s = c * jax.nn.sigmoid(c)
    o_ref[...] = _dot3(s, w_ref[...]) + b_ref[...]


def _modulation(cond8, w_mod, b_mod):
    n = w_mod.shape[1]
    tn = 1024
    return pl.pallas_call(
        _mod_kernel,
        grid=(n // tn,),
        in_specs=[pl.BlockSpec((8, D_MODEL), lambda j: (0, 0)),
                  pl.BlockSpec((D_MODEL, tn), lambda j: (0, j)),
                  pl.BlockSpec((1, tn), lambda j: (0, j))],
        out_specs=pl.BlockSpec((8, tn), lambda j: (0, j)),
        out_shape=jax.ShapeDtypeStruct((8, n), F32),
        compiler_params=_cparams(("arbitrary",)),
        name="modulation",
    )(cond8, w_mod, b_mod.reshape(1, n))


def _inproj_kernel(x_ref, mod_ref, w_ref, o_ref):
    m = mod_ref[0]
    u = _layer_norm(x_ref[...]) * (1.0 + m[1:2]) + m[0:1]
    o_ref[...] = _dot1(u, w_ref[...])


def _inproj(x, mod_rows, w_in_bf16, tok_per_row):
    ntok = x.shape[0]
    tm = 512
    per = tok_per_row // tm
    return pl.pallas_call(
        _inproj_kernel,
        grid=(ntok // tm,),
        in_specs=[pl.BlockSpec((tm, D_MODEL), lambda i: (i, 0)),
                  pl.BlockSpec((1, 6, D_MODEL), lambda i: (i // per, 0, 0)),
                  pl.BlockSpec((D_MODEL, D_IN), lambda i: (0, 0))],
        out_specs=pl.BlockSpec((tm, D_IN), lambda i: (i, 0)),
        out_shape=jax.ShapeDtypeStruct((ntok, D_IN), F32),
        compiler_params=_cparams(("arbitrary",)),
        name="inproj",
    )(x, mod_rows, w_in_bf16)


def _rwkv_kernel(rf_ref, kf_ref, vf_ref, lwf_ref, laf_ref, rb_ref, kb_ref, vb_ref, lwb_ref, lab_ref,
                 s0_ref, wdec_ref, bdec_ref, wic_ref, bic_ref, kkw_ref, ka_ref, rk_ref, hs_ref,
                 of_ref, ob_ref, bonf_ref, bonb_ref, sfin_ref,
                 s_scr, t_scr, w2_scr, mr_scr, ar_scr, bk_scr, gt_scr, *, G, ng):
    L = CHUNK
    g = pl.program_id(1)
    ins = ((rf_ref, kf_ref, vf_ref, lwf_ref, laf_ref), (rb_ref, kb_ref, vb_ref, lwb_ref, lab_ref))
    outs = ((of_ref, bonf_ref), (ob_ref, bonb_ref))

    @pl.when(g == 0)
    def _():
        s_scr[...] = s0_ref[0]

    hs = hs_ref[...]
    lane = lax.broadcasted_iota(jnp.int32, (L, 128), 1)
    rowp = lax.broadcasted_iota(jnp.int32, (L, 128), 0)
    colp = jnp.bitwise_and(lane, 63)
    h0 = lane < 64
    m0 = jnp.where(h0, 1.0, 0.0).astype(BF16)
    m1 = jnp.where(h0, 0.0, 1.0).astype(BF16)
    eye2 = jnp.where(rowp == colp, 1.0, 0.0)
    rowL = lax.broadcasted_iota(jnp.int32, (L, L), 0)
    colL = lax.broadcasted_iota(jnp.int32, (L, L), 1)

    def same(sh):
        return lax.shift_right_logical(rowp, sh) == lax.shift_right_logical(colp, sh)

    same8, same16, same32 = same(3), same(4), same(5)

    def bmask(c):
        return jnp.where(c, 1.0, 0.0).astype(BF16)

    mk8 = bmask(same8)
    merge_masks = (bmask(same16 & (~same8)), bmask(same32 & (~same16)), bmask(~same32))

    def smb(x):
        return jnp.concatenate([x * m0, x * m1], axis=0)

    def sms(s):
        return tuple(smb(x) for x in s)

    def pm(x_b, ys):
        return _mm(jnp.concatenate([x_b, x_b], axis=1), jnp.concatenate(sms(ys), axis=0))

    chains = [(d, p) for d in range(2) for p in range(N_PAIRS)]

    def each(fn, *lists):
        return [fn(*a) for a in zip(*lists)]

    def prep_stages(jb):
        pre, units = [], []
        for j, d in [(jb * PREP_CHUNKS + jo, d) for jo in range(PREP_CHUNKS) for d in range(2)]:
            r_ref, k_ref, v_ref, lw_ref, la_ref = ins[d]
            jn = j if d == 0 else G - 1 - j
            rows = pl.ds(pl.multiple_of(jn * L, L), L)
            u = d * G + j
            r = r_ref[rows, :]
            k = k_ref[rows, :]
            v = v_ref[rows, :]
            dsel = (lane >= 64) if d else h0
            tl = jnp.where(dsel, jnp.tanh(lw_ref[rows, :]), 0.0)
            dw = _dot3(tl, wdec_ref[...]) + bdec_ref[d:d + 1, :]
            logw = (-math.exp(-0.5)) * jax.nn.sigmoid(dw)
            la_m = jnp.where(dsel, la_ref[rows, :], 0.0)
            iclr = jax.nn.sigmoid(_dot3(la_m, wic_ref[...]) + bic_ref[d:d + 1, :])
            kkr = k * kkw_ref[...]
            nrm = jnp.sqrt(_dot_xb2(kkr * kkr, hs))
            kk = kkr / jnp.maximum(nrm, 1e-12)
            kdir = k * (1.0 + (iclr - 1.0) * ka_ref[...])
            bv = kk * iclr
            outs[d][1][rows, :] = _dot_xb2(r * kdir * rk_ref[...], hs) * v

            tri = bmask((rowL <= colL) if d else (rowL >= colL))
            cs = _dot_xa(tri, logw)
            ctot = jnp.sum(logw, axis=0, keepdims=True)
            g_inv = jnp.exp(-cs)
            g_rem = jnp.exp(ctot - cs)
            gt_scr[u] = jnp.exp(ctot)
            At = -kk * jnp.exp(cs - logw)
            Rt = r * jnp.exp(cs)
            Bt = bv * g_inv
            Kt = kdir * g_inv
            Bg = bv * g_rem
            Kg = kdir * g_rem
            for p in range(N_PAIRS):
                sl = slice(128 * p, 128 * (p + 1))
                ar = jnp.concatenate([At[:, sl], Rt[:, sl]], axis=0).astype(BF16)
                ar_scr[u, p] = ar
                bk_scr[u, p] = jnp.concatenate([Bg[:, sl], Kg[:, sl]], axis=0).astype(BF16)
                pre.append((ar, Bt[:, sl].astype(BF16), Kt[:, sl].astype(BF16), v[:, sl].astype(BF16)))
                units.append((d, p, u))

        def stage_m(c, pr):
            d = c[0]
            ar, bt, kt, _ = pr
            strict2 = (rowp < colp) if d else (rowp > colp)
            incl2 = (rowp <= colp) if d else (rowp >= colp)
            M = _mm(ar, jnp.concatenate([smb(bt), smb(kt)], axis=0), _NT)
            N = jnp.where(strict2, M[:L, :128], 0.0)
            Mak = jnp.where(strict2, M[:L, 128:], 0.0)
            mr_scr[c[2], c[1]] = jnp.concatenate([jnp.where(incl2, M[L:, :128], 0.0),
                                                       jnp.where(incl2, M[L:, 128:], 0.0)], axis=1).astype(BF16)
            return N, Mak

        yield
        NM = each(stage_m, units, pre)
        yield
        Ns = [_split2(nm[0]) for nm in NM]
        N0s = [(n[0] * mk8, n[1] * mk8) for n in Ns]
        T = [eye2 + jnp.where(same8, nm[0], 0.0) for nm in NM]
        P2s = [_split2(pm(n[0], n)) for n in N0s]
        W2 = [_mm(nm[1].astype(BF16), smb(pr[3])) for nm, pr in zip(NM, pre)]
        yield
        T = [t + pm(t.astype(BF16), p2) for t, p2 in zip(T, P2s)]
        yield
        P4s = [_split2(pm(p2[0], p2)) for p2 in P2s]
        yield
        T = [t + pm(t.astype(BF16), p4) for t, p4 in zip(T, P4s)]
        yield
        for mk in merge_masks:
            Ts = [_split2(t) for t in T]
            Y = [pm(ts[0], (n[0] * mk, n[1] * mk)) for ts, n in zip(Ts, Ns)]
            yield
            T = [t + pm(y.astype(BF16), ts) for t, y, ts in zip(T, Y, Ts)]
            yield
        for (d, p, u), t, w2 in zip(units, T, W2):
            t_scr[u, p] = t.astype(BF16)
            w2_scr[u, p] = w2

    def serial_stages(j):
        us = [d * G + j for d, _ in chains]
        rws = [pl.ds(pl.multiple_of((j if d == 0 else G - 1 - j) * L, L), L) for d, _ in chains]
        sls = [slice(128 * p, 128 * (p + 1)) for _, p in chains]
        S2 = [s_scr[d, p] for d, p in chains]
        XR = [_mm(ar_scr[u, p], smb(s2.astype(BF16)), _NT) for (d, p), u, s2 in zip(chains, us, S2)]
        yield
        U = [pm(t_scr[u, p], _split2(xr[:L] + w2_scr[u, p])) for (d, p), u, xr in zip(chains, us, XR)]
        V = [ins[d][2][rw, sl] for (d, p), rw, sl in zip(chains, rws, sls)]
        yield
        for (d, p), u, rw, sl, s2, xr, uu, vv in zip(chains, us, rws, sls, S2, XR, U, V):
            UVt = jnp.concatenate([uu, vv], axis=0).T
            Rm = _mm(UVt.astype(BF16), bk_scr[u, p])
            s_scr[d, p] = gt_scr[u][:, sl] * s2 + jnp.where(h0, Rm[:HEAD], Rm[HEAD:])
        for (d, p), u, rw, sl, xr, uu, vv in zip(chains, us, rws, sls, XR, U, V):
            rhs = jnp.concatenate([smb(uu.astype(BF16)), smb(vv.astype(BF16))], axis=0)
            outs[d][0][rw, sl] = xr[L:] + _mm(mr_scr[u, p], rhs)
        yield

    def serial_group(jb):
        for o in range(PREP_CHUNKS):
            yield from serial_stages(jb * PREP_CHUNKS + o)

    def run(lead, follow=None, pace=2):
        n = 0
        for _ in lead:
            n += 1
            if follow is not None and n % pace == 0:
                next(follow, None)
        if follow is not None:
            for _ in follow:
                pass

    ngrp = G // PREP_CHUNKS
    run(prep_stages(0))

    def fused(jb, carry):
        run(prep_stages(jb), serial_group(jb - 1))
        return carry

    lax.fori_loop(1, ngrp, fused, 0)
    run(serial_group(ngrp - 1))

    @pl.when(g == ng - 1)
    def _():
        for d in range(2):
            for p in range(N_PAIRS):
                sp = s_scr[d, p]
                sfin_ref[0, d, 2 * p] = sp[:, :HEAD]
                sfin_ref[0, d, 2 * p + 1] = sp[:, HEAD:]


def _rwkv(proj, s0_pairs, wdec, bdec, wic, bic, kkw, ka, rk, hs, nseq, T):
    ntok = proj.shape[0]
    nc = T // CHUNK
    G = min(nc, 8)
    ng = nc // G
    GL = G * CHUNK

    def fwd(b, g):
        return b * ng + g

    def bwd(b, g):
        return b * ng + ng - 1 - g

    def pspecs(rowblk):
        return [pl.BlockSpec((GL, D_A), lambda b, g: (rowblk(b, g), COL_R)),
                pl.BlockSpec((GL, D_A), lambda b, g: (rowblk(b, g), COL_K)),
                pl.BlockSpec((GL, D_A), lambda b, g: (rowblk(b, g), COL_V)),
                pl.BlockSpec((GL, 128), lambda b, g: (rowblk(b, g), COL_LW)),
                pl.BlockSpec((GL, 128), lambda b, g: (rowblk(b, g), COL_LA))]

    def wspec(shape):
        return pl.BlockSpec(shape, lambda b, g: (0,) * len(shape))

    ospec_f = pl.BlockSpec((GL, D_A), lambda b, g: (fwd(b, g), 0))
    ospec_b = pl.BlockSpec((GL, D_A), lambda b, g: (bwd(b, g), 0))
    tok = jax.ShapeDtypeStruct((ntok, D_A), F32)
    return pl.pallas_call(
        functools.partial(_rwkv_kernel, G=G, ng=ng),
        grid=(nseq, ng),
        in_specs=pspecs(fwd) + pspecs(bwd) + [
            pl.BlockSpec((1, 2, N_PAIRS, HEAD, 128), lambda b, g: (b, 0, 0, 0, 0)),
            wspec((128, D_A)), wspec((2, D_A)), wspec((128, D_A)), wspec((2, D_A)),
            wspec((1, D_A)), wspec((1, D_A)), wspec((1, D_A)), wspec((D_A, D_A))],
        out_specs=[ospec_f, ospec_b, ospec_f, ospec_b,
                   pl.BlockSpec((1, 2, N_HEADS, HEAD, HEAD), lambda b, g: (b, 0, 0, 0, 0))],
        out_shape=[tok, tok, tok, tok, jax.ShapeDtypeStruct((nseq, 2, N_HEADS, HEAD, HEAD), F32)],
        scratch_shapes=[pltpu.VMEM((2, N_PAIRS, HEAD, 128), F32),
                        pltpu.VMEM((2 * G, N_PAIRS, CHUNK, 128), BF16),
                        pltpu.VMEM((2 * G, N_PAIRS, CHUNK, 128), F32),
                        pltpu.VMEM((2 * G, N_PAIRS, CHUNK, 256), BF16),
                        pltpu.VMEM((2 * G, N_PAIRS, 2 * CHUNK, 128), BF16),
                        pltpu.VMEM((2 * G, N_PAIRS, 2 * CHUNK, 128), BF16),
                        pltpu.VMEM((2 * G, 1, D_A), F32)],
        compiler_params=_cparams(("arbitrary", "arbitrary")),
        name="rwkv_scan",
    )(*([proj] * 10), s0_pairs, wdec, bdec, wic, bic, kkw, ka, rk, hs)


def _gelu_tanh(x):
    return 0.5 * x * (1.0 + jnp.tanh(math.sqrt(2.0 / math.pi) * (x + 0.044715 * (x * x * x))))


def _rglru_kernel(xr_ref, xg_ref, h0_ref, cw_ref, cb_ref, wa_ref, ba_ref, wi_ref, bi_ref, lam_ref,
                  y_ref, hfin_ref, a_scr, u_scr, *, T, lrow):
    RB = min(T, 256)
    nblk = T // RB
    rowi = lax.broadcasted_iota(jnp.int32, (RB, D_B), 0)
    pos = jnp.bitwise_and(rowi, lrow - 1)
    cw = cw_ref[...]
    lam = lam_ref[...]
    sp = jnp.maximum(-lam, 0.0) + jnp.log1p(jnp.exp(-jnp.abs(lam)))

    def gates(blk, carry):
        r0 = pl.multiple_of(blk * RB, RB)
        x = xr_ref[pl.ds(r0, RB), :]
        xm1 = jnp.where(pos >= 1, pltpu.roll(x, 1, 0), 0.0)
        xm2 = jnp.where(pos >= 2, pltpu.roll(x, 2, 0), 0.0)
        xp1 = jnp.where(pos <= lrow - 2, pltpu.roll(x, RB - 1, 0), 0.0)
        xc = xm2 * cw[0:1] + xm1 * cw[1:2] + x * cw[2:3] + xp1 * cw[3:4] + cb_ref[...]
        for dd in range(2):
            gr = jax.nn.sigmoid(_dot1(xc, wa_ref[dd]) + ba_ref[dd:dd + 1, :])
            gi = jax.nn.sigmoid(_dot1(xc, wi_ref[dd]) + bi_ref[dd:dd + 1, :])
            log_a = (-RGLRU_C) * gr * sp[dd:dd + 1, :]
            a = jnp.exp(log_a)
            a_scr[dd, pl.ds(r0, RB), :] = a
            u_scr[dd, pl.ds(r0, RB), :] = jnp.sqrt((1.0 - a) * (1.0 + a)) * gi * xc
        return carry

    lax.fori_loop(0, nblk, gates, 0)

    SUB = 8
    rowt = lax.broadcasted_iota(jnp.int32, (SUB, D_B), 0)

    def tile_scan(a, u, h_in, rev):
        for s_ in (1, 2, 4):
            keep = (rowt < SUB - s_) if rev else (rowt >= s_)
            sh = SUB - s_ if rev else s_
            a_sh = jnp.where(keep, pltpu.roll(a, sh, 0), 1.0)
            u_sh = jnp.where(keep, pltpu.roll(u, sh, 0), 0.0)
            u = a * u_sh + u
            a = a * a_sh
        return a * h_in + u

    def step(i, carry):
        hf, hb = carry
        rf = pl.ds(pl.multiple_of(i * SUB, SUB), SUB)
        hf_tile = tile_scan(a_scr[0, rf, :], u_scr[0, rf, :], hf, False)
        u_scr[0, rf, :] = hf_tile
        rb = pl.ds(pl.multiple_of(T - SUB - i * SUB, SUB), SUB)
        hb_tile = tile_scan(a_scr[1, rb, :], u_scr[1, rb, :], hb, True)
        u_scr[1, rb, :] = hb_tile
        return hf_tile[SUB - 1:SUB], hb_tile[0:1]

    h0 = h0_ref[0]
    hf, hb = lax.fori_loop(0, T // SUB, step, (h0[0:1], h0[1:2]), unroll=2)
    hfin_ref[0] = jnp.concatenate([hf, hb], axis=0)

    def outp(blk, carry):
        r0 = pl.multiple_of(blk * RB, RB)
        h = u_scr[0, pl.ds(r0, RB), :] + u_scr[1, pl.ds(r0, RB), :]
        y_ref[pl.ds(r0, RB), :] = h * _gelu_tanh(xg_ref[pl.ds(r0, RB), :])
        return carry

    lax.fori_loop(0, nblk, outp, 0)


def _rglru(proj, h0, conv_w, conv_b, wa_bd, ba, wi_bd, bi, lam, nseq, T, lrow):
    ntok = proj.shape[0]

    def wspec(shape):
        return pl.BlockSpec(shape, lambda b: (0,) * len(shape))

    return pl.pallas_call(
        functools.partial(_rglru_kernel, T=T, lrow=lrow),
        grid=(nseq,),
        in_specs=[pl.BlockSpec((T, D_B), lambda b: (b, COL_XR)),
                  pl.BlockSpec((T, D_B), lambda b: (b, COL_XGB)),
                  pl.BlockSpec((1, 2, D_B), lambda b: (b, 0, 0)),
                  wspec((4, D_B)), wspec((1, D_B)),
                  wspec((2, D_B, D_B)), wspec((2, D_B)), wspec((2, D_B, D_B)), wspec((2, D_B)),
                  wspec((2, D_B))],
        out_specs=[pl.BlockSpec((T, D_B), lambda b: (b, 0)),
                   pl.BlockSpec((1, 2, D_B), lambda b: (b, 0, 0))],
        out_shape=[jax.ShapeDtypeStruct((ntok, D_B), F32),
                   jax.ShapeDtypeStruct((nseq, 2, D_B), F32)],
        scratch_shapes=[pltpu.VMEM((2, T, D_B), F32), pltpu.VMEM((2, T, D_B), F32)],
        compiler_params=_cparams(("arbitrary",)),
        name="rglru",
    )(proj, proj, h0, conv_w, conv_b, wa_bd, ba, wi_bd, bi, lam)


def _mixout_kernel(of_ref, ob_ref, bf_ref, bb_ref, yb_ref, lg_ref, x_ref, mod_ref, wgu_ref, avg_ref,
                   gnw_ref, gnb_ref, wout_ref, l1w_ref, l1b_ref, wr_ref, br_ref,
                   x1_ref, u2_ref, eid_ref, prob_ref):
    m = mod_ref[0]
    avg = avg_ref[...]
    wkv = of_ref[...] + ob_ref[...]
    mu = _dot_xb2(wkv, avg)
    dv = wkv - mu
    var = _dot_xb2(dv * dv, avg)
    gn = dv * lax.rsqrt(var + GN_EPS) * gnw_ref[...] + gnb_ref[...]
    g = _dot1(jax.nn.sigmoid(lg_ref[...]), wgu_ref[...])
    ya = (gn + (bf_ref[...] + bb_ref[...])) * g
    mix = _dot1(ya, wout_ref[0:D_A, :]) + _dot1(yb_ref[...], wout_ref[D_A:, :])
    x1 = _layer_norm(DEEPNORM_ALPHA * x_ref[...] + m[2:3] * mix) * l1w_ref[...] + l1b_ref[...]
    x1_ref[...] = x1
    u2 = _layer_norm(x1) * (1.0 + m[4:5]) + m[3:4]
    _store_tok_tiles(u2_ref, u2)
    logits = _dot3(u2, wr_ref[...]) + br_ref[...]
    tm = logits.shape[0]
    lane = lax.broadcasted_iota(jnp.int32, (tm, N_EXPERTS), 1)
    lane4 = lax.broadcasted_iota(jnp.int32, (tm, TOP_K), 1)
    work = logits
    tops, idxs = [], []
    for _ in range(TOP_K):
        mx = jnp.max(work, axis=1, keepdims=True)
        idx = jnp.min(jnp.where(work == mx, lane, N_EXPERTS), axis=1, keepdims=True)
        tops.append(mx)
        idxs.append(idx)
        work = jnp.where(lane == idx, -jnp.inf, work)
    es = [jnp.exp(t - tops[0]) for t in tops]
    den = es[0] + es[1] + es[2] + es[3]
    eid = jnp.zeros((tm, TOP_K), jnp.int32)
    prob = jnp.zeros((tm, TOP_K), F32)
    for j in range(TOP_K):
        eid = jnp.where(lane4 == j, idxs[j], eid)
        prob = jnp.where(lane4 == j, es[j] / den, prob)
    eid_ref[...] = eid
    prob_ref[...] = prob


def _mixout_alias_kernel(*refs):
    _mixout_kernel(*refs[:17], *refs[18:])


def _mixout_first_kernel(*refs, ntiles):
    i = pl.program_id(0)

    @pl.when(i < ntiles)
    def _():
        _mixout_kernel(*refs)

    @pl.when(i >= ntiles)
    def _():
        refs[18][...] = jnp.zeros_like(refs[18])


def _mixout(o_f, o_b, bon_f, bon_b, yb, proj, x, mod_rows, tok_per_row, wgu, avg, gnw, gnb, wout, l1w, l1b,
            wr, br, u2_all, tok_base, ntok_all):
    ntok = x.shape[0]
    tm = 256
    per = tok_per_row // tm
    base_tiles = tok_base // tm
    alias = u2_all is not None
    ntiles = ntok // tm

    def wspec(shape):
        return pl.BlockSpec(shape, lambda i: (0,) * len(shape))

    def tl(i):
        return jnp.minimum(i, ntiles - 1)

    return pl.pallas_call(
        _mixout_alias_kernel if alias else functools.partial(_mixout_first_kernel, ntiles=ntiles),
        grid=(ntiles if alias else ntok_all // tm,),
        input_output_aliases={17: 1} if alias else {},
        in_specs=[pl.BlockSpec((tm, D_A), lambda i: (tl(i), 0)),
                  pl.BlockSpec((tm, D_A), lambda i: (tl(i), 0)),
                  pl.BlockSpec((tm, D_A), lambda i: (tl(i), 0)),
                  pl.BlockSpec((tm, D_A), lambda i: (tl(i), 0)),
                  pl.BlockSpec((tm, D_B), lambda i: (tl(i), 0)),
                  pl.BlockSpec((tm, LORA_G), lambda i: (tl(i), COL_LG)),
                  pl.BlockSpec((tm, D_MODEL), lambda i: (tl(i), 0)),
                  pl.BlockSpec((1, 6, D_MODEL), lambda i: (tl(i) // per, 0, 0)),
                  wspec((LORA_G, D_A)), wspec((D_A, D_A)), wspec((1, D_A)), wspec((1, D_A)),
                  wspec((D_MODEL, D_MODEL)), wspec((1, D_MODEL)), wspec((1, D_MODEL)),
                  wspec((D_MODEL, N_EXPERTS)), wspec((1, N_EXPERTS))]
                 + ([pl.BlockSpec(memory_space=pl.ANY)] if alias else []),
        out_specs=[pl.BlockSpec((tm, D_MODEL), lambda i: (tl(i), 0)),
                   pl.BlockSpec((tm * ROW_TILE, 128), lambda i: (i + base_tiles, 0)),
                   pl.BlockSpec((tm, TOP_K), lambda i: (tl(i), 0)),
                   pl.BlockSpec((tm, TOP_K), lambda i: (tl(i), 0))],
        out_shape=[jax.ShapeDtypeStruct((ntok, D_MODEL), F32),
                   jax.ShapeDtypeStruct((ntok_all * ROW_TILE, 128), F32),
                   jax.ShapeDtypeStruct((ntok, TOP_K), jnp.int32),
                   jax.ShapeDtypeStruct((ntok, TOP_K), F32)],
        compiler_params=_cparams(("arbitrary",)),
        name="mixout",
    )(o_f, o_b, bon_f, bon_b, yb, proj, x, mod_rows, wgu, avg, gnw, gnb, wout, l1w, l1b, wr, br,
      *([u2_all] if alias else []))


MOE_TM = 256
ROUTE_BLK = 256
INVERT_CHUNK = 2048


def _route_kernel(eid_ref, slot_ref, te_ref, nv_ref, c_scr, *, ntok, nt_pad):
    nblk = ntok // ROUTE_BLK
    lane32 = lax.broadcasted_iota(jnp.int32, (ROUTE_BLK, N_EXPERTS), 1)
    lane4 = lax.broadcasted_iota(jnp.int32, (ROUTE_BLK, TOP_K), 1)
    rowb = lax.broadcasted_iota(jnp.int32, (ROUTE_BLK, ROUTE_BLK), 0)
    colb = lax.broadcasted_iota(jnp.int32, (ROUTE_BLK, ROUTE_BLK), 1)
    tri = jnp.where(rowb > colb, 1.0, 0.0).astype(BF16)

    def count(b, carry):
        rows = pl.ds(pl.multiple_of(b * ROUTE_BLK, ROUTE_BLK), ROUTE_BLK)
        e = eid_ref[rows, :]
        hot = jnp.zeros((ROUTE_BLK, N_EXPERTS), F32)
        for j in range(TOP_K):
            hot = hot + jnp.where(lane32 == e[:, j:j + 1], 1.0, 0.0)
        c_scr[rows, :] = _mm(tri, hot.astype(BF16)) + carry
        return carry + jnp.sum(hot, axis=0, keepdims=True)

    n = lax.fori_loop(0, nblk, count, jnp.zeros((1, N_EXPERTS), F32))
    padded = jnp.floor((n + (MOE_TM - 1)) * (1.0 / MOE_TM)) * MOE_TM
    r32 = lax.broadcasted_iota(jnp.int32, (N_EXPERTS, N_EXPERTS), 0)
    c32 = lax.broadcasted_iota(jnp.int32, (N_EXPERTS, N_EXPERTS), 1)
    upper = jnp.where(r32 < c32, 1.0, 0.0).astype(BF16)
    off = _dot_xb(jnp.broadcast_to(padded, (8, N_EXPERTS)), upper)[0:1]
    gend = off + padded

    def place(b, carry):
        rows = pl.ds(pl.multiple_of(b * ROUTE_BLK, ROUTE_BLK), ROUTE_BLK)
        e = eid_ref[rows, :]
        base = off + c_scr[rows, :]
        s = jnp.zeros((ROUTE_BLK, TOP_K), F32)
        for j in range(TOP_K):
            sj = jnp.sum(jnp.where(lane32 == e[:, j:j + 1], base, 0.0), axis=1, keepdims=True)
            s = jnp.where(lane4 == j, sj, s)
        slot_ref[rows, :] = s.astype(jnp.int32)
        return carry

    lax.fori_loop(0, nblk, place, 0)

    gcol = jnp.sum(jnp.where(r32 == c32, jnp.broadcast_to(gend, (N_EXPERTS, N_EXPERTS)), 0.0),
                   axis=1, keepdims=True)
    tstart = (lax.broadcasted_iota(jnp.int32, (N_EXPERTS, nt_pad), 1) * MOE_TM).astype(F32)
    te = jnp.sum(jnp.where(gcol <= tstart, 1.0, 0.0), axis=0, keepdims=True)
    te_ref[...] = jnp.minimum(te, N_EXPERTS - 1.0).astype(jnp.int32)
    total = jnp.sum(padded, axis=1, keepdims=True)
    nv_ref[...] = jnp.broadcast_to(total * (1.0 / MOE_TM), (1, 128)).astype(jnp.int32)


def _route(eid):
    ntok = eid.shape[0]
    nt_max = ntok * TOP_K // MOE_TM + N_EXPERTS
    nt_pad = -(-(nt_max + 1) // 128) * 128
    full = lambda shape: pl.BlockSpec(shape, lambda i: (0,) * len(shape))
    slot, te, nv = pl.pallas_call(
        functools.partial(_route_kernel, ntok=ntok, nt_pad=nt_pad),
        grid=(1,),
        in_specs=[full((ntok, TOP_K))],
        out_specs=[full((ntok, TOP_K)), full((1, nt_pad)), full((1, 128))],
        out_shape=[jax.ShapeDtypeStruct((ntok, TOP_K), jnp.int32),
                   jax.ShapeDtypeStruct((1, nt_pad), jnp.int32),
                   jax.ShapeDtypeStruct((1, 128), jnp.int32)],
        scratch_shapes=[pltpu.VMEM((ntok, N_EXPERTS), F32)],
        compiler_params=_cparams(("arbitrary",)),
        name="moe_route",
    )(eid)
    return slot.reshape(ntok * TOP_K), te.reshape(nt_pad), nv.reshape(128), nt_max


def _invert_kernel(slot_ref, inv_ref, fill_vmem, slot_smem, inv_smem, sem_in, sem_out, *, nassign, plane):
    slot_id = lax.broadcasted_iota(jnp.int32, fill_vmem.shape, 0)
    fill_vmem[...] = jnp.bitwise_and(slot_id, MOE_TM - 1) * ROW_TILE + TOP_K * plane
    fill = pltpu.make_async_copy(fill_vmem, inv_smem, sem_out)
    fill.start()
    fill.wait()

    def chunk(ci, c):
        a0 = pl.multiple_of(ci * INVERT_CHUNK, INVERT_CHUNK)
        cp = pltpu.make_async_copy(slot_ref.at[pl.ds(a0, INVERT_CHUNK)], slot_smem, sem_in)
        cp.start()
        cp.wait()

        def put(tt, c2):
            base = (lax.shift_right_logical(a0, 2) + tt) * ROW_TILE
            for k in range(TOP_K):
                inv_smem[slot_smem[tt * TOP_K + k]] = base + k * plane
            return c2

        lax.fori_loop(0, INVERT_CHUNK // TOP_K, put, 0, unroll=4)
        return c

    lax.fori_loop(0, nassign // INVERT_CHUNK, chunk, 0)
    out = pltpu.make_async_copy(inv_smem, inv_ref, sem_out)
    out.start()
    out.wait()


def _invert(slot, ntiles, ntok):
    nassign = slot.shape[0]
    plane = ntok * ROW_TILE
    assert plane & (plane - 1) == 0, "token count must be a power of two"
    anyspec = pl.BlockSpec(memory_space=pl.ANY)
    return pl.pallas_call(
        functools.partial(_invert_kernel, nassign=nassign, plane=plane),
        grid=(1,),
        in_specs=[anyspec], out_specs=anyspec,
        out_shape=jax.ShapeDtypeStruct((ntiles * MOE_TM,), jnp.int32),
        scratch_shapes=[pltpu.VMEM((ntiles * MOE_TM,), jnp.int32),
                        pltpu.SMEM((INVERT_CHUNK,), jnp.int32), pltpu.SMEM((ntiles * MOE_TM,), jnp.int32),
                        pltpu.SemaphoreType.DMA(()), pltpu.SemaphoreType.DMA(())],
        compiler_params=_cparams(("arbitrary",)),
        name="moe_invert",
    )(slot)


def _expert_kernel(te_ref, nv_ref, inv_ref, u2_ref, wg_ref, bg_ref, wu_ref, bu_ref, wd_ref, bd_ref, y4_ref,
                   xbuf, ybuf, wg_b, wu_b, wd_b, sem_g, sem_s, *, ntok):
    i = pl.program_id(0)
    nvalid = nv_ref[0]
    valid = i < nvalid
    trash = TOP_K * ntok
    plane = ntok * ROW_TILE

    def rows8(row):
        return pl.ds(pl.multiple_of(row, ROW_TILE), ROW_TILE)

    def gather_copy(tile, r, buf):
        src = jnp.bitwise_and(inv_ref[tile * MOE_TM + r], plane - 1)
        return pltpu.make_async_copy(u2_ref.at[rows8(src)], xbuf.at[buf, _tok_rows(r)], sem_g.at[buf])

    def scatter_copy(tile, r):
        return pltpu.make_async_copy(ybuf.at[_tok_rows(r)], y4_ref.at[rows8(inv_ref[tile * MOE_TM + r])], sem_s)

    def wait_gather(buf):
        pltpu.make_async_copy(xbuf.at[buf], xbuf.at[buf], sem_g.at[buf]).wait()

    def wait_scatter():
        pltpu.make_async_copy(ybuf, ybuf, sem_s).wait()

    @pl.when(i == 0)
    def _():
        def first(r2, c):
            gather_copy(0, 2 * r2, 0).start(priority=0)
            gather_copy(0, 2 * r2 + 1, 0).start(priority=1)
            return c

        lax.fori_loop(0, MOE_TM // 2, first, 0)
        ybuf[...] = jnp.zeros_like(ybuf)
        spare = pltpu.make_async_copy(ybuf, y4_ref.at[_tok_rows(trash, MOE_TM)], sem_s)
        spare.start()
        spare.wait()

    fresh = jnp.logical_or(i == 0, te_ref[i] != te_ref[jnp.maximum(i - 1, 0)])

    @pl.when(jnp.logical_and(valid, fresh))
    def _():
        wg_b[...] = wg_ref[0].astype(BF16)
        wu_b[...] = wu_ref[0].astype(BF16)
        wd_b[...] = wd_ref[0].astype(BF16)

    @pl.when(valid)
    def _():
        buf = jnp.bitwise_and(i, 1)
        nxt = jnp.minimum(i + 1, nvalid - 1)
        wait_gather(buf)
        x = _load_tok_tiles(xbuf, MOE_TM, (buf,)).astype(BF16)
        FC = 512
        nfc = D_FF // FC
        per = MOE_TM // nfc
        y = jnp.zeros((MOE_TM, D_MODEL), F32)
        for j in range(nfc):
            for r in range(j * per, (j + 1) * per):
                gather_copy(nxt, r, 1 - buf).start(priority=r % 2)
            cs = slice(j * FC, (j + 1) * FC)
            gate = jnp.minimum(_mm(x, wg_b[:, cs]) + bg_ref[0, :, cs], SWIGLU_LIMIT)
            up = jnp.clip(_mm(x, wu_b[:, cs]) + bu_ref[0, :, cs], -SWIGLU_LIMIT, SWIGLU_LIMIT)
            h = (up + 1.0) * gate * jax.nn.sigmoid(SWIGLU_ALPHA * gate)
            y = y + _mm(h.astype(BF16), wd_b[cs, :])

        @pl.when(i > 0)
        def _():
            wait_scatter()

        _store_tok_tiles(ybuf, y + bd_ref[0])
        for r in range(MOE_TM):
            scatter_copy(i, r).start(priority=r % 2)

    @pl.when(i == nvalid)
    def _():
        wait_gather(jnp.bitwise_and(nvalid, 1))
        wait_scatter()


def _experts(te, nv, inv, u2_all, nt_max, wg, bg, wu, bu, wd, bd):
    ntok = u2_all.shape[0] // ROW_TILE

    def tile(i, te, nv, inv):
        return te[jnp.minimum(i, nv[0] - 1)]

    wspec = pl.BlockSpec((1, D_MODEL, D_FF), lambda i, te, nv, inv: (tile(i, te, nv, inv), 0, 0))
    bspec = pl.BlockSpec((1, 1, D_FF), lambda i, te, nv, inv: (tile(i, te, nv, inv), 0, 0))
    anyspec = pl.BlockSpec(memory_space=pl.ANY)
    return pl.pallas_call(
        functools.partial(_expert_kernel, ntok=ntok),
        grid_spec=pltpu.PrefetchScalarGridSpec(
            num_scalar_prefetch=3, grid=(nt_max + 1,),
            in_specs=[anyspec, wspec, bspec, wspec, bspec, wspec, bspec], out_specs=anyspec,
            scratch_shapes=[pltpu.VMEM((2, MOE_TM * ROW_TILE, 128), F32),
                            pltpu.VMEM((MOE_TM * ROW_TILE, 128), F32),
                            pltpu.VMEM((D_MODEL, D_FF), BF16), pltpu.VMEM((D_MODEL, D_FF), BF16),
                            pltpu.VMEM((D_MODEL, D_FF), BF16),
                            pltpu.SemaphoreType.DMA((2,)), pltpu.SemaphoreType.DMA(())]),
        out_shape=jax.ShapeDtypeStruct(((TOP_K * ntok + MOE_TM) * ROW_TILE, 128), F32),
        compiler_params=_cparams(("arbitrary",)),
        name="moe_experts",
    )(te, nv, inv, u2_all, wg, bg.reshape(N_EXPERTS, 1, D_FF), wu, bu.reshape(N_EXPERTS, 1, D_FF),
      wd, bd.reshape(N_EXPERTS, 1, D_MODEL))


def _combine_kernel(y0_ref, y1_ref, y2_ref, y3_ref, prob_ref, x1_ref, mod_ref, w_ref, b_ref, o_ref):
    p = prob_ref[...]
    tm = p.shape[0]
    moe = p[:, 0:1] * _load_tok_tiles(y0_ref, tm)
    for j, y_ref in enumerate((y1_ref, y2_ref, y3_ref), start=1):
        moe = moe + p[:, j:j + 1] * _load_tok_tiles(y_ref, tm)
    m = mod_ref[0]
    o_ref[...] = _layer_norm(DEEPNORM_ALPHA * x1_ref[...] + m[5:6] * moe) * w_ref[...] + b_ref[...]


def _combine(y4, prob, x1, tok_base, ntok_all, mod_rows, tok_per_row, w, b):
    ntok = x1.shape[0]
    tm = 256
    per = tok_per_row // tm

    def yspec(k):
        return pl.BlockSpec((tm * ROW_TILE, 128), lambda i: ((k * ntok_all + tok_base) // tm + i, 0))

    return pl.pallas_call(
        _combine_kernel,
        grid=(ntok // tm,),
        in_specs=[yspec(0), yspec(1), yspec(2), yspec(3),
                  pl.BlockSpec((tm, TOP_K), lambda i: (i, 0)),
                  pl.BlockSpec((tm, D_MODEL), lambda i: (i, 0)),
                  pl.BlockSpec((1, 6, D_MODEL), lambda i: (i // per, 0, 0)),
                  pl.BlockSpec((1, D_MODEL), lambda i: (0, 0)),
                  pl.BlockSpec((1, D_MODEL), lambda i: (0, 0))],
        out_specs=pl.BlockSpec((tm, D_MODEL), lambda i: (i, 0)),
        out_shape=jax.ShapeDtypeStruct((ntok, D_MODEL), F32),
        compiler_params=_cparams(("arbitrary",)),
        name="moe_combine",
    )(y4, y4, y4, y4, prob, x1, mod_rows, w, b)


def _block_diag(w):
    nb, bb, _ = w.shape
    eye = jnp.eye(nb, dtype=w.dtype)
    return jnp.einsum('nij,nm->nimj', w, eye).reshape(nb * bb, nb * bb)


def kernel(x_prompt, x_sample, state_rwkv, state_rglru, c, c_ctx, w_mod, b_mod, w_in, w_decay_up, b_decay, w_iclr_up, b_iclr, w_gate_up, k_k, k_a, r_k, gn_w, gn_b, conv_w, conv_b, w_rg_a, b_rg_a, w_rg_i, b_rg_i, lam, w_out, ln1_w, ln1_b, w_router, b_router, w_e_gate, b_e_gate, w_e_up, b_e_up, w_e_down, b_e_down, ln2_w, ln2_b):
    Bp, Tp, D = x_prompt.shape
    Bs, Ts, _ = x_sample.shape
    l = 0
    row = lambda a: a[l].reshape(1, -1)

    cond8 = jnp.concatenate([c_ctx[None, :], c, jnp.zeros((8 - 1 - Bs, D), F32)], axis=0)
    mod = _modulation(cond8, w_mod[l], b_mod[l]).reshape(8, 6, D)
    mod_p, mod_s = mod[0:1], mod[1:1 + Bs]

    w_in_b = w_in[l].astype(BF16)
    w_out_b = w_out[l].astype(BF16)
    head_id = jnp.arange(D_A) // HEAD
    hs = (head_id[:, None] == head_id[None, :]).astype(BF16)
    avg = (hs.astype(F32) / HEAD).astype(BF16)
    wdec = w_decay_up[l].reshape(2 * LORA, D_A)
    wic = w_iclr_up[l].reshape(2 * LORA, D_A)
    wa_bd = jnp.stack([_block_diag(w_rg_a[l, 0]), _block_diag(w_rg_a[l, 1])])
    wi_bd = jnp.stack([_block_diag(w_rg_i[l, 0]), _block_diag(w_rg_i[l, 1])])

    xp = x_prompt.reshape(Bp * Tp, D)
    xs = x_sample.reshape(Bs * Ts, D)

    s0_s = state_rwkv[:, l].reshape(Bs, 2, N_PAIRS, 2, HEAD, HEAD).transpose(0, 1, 2, 4, 3, 5)
    s0_s = s0_s.reshape(Bs, 2, N_PAIRS, HEAD, 2 * HEAD)
    s0_p = jnp.zeros((Bp, 2, N_PAIRS, HEAD, 2 * HEAD), F32)
    h0_p = jnp.zeros((Bp, 2, D_B), F32)
    h0_s = state_rglru[:, l]

    ntok_all = Bp * Tp + Bs * Ts
    outs = []
    u2_all = None
    for x, mod_rows, tok_per_row, nseq, T, s0, h0, lrow, tok_base in (
            (xp, mod_p, Bp * Tp, Bp, Tp, s0_p, h0_p, Tp, 0),
            (xs, mod_s, Ts, Bs, Ts, s0_s, h0_s, GRID_W, Bp * Tp)):
        proj = _inproj(x, mod_rows, w_in_b, tok_per_row)
        o_f, o_b, bon_f, bon_b, s_fin = _rwkv(proj, s0, wdec, b_decay[l], wic, b_iclr[l], row(k_k), row(k_a),
                                              r_k[l].reshape(1, D_A), hs, nseq, T)
        yb, h_fin = _rglru(proj, h0, conv_w[l], row(conv_b), wa_bd, b_rg_a[l], wi_bd, b_rg_i[l], lam[l],
                           nseq, T, lrow)
        x1, u2_all, eid, prob = _mixout(o_f, o_b, bon_f, bon_b, yb, proj, x, mod_rows, tok_per_row,
                                        w_gate_up[l], avg, row(gn_w), row(gn_b), w_out_b, row(ln1_w),
                                        row(ln1_b), w_router[l], row(b_router), u2_all, tok_base, ntok_all)
        outs.append((x1, eid, prob, s_fin, h_fin))

    (x1_p, eid_p, prob_p, sfin_p, hfin_p), (x1_s, eid_s, prob_s, _, _) = outs
    slot, te, nv, nt_max = _route(jnp.concatenate([eid_p, eid_s], axis=0))
    inv = _invert(slot, nt_max, ntok_all)
    y4 = _experts(te, nv, inv, u2_all, nt_max, w_e_gate[l], b_e_gate[l], w_e_up[l], b_e_up[l], w_e_down[l],
                  b_e_down[l])
    y_p = _combine(y4, prob_p, x1_p, 0, ntok_all, mod_p, Bp * Tp, row(ln2_w), row(ln2_b))
    y_s = _combine(y4, prob_s, x1_s, Bp * Tp, ntok_all, mod_s, Ts, row(ln2_w), row(ln2_b))
    return (y_p.reshape(Bp, Tp, D), y_s.reshape(Bs, Ts, D),
            sfin_p[:, None], hfin_p[:, None])
```

```python
import functools
import math

import jax
import jax.numpy as jnp
from jax import lax
from jax.experimental import pallas as pl
from jax.experimental.pallas import tpu as pltpu

F32 = jnp.float32
BF16 = jnp.bfloat16

D_MODEL = 1024
D_A = 512
D_B = 512
HEAD = 64
N_HEADS = 8
N_PAIRS = N_HEADS // 2
GRID_W = 64
LORA = 64
LORA_G = 128
RGLRU_C = 8.0
N_EXPERTS = 32
TOP_K = 4
D_FF = 1024
SWIGLU_LIMIT = 7.0
SWIGLU_ALPHA = 1.702
LN_EPS = 1e-5
GN_EPS = 1e-5 * HEAD
D_IN = 3 * D_A + 2 * D_B + 2 * LORA + 2 * LORA + LORA_G
DEPTH = 1
DEEPNORM_ALPHA = (2 * DEPTH) ** 0.25

COL_R, COL_K, COL_V, COL_XR, COL_XGB = 0, 1, 2, 3, 4
COL_LW, COL_LA, COL_LG = 20, 21, 22

CHUNK = 64
PREP_CHUNKS = 2
VMEM_LIMIT = 56 * 1024 * 1024

_NN = (((1,), (0,)), ((), ()))
_NT = (((1,), (1,)), ((), ()))


def _mm(a, b, dims=_NN):
    return lax.dot_general(a, b, dims, preferred_element_type=F32)


def _split2(x):
    hi = x.astype(BF16)
    lo = (x - hi.astype(F32)).astype(BF16)
    return hi, lo


def _split3(x):
    hi = x.astype(BF16)
    r1 = x - hi.astype(F32)
    mid = r1.astype(BF16)
    lo = (r1 - mid.astype(F32)).astype(BF16)
    return hi, mid, lo


def _dot1(a, b, dims=_NN):
    return _mm(a.astype(BF16), b.astype(BF16), dims)


def _dot3(a, b):
    ah, al = _split2(a)
    bh, bl = _split2(b)
    return _mm(jnp.concatenate([ah, ah, al], axis=1), jnp.concatenate([bh, bl, bh], axis=0))


def _dot_xb(a, b_bf16):
    return _mm(jnp.concatenate(_split3(a), axis=1), jnp.concatenate([b_bf16] * 3, axis=0))


def _dot_xb2(a, b_bf16):
    return _mm(jnp.concatenate(_split2(a), axis=1), jnp.concatenate([b_bf16] * 2, axis=0))


def _dot_xa(a_bf16, b):
    return _mm(jnp.concatenate([a_bf16] * 3, axis=1), jnp.concatenate(_split3(b), axis=0))


ROW_TILE = 8


def _tok_rows(t, n=1):
    return pl.ds(pl.multiple_of(t * ROW_TILE, ROW_TILE), n * ROW_TILE)


def _load_tok_tiles(ref, ntok, lead=()):
    return jnp.concatenate([ref[lead + (pl.ds(s_, ntok, stride=ROW_TILE), slice(None))]
                            for s_ in range(ROW_TILE)], axis=1)


def _store_tok_tiles(ref, x):
    ntok = x.shape[0]
    for s_ in range(ROW_TILE):
        ref[pl.ds(s_, ntok, stride=ROW_TILE), :] = x[:, 128 * s_:128 * (s_ + 1)]


def _layer_norm(x):
    mu = jnp.mean(x, axis=-1, keepdims=True)
    xc = x - mu
    var = jnp.mean(xc * xc, axis=-1, keepdims=True)
    return xc * lax.rsqrt(var + LN_EPS)


def _cparams(sem):
    return pltpu.CompilerParams(dimension_semantics=sem, vmem_limit_bytes=VMEM_LIMIT)


def _mod_kernel(c_ref, w_ref, b_ref, o_ref):
    c = c_ref[...]
    s = c * jax.nn.sigmoid(c)
    o_ref[...] = _dot3(s, w_ref[...]) + b_ref[...]


def _modulation(cond8, w_mod, b_mod):
    n = w_mod.shape[1]
    tn = 1024
    return pl.pallas_call(
        _mod_kernel,
        grid=(n // tn,),
        in_specs=[pl.BlockSpec((8, D_MODEL), lambda j: (0, 0)),
                  pl.BlockSpec((D_MODEL, tn), lambda j: (0, j)),
                  pl.BlockSpec((1, tn), lambda j: (0, j))],
        out_specs=pl.BlockSpec((8, tn), lambda j: (0, j)),
        out_shape=jax.ShapeDtypeStruct((8, n), F32),
        compiler_params=_cparams(("arbitrary",)),
        name="modulation",
    )(cond8, w_mod, b_mod.reshape(1, n))


def _inproj_kernel(x_ref, mod_ref, w_ref, o_ref):
    m = mod_ref[0]
    u = _layer_norm(x_ref[...]) * (1.0 + m[1:2]) + m[0:1]
    o_ref[...] = _dot1(u, w_ref[...])


def _inproj(x, mod_rows, w_in_bf16, tok_per_row):
    ntok = x.shape[0]
    tm = 512
    per = tok_per_row // tm
    return pl.pallas_call(
        _inproj_kernel,
        grid=(ntok // tm,),
        in_specs=[pl.BlockSpec((tm, D_MODEL), lambda i: (i, 0)),
                  pl.BlockSpec((1, 6, D_MODEL), lambda i: (i // per, 0, 0)),
                  pl.BlockSpec((D_MODEL, D_IN), lambda i: (0, 0))],
        out_specs=pl.BlockSpec((tm, D_IN), lambda i: (i, 0)),
        out_shape=jax.ShapeDtypeStruct((ntok, D_IN), F32),
        compiler_params=_cparams(("arbitrary",)),
        name="inproj",
    )(x, mod_rows, w_in_bf16)


def _rwkv_kernel(rf_ref, kf_ref, vf_ref, lwf_ref, laf_ref, rb_ref, kb_ref, vb_ref, lwb_ref, lab_ref,
                 s0_ref, wdec_ref, bdec_ref, wic_ref, bic_ref, kkw_ref, ka_ref, rk_ref, hs_ref,
                 of_ref, ob_ref, bonf_ref, bonb_ref, sfin_ref,
                 s_scr, t_scr, w2_scr, mr_scr, ar_scr, bk_scr, gt_scr, *, G, ng):
    L = CHUNK
    g = pl.program_id(1)
    ins = ((rf_ref, kf_ref, vf_ref, lwf_ref, laf_ref), (rb_ref, kb_ref, vb_ref, lwb_ref, lab_ref))
    outs = ((of_ref, bonf_ref), (ob_ref, bonb_ref))

    @pl.when(g == 0)
    def _():
        s_scr[...] = s0_ref[0]

    hs = hs_ref[...]
    lane = lax.broadcasted_iota(jnp.int32, (L, 128), 1)
    rowp = lax.broadcasted_iota(jnp.int32, (L, 128), 0)
    colp = jnp.bitwise_and(lane, 63)
    h0 = lane < 64
    m0 = jnp.where(h0, 1.0, 0.0).astype(BF16)
    m1 = jnp.where(h0, 0.0, 1.0).astype(BF16)
    eye2 = jnp.where(rowp == colp, 1.0, 0.0)
    rowL = lax.broadcasted_iota(jnp.int32, (L, L), 0)
    colL = lax.broadcasted_iota(jnp.int32, (L, L), 1)

    def same(sh):
        return lax.shift_right_logical(rowp, sh) == lax.shift_right_logical(colp, sh)

    same8, same16, same32 = same(3), same(4), same(5)

    def bmask(c):
        return jnp.where(c, 1.0, 0.0).astype(BF16)

    mk8 = bmask(same8)
    merge_masks = (bmask(same16 & (~same8)), bmask(same32 & (~same16)), bmask(~same32))

    def smb(x):
        return jnp.concatenate([x * m0, x * m1], axis=0)

    def sms(s):
        return tuple(smb(x) for x in s)

    def pm(x_b, ys):
        return _mm(jnp.concatenate([x_b, x_b], axis=1), jnp.concatenate(sms(ys), axis=0))

    chains = [(d, p) for d in range(2) for p in range(N_PAIRS)]

    def each(fn, *lists):
        return [fn(*a) for a in zip(*lists)]

    def prep_stages(jb):
        pre, units = [], []
        for j, d in [(jb * PREP_CHUNKS + jo, d) for jo in range(PREP_CHUNKS) for d in range(2)]:
            r_ref, k_ref, v_ref, lw_ref, la_ref = ins[d]
            jn = j if d == 0 else G - 1 - j
            rows = pl.ds(pl.multiple_of(jn * L, L), L)
            u = d * G + j
            r = r_ref[rows, :]
            k = k_ref[rows, :]
            v = v_ref[rows, :]
            dsel = (lane >= 64) if d else h0
            tl = jnp.where(dsel, jnp.tanh(lw_ref[rows, :]), 0.0)
            dw = _dot3(tl, wdec_ref[...]) + bdec_ref[d:d + 1, :]
            logw = (-math.exp(-0.5)) * jax.nn.sigmoid(dw)
            la_m = jnp.where(dsel, la_ref[rows, :], 0.0)
            iclr = jax.nn.sigmoid(_dot3(la_m, wic_ref[...]) + bic_ref[d:d + 1, :])
            kkr = k * kkw_ref[...]
            nrm = jnp.sqrt(_dot_xb2(kkr * kkr, hs))
            kk = kkr / jnp.maximum(nrm, 1e-12)
            kdir = k * (1.0 + (iclr - 1.0) * ka_ref[...])
            bv = kk * iclr
            outs[d][1][rows, :] = _dot_xb2(r * kdir * rk_ref[...], hs) * v

            tri = bmask((rowL <= colL) if d else (rowL >= colL))
            cs = _dot_xa(tri, logw)
            ctot = jnp.sum(logw, axis=0, keepdims=True)
            g_inv = jnp.exp(-cs)
            g_rem = jnp.exp(ctot - cs)
            gt_scr[u] = jnp.exp(ctot)
            At = -kk * jnp.exp(cs - logw)
            Rt = r * jnp.exp(cs)
            Bt = bv * g_inv
            Kt = kdir * g_inv
            Bg = bv * g_rem
            Kg = kdir * g_rem
            for p in range(N_PAIRS):
                sl = slice(128 * p, 128 * (p + 1))
                ar = jnp.concatenate([At[:, sl], Rt[:, sl]], axis=0).astype(BF16)
                ar_scr[u, p] = ar
                bk_scr[u, p] = jnp.concatenate([Bg[:, sl], Kg[:, sl]], axis=0).astype(BF16)
                pre.append((ar, Bt[:, sl].astype(BF16), Kt[:, sl].astype(BF16), v[:, sl].astype(BF16)))
                units.append((d, p, u))

        def stage_m(c, pr):
            d = c[0]
            ar, bt, kt, _ = pr
            strict2 = (rowp < colp) if d else (rowp > colp)
            incl2 = (rowp <= colp) if d else (rowp >= colp)
            M = _mm(ar, jnp.concatenate([smb(bt), smb(kt)], axis=0), _NT)
            N = jnp.where(strict2, M[:L, :128], 0.0)
            Mak = jnp.where(strict2, M[:L, 128:], 0.0)
            mr_scr[c[2], c[1]] = jnp.concatenate([jnp.where(incl2, M[L:, :128], 0.0),
                                                       jnp.where(incl2, M[L:, 128:], 0.0)], axis=1).astype(BF16)
            return N, Mak

        yield
        NM = each(stage_m, units, pre)
        yield
        Ns = [_split2(nm[0]) for nm in NM]
        N0s = [(n[0] * mk8, n[1] * mk8) for n in Ns]
        T = [eye2 + jnp.where(same8, nm[0], 0.0) for nm in NM]
        P2s = [_split2(pm(n[0], n)) for n in N0s]
        W2 = [_mm(nm[1].astype(BF16), smb(pr[3])) for nm, pr in zip(NM, pre)]
        yield
        T = [t + pm(t.astype(BF16), p2) for t, p2 in zip(T, P2s)]
        yield
        P4s = [_split2(pm(p2[0], p2)) for p2 in P2s]
        yield
        T = [t + pm(t.astype(BF16), p4) for t, p4 in zip(T, P4s)]
        yield
        for mk in merge_masks:
            Ts = [_split2(t) for t in T]
            Y = [pm(ts[0], (n[0] * mk, n[1] * mk)) for ts, n in zip(Ts, Ns)]
            yield
            T = [t + pm(y.astype(BF16), ts) for t, y, ts in zip(T, Y, Ts)]
            yield
        for (d, p, u), t, w2 in zip(units, T, W2):
            t_scr[u, p] = t.astype(BF16)
            w2_scr[u, p] = w2

    def serial_stages(j):
        us = [d * G + j for d, _ in chains]
        rws = [pl.ds(pl.multiple_of((j if d == 0 else G - 1 - j) * L, L), L) for d, _ in chains]
        sls = [slice(128 * p, 128 * (p + 1)) for _, p in chains]
        S2 = [s_scr[d, p] for d, p in chains]
        XR = [_mm(ar_scr[u, p], smb(s2.astype(BF16)), _NT) for (d, p), u, s2 in zip(chains, us, S2)]
        yield
        U = [pm(t_scr[u, p], _split2(xr[:L] + w2_scr[u, p])) for (d, p), u, xr in zip(chains, us, XR)]
        V = [ins[d][2][rw, sl] for (d, p), rw, sl in zip(chains, rws, sls)]
        yield
        for (d, p), u, rw, sl, s2, xr, uu, vv in zip(chains, us, rws, sls, S2, XR, U, V):
            UVt = jnp.concatenate([uu, vv], axis=0).T
            Rm = _mm(UVt.astype(BF16), bk_scr[u, p])
            s_scr[d, p] = gt_scr[u][:, sl] * s2 + jnp.where(h0, Rm[:HEAD], Rm[HEAD:])
        for (d, p), u, rw, sl, xr, uu, vv in zip(chains, us, rws, sls, XR, U, V):
            rhs = jnp.concatenate([smb(uu.astype(BF16)), smb(vv.astype(BF16))], axis=0)
            outs[d][0][rw, sl] = xr[L:] + _mm(mr_scr[u, p], rhs)
        yield

    def serial_group(jb):
        for o in range(PREP_CHUNKS):
            yield from serial_stages(jb * PREP_CHUNKS + o)

    def run(lead, follow=None, pace=2):
        n = 0
        for _ in lead:
            n += 1
            if follow is not None and n % pace == 0:
                next(follow, None)
        if follow is not None:
            for _ in follow:
                pass

    ngrp = G // PREP_CHUNKS
    run(prep_stages(0))

    def fused(jb, carry):
        run(prep_stages(jb), serial_group(jb - 1))
        return carry

    lax.fori_loop(1, ngrp, fused, 0)
    run(serial_group(ngrp - 1))

    @pl.when(g == ng - 1)
    def _():
        for d in range(2):
            for p in range(N_PAIRS):
                sp = s_scr[d, p]
                sfin_ref[0, d, 2 * p] = sp[:, :HEAD]
                sfin_ref[0, d, 2 * p + 1] = sp[:, HEAD:]


def _rwkv(proj, s0_pairs, wdec, bdec, wic, bic, kkw, ka, rk, hs, nseq, T):
    ntok = proj.shape[0]
    nc = T // CHUNK
    G = min(nc, 8)
    ng = nc // G
    GL = G * CHUNK

    def fwd(b, g):
        return b * ng + g

    def bwd(b, g):
        return b * ng + ng - 1 - g

    def pspecs(rowblk):
        return [pl.BlockSpec((GL, D_A), lambda b, g: (rowblk(b, g), COL_R)),
                pl.BlockSpec((GL, D_A), lambda b, g: (rowblk(b, g), COL_K)),
                pl.BlockSpec((GL, D_A), lambda b, g: (rowblk(b, g), COL_V)),
                pl.BlockSpec((GL, 128), lambda b, g: (rowblk(b, g), COL_LW)),
                pl.BlockSpec((GL, 128), lambda b, g: (rowblk(b, g), COL_LA))]

    def wspec(shape):
        return pl.BlockSpec(shape, lambda b, g: (0,) * len(shape))

    ospec_f = pl.BlockSpec((GL, D_A), lambda b, g: (fwd(b, g), 0))
    ospec_b = pl.BlockSpec((GL, D_A), lambda b, g: (bwd(b, g), 0))
    tok = jax.ShapeDtypeStruct((ntok, D_A), F32)
    return pl.pallas_call(
        functools.partial(_rwkv_kernel, G=G, ng=ng),
        grid=(nseq, ng),
        in_specs=pspecs(fwd) + pspecs(bwd) + [
            pl.BlockSpec((1, 2, N_PAIRS, HEAD, 128), lambda b, g: (b, 0, 0, 0, 0)),
            wspec((128, D_A)), wspec((2, D_A)), wspec((128, D_A)), wspec((2, D_A)),
            wspec((1, D_A)), wspec((1, D_A)), wspec((1, D_A)), wspec((D_A, D_A))],
        out_specs=[ospec_f, ospec_b, ospec_f, ospec_b,
                   pl.BlockSpec((1, 2, N_HEADS, HEAD, HEAD), lambda b, g: (b, 0, 0, 0, 0))],
        out_shape=[tok, tok, tok, tok, jax.ShapeDtypeStruct((nseq, 2, N_HEADS, HEAD, HEAD), F32)],
        scratch_shapes=[pltpu.VMEM((2, N_PAIRS, HEAD, 128), F32),
                        pltpu.VMEM((2 * G, N_PAIRS, CHUNK, 128), BF16),
                        pltpu.VMEM((2 * G, N_PAIRS, CHUNK, 128), F32),
                        pltpu.VMEM((2 * G, N_PAIRS, CHUNK, 256), BF16),
                        pltpu.VMEM((2 * G, N_PAIRS, 2 * CHUNK, 128), BF16),
                        pltpu.VMEM((2 * G, N_PAIRS, 2 * CHUNK, 128), BF16),
                        pltpu.VMEM((2 * G, 1, D_A), F32)],
        compiler_params=_cparams(("arbitrary", "arbitrary")),
        name="rwkv_scan",
    )(*([proj] * 10), s0_pairs, wdec, bdec, wic, bic, kkw, ka, rk, hs)


def _gelu_tanh(x):
    return 0.5 * x * (1.0 + jnp.tanh(math.sqrt(2.0 / math.pi) * (x + 0.044715 * (x * x * x))))


def _rglru_kernel(xr_ref, xg_ref, h0_ref, cw_ref, cb_ref, wa_ref, ba_ref, wi_ref, bi_ref, lam_ref,
                  y_ref, hfin_ref, a_scr, u_scr, *, T, lrow):
    RB = min(T, 256)
    nblk = T // RB
    rowi = lax.broadcasted_iota(jnp.int32, (RB, D_B), 0)
    pos = jnp.bitwise_and(rowi, lrow - 1)
    cw = cw_ref[...]
    lam = lam_ref[...]
    sp = jnp.maximum(-lam, 0.0) + jnp.log1p(jnp.exp(-jnp.abs(lam)))

    def gates(blk, carry):
        r0 = pl.multiple_of(blk * RB, RB)
        x = xr_ref[pl.ds(r0, RB), :]
        xm1 = jnp.where(pos >= 1, pltpu.roll(x, 1, 0), 0.0)
        xm2 = jnp.where(pos >= 2, pltpu.roll(x, 2, 0), 0.0)
        xp1 = jnp.where(pos <= lrow - 2, pltpu.roll(x, RB - 1, 0), 0.0)
        xc = xm2 * cw[0:1] + xm1 * cw[1:2] + x * cw[2:3] + xp1 * cw[3:4] + cb_ref[...]
        for dd in range(2):
            gr = jax.nn.sigmoid(_dot1(xc, wa_ref[dd]) + ba_ref[dd:dd + 1, :])
            gi = jax.nn.sigmoid(_dot1(xc, wi_ref[dd]) + bi_ref[dd:dd + 1, :])
            log_a = (-RGLRU_C) * gr * sp[dd:dd + 1, :]
            a = jnp.exp(log_a)
            a_scr[dd, pl.ds(r0, RB), :] = a
            u_scr[dd, pl.ds(r0, RB), :] = jnp.sqrt((1.0 - a) * (1.0 + a)) * gi * xc
        return carry

    lax.fori_loop(0, nblk, gates, 0)

    SUB = 8
    rowt = lax.broadcasted_iota(jnp.int32, (SUB, D_B), 0)

    def tile_scan(a, u, h_in, rev):
        for s_ in (1, 2, 4):
            keep = (rowt < SUB - s_) if rev else (rowt >= s_)
            sh = SUB - s_ if rev else s_
            a_sh = jnp.where(keep, pltpu.roll(a, sh, 0), 1.0)
            u_sh = jnp.where(keep, pltpu.roll(u, sh, 0), 0.0)
            u = a * u_sh + u
            a = a * a_sh
        return a * h_in + u

    def step(i, carry):
        hf, hb = carry
        rf = pl.ds(pl.multiple_of(i * SUB, SUB), SUB)
        hf_tile = tile_scan(a_scr[0, rf, :], u_scr[0, rf, :], hf, False)
        u_scr[0, rf, :] = hf_tile
        rb = pl.ds(pl.multiple_of(T - SUB - i * SUB, SUB), SUB)
        hb_tile = tile_scan(a_scr[1, rb, :], u_scr[1, rb, :], hb, True)
        u_scr[1, rb, :] = hb_tile
        return hf_tile[SUB - 1:SUB], hb_tile[0:1]

    h0 = h0_ref[0]
    hf, hb = lax.fori_loop(0, T // SUB, step, (h0[0:1], h0[1:2]), unroll=2)
    hfin_ref[0] = jnp.concatenate([hf, hb], axis=0)

    def outp(blk, carry):
        r0 = pl.multiple_of(blk * RB, RB)
        h = u_scr[0, pl.ds(r0, RB), :] + u_scr[1, pl.ds(r0, RB), :]
        y_ref[pl.ds(r0, RB), :] = h * _gelu_tanh(xg_ref[pl.ds(r0, RB), :])
        return carry

    lax.fori_loop(0, nblk, outp, 0)


def _rglru(proj, h0, conv_w, conv_b, wa_bd, ba, wi_bd, bi, lam, nseq, T, lrow):
    ntok = proj.shape[0]

    def wspec(shape):
        return pl.BlockSpec(shape, lambda b: (0,) * len(shape))

    return pl.pallas_call(
        functools.partial(_rglru_kernel, T=T, lrow=lrow),
        grid=(nseq,),
        in_specs=[pl.BlockSpec((T, D_B), lambda b: (b, COL_XR)),
                  pl.BlockSpec((T, D_B), lambda b: (b, COL_XGB)),
                  pl.BlockSpec((1, 2, D_B), lambda b: (b, 0, 0)),
                  wspec((4, D_B)), wspec((1, D_B)),
                  wspec((2, D_B, D_B)), wspec((2, D_B)), wspec((2, D_B, D_B)), wspec((2, D_B)),
                  wspec((2, D_B))],
        out_specs=[pl.BlockSpec((T, D_B), lambda b: (b, 0)),
                   pl.BlockSpec((1, 2, D_B), lambda b: (b, 0, 0))],
        out_shape=[jax.ShapeDtypeStruct((ntok, D_B), F32),
                   jax.ShapeDtypeStruct((nseq, 2, D_B), F32)],
        scratch_shapes=[pltpu.VMEM((2, T, D_B), F32), pltpu.VMEM((2, T, D_B), F32)],
        compiler_params=_cparams(("arbitrary",)),
        name="rglru",
    )(proj, proj, h0, conv_w, conv_b, wa_bd, ba, wi_bd, bi, lam)


def _mixout_kernel(of_ref, ob_ref, bf_ref, bb_ref, yb_ref, lg_ref, x_ref, mod_ref, wgu_ref, avg_ref,
                   gnw_ref, gnb_ref, wout_ref, l1w_ref, l1b_ref, wr_ref, br_ref,
                   x1_ref, u2_ref, rank_ref, prob_ref):
    m = mod_ref[0]
    avg = avg_ref[...]
    wkv = of_ref[...] + ob_ref[...]
    mu = _dot_xb2(wkv, avg)
    dv = wkv - mu
    var = _dot_xb2(dv * dv, avg)
    gn = dv * lax.rsqrt(var + GN_EPS) * gnw_ref[...] + gnb_ref[...]
    g = _dot1(jax.nn.sigmoid(lg_ref[...]), wgu_ref[...])
    ya = (gn + (bf_ref[...] + bb_ref[...])) * g
    mix = _dot1(ya, wout_ref[0:D_A, :]) + _dot1(yb_ref[...], wout_ref[D_A:, :])
    x1 = _layer_norm(DEEPNORM_ALPHA * x_ref[...] + m[2:3] * mix) * l1w_ref[...] + l1b_ref[...]
    x1_ref[...] = x1
    u2 = _layer_norm(x1) * (1.0 + m[4:5]) + m[3:4]
    _store_tok_tiles(u2_ref, u2)
    logits = _dot3(u2, wr_ref[...]) + br_ref[...]
    tm = logits.shape[0]
    lane = lax.broadcasted_iota(jnp.int32, (tm, N_EXPERTS), 1).astype(F32)
    lane4 = lax.broadcasted_iota(jnp.int32, (tm, TOP_K), 1)
    work = logits
    rank = jnp.zeros((tm, N_EXPERTS), F32)
    tops = []
    for j in range(TOP_K):
        mx = jnp.max(work, axis=1, keepdims=True)
        idx = jnp.min(jnp.where(work == mx, lane, float(N_EXPERTS)), axis=1, keepdims=True)
        hot = lane == idx
        tops.append(mx)
        rank = jnp.where(hot, j + 1.0, rank)
        work = jnp.where(hot, -jnp.inf, work)
    es = [jnp.exp(t - tops[0]) for t in tops]
    den = es[0] + es[1] + es[2] + es[3]
    prob = jnp.zeros((tm, TOP_K), F32)
    for j in range(TOP_K):
        prob = jnp.where(lane4 == j, es[j] / den, prob)
    rank_ref[...] = rank
    prob_ref[...] = prob


def _mixout_alias_kernel(*refs):
    _mixout_kernel(*refs[:17], *refs[18:])


def _mixout_first_kernel(*refs, ntiles):
    i = pl.program_id(0)

    @pl.when(i < ntiles)
    def _():
        _mixout_kernel(*refs)

    @pl.when(i >= ntiles)
    def _():
        refs[18][...] = jnp.zeros_like(refs[18])


def _mixout(o_f, o_b, bon_f, bon_b, yb, proj, x, mod_rows, tok_per_row, wgu, avg, gnw, gnb, wout, l1w, l1b,
            wr, br, u2_all, tok_base, ntok_all):
    ntok = x.shape[0]
    tm = 256
    per = tok_per_row // tm
    base_tiles = tok_base // tm
    alias = u2_all is not None
    ntiles = ntok // tm

    def wspec(shape):
        return pl.BlockSpec(shape, lambda i: (0,) * len(shape))

    def tl(i):
        return jnp.minimum(i, ntiles - 1)

    return pl.pallas_call(
        _mixout_alias_kernel if alias else functools.partial(_mixout_first_kernel, ntiles=ntiles),
        grid=(ntiles if alias else ntok_all // tm,),
        input_output_aliases={17: 1} if alias else {},
        in_specs=[pl.BlockSpec((tm, D_A), lambda i: (tl(i), 0)),
                  pl.BlockSpec((tm, D_A), lambda i: (tl(i), 0)),
                  pl.BlockSpec((tm, D_A), lambda i: (tl(i), 0)),
                  pl.BlockSpec((tm, D_A), lambda i: (tl(i), 0)),
                  pl.BlockSpec((tm, D_B), lambda i: (tl(i), 0)),
                  pl.BlockSpec((tm, LORA_G), lambda i: (tl(i), COL_LG)),
                  pl.BlockSpec((tm, D_MODEL), lambda i: (tl(i), 0)),
                  pl.BlockSpec((1, 6, D_MODEL), lambda i: (tl(i) // per, 0, 0)),
                  wspec((LORA_G, D_A)), wspec((D_A, D_A)), wspec((1, D_A)), wspec((1, D_A)),
                  wspec((D_MODEL, D_MODEL)), wspec((1, D_MODEL)), wspec((1, D_MODEL)),
                  wspec((D_MODEL, N_EXPERTS)), wspec((1, N_EXPERTS))]
                 + ([pl.BlockSpec(memory_space=pl.ANY)] if alias else []),
        out_specs=[pl.BlockSpec((tm, D_MODEL), lambda i: (tl(i), 0)),
                   pl.BlockSpec((tm * ROW_TILE, 128), lambda i: (i + base_tiles, 0)),
                   pl.BlockSpec((tm, N_EXPERTS), lambda i: (tl(i), 0)),
                   pl.BlockSpec((tm, TOP_K), lambda i: (tl(i), 0))],
        out_shape=[jax.ShapeDtypeStruct((ntok, D_MODEL), F32),
                   jax.ShapeDtypeStruct((ntok_all * ROW_TILE, 128), F32),
                   jax.ShapeDtypeStruct((ntok, N_EXPERTS), F32),
                   jax.ShapeDtypeStruct((ntok, TOP_K), F32)],
        compiler_params=_cparams(("arbitrary",)),
        name="mixout",
    )(o_f, o_b, bon_f, bon_b, yb, proj, x, mod_rows, wgu, avg, gnw, gnb, wout, l1w, l1b, wr, br,
      *([u2_all] if alias else []))


MOE_TM = 256
ROUTE_BLK = 256
INVERT_CHUNK = 2048


def _route_kernel(rank_ref, slot_ref, te_ref, nv_ref, c_scr, *, ntok, nt_pad):
    nblk = ntok // ROUTE_BLK
    ones = jnp.ones((N_EXPERTS, 128), BF16)
    lane4 = lax.broadcasted_iota(jnp.int32, (ROUTE_BLK, TOP_K), 1)
    rowb = lax.broadcasted_iota(jnp.int32, (ROUTE_BLK, ROUTE_BLK), 0)
    colb = lax.broadcasted_iota(jnp.int32, (ROUTE_BLK, ROUTE_BLK), 1)
    tri = jnp.where(rowb > colb, 1.0, 0.0).astype(BF16)

    def count(b, carry):
        rows = pl.ds(pl.multiple_of(b * ROUTE_BLK, ROUTE_BLK), ROUTE_BLK)
        hot = jnp.where(rank_ref[rows, :] > 0.0, 1.0, 0.0)
        c_scr[rows, :] = _mm(tri, hot.astype(BF16)) + carry
        return carry + jnp.sum(hot, axis=0, keepdims=True)

    n = lax.fori_loop(0, nblk, count, jnp.zeros((1, N_EXPERTS), F32))
    padded = jnp.floor((n + (MOE_TM - 1)) * (1.0 / MOE_TM)) * MOE_TM
    r32 = lax.broadcasted_iota(jnp.int32, (N_EXPERTS, N_EXPERTS), 0)
    c32 = lax.broadcasted_iota(jnp.int32, (N_EXPERTS, N_EXPERTS), 1)
    upper = jnp.where(r32 < c32, 1.0, 0.0).astype(BF16)
    off = _dot_xb(jnp.broadcast_to(padded, (8, N_EXPERTS)), upper)[0:1]
    gend = off + padded

    def place(b, carry):
        rows = pl.ds(pl.multiple_of(b * ROUTE_BLK, ROUTE_BLK), ROUTE_BLK)
        rk = rank_ref[rows, :]
        base = off + c_scr[rows, :]
        s = jnp.zeros((ROUTE_BLK, TOP_K), F32)
        for j in range(TOP_K):
            sj = _dot_xb2(jnp.where(rk == j + 1.0, base, 0.0), ones)
            s = jnp.where(lane4 == j, sj[:, :TOP_K], s)
        slot_ref[rows, :] = s.astype(jnp.int32)
        return carry

    lax.fori_loop(0, nblk, place, 0)

    gcol = jnp.sum(jnp.where(r32 == c32, jnp.broadcast_to(gend, (N_EXPERTS, N_EXPERTS)), 0.0),
                   axis=1, keepdims=True)
    tstart = (lax.broadcasted_iota(jnp.int32, (N_EXPERTS, nt_pad), 1) * MOE_TM).astype(F32)
    te = jnp.sum(jnp.where(gcol <= tstart, 1.0, 0.0), axis=0, keepdims=True)
    te_ref[...] = jnp.minimum(te, N_EXPERTS - 1.0).astype(jnp.int32)
    total = jnp.sum(padded, axis=1, keepdims=True)
    nv_ref[...] = jnp.broadcast_to(total * (1.0 / MOE_TM), (1, 128)).astype(jnp.int32)


def _route(rank):
    ntok = rank.shape[0]
    nt_max = ntok * TOP_K // MOE_TM + N_EXPERTS
    nt_pad = -(-(nt_max + 1) // 128) * 128
    full = lambda shape: pl.BlockSpec(shape, lambda i: (0,) * len(shape))
    slot, te, nv = pl.pallas_call(
        functools.partial(_route_kernel, ntok=ntok, nt_pad=nt_pad),
        grid=(1,),
        in_specs=[full((ntok, N_EXPERTS))],
        out_specs=[full((ntok, TOP_K)), full((1, nt_pad)), full((1, 128))],
        out_shape=[jax.ShapeDtypeStruct((ntok, TOP_K), jnp.int32),
                   jax.ShapeDtypeStruct((1, nt_pad), jnp.int32),
                   jax.ShapeDtypeStruct((1, 128), jnp.int32)],
        scratch_shapes=[pltpu.VMEM((ntok, N_EXPERTS), F32)],
        compiler_params=_cparams(("arbitrary",)),
        name="moe_route",
    )(rank)
    return slot.reshape(ntok * TOP_K), te.reshape(nt_pad), nv.reshape(128), nt_max


def _invert_kernel(slot_ref, inv_ref, fill_vmem, slot_smem, inv_smem, sem_in, sem_out, *, nassign, plane):
    slot_id = lax.broadcasted_iota(jnp.int32, fill_vmem.shape, 0)
    fill_vmem[...] = jnp.bitwise_and(slot_id, MOE_TM - 1) * ROW_TILE + TOP_K * plane
    fill = pltpu.make_async_copy(fill_vmem, inv_smem, sem_out)
    fill.start()
    fill.wait()

    def chunk(ci, c):
        a0 = pl.multiple_of(ci * INVERT_CHUNK, INVERT_CHUNK)
        cp = pltpu.make_async_copy(slot_ref.at[pl.ds(a0, INVERT_CHUNK)], slot_smem, sem_in)
        cp.start()
        cp.wait()

        def put(tt, c2):
            base = (lax.shift_right_logical(a0, 2) + tt) * ROW_TILE
            for k in range(TOP_K):
                inv_smem[slot_smem[tt * TOP_K + k]] = base + k * plane
            return c2

        lax.fori_loop(0, INVERT_CHUNK // TOP_K, put, 0, unroll=4)
        return c

    lax.fori_loop(0, nassign // INVERT_CHUNK, chunk, 0)
    out = pltpu.make_async_copy(inv_smem, inv_ref, sem_out)
    out.start()
    out.wait()


def _invert(slot, ntiles, ntok):
    nassign = slot.shape[0]
    plane = ntok * ROW_TILE
    assert plane & (plane - 1) == 0, "token count must be a power of two"
    anyspec = pl.BlockSpec(memory_space=pl.ANY)
    return pl.pallas_call(
        functools.partial(_invert_kernel, nassign=nassign, plane=plane),
        grid=(1,),
        in_specs=[anyspec], out_specs=anyspec,
        out_shape=jax.ShapeDtypeStruct((ntiles * MOE_TM,), jnp.int32),
        scratch_shapes=[pltpu.VMEM((ntiles * MOE_TM,), jnp.int32),
                        pltpu.SMEM((INVERT_CHUNK,), jnp.int32), pltpu.SMEM((ntiles * MOE_TM,), jnp.int32),
                        pltpu.SemaphoreType.DMA(()), pltpu.SemaphoreType.DMA(())],
        compiler_params=_cparams(("arbitrary",)),
        name="moe_invert",
    )(slot)


def _expert_kernel(te_ref, nv_ref, inv_ref, u2_ref, wg_ref, bg_ref, wu_ref, bu_ref, wd_ref, bd_ref, y4_ref,
                   xbuf, ybuf, wg_b, wu_b, wd_b, sem_g, sem_s, *, ntok):
    i = pl.program_id(0)
    nvalid = nv_ref[0]
    valid = i < nvalid
    trash = TOP_K * ntok
    plane = ntok * ROW_TILE

    def rows8(row):
        return pl.ds(pl.multiple_of(row, ROW_TILE), ROW_TILE)

    def gather_copy(tile, r, buf):
        src = jnp.bitwise_and(inv_ref[tile * MOE_TM + r], plane - 1)
        return pltpu.make_async_copy(u2_ref.at[rows8(src)], xbuf.at[buf, _tok_rows(r)], sem_g.at[buf])

    def scatter_copy(tile, r):
        return pltpu.make_async_copy(ybuf.at[_tok_rows(r)], y4_ref.at[rows8(inv_ref[tile * MOE_TM + r])], sem_s)

    def wait_gather(buf):
        pltpu.make_async_copy(xbuf.at[buf], xbuf.at[buf], sem_g.at[buf]).wait()

    def wait_scatter():
        pltpu.make_async_copy(ybuf, ybuf, sem_s).wait()

    @pl.when(i == 0)
    def _():
        def first(r2, c):
            gather_copy(0, 2 * r2, 0).start(priority=0)
            gather_copy(0, 2 * r2 + 1, 0).start(priority=1)
            return c

        lax.fori_loop(0, MOE_TM // 2, first, 0)
        ybuf[...] = jnp.zeros_like(ybuf)
        spare = pltpu.make_async_copy(ybuf, y4_ref.at[_tok_rows(trash, MOE_TM)], sem_s)
        spare.start()
        spare.wait()

    fresh = jnp.logical_or(i == 0, te_ref[i] != te_ref[jnp.maximum(i - 1, 0)])

    @pl.when(jnp.logical_and(valid, fresh))
    def _():
        wg_b[...] = wg_ref[0].astype(BF16)
        wu_b[...] = wu_ref[0].astype(BF16)
        wd_b[...] = wd_ref[0].astype(BF16)

    @pl.when(valid)
    def _():
        buf = jnp.bitwise_and(i, 1)
        nxt = jnp.minimum(i + 1, nvalid - 1)
        wait_gather(buf)
        x = _load_tok_tiles(xbuf, MOE_TM, (buf,)).astype(BF16)
        FC = 512
        nfc = D_FF // FC
        per = MOE_TM // nfc
        y = jnp.zeros((MOE_TM, D_MODEL), F32)
        for j in range(nfc):
            for r in range(j * per, (j + 1) * per):
                gather_copy(nxt, r, 1 - buf).start(priority=r % 2)
            cs = slice(j * FC, (j + 1) * FC)
            gate = jnp.minimum(_mm(x, wg_b[:, cs]) + bg_ref[0, :, cs], SWIGLU_LIMIT)
            up = jnp.clip(_mm(x, wu_b[:, cs]) + bu_ref[0, :, cs], -SWIGLU_LIMIT, SWIGLU_LIMIT)
            h = (up + 1.0) * gate * jax.nn.sigmoid(SWIGLU_ALPHA * gate)
            y = y + _mm(h.astype(BF16), wd_b[cs, :])

        @pl.when(i > 0)
        def _():
            wait_scatter()

        _store_tok_tiles(ybuf, y + bd_ref[0])
        for r in range(MOE_TM):
            scatter_copy(i, r).start(priority=r % 2)

    @pl.when(i == nvalid)
    def _():
        wait_gather(jnp.bitwise_and(nvalid, 1))
        wait_scatter()


def _experts(te, nv, inv, u2_all, nt_max, wg, bg, wu, bu, wd, bd):
    ntok = u2_all.shape[0] // ROW_TILE

    def tile(i, te, nv, inv):
        return te[jnp.minimum(i, nv[0] - 1)]

    wspec = pl.BlockSpec((1, D_MODEL, D_FF), lambda i, te, nv, inv: (tile(i, te, nv, inv), 0, 0))
    bspec = pl.BlockSpec((1, 1, D_FF), lambda i, te, nv, inv: (tile(i, te, nv, inv), 0, 0))
    anyspec = pl.BlockSpec(memory_space=pl.ANY)
    return pl.pallas_call(
        functools.partial(_expert_kernel, ntok=ntok),
        grid_spec=pltpu.PrefetchScalarGridSpec(
            num_scalar_prefetch=3, grid=(nt_max + 1,),
            in_specs=[anyspec, wspec, bspec, wspec, bspec, wspec, bspec], out_specs=anyspec,
            scratch_shapes=[pltpu.VMEM((2, MOE_TM * ROW_TILE, 128), F32),
                            pltpu.VMEM((MOE_TM * ROW_TILE, 128), F32),
                            pltpu.VMEM((D_MODEL, D_FF), BF16), pltpu.VMEM((D_MODEL, D_FF), BF16),
                            pltpu.VMEM((D_MODEL, D_FF), BF16),
                            pltpu.SemaphoreType.DMA((2,)), pltpu.SemaphoreType.DMA(())]),
        out_shape=jax.ShapeDtypeStruct(((TOP_K * ntok + MOE_TM) * ROW_TILE, 128), F32),
        compiler_params=_cparams(("arbitrary",)),
        name="moe_experts",
    )(te, nv, inv, u2_all, wg, bg.reshape(N_EXPERTS, 1, D_FF), wu, bu.reshape(N_EXPERTS, 1, D_FF),
      wd, bd.reshape(N_EXPERTS, 1, D_MODEL))


def _combine_kernel(y0_ref, y1_ref, y2_ref, y3_ref, prob_ref, x1_ref, mod_ref, w_ref, b_ref, o_ref):
    p = prob_ref[...]
    tm = p.shape[0]
    moe = p[:, 0:1] * _load_tok_tiles(y0_ref, tm)
    for j, y_ref in enumerate((y1_ref, y2_ref, y3_ref), start=1):
        moe = moe + p[:, j:j + 1] * _load_tok_tiles(y_ref, tm)
    m = mod_ref[0]
    o_ref[...] = _layer_norm(DEEPNORM_ALPHA * x1_ref[...] + m[5:6] * moe) * w_ref[...] + b_ref[...]


def _combine(y4, prob, x1, tok_base, ntok_all, mod_rows, tok_per_row, w, b):
    ntok = x1.shape[0]
    tm = 256
    per = tok_per_row // tm

    def yspec(k):
        return pl.BlockSpec((tm * ROW_TILE, 128), lambda i: ((k * ntok_all + tok_base) // tm + i, 0))

    return pl.pallas_call(
        _combine_kernel,
        grid=(ntok // tm,),
        in_specs=[yspec(0), yspec(1), yspec(2), yspec(3),
                  pl.BlockSpec((tm, TOP_K), lambda i: (i, 0)),
                  pl.BlockSpec((tm, D_MODEL), lambda i: (i, 0)),
                  pl.BlockSpec((1, 6, D_MODEL), lambda i: (i // per, 0, 0)),
                  pl.BlockSpec((1, D_MODEL), lambda i: (0, 0)),
                  pl.BlockSpec((1, D_MODEL), lambda i: (0, 0))],
        out_specs=pl.BlockSpec((tm, D_MODEL), lambda i: (i, 0)),
        out_shape=jax.ShapeDtypeStruct((ntok, D_MODEL), F32),
        compiler_params=_cparams(("arbitrary",)),
        name="moe_combine",
    )(y4, y4, y4, y4, prob, x1, mod_rows, w, b)


def _block_diag(w):
    nb, bb, _ = w.shape
    eye = jnp.eye(nb, dtype=w.dtype)
    return jnp.einsum('nij,nm->nimj', w, eye).reshape(nb * bb, nb * bb)


def kernel(x_prompt, x_sample, state_rwkv, state_rglru, c, c_ctx, w_mod, b_mod, w_in, w_decay_up, b_decay, w_iclr_up, b_iclr, w_gate_up, k_k, k_a, r_k, gn_w, gn_b, conv_w, conv_b, w_rg_a, b_rg_a, w_rg_i, b_rg_i, lam, w_out, ln1_w, ln1_b, w_router, b_router, w_e_gate, b_e_gate, w_e_up, b_e_up, w_e_down, b_e_down, ln2_w, ln2_b):
    Bp, Tp, D = x_prompt.shape
    Bs, Ts, _ = x_sample.shape
    l = 0
    row = lambda a: a[l].reshape(1, -1)

    cond8 = jnp.concatenate([c_ctx[None, :], c, jnp.zeros((8 - 1 - Bs, D), F32)], axis=0)
    mod = _modulation(cond8, w_mod[l], b_mod[l]).reshape(8, 6, D)
    mod_p, mod_s = mod[0:1], mod[1:1 + Bs]

    w_in_b = w_in[l].astype(BF16)
    w_out_b = w_out[l].astype(BF16)
    head_id = jnp.arange(D_A) // HEAD
    hs = (head_id[:, None] == head_id[None, :]).astype(BF16)
    avg = (hs.astype(F32) / HEAD).astype(BF16)
    wdec = w_decay_up[l].reshape(2 * LORA, D_A)
    wic = w_iclr_up[l].reshape(2 * LORA, D_A)
    wa_bd = jnp.stack([_block_diag(w_rg_a[l, 0]), _block_diag(w_rg_a[l, 1])])
    wi_bd = jnp.stack([_block_diag(w_rg_i[l, 0]), _block_diag(w_rg_i[l, 1])])

    xp = x_prompt.reshape(Bp * Tp, D)
    xs = x_sample.reshape(Bs * Ts, D)

    s0_s = state_rwkv[:, l].reshape(Bs, 2, N_PAIRS, 2, HEAD, HEAD).transpose(0, 1, 2, 4, 3, 5)
    s0_s = s0_s.reshape(Bs, 2, N_PAIRS, HEAD, 2 * HEAD)
    s0_p = jnp.zeros((Bp, 2, N_PAIRS, HEAD, 2 * HEAD), F32)
    h0_p = jnp.zeros((Bp, 2, D_B), F32)
    h0_s = state_rglru[:, l]

    ntok_all = Bp * Tp + Bs * Ts
    outs = []
    u2_all = None
    for x, mod_rows, tok_per_row, nseq, T, s0, h0, lrow, tok_base in (
            (xp, mod_p, Bp * Tp, Bp, Tp, s0_p, h0_p, Tp, 0),
            (xs, mod_s, Ts, Bs, Ts, s0_s, h0_s, GRID_W, Bp * Tp)):
        proj = _inproj(x, mod_rows, w_in_b, tok_per_row)
        o_f, o_b, bon_f, bon_b, s_fin = _rwkv(proj, s0, wdec, b_decay[l], wic, b_iclr[l], row(k_k), row(k_a),
                                              r_k[l].reshape(1, D_A), hs, nseq, T)
        yb, h_fin = _rglru(proj, h0, conv_w[l], row(conv_b), wa_bd, b_rg_a[l], wi_bd, b_rg_i[l], lam[l],
                           nseq, T, lrow)
        x1, u2_all, rank, prob = _mixout(o_f, o_b, bon_f, bon_b, yb, proj, x, mod_rows, tok_per_row,
                                        w_gate_up[l], avg, row(gn_w), row(gn_b), w_out_b, row(ln1_w),
                                        row(ln1_b), w_router[l], row(b_router), u2_all, tok_base, ntok_all)
        outs.append((x1, rank, prob, s_fin, h_fin))

    (x1_p, rank_p, prob_p, sfin_p, hfin_p), (x1_s, rank_s, prob_s, _, _) = outs
    slot, te, nv, nt_max = _route(jnp.concatenate([rank_p, rank_s], axis=0))
    inv = _invert(slot, nt_max, ntok_all)
    y4 = _experts(te, nv, inv, u2_all, nt_max, w_e_gate[l], b_e_gate[l], w_e_up[l], b_e_up[l], w_e_down[l],
                  b_e_down[l])
    y_p = _combine(y4, prob_p, x1_p, 0, ntok_all, mod_p, Bp * Tp, row(ln2_w), row(ln2_b))
    y_s = _combine(y4, prob_s, x1_s, Bp * Tp, ntok_all, mod_s, Ts, row(ln2_w), row(ln2_b))
    return (y_p.reshape(Bp, Tp, D), y_s.reshape(Bs, Ts, D),
            sfin_p[:, None], hfin_p[:, None])
```

```python
import functools
import math

import jax
import jax.numpy as jnp
from jax import lax
from jax.experimental import pallas as pl
from jax.experimental.pallas import tpu as pltpu

F32 = jnp.float32
BF16 = jnp.bfloat16

D_MODEL = 1024
D_A = 512
D_B = 512
HEAD = 64
N_HEADS = 8
N_PAIRS = N_HEADS // 2
GRID_W = 64
LORA = 64
LORA_G = 128
RGLRU_C = 8.0
N_EXPERTS = 32
TOP_K = 4
D_FF = 1024
SWIGLU_LIMIT = 7.0
SWIGLU_ALPHA = 1.702
LN_EPS = 1e-5
GN_EPS = 1e-5 * HEAD
D_IN = 3 * D_A + 2 * D_B + 2 * LORA + 2 * LORA + LORA_G
DEPTH = 1
DEEPNORM_ALPHA = (2 * DEPTH) ** 0.25

COL_R, COL_K, COL_V, COL_XR, COL_XGB = 0, 1, 2, 3, 4
COL_LW, COL_LA, COL_LG = 20, 21, 22

CHUNK = 64
PREP_CHUNKS = 2
VMEM_LIMIT = 56 * 1024 * 1024

_NN = (((1,), (0,)), ((), ()))
_NT = (((1,), (1,)), ((), ()))


def _mm(a, b, dims=_NN):
    return lax.dot_general(a, b, dims, preferred_element_type=F32)


def _split2(x):
    hi = x.astype(BF16)
    lo = (x - hi.astype(F32)).astype(BF16)
    return hi, lo


def _split3(x):
    hi = x.astype(BF16)
    r1 = x - hi.astype(F32)
    mid = r1.astype(BF16)
    lo = (r1 - mid.astype(F32)).astype(BF16)
    return hi, mid, lo


def _dot1(a, b, dims=_NN):
    return _mm(a.astype(BF16), b.astype(BF16), dims)


def _dot3(a, b):
    ah, al = _split2(a)
    bh, bl = _split2(b)
    return _mm(jnp.concatenate([ah, ah, al], axis=1), jnp.concatenate([bh, bl, bh], axis=0))


def _dot_xb(a, b_bf16):
    return _mm(jnp.concatenate(_split3(a), axis=1), jnp.concatenate([b_bf16] * 3, axis=0))


def _dot_xb2(a, b_bf16):
    return _mm(jnp.concatenate(_split2(a), axis=1), jnp.concatenate([b_bf16] * 2, axis=0))


def _dot_xa(a_bf16, b):
    return _mm(jnp.concatenate([a_bf16] * 3, axis=1), jnp.concatenate(_split3(b), axis=0))


ROW_TILE = 8


def _tok_rows(t, n=1):
    return pl.ds(pl.multiple_of(t * ROW_TILE, ROW_TILE), n * ROW_TILE)


def _load_tok_tiles(ref, ntok, lead=()):
    return jnp.concatenate([ref[lead + (pl.ds(s_, ntok, stride=ROW_TILE), slice(None))]
                            for s_ in range(ROW_TILE)], axis=1)


def _store_tok_tiles(ref, x):
    ntok = x.shape[0]
    for s_ in range(ROW_TILE):
        ref[pl.ds(s_, ntok, stride=ROW_TILE), :] = x[:, 128 * s_:128 * (s_ + 1)]


def _layer_norm(x):
    mu = jnp.mean(x, axis=-1, keepdims=True)
    xc = x - mu
    var = jnp.mean(xc * xc, axis=-1, keepdims=True)
    return xc * lax.rsqrt(var + LN_EPS)


def _cparams(sem):
    return pltpu.CompilerParams(dimension_semantics=sem, vmem_limit_bytes=VMEM_LIMIT)


def _mod_kernel(c_ref, w_ref, b_ref, o_ref):
    c = c_ref[...]
    s = c * jax.nn.sigmoid(c)
    o_ref[...] = _dot3(s, w_ref[...]) + b_ref[...]


def _modulation(cond8, w_mod, b_mod):
    n = w_mod.shape[1]
    tn = 1024
    return pl.pallas_call(
        _mod_kernel,
        grid=(n // tn,),
        in_specs=[pl.BlockSpec((8, D_MODEL), lambda j: (0, 0)),
                  pl.BlockSpec((D_MODEL, tn), lambda j: (0, j)),
                  pl.BlockSpec((1, tn), lambda j: (0, j))],
        out_specs=pl.BlockSpec((8, tn), lambda j: (0, j)),
        out_shape=jax.ShapeDtypeStruct((8, n), F32),
        compiler_params=_cparams(("arbitrary",)),
        name="modulation",
    )(cond8, w_mod, b_mod.reshape(1, n))


def _inproj_kernel(x_ref, mod_ref, w_ref, o_ref):
    m = mod_ref[0]
    u = _layer_norm(x_ref[...]) * (1.0 + m[1:2]) + m[0:1]
    o_ref[...] = _dot1(u, w_ref[...])


def _inproj(x, mod_rows, w_in_bf16, tok_per_row):
    ntok = x.shape[0]
    tm = 512
    per = tok_per_row // tm
    return pl.pallas_call(
        _inproj_kernel,
        grid=(ntok // tm,),
        in_specs=[pl.BlockSpec((tm, D_MODEL), lambda i: (i, 0)),
                  pl.BlockSpec((1, 6, D_MODEL), lambda i: (i // per, 0, 0)),
                  pl.BlockSpec((D_MODEL, D_IN), lambda i: (0, 0))],
        out_specs=pl.BlockSpec((tm, D_IN), lambda i: (i, 0)),
        out_shape=jax.ShapeDtypeStruct((ntok, D_IN), F32),
        compiler_params=_cparams(("arbitrary",)),
        name="inproj",
    )(x, mod_rows, w_in_bf16)


def _rwkv_kernel(rf_ref, kf_ref, vf_ref, lwf_ref, laf_ref, rb_ref, kb_ref, vb_ref, lwb_ref, lab_ref,
                 s0_ref, wdec_ref, bdec_ref, wic_ref, bic_ref, kkw_ref, ka_ref, rk_ref, hs_ref,
                 of_ref, ob_ref, bonf_ref, bonb_ref, sfin_ref,
                 s_scr, t_scr, w2_scr, mr_scr, ar_scr, bk_scr, gt_scr, *, G, ng):
    L = CHUNK
    g = pl.program_id(1)
    ins = ((rf_ref, kf_ref, vf_ref, lwf_ref, laf_ref), (rb_ref, kb_ref, vb_ref, lwb_ref, lab_ref))
    outs = ((of_ref, bonf_ref), (ob_ref, bonb_ref))

    @pl.when(g == 0)
    def _():
        s_scr[...] = s0_ref[0]

    hs = hs_ref[...]
    lane = lax.broadcasted_iota(jnp.int32, (L, 128), 1)
    rowp = lax.broadcasted_iota(jnp.int32, (L, 128), 0)
    colp = jnp.bitwise_and(lane, 63)
    h0 = lane < 64
    m0 = jnp.where(h0, 1.0, 0.0).astype(BF16)
    m1 = jnp.where(h0, 0.0, 1.0).astype(BF16)
    eye2 = jnp.where(rowp == colp, 1.0, 0.0)
    rowL = lax.broadcasted_iota(jnp.int32, (L, L), 0)
    colL = lax.broadcasted_iota(jnp.int32, (L, L), 1)

    def same(sh):
        return lax.shift_right_logical(rowp, sh) == lax.shift_right_logical(colp, sh)

    same8, same16, same32 = same(3), same(4), same(5)

    def bmask(c):
        return jnp.where(c, 1.0, 0.0).astype(BF16)

    mk8 = bmask(same8)
    merge_masks = (bmask(same16 & (~same8)), bmask(same32 & (~same16)), bmask(~same32))

    def smb(x):
        return jnp.concatenate([x * m0, x * m1], axis=0)

    def sms(s):
        return tuple(smb(x) for x in s)

    def pm(x_b, ys):
        return _mm(jnp.concatenate([x_b, x_b], axis=1), jnp.concatenate(sms(ys), axis=0))

    chains = [(d, p) for d in range(2) for p in range(N_PAIRS)]

    def each(fn, *lists):
        return [fn(*a) for a in zip(*lists)]

    def prep_stages(jb):
        pre, units = [], []
        for j, d in [(jb * PREP_CHUNKS + jo, d) for jo in range(PREP_CHUNKS) for d in range(2)]:
            r_ref, k_ref, v_ref, lw_ref, la_ref = ins[d]
            jn = j if d == 0 else G - 1 - j
            rows = pl.ds(pl.multiple_of(jn * L, L), L)
            u = d * G + j
            r = r_ref[rows, :]
            k = k_ref[rows, :]
            v = v_ref[rows, :]
            dsel = (lane >= 64) if d else h0
            tl = jnp.where(dsel, jnp.tanh(lw_ref[rows, :]), 0.0)
            dw = _dot3(tl, wdec_ref[...]) + bdec_ref[d:d + 1, :]
            logw = (-math.exp(-0.5)) * jax.nn.sigmoid(dw)
            la_m = jnp.where(dsel, la_ref[rows, :], 0.0)
            iclr = jax.nn.sigmoid(_dot3(la_m, wic_ref[...]) + bic_ref[d:d + 1, :])
            kkr = k * kkw_ref[...]
            nrm = jnp.sqrt(_dot_xb2(kkr * kkr, hs))
            kk = kkr / jnp.maximum(nrm, 1e-12)
            kdir = k * (1.0 + (iclr - 1.0) * ka_ref[...])
            bv = kk * iclr
            outs[d][1][rows, :] = _dot_xb2(r * kdir * rk_ref[...], hs) * v

            tri = bmask((rowL <= colL) if d else (rowL >= colL))
            cs = _dot_xa(tri, logw)
            ctot = jnp.sum(logw, axis=0, keepdims=True)
            g_inv = jnp.exp(-cs)
            g_rem = jnp.exp(ctot - cs)
            gt_scr[u] = jnp.exp(ctot)
            At = -kk * jnp.exp(cs - logw)
            Rt = r * jnp.exp(cs)
            Bt = bv * g_inv
            Kt = kdir * g_inv
            Bg = bv * g_rem
            Kg = kdir * g_rem
            for p in range(N_PAIRS):
                sl = slice(128 * p, 128 * (p + 1))
                ar = jnp.concatenate([At[:, sl], Rt[:, sl]], axis=0).astype(BF16)
                ar_scr[u, p] = ar
                bk_scr[u, p] = jnp.concatenate([Bg[:, sl], Kg[:, sl]], axis=0).astype(BF16)
                pre.append((ar, Bt[:, sl].astype(BF16), Kt[:, sl].astype(BF16), v[:, sl].astype(BF16)))
                units.append((d, p, u))

        def stage_m(c, pr):
            d = c[0]
            ar, bt, kt, _ = pr
            strict2 = (rowp < colp) if d else (rowp > colp)
            incl2 = (rowp <= colp) if d else (rowp >= colp)
            M = _mm(ar, jnp.concatenate([smb(bt), smb(kt)], axis=0), _NT)
            N = jnp.where(strict2, M[:L, :128], 0.0)
            Mak = jnp.where(strict2, M[:L, 128:], 0.0)
            mr_scr[c[2], c[1]] = jnp.concatenate([jnp.where(incl2, M[L:, :128], 0.0),
                                                       jnp.where(incl2, M[L:, 128:], 0.0)], axis=1).astype(BF16)
            return N, Mak

        yield
        NM = each(stage_m, units, pre)
        yield
        Ns = [_split2(nm[0]) for nm in NM]
        N0s = [(n[0] * mk8, n[1] * mk8) for n in Ns]
        T = [eye2 + jnp.where(same8, nm[0], 0.0) for nm in NM]
        P2s = [_split2(pm(n[0], n)) for n in N0s]
        W2 = [_mm(nm[1].astype(BF16), smb(pr[3])) for nm, pr in zip(NM, pre)]
        yield
        T = [t + pm(t.astype(BF16), p2) for t, p2 in zip(T, P2s)]
        yield
        P4s = [_split2(pm(p2[0], p2)) for p2 in P2s]
        yield
        T = [t + pm(t.astype(BF16), p4) for t, p4 in zip(T, P4s)]
        yield
        for mk in merge_masks:
            Ts = [_split2(t) for t in T]
            Y = [pm(ts[0], (n[0] * mk, n[1] * mk)) for ts, n in zip(Ts, Ns)]
            yield
            T = [t + pm(y.astype(BF16), ts) for t, y, ts in zip(T, Y, Ts)]
            yield
        for (d, p, u), t, w2 in zip(units, T, W2):
            t_scr[u, p] = t.astype(BF16)
            w2_scr[u, p] = w2

    def serial_stages(j):
        us = [d * G + j for d, _ in chains]
        rws = [pl.ds(pl.multiple_of((j if d == 0 else G - 1 - j) * L, L), L) for d, _ in chains]
        sls = [slice(128 * p, 128 * (p + 1)) for _, p in chains]
        S2 = [s_scr[d, p] for d, p in chains]
        XR = [_mm(ar_scr[u, p], smb(s2.astype(BF16)), _NT) for (d, p), u, s2 in zip(chains, us, S2)]
        yield
        U = [pm(t_scr[u, p], _split2(xr[:L] + w2_scr[u, p])) for (d, p), u, xr in zip(chains, us, XR)]
        V = [ins[d][2][rw, sl] for (d, p), rw, sl in zip(chains, rws, sls)]
        yield
        for (d, p), u, rw, sl, s2, xr, uu, vv in zip(chains, us, rws, sls, S2, XR, U, V):
            UVt = jnp.concatenate([uu, vv], axis=0).T
            Rm = _mm(UVt.astype(BF16), bk_scr[u, p])
            s_scr[d, p] = gt_scr[u][:, sl] * s2 + jnp.where(h0, Rm[:HEAD], Rm[HEAD:])
        for (d, p), u, rw, sl, xr, uu, vv in zip(chains, us, rws, sls, XR, U, V):
            rhs = jnp.concatenate([smb(uu.astype(BF16)), smb(vv.astype(BF16))], axis=0)
            outs[d][0][rw, sl] = xr[L:] + _mm(mr_scr[u, p], rhs)
        yield

    def serial_group(jb):
        for o in range(PREP_CHUNKS):
            yield from serial_stages(jb * PREP_CHUNKS + o)

    def run(lead, follow=None, pace=2):
        n = 0
        for _ in lead:
            n += 1
            if follow is not None and n % pace == 0:
                next(follow, None)
        if follow is not None:
            for _ in follow:
                pass

    ngrp = G // PREP_CHUNKS
    run(prep_stages(0))

    def fused(jb, carry):
        run(prep_stages(jb), serial_group(jb - 1))
        return carry

    lax.fori_loop(1, ngrp, fused, 0)
    run(serial_group(ngrp - 1))

    @pl.when(g == ng - 1)
    def _():
        for d in range(2):
            for p in range(N_PAIRS):
                sp = s_scr[d, p]
                sfin_ref[0, d, 2 * p] = sp[:, :HEAD]
                sfin_ref[0, d, 2 * p + 1] = sp[:, HEAD:]


def _rwkv(proj, s0_pairs, wdec, bdec, wic, bic, kkw, ka, rk, hs, nseq, T):
    ntok = proj.shape[0]
    nc = T // CHUNK
    G = min(nc, 8)
    ng = nc // G
    GL = G * CHUNK

    def fwd(b, g):
        return b * ng + g

    def bwd(b, g):
        return b * ng + ng - 1 - g

    def pspecs(rowblk):
        return [pl.BlockSpec((GL, D_A), lambda b, g: (rowblk(b, g), COL_R)),
                pl.BlockSpec((GL, D_A), lambda b, g: (rowblk(b, g), COL_K)),
                pl.BlockSpec((GL, D_A), lambda b, g: (rowblk(b, g), COL_V)),
                pl.BlockSpec((GL, 128), lambda b, g: (rowblk(b, g), COL_LW)),
                pl.BlockSpec((GL, 128), lambda b, g: (rowblk(b, g), COL_LA))]

    def wspec(shape):
        return pl.BlockSpec(shape, lambda b, g: (0,) * len(shape))

    ospec_f = pl.BlockSpec((GL, D_A), lambda b, g: (fwd(b, g), 0))
    ospec_b = pl.BlockSpec((GL, D_A), lambda b, g: (bwd(b, g), 0))
    tok = jax.ShapeDtypeStruct((ntok, D_A), F32)
    return pl.pallas_call(
        functools.partial(_rwkv_kernel, G=G, ng=ng),
        grid=(nseq, ng),
        in_specs=pspecs(fwd) + pspecs(bwd) + [
            pl.BlockSpec((1, 2, N_PAIRS, HEAD, 128), lambda b, g: (b, 0, 0, 0, 0)),
            wspec((128, D_A)), wspec((2, D_A)), wspec((128, D_A)), wspec((2, D_A)),
            wspec((1, D_A)), wspec((1, D_A)), wspec((1, D_A)), wspec((D_A, D_A))],
        out_specs=[ospec_f, ospec_b, ospec_f, ospec_b,
                   pl.BlockSpec((1, 2, N_HEADS, HEAD, HEAD), lambda b, g: (b, 0, 0, 0, 0))],
        out_shape=[tok, tok, tok, tok, jax.ShapeDtypeStruct((nseq, 2, N_HEADS, HEAD, HEAD), F32)],
        scratch_shapes=[pltpu.VMEM((2, N_PAIRS, HEAD, 128), F32),
                        pltpu.VMEM((2 * G, N_PAIRS, CHUNK, 128), BF16),
                        pltpu.VMEM((2 * G, N_PAIRS, CHUNK, 128), F32),
                        pltpu.VMEM((2 * G, N_PAIRS, CHUNK, 256), BF16),
                        pltpu.VMEM((2 * G, N_PAIRS, 2 * CHUNK, 128), BF16),
                        pltpu.VMEM((2 * G, N_PAIRS, 2 * CHUNK, 128), BF16),
                        pltpu.VMEM((2 * G, 1, D_A), F32)],
        compiler_params=_cparams(("arbitrary", "arbitrary")),
        name="rwkv_scan",
    )(*([proj] * 10), s0_pairs, wdec, bdec, wic, bic, kkw, ka, rk, hs)


def _gelu_tanh(x):
    return 0.5 * x * (1.0 + jnp.tanh(math.sqrt(2.0 / math.pi) * (x + 0.044715 * (x * x * x))))


def _rglru_kernel(xr_ref, xg_ref, h0_ref, cw_ref, cb_ref, wa_ref, ba_ref, wi_ref, bi_ref, lam_ref,
                  y_ref, hfin_ref, a_scr, u_scr, *, T, lrow):
    RB = min(T, 256)
    nblk = T // RB
    rowi = lax.broadcasted_iota(jnp.int32, (RB, D_B), 0)
    pos = jnp.bitwise_and(rowi, lrow - 1)
    cw = cw_ref[...]
    lam = lam_ref[...]
    sp = jnp.maximum(-lam, 0.0) + jnp.log1p(jnp.exp(-jnp.abs(lam)))

    def gates(blk, carry):
        r0 = pl.multiple_of(blk * RB, RB)
        x = xr_ref[pl.ds(r0, RB), :]
        xm1 = jnp.where(pos >= 1, pltpu.roll(x, 1, 0), 0.0)
        xm2 = jnp.where(pos >= 2, pltpu.roll(x, 2, 0), 0.0)
        xp1 = jnp.where(pos <= lrow - 2, pltpu.roll(x, RB - 1, 0), 0.0)
        xc = xm2 * cw[0:1] + xm1 * cw[1:2] + x * cw[2:3] + xp1 * cw[3:4] + cb_ref[...]
        xc_b = xc.astype(BF16)
        for dd in range(2):
            gr = jax.nn.sigmoid(_mm(xc_b, wa_ref[dd]) + ba_ref[dd:dd + 1, :])
            gi = jax.nn.sigmoid(_mm(xc_b, wi_ref[dd]) + bi_ref[dd:dd + 1, :])
            log_a = (-RGLRU_C) * gr * sp[dd:dd + 1, :]
            a = jnp.exp(log_a)
            a_scr[dd, pl.ds(r0, RB), :] = a
            u_scr[dd, pl.ds(r0, RB), :] = jnp.sqrt((1.0 - a) * (1.0 + a)) * gi * xc
        return carry

    lax.fori_loop(0, nblk, gates, 0)

    SUB = 8
    rowt = lax.broadcasted_iota(jnp.int32, (SUB, D_B), 0)

    def tile_scan(a, u, h_in, rev):
        for s_ in (1, 2, 4):
            keep = (rowt < SUB - s_) if rev else (rowt >= s_)
            sh = SUB - s_ if rev else s_
            a_sh = jnp.where(keep, pltpu.roll(a, sh, 0), 1.0)
            u_sh = jnp.where(keep, pltpu.roll(u, sh, 0), 0.0)
            u = a * u_sh + u
            a = a * a_sh
        return a * h_in + u

    def step(i, carry):
        hf, hb = carry
        rf = pl.ds(pl.multiple_of(i * SUB, SUB), SUB)
        hf_tile = tile_scan(a_scr[0, rf, :], u_scr[0, rf, :], hf, False)
        u_scr[0, rf, :] = hf_tile
        rb = pl.ds(pl.multiple_of(T - SUB - i * SUB, SUB), SUB)
        hb_tile = tile_scan(a_scr[1, rb, :], u_scr[1, rb, :], hb, True)
        u_scr[1, rb, :] = hb_tile
        return hf_tile[SUB - 1:SUB], hb_tile[0:1]

    h0 = h0_ref[0]
    hf, hb = lax.fori_loop(0, T // SUB, step, (h0[0:1], h0[1:2]), unroll=2)
    hfin_ref[0] = jnp.concatenate([hf, hb], axis=0)

    def outp(blk, carry):
        r0 = pl.multiple_of(blk * RB, RB)
        h = u_scr[0, pl.ds(r0, RB), :] + u_scr[1, pl.ds(r0, RB), :]
        y_ref[pl.ds(r0, RB), :] = h * _gelu_tanh(xg_ref[pl.ds(r0, RB), :])
        return carry

    lax.fori_loop(0, nblk, outp, 0)


def _rglru(proj, h0, conv_w, conv_b, wa_bd, ba, wi_bd, bi, lam, nseq, T, lrow):
    ntok = proj.shape[0]

    def wspec(shape):
        return pl.BlockSpec(shape, lambda b: (0,) * len(shape))

    return pl.pallas_call(
        functools.partial(_rglru_kernel, T=T, lrow=lrow),
        grid=(nseq,),
        in_specs=[pl.BlockSpec((T, D_B), lambda b: (b, COL_XR)),
                  pl.BlockSpec((T, D_B), lambda b: (b, COL_XGB)),
                  pl.BlockSpec((1, 2, D_B), lambda b: (b, 0, 0)),
                  wspec((4, D_B)), wspec((1, D_B)),
                  wspec((2, D_B, D_B)), wspec((2, D_B)), wspec((2, D_B, D_B)), wspec((2, D_B)),
                  wspec((2, D_B))],
        out_specs=[pl.BlockSpec((T, D_B), lambda b: (b, 0)),
                   pl.BlockSpec((1, 2, D_B), lambda b: (b, 0, 0))],
        out_shape=[jax.ShapeDtypeStruct((ntok, D_B), F32),
                   jax.ShapeDtypeStruct((nseq, 2, D_B), F32)],
        scratch_shapes=[pltpu.VMEM((2, T, D_B), F32), pltpu.VMEM((2, T, D_B), F32)],
        compiler_params=_cparams(("arbitrary",)),
        name="rglru",
    )(proj, proj, h0, conv_w, conv_b, wa_bd, ba, wi_bd, bi, lam)


def _mixout_kernel(of_ref, ob_ref, bf_ref, bb_ref, yb_ref, lg_ref, x_ref, mod_ref, wgu_ref, avg_ref,
                   gnw_ref, gnb_ref, wout_ref, l1w_ref, l1b_ref, wr_ref, br_ref,
                   x1_ref, u2_ref, rank_ref, prob_ref):
    m = mod_ref[0]
    avg = avg_ref[...]
    wkv = of_ref[...] + ob_ref[...]
    mu = _dot_xb2(wkv, avg)
    dv = wkv - mu
    var = _dot_xb2(dv * dv, avg)
    gn = dv * lax.rsqrt(var + GN_EPS) * gnw_ref[...] + gnb_ref[...]
    g = _dot1(jax.nn.sigmoid(lg_ref[...]), wgu_ref[...])
    ya = (gn + (bf_ref[...] + bb_ref[...])) * g
    mix = _dot1(ya, wout_ref[0:D_A, :]) + _dot1(yb_ref[...], wout_ref[D_A:, :])
    x1 = _layer_norm(DEEPNORM_ALPHA * x_ref[...] + m[2:3] * mix) * l1w_ref[...] + l1b_ref[...]
    x1_ref[...] = x1
    u2 = _layer_norm(x1) * (1.0 + m[4:5]) + m[3:4]
    _store_tok_tiles(u2_ref, u2)
    logits = _dot3(u2, wr_ref[...]) + br_ref[...]
    tm = logits.shape[0]
    lane = lax.broadcasted_iota(jnp.int32, (tm, N_EXPERTS), 1).astype(F32)
    lane4 = lax.broadcasted_iota(jnp.int32, (tm, TOP_K), 1)
    work = logits
    rank = jnp.zeros((tm, N_EXPERTS), F32)
    tops = []
    for j in range(TOP_K):
        mx = jnp.max(work, axis=1, keepdims=True)
        idx = jnp.min(jnp.where(work == mx, lane, float(N_EXPERTS)), axis=1, keepdims=True)
        hot = lane == idx
        tops.append(mx)
        rank = jnp.where(hot, j + 1.0, rank)
        work = jnp.where(hot, -jnp.inf, work)
    es = [jnp.exp(t - tops[0]) for t in tops]
    den = es[0] + es[1] + es[2] + es[3]
    prob = jnp.zeros((tm, TOP_K), F32)
    for j in range(TOP_K):
        prob = jnp.where(lane4 == j, es[j] / den, prob)
    rank_ref[...] = rank
    prob_ref[...] = prob


def _mixout_alias_kernel(*refs):
    _mixout_kernel(*refs[:17], *refs[18:])


def _mixout_first_kernel(*refs, ntiles):
    i = pl.program_id(0)

    @pl.when(i < ntiles)
    def _():
        _mixout_kernel(*refs)

    @pl.when(i >= ntiles)
    def _():
        refs[18][...] = jnp.zeros_like(refs[18])


def _mixout(o_f, o_b, bon_f, bon_b, yb, proj, x, mod_rows, tok_per_row, wgu, avg, gnw, gnb, wout, l1w, l1b,
            wr, br, u2_all, tok_base, ntok_all):
    ntok = x.shape[0]
    tm = 256
    per = tok_per_row // tm
    base_tiles = tok_base // tm
    alias = u2_all is not None
    ntiles = ntok // tm

    def wspec(shape):
        return pl.BlockSpec(shape, lambda i: (0,) * len(shape))

    def tl(i):
        return jnp.minimum(i, ntiles - 1)

    return pl.pallas_call(
        _mixout_alias_kernel if alias else functools.partial(_mixout_first_kernel, ntiles=ntiles),
        grid=(ntiles if alias else ntok_all // tm,),
        input_output_aliases={17: 1} if alias else {},
        in_specs=[pl.BlockSpec((tm, D_A), lambda i: (tl(i), 0)),
                  pl.BlockSpec((tm, D_A), lambda i: (tl(i), 0)),
                  pl.BlockSpec((tm, D_A), lambda i: (tl(i), 0)),
                  pl.BlockSpec((tm, D_A), lambda i: (tl(i), 0)),
                  pl.BlockSpec((tm, D_B), lambda i: (tl(i), 0)),
                  pl.BlockSpec((tm, LORA_G), lambda i: (tl(i), COL_LG)),
                  pl.BlockSpec((tm, D_MODEL), lambda i: (tl(i), 0)),
                  pl.BlockSpec((1, 6, D_MODEL), lambda i: (tl(i) // per, 0, 0)),
                  wspec((LORA_G, D_A)), wspec((D_A, D_A)), wspec((1, D_A)), wspec((1, D_A)),
                  wspec((D_MODEL, D_MODEL)), wspec((1, D_MODEL)), wspec((1, D_MODEL)),
                  wspec((D_MODEL, N_EXPERTS)), wspec((1, N_EXPERTS))]
                 + ([pl.BlockSpec(memory_space=pl.ANY)] if alias else []),
        out_specs=[pl.BlockSpec((tm, D_MODEL), lambda i: (tl(i), 0)),
                   pl.BlockSpec((tm * ROW_TILE, 128), lambda i: (i + base_tiles, 0)),
                   pl.BlockSpec((tm, N_EXPERTS), lambda i: (tl(i), 0)),
                   pl.BlockSpec((tm, TOP_K), lambda i: (tl(i), 0))],
        out_shape=[jax.ShapeDtypeStruct((ntok, D_MODEL), F32),
                   jax.ShapeDtypeStruct((ntok_all * ROW_TILE, 128), F32),
                   jax.ShapeDtypeStruct((ntok, N_EXPERTS), F32),
                   jax.ShapeDtypeStruct((ntok, TOP_K), F32)],
        compiler_params=_cparams(("arbitrary",)),
        name="mixout",
    )(o_f, o_b, bon_f, bon_b, yb, proj, x, mod_rows, wgu, avg, gnw, gnb, wout, l1w, l1b, wr, br,
      *([u2_all] if alias else []))


MOE_TM = 256
ROUTE_BLK = 256
INVERT_CHUNK = 2048


def _route_kernel(rank_ref, slot_ref, te_ref, nv_ref, c_scr, *, ntok, nt_pad):
    nblk = ntok // ROUTE_BLK
    ones = jnp.ones((N_EXPERTS, 128), BF16)
    lane4 = lax.broadcasted_iota(jnp.int32, (ROUTE_BLK, TOP_K), 1)
    rowb = lax.broadcasted_iota(jnp.int32, (ROUTE_BLK, ROUTE_BLK), 0)
    colb = lax.broadcasted_iota(jnp.int32, (ROUTE_BLK, ROUTE_BLK), 1)
    tri = jnp.where(rowb > colb, 1.0, 0.0).astype(BF16)

    def count(b, carry):
        rows = pl.ds(pl.multiple_of(b * ROUTE_BLK, ROUTE_BLK), ROUTE_BLK)
        hot = jnp.where(rank_ref[rows, :] > 0.0, 1.0, 0.0)
        c_scr[rows, :] = _mm(tri, hot.astype(BF16)) + carry
        return carry + jnp.sum(hot, axis=0, keepdims=True)

    n = lax.fori_loop(0, nblk, count, jnp.zeros((1, N_EXPERTS), F32))
    padded = jnp.floor((n + (MOE_TM - 1)) * (1.0 / MOE_TM)) * MOE_TM
    r32 = lax.broadcasted_iota(jnp.int32, (N_EXPERTS, N_EXPERTS), 0)
    c32 = lax.broadcasted_iota(jnp.int32, (N_EXPERTS, N_EXPERTS), 1)
    upper = jnp.where(r32 < c32, 1.0, 0.0).astype(BF16)
    off = _dot_xb(jnp.broadcast_to(padded, (8, N_EXPERTS)), upper)[0:1]
    gend = off + padded

    def place(b, carry):
        rows = pl.ds(pl.multiple_of(b * ROUTE_BLK, ROUTE_BLK), ROUTE_BLK)
        rk = rank_ref[rows, :]
        base = off + c_scr[rows, :]
        s = jnp.zeros((ROUTE_BLK, TOP_K), F32)
        for j in range(TOP_K):
            sj = _dot_xb2(jnp.where(rk == j + 1.0, base, 0.0), ones)
            s = jnp.where(lane4 == j, sj[:, :TOP_K], s)
        slot_ref[rows, :] = s.astype(jnp.int32)
        return carry

    lax.fori_loop(0, nblk, place, 0)

    gcol = jnp.sum(jnp.where(r32 == c32, jnp.broadcast_to(gend, (N_EXPERTS, N_EXPERTS)), 0.0),
                   axis=1, keepdims=True)
    tstart = (lax.broadcasted_iota(jnp.int32, (N_EXPERTS, nt_pad), 1) * MOE_TM).astype(F32)
    te = jnp.sum(jnp.where(gcol <= tstart, 1.0, 0.0), axis=0, keepdims=True)
    te_ref[...] = jnp.minimum(te, N_EXPERTS - 1.0).astype(jnp.int32)
    total = jnp.sum(padded, axis=1, keepdims=True)
    nv_ref[...] = jnp.broadcast_to(total * (1.0 / MOE_TM), (1, 128)).astype(jnp.int32)


def _route(rank):
    ntok = rank.shape[0]
    nt_max = ntok * TOP_K // MOE_TM + N_EXPERTS
    nt_pad = -(-(nt_max + 1) // 128) * 128
    full = lambda shape: pl.BlockSpec(shape, lambda i: (0,) * len(shape))
    slot, te, nv = pl.pallas_call(
        functools.partial(_route_kernel, ntok=ntok, nt_pad=nt_pad),
        grid=(1,),
        in_specs=[full((ntok, N_EXPERTS))],
        out_specs=[full((ntok, TOP_K)), full((1, nt_pad)), full((1, 128))],
        out_shape=[jax.ShapeDtypeStruct((ntok, TOP_K), jnp.int32),
                   jax.ShapeDtypeStruct((1, nt_pad), jnp.int32),
                   jax.ShapeDtypeStruct((1, 128), jnp.int32)],
        scratch_shapes=[pltpu.VMEM((ntok, N_EXPERTS), F32)],
        compiler_params=_cparams(("arbitrary",)),
        name="moe_route",
    )(rank)
    return slot.reshape(ntok * TOP_K), te.reshape(nt_pad), nv.reshape(128), nt_max


def _invert_kernel(slot_ref, inv_ref, fill_vmem, slot_smem, inv_smem, sem_in, sem_out, *, nassign, plane):
    slot_id = lax.broadcasted_iota(jnp.int32, fill_vmem.shape, 0)
    fill_vmem[...] = jnp.bitwise_and(slot_id, MOE_TM - 1) * ROW_TILE + TOP_K * plane
    fill = pltpu.make_async_copy(fill_vmem, inv_smem, sem_out)
    fill.start()
    fill.wait()

    def chunk(ci, c):
        a0 = pl.multiple_of(ci * INVERT_CHUNK, INVERT_CHUNK)
        cp = pltpu.make_async_copy(slot_ref.at[pl.ds(a0, INVERT_CHUNK)], slot_smem, sem_in)
        cp.start()
        cp.wait()

        def put(tt, c2):
            base = (lax.shift_right_logical(a0, 2) + tt) * ROW_TILE
            for k in range(TOP_K):
                inv_smem[slot_smem[tt * TOP_K + k]] = base + k * plane
            return c2

        lax.fori_loop(0, INVERT_CHUNK // TOP_K, put, 0, unroll=4)
        return c

    lax.fori_loop(0, nassign // INVERT_CHUNK, chunk, 0)
    out = pltpu.make_async_copy(inv_smem, inv_ref, sem_out)
    out.start()
    out.wait()


def _invert(slot, ntiles, ntok):
    nassign = slot.shape[0]
    plane = ntok * ROW_TILE
    assert plane & (plane - 1) == 0, "token count must be a power of two"
    anyspec = pl.BlockSpec(memory_space=pl.ANY)
    return pl.pallas_call(
        functools.partial(_invert_kernel, nassign=nassign, plane=plane),
        grid=(1,),
        in_specs=[anyspec], out_specs=anyspec,
        out_shape=jax.ShapeDtypeStruct((ntiles * MOE_TM,), jnp.int32),
        scratch_shapes=[pltpu.VMEM((ntiles * MOE_TM,), jnp.int32),
                        pltpu.SMEM((INVERT_CHUNK,), jnp.int32), pltpu.SMEM((ntiles * MOE_TM,), jnp.int32),
                        pltpu.SemaphoreType.DMA(()), pltpu.SemaphoreType.DMA(())],
        compiler_params=_cparams(("arbitrary",)),
        name="moe_invert",
    )(slot)


def _expert_kernel(te_ref, nv_ref, inv_ref, u2_ref, wg_ref, bg_ref, wu_ref, bu_ref, wd_ref, bd_ref, y4_ref,
                   xbuf, ybuf, wg_b, wu_b, wd_b, sem_g, sem_s, *, ntok):
    i = pl.program_id(0)
    nvalid = nv_ref[0]
    valid = i < nvalid
    trash = TOP_K * ntok
    plane = ntok * ROW_TILE

    def rows8(row):
        return pl.ds(pl.multiple_of(row, ROW_TILE), ROW_TILE)

    def gather_copy(tile, r, buf):
        src = jnp.bitwise_and(inv_ref[tile * MOE_TM + r], plane - 1)
        return pltpu.make_async_copy(u2_ref.at[rows8(src)], xbuf.at[buf, _tok_rows(r)], sem_g.at[buf])

    def scatter_copy(tile, r):
        return pltpu.make_async_copy(ybuf.at[_tok_rows(r)], y4_ref.at[rows8(inv_ref[tile * MOE_TM + r])], sem_s)

    def wait_gather(buf):
        pltpu.make_async_copy(xbuf.at[buf], xbuf.at[buf], sem_g.at[buf]).wait()

    def wait_scatter():
        pltpu.make_async_copy(ybuf, ybuf, sem_s).wait()

    @pl.when(i == 0)
    def _():
        def first(r2, c):
            gather_copy(0, 2 * r2, 0).start(priority=0)
            gather_copy(0, 2 * r2 + 1, 0).start(priority=1)
            return c

        lax.fori_loop(0, MOE_TM // 2, first, 0)
        ybuf[...] = jnp.zeros_like(ybuf)
        spare = pltpu.make_async_copy(ybuf, y4_ref.at[_tok_rows(trash, MOE_TM)], sem_s)
        spare.start()
        spare.wait()

    fresh = jnp.logical_or(i == 0, te_ref[i] != te_ref[jnp.maximum(i - 1, 0)])

    @pl.when(jnp.logical_and(valid, fresh))
    def _():
        wg_b[...] = wg_ref[0].astype(BF16)
        wu_b[...] = wu_ref[0].astype(BF16)
        wd_b[...] = wd_ref[0].astype(BF16)

    @pl.when(valid)
    def _():
        buf = jnp.bitwise_and(i, 1)
        nxt = jnp.minimum(i + 1, nvalid - 1)
        wait_gather(buf)
        x = _load_tok_tiles(xbuf, MOE_TM, (buf,)).astype(BF16)
        FC = 512
        nfc = D_FF // FC
        per = MOE_TM // nfc
        y = jnp.zeros((MOE_TM, D_MODEL), F32)
        for j in range(nfc):
            for r in range(j * per, (j + 1) * per):
                gather_copy(nxt, r, 1 - buf).start(priority=r % 2)
            cs = slice(j * FC, (j + 1) * FC)
            gate = jnp.minimum(_mm(x, wg_b[:, cs]) + bg_ref[0, :, cs], SWIGLU_LIMIT)
            up = jnp.clip(_mm(x, wu_b[:, cs]) + bu_ref[0, :, cs], -SWIGLU_LIMIT, SWIGLU_LIMIT)
            h = (up + 1.0) * gate * jax.nn.sigmoid(SWIGLU_ALPHA * gate)
            y = y + _mm(h.astype(BF16), wd_b[cs, :])

        @pl.when(i > 0)
        def _():
            wait_scatter()

        _store_tok_tiles(ybuf, y + bd_ref[0])
        for r in range(MOE_TM):
            scatter_copy(i, r).start(priority=r % 2)

    @pl.when(i == nvalid)
    def _():
        wait_gather(jnp.bitwise_and(nvalid, 1))
        wait_scatter()


def _experts(te, nv, inv, u2_all, nt_max, wg, bg, wu, bu, wd, bd):
    ntok = u2_all.shape[0] // ROW_TILE

    def tile(i, te, nv, inv):
        return te[jnp.minimum(i, nv[0] - 1)]

    wspec = pl.BlockSpec((1, D_MODEL, D_FF), lambda i, te, nv, inv: (tile(i, te, nv, inv), 0, 0))
    bspec = pl.BlockSpec((1, 1, D_FF), lambda i, te, nv, inv: (tile(i, te, nv, inv), 0, 0))
    anyspec = pl.BlockSpec(memory_space=pl.ANY)
    return pl.pallas_call(
        functools.partial(_expert_kernel, ntok=ntok),
        grid_spec=pltpu.PrefetchScalarGridSpec(
            num_scalar_prefetch=3, grid=(nt_max + 1,),
            in_specs=[anyspec, wspec, bspec, wspec, bspec, wspec, bspec], out_specs=anyspec,
            scratch_shapes=[pltpu.VMEM((2, MOE_TM * ROW_TILE, 128), F32),
                            pltpu.VMEM((MOE_TM * ROW_TILE, 128), F32),
                            pltpu.VMEM((D_MODEL, D_FF), BF16), pltpu.VMEM((D_MODEL, D_FF), BF16),
                            pltpu.VMEM((D_MODEL, D_FF), BF16),
                            pltpu.SemaphoreType.DMA((2,)), pltpu.SemaphoreType.DMA(())]),
        out_shape=jax.ShapeDtypeStruct(((TOP_K * ntok + MOE_TM) * ROW_TILE, 128), F32),
        compiler_params=_cparams(("arbitrary",)),
        name="moe_experts",
    )(te, nv, inv, u2_all, wg, bg.reshape(N_EXPERTS, 1, D_FF), wu, bu.reshape(N_EXPERTS, 1, D_FF),
      wd, bd.reshape(N_EXPERTS, 1, D_MODEL))


def _combine_kernel(y0_ref, y1_ref, y2_ref, y3_ref, prob_ref, x1_ref, mod_ref, w_ref, b_ref, o_ref):
    p = prob_ref[...]
    tm = p.shape[0]
    moe = p[:, 0:1] * _load_tok_tiles(y0_ref, tm)
    for j, y_ref in enumerate((y1_ref, y2_ref, y3_ref), start=1):
        moe = moe + p[:, j:j + 1] * _load_tok_tiles(y_ref, tm)
    m = mod_ref[0]
    o_ref[...] = _layer_norm(DEEPNORM_ALPHA * x1_ref[...] + m[5:6] * moe) * w_ref[...] + b_ref[...]


def _combine(y4, prob, x1, tok_base, ntok_all, mod_rows, tok_per_row, w, b):
    ntok = x1.shape[0]
    tm = 512
    per = tok_per_row // tm

    def yspec(k):
        return pl.BlockSpec((tm * ROW_TILE, 128), lambda i: ((k * ntok_all + tok_base) // tm + i, 0))

    return pl.pallas_call(
        _combine_kernel,
        grid=(ntok // tm,),
        in_specs=[yspec(0), yspec(1), yspec(2), yspec(3),
                  pl.BlockSpec((tm, TOP_K), lambda i: (i, 0)),
                  pl.BlockSpec((tm, D_MODEL), lambda i: (i, 0)),
                  pl.BlockSpec((1, 6, D_MODEL), lambda i: (i // per, 0, 0)),
                  pl.BlockSpec((1, D_MODEL), lambda i: (0, 0)),
                  pl.BlockSpec((1, D_MODEL), lambda i: (0, 0))],
        out_specs=pl.BlockSpec((tm, D_MODEL), lambda i: (i, 0)),
        out_shape=jax.ShapeDtypeStruct((ntok, D_MODEL), F32),
        compiler_params=_cparams(("arbitrary",)),
        name="moe_combine",
    )(y4, y4, y4, y4, prob, x1, mod_rows, w, b)


def _block_diag(w):
    nb, bb, _ = w.shape
    eye = jnp.eye(nb, dtype=w.dtype)
    return jnp.einsum('nij,nm->nimj', w, eye).reshape(nb * bb, nb * bb)


def kernel(x_prompt, x_sample, state_rwkv, state_rglru, c, c_ctx, w_mod, b_mod, w_in, w_decay_up, b_decay, w_iclr_up, b_iclr, w_gate_up, k_k, k_a, r_k, gn_w, gn_b, conv_w, conv_b, w_rg_a, b_rg_a, w_rg_i, b_rg_i, lam, w_out, ln1_w, ln1_b, w_router, b_router, w_e_gate, b_e_gate, w_e_up, b_e_up, w_e_down, b_e_down, ln2_w, ln2_b):
    Bp, Tp, D = x_prompt.shape
    Bs, Ts, _ = x_sample.shape
    l = 0
    row = lambda a: a[l].reshape(1, -1)

    cond8 = jnp.concatenate([c_ctx[None, :], c, jnp.zeros((8 - 1 - Bs, D), F32)], axis=0)
    mod = _modulation(cond8, w_mod[l], b_mod[l]).reshape(8, 6, D)
    mod_p, mod_s = mod[0:1], mod[1:1 + Bs]

    w_in_b = w_in[l].astype(BF16)
    w_out_b = w_out[l].astype(BF16)
    head_id = jnp.arange(D_A) // HEAD
    hs = (head_id[:, None] == head_id[None, :]).astype(BF16)
    avg = (hs.astype(F32) / HEAD).astype(BF16)
    wdec = w_decay_up[l].reshape(2 * LORA, D_A)
    wic = w_iclr_up[l].reshape(2 * LORA, D_A)
    wa_bd = jnp.stack([_block_diag(w_rg_a[l, 0]), _block_diag(w_rg_a[l, 1])]).astype(BF16)
    wi_bd = jnp.stack([_block_diag(w_rg_i[l, 0]), _block_diag(w_rg_i[l, 1])]).astype(BF16)

    xp = x_prompt.reshape(Bp * Tp, D)
    xs = x_sample.reshape(Bs * Ts, D)

    s0_s = state_rwkv[:, l].reshape(Bs, 2, N_PAIRS, 2, HEAD, HEAD).transpose(0, 1, 2, 4, 3, 5)
    s0_s = s0_s.reshape(Bs, 2, N_PAIRS, HEAD, 2 * HEAD)
    s0_p = jnp.zeros((Bp, 2, N_PAIRS, HEAD, 2 * HEAD), F32)
    h0_p = jnp.zeros((Bp, 2, D_B), F32)
    h0_s = state_rglru[:, l]

    ntok_all = Bp * Tp + Bs * Ts
    outs = []
    u2_all = None
    for x, mod_rows, tok_per_row, nseq, T, s0, h0, lrow, tok_base in (
            (xp, mod_p, Bp * Tp, Bp, Tp, s0_p, h0_p, Tp, 0),
            (xs, mod_s, Ts, Bs, Ts, s0_s, h0_s, GRID_W, Bp * Tp)):
        proj = _inproj(x, mod_rows, w_in_b, tok_per_row)
        o_f, o_b, bon_f, bon_b, s_fin = _rwkv(proj, s0, wdec, b_decay[l], wic, b_iclr[l], row(k_k), row(k_a),
                                              r_k[l].reshape(1, D_A), hs, nseq, T)
        yb, h_fin = _rglru(proj, h0, conv_w[l], row(conv_b), wa_bd, b_rg_a[l], wi_bd, b_rg_i[l], lam[l],
                           nseq, T, lrow)
        x1, u2_all, rank, prob = _mixout(o_f, o_b, bon_f, bon_b, yb, proj, x, mod_rows, tok_per_row,
                                        w_gate_up[l], avg, row(gn_w), row(gn_b), w_out_b, row(ln1_w),
                                        row(ln1_b), w_router[l], row(b_router), u2_all, tok_base, ntok_all)
        outs.append((x1, rank, prob, s_fin, h_fin))

    (x1_p, rank_p, prob_p, sfin_p, hfin_p), (x1_s, rank_s, prob_s, _, _) = outs
    slot, te, nv, nt_max = _route(jnp.concatenate([rank_p, rank_s], axis=0))
    inv = _invert(slot, nt_max, ntok_all)
    y4 = _experts(te, nv, inv, u2_all, nt_max, w_e_gate[l], b_e_gate[l], w_e_up[l], b_e_up[l], w_e_down[l],
                  b_e_down[l])
    y_p = _combine(y4, prob_p, x1_p, 0, ntok_all, mod_p, Bp * Tp, row(ln2_w), row(ln2_b))
    y_s = _combine(y4, prob_s, x1_s, Bp * Tp, ntok_all, mod_s, Ts, row(ln2_w), row(ln2_b))
    return (y_p.reshape(Bp, Tp, D), y_s.reshape(Bs, Ts, D),
            sfin_p[:, None], hfin_p[:, None])
```

```python
import functools
import math

import jax
import jax.numpy as jnp
from jax import lax
from jax.experimental import pallas as pl
from jax.experimental.pallas import tpu as pltpu

F32 = jnp.float32
BF16 = jnp.bfloat16

D_MODEL = 1024
D_A = 512
D_B = 512
HEAD = 64
N_HEADS = 8
N_PAIRS = N_HEADS // 2
GRID_W = 64
LORA = 64
LORA_G = 128
RGLRU_C = 8.0
N_EXPERTS = 32
TOP_K = 4
D_FF = 1024
SWIGLU_LIMIT = 7.0
SWIGLU_ALPHA = 1.702
LN_EPS = 1e-5
GN_EPS = 1e-5 * HEAD
D_IN = 3 * D_A + 2 * D_B + 2 * LORA + 2 * LORA + LORA_G
DEPTH = 1
DEEPNORM_ALPHA = (2 * DEPTH) ** 0.25

COL_R, COL_K, COL_V, COL_XR, COL_XGB = 0, 1, 2, 3, 4
COL_LW, COL_LA, COL_LG = 20, 21, 22

CHUNK = 64
PREP_CHUNKS = 2
VMEM_LIMIT = 56 * 1024 * 1024

_NN = (((1,), (0,)), ((), ()))
_NT = (((1,), (1,)), ((), ()))


def _mm(a, b, dims=_NN):
    return lax.dot_general(a, b, dims, preferred_element_type=F32)


def _split2(x):
    hi = x.astype(BF16)
    lo = (x - hi.astype(F32)).astype(BF16)
    return hi, lo


def _split3(x):
    hi = x.astype(BF16)
    r1 = x - hi.astype(F32)
    mid = r1.astype(BF16)
    lo = (r1 - mid.astype(F32)).astype(BF16)
    return hi, mid, lo


def _dot1(a, b, dims=_NN):
    return _mm(a.astype(BF16), b.astype(BF16), dims)


def _dot3(a, b):
    ah, al = _split2(a)
    bh, bl = _split2(b)
    return _mm(jnp.concatenate([ah, ah, al], axis=1), jnp.concatenate([bh, bl, bh], axis=0))


def _dot_xb(a, b_bf16):
    return _mm(jnp.concatenate(_split3(a), axis=1), jnp.concatenate([b_bf16] * 3, axis=0))


def _dot_xb2(a, b_bf16):
    return _mm(jnp.concatenate(_split2(a), axis=1), jnp.concatenate([b_bf16] * 2, axis=0))


def _dot_xa(a_bf16, b):
    return _mm(jnp.concatenate([a_bf16] * 3, axis=1), jnp.concatenate(_split3(b), axis=0))


ROW_TILE = 8


def _tok_rows(t, n=1):
    return pl.ds(pl.multiple_of(t * ROW_TILE, ROW_TILE), n * ROW_TILE)


def _load_tok_tiles(ref, ntok, lead=()):
    return jnp.concatenate([ref[lead + (pl.ds(s_, ntok, stride=ROW_TILE), slice(None))]
                            for s_ in range(ROW_TILE)], axis=1)


def _store_tok_tiles(ref, x):
    ntok = x.shape[0]
    for s_ in range(ROW_TILE):
        ref[pl.ds(s_, ntok, stride=ROW_TILE), :] = x[:, 128 * s_:128 * (s_ + 1)]


def _layer_norm(x):
    mu = jnp.mean(x, axis=-1, keepdims=True)
    xc = x - mu
    var = jnp.mean(xc * xc, axis=-1, keepdims=True)
    return xc * lax.rsqrt(var + LN_EPS)


def _cparams(sem):
    return pltpu.CompilerParams(dimension_semantics=sem, vmem_limit_bytes=VMEM_LIMIT)


def _mod_kernel(c_ref, w_ref, b_ref, o_ref):
    c = c_ref[...]
    s = c * jax.nn.sigmoid(c)
    o_ref[...] = _dot3(s, w_ref[...]) + b_ref[...]


def _modulation(cond8, w_mod, b_mod):
    n = w_mod.shape[1]
    tn = 1024
    return pl.pallas_call(
        _mod_kernel,
        grid=(n // tn,),
        in_specs=[pl.BlockSpec((8, D_MODEL), lambda j: (0, 0)),
                  pl.BlockSpec((D_MODEL, tn), lambda j: (0, j)),
                  pl.BlockSpec((1, tn), lambda j: (0, j))],
        out_specs=pl.BlockSpec((8, tn), lambda j: (0, j)),
        out_shape=jax.ShapeDtypeStruct((8, n), F32),
        compiler_params=_cparams(("arbitrary",)),
        name="modulation",
    )(cond8, w_mod, b_mod.reshape(1, n))


def _inproj_kernel(x_ref, mod_ref, w_ref, o_ref):
    m = mod_ref[0]
    u = _layer_norm(x_ref[...]) * (1.0 + m[1:2]) + m[0:1]
    o_ref[...] = _dot1(u, w_ref[...])


def _inproj(x, mod_rows, w_in_bf16, tok_per_row):
    ntok = x.shape[0]
    tm = 1024
    per = tok_per_row // tm
    return pl.pallas_call(
        _inproj_kernel,
        grid=(ntok // tm,),
        in_specs=[pl.BlockSpec((tm, D_MODEL), lambda i: (i, 0)),
                  pl.BlockSpec((1, 6, D_MODEL), lambda i: (i // per, 0, 0)),
                  pl.BlockSpec((D_MODEL, D_IN), lambda i: (0, 0))],
        out_specs=pl.BlockSpec((tm, D_IN), lambda i: (i, 0)),
        out_shape=jax.ShapeDtypeStruct((ntok, D_IN), F32),
        compiler_params=_cparams(("arbitrary",)),
        name="inproj",
    )(x, mod_rows, w_in_bf16)


def _rwkv_kernel(rf_ref, kf_ref, vf_ref, lwf_ref, laf_ref, rb_ref, kb_ref, vb_ref, lwb_ref, lab_ref,
                 s0_ref, wdec_ref, bdec_ref, wic_ref, bic_ref, kkw_ref, ka_ref, rk_ref, hs_ref,
                 of_ref, ob_ref, bonf_ref, bonb_ref, sfin_ref,
                 s_scr, t_scr, w2_scr, mr_scr, ar_scr, bk_scr, gt_scr, *, G, ng):
    L = CHUNK
    g = pl.program_id(1)
    ins = ((rf_ref, kf_ref, vf_ref, lwf_ref, laf_ref), (rb_ref, kb_ref, vb_ref, lwb_ref, lab_ref))
    outs = ((of_ref, bonf_ref), (ob_ref, bonb_ref))

    @pl.when(g == 0)
    def _():
        s_scr[...] = s0_ref[0]

    hs = hs_ref[...]
    lane = lax.broadcasted_iota(jnp.int32, (L, 128), 1)
    rowp = lax.broadcasted_iota(jnp.int32, (L, 128), 0)
    colp = jnp.bitwise_and(lane, 63)
    h0 = lane < 64
    m0 = jnp.where(h0, 1.0, 0.0).astype(BF16)
    m1 = jnp.where(h0, 0.0, 1.0).astype(BF16)
    eye2 = jnp.where(rowp == colp, 1.0, 0.0)
    rowL = lax.broadcasted_iota(jnp.int32, (L, L), 0)
    colL = lax.broadcasted_iota(jnp.int32, (L, L), 1)

    def same(sh):
        return lax.shift_right_logical(rowp, sh) == lax.shift_right_logical(colp, sh)

    same8, same16, same32 = same(3), same(4), same(5)

    def bmask(c):
        return jnp.where(c, 1.0, 0.0).astype(BF16)

    mk8 = bmask(same8)
    merge_masks = (bmask(same16 & (~same8)), bmask(same32 & (~same16)), bmask(~same32))

    def smb(x):
        return jnp.concatenate([x * m0, x * m1], axis=0)

    def sms(s):
        return tuple(smb(x) for x in s)

    def pm(x_b, ys):
        return _mm(jnp.concatenate([x_b, x_b], axis=1), jnp.concatenate(sms(ys), axis=0))

    chains = [(d, p) for d in range(2) for p in range(N_PAIRS)]

    def each(fn, *lists):
        return [fn(*a) for a in zip(*lists)]

    def prep_stages(jb):
        pre, units = [], []
        for j, d in [(jb * PREP_CHUNKS + jo, d) for jo in range(PREP_CHUNKS) for d in range(2)]:
            r_ref, k_ref, v_ref, lw_ref, la_ref = ins[d]
            jn = j if d == 0 else G - 1 - j
            rows = pl.ds(pl.multiple_of(jn * L, L), L)
            u = d * G + j
            r = r_ref[rows, :]
            k = k_ref[rows, :]
            v = v_ref[rows, :]
            dsel = (lane >= 64) if d else h0
            tl = jnp.where(dsel, jnp.tanh(lw_ref[rows, :]), 0.0)
            dw = _dot3(tl, wdec_ref[...]) + bdec_ref[d:d + 1, :]
            logw = (-math.exp(-0.5)) * jax.nn.sigmoid(dw)
            la_m = jnp.where(dsel, la_ref[rows, :], 0.0)
            iclr = jax.nn.sigmoid(_dot3(la_m, wic_ref[...]) + bic_ref[d:d + 1, :])
            kkr = k * kkw_ref[...]
            nrm = jnp.sqrt(_dot_xb2(kkr * kkr, hs))
            kk = kkr / jnp.maximum(nrm, 1e-12)
            kdir = k * (1.0 + (iclr - 1.0) * ka_ref[...])
            bv = kk * iclr
            outs[d][1][rows, :] = _dot_xb2(r * kdir * rk_ref[...], hs) * v

            tri = bmask((rowL <= colL) if d else (rowL >= colL))
            cs = _dot_xa(tri, logw)
            ctot = jnp.sum(logw, axis=0, keepdims=True)
            g_inv = jnp.exp(-cs)
            g_rem = jnp.exp(ctot - cs)
            gt_scr[u] = jnp.exp(ctot)
            At = -kk * jnp.exp(cs - logw)
            Rt = r * jnp.exp(cs)
            Bt = bv * g_inv
            Kt = kdir * g_inv
            Bg = bv * g_rem
            Kg = kdir * g_rem
            for p in range(N_PAIRS):
                sl = slice(128 * p, 128 * (p + 1))
                ar = jnp.concatenate([At[:, sl], Rt[:, sl]], axis=0).astype(BF16)
                ar_scr[u, p] = ar
                bk_scr[u, p] = jnp.concatenate([Bg[:, sl], Kg[:, sl]], axis=0).astype(BF16)
                pre.append((ar, Bt[:, sl].astype(BF16), Kt[:, sl].astype(BF16), v[:, sl].astype(BF16)))
                units.append((d, p, u))

        def stage_m(c, pr):
            d = c[0]
            ar, bt, kt, _ = pr
            strict2 = (rowp < colp) if d else (rowp > colp)
            incl2 = (rowp <= colp) if d else (rowp >= colp)
            M = _mm(ar, jnp.concatenate([smb(bt), smb(kt)], axis=0), _NT)
            N = jnp.where(strict2, M[:L, :128], 0.0)
            Mak = jnp.where(strict2, M[:L, 128:], 0.0)
            mr_scr[c[2], c[1]] = jnp.concatenate([jnp.where(incl2, M[L:, :128], 0.0),
                                                       jnp.where(incl2, M[L:, 128:], 0.0)], axis=1).astype(BF16)
            return N, Mak

        yield
        NM = each(stage_m, units, pre)
        yield
        Ns = [_split2(nm[0]) for nm in NM]
        N0s = [(n[0] * mk8, n[1] * mk8) for n in Ns]
        T = [eye2 + jnp.where(same8, nm[0], 0.0) for nm in NM]
        P2s = [_split2(pm(n[0], n)) for n in N0s]
        W2 = [_mm(nm[1].astype(BF16), smb(pr[3])) for nm, pr in zip(NM, pre)]
        yield
        T = [t + pm(t.astype(BF16), p2) for t, p2 in zip(T, P2s)]
        yield
        P4s = [_split2(pm(p2[0], p2)) for p2 in P2s]
        yield
        T = [t + pm(t.astype(BF16), p4) for t, p4 in zip(T, P4s)]
        yield
        for mk in merge_masks:
            Ts = [_split2(t) for t in T]
            Y = [pm(ts[0], (n[0] * mk, n[1] * mk)) for ts, n in zip(Ts, Ns)]
            yield
            T = [t + pm(y.astype(BF16), ts) for t, y, ts in zip(T, Y, Ts)]
            yield
        for (d, p, u), t, w2 in zip(units, T, W2):
            t_scr[u, p] = t.astype(BF16)
            w2_scr[u, p] = w2

    def serial_stages(j):
        us = [d * G + j for d, _ in chains]
        rws = [pl.ds(pl.multiple_of((j if d == 0 else G - 1 - j) * L, L), L) for d, _ in chains]
        sls = [slice(128 * p, 128 * (p + 1)) for _, p in chains]
        S2 = [s_scr[d, p] for d, p in chains]
        XR = [_mm(ar_scr[u, p], smb(s2.astype(BF16)), _NT) for (d, p), u, s2 in zip(chains, us, S2)]
        yield
        U = [pm(t_scr[u, p], _split2(xr[:L] + w2_scr[u, p])) for (d, p), u, xr in zip(chains, us, XR)]
        V = [ins[d][2][rw, sl] for (d, p), rw, sl in zip(chains, rws, sls)]
        yield
        for (d, p), u, rw, sl, s2, xr, uu, vv in zip(chains, us, rws, sls, S2, XR, U, V):
            UVt = jnp.concatenate([uu, vv], axis=0).T
            Rm = _mm(UVt.astype(BF16), bk_scr[u, p])
            s_scr[d, p] = gt_scr[u][:, sl] * s2 + jnp.where(h0, Rm[:HEAD], Rm[HEAD:])
        for (d, p), u, rw, sl, xr, uu, vv in zip(chains, us, rws, sls, XR, U, V):
            rhs = jnp.concatenate([smb(uu.astype(BF16)), smb(vv.astype(BF16))], axis=0)
            outs[d][0][rw, sl] = xr[L:] + _mm(mr_scr[u, p], rhs)
        yield

    def serial_group(jb):
        for o in range(PREP_CHUNKS):
            yield from serial_stages(jb * PREP_CHUNKS + o)

    def run(lead, follow=None, pace=2):
        n = 0
        for _ in lead:
            n += 1
            if follow is not None and n % pace == 0:
                next(follow, None)
        if follow is not None:
            for _ in follow:
                pass

    ngrp = G // PREP_CHUNKS
    run(prep_stages(0))

    def fused(jb, carry):
        run(prep_stages(jb), serial_group(jb - 1))
        return carry

    lax.fori_loop(1, ngrp, fused, 0)
    run(serial_group(ngrp - 1))

    @pl.when(g == ng - 1)
    def _():
        for d in range(2):
            for p in range(N_PAIRS):
                sp = s_scr[d, p]
                sfin_ref[0, d, 2 * p] = sp[:, :HEAD]
                sfin_ref[0, d, 2 * p + 1] = sp[:, HEAD:]


def _rwkv(proj, s0_pairs, wdec, bdec, wic, bic, kkw, ka, rk, hs, nseq, T):
    ntok = proj.shape[0]
    nc = T // CHUNK
    G = min(nc, 8)
    ng = nc // G
    GL = G * CHUNK

    def fwd(b, g):
        return b * ng + g

    def bwd(b, g):
        return b * ng + ng - 1 - g

    def pspecs(rowblk):
        return [pl.BlockSpec((GL, D_A), lambda b, g: (rowblk(b, g), COL_R)),
                pl.BlockSpec((GL, D_A), lambda b, g: (rowblk(b, g), COL_K)),
                pl.BlockSpec((GL, D_A), lambda b, g: (rowblk(b, g), COL_V)),
                pl.BlockSpec((GL, 128), lambda b, g: (rowblk(b, g), COL_LW)),
                pl.BlockSpec((GL, 128), lambda b, g: (rowblk(b, g), COL_LA))]

    def wspec(shape):
        return pl.BlockSpec(shape, lambda b, g: (0,) * len(shape))

    ospec_f = pl.BlockSpec((GL, D_A), lambda b, g: (fwd(b, g), 0))
    ospec_b = pl.BlockSpec((GL, D_A), lambda b, g: (bwd(b, g), 0))
    tok = jax.ShapeDtypeStruct((ntok, D_A), F32)
    return pl.pallas_call(
        functools.partial(_rwkv_kernel, G=G, ng=ng),
        grid=(nseq, ng),
        in_specs=pspecs(fwd) + pspecs(bwd) + [
            pl.BlockSpec((1, 2, N_PAIRS, HEAD, 128), lambda b, g: (b, 0, 0, 0, 0)),
            wspec((128, D_A)), wspec((2, D_A)), wspec((128, D_A)), wspec((2, D_A)),
            wspec((1, D_A)), wspec((1, D_A)), wspec((1, D_A)), wspec((D_A, D_A))],
        out_specs=[ospec_f, ospec_b, ospec_f, ospec_b,
                   pl.BlockSpec((1, 2, N_HEADS, HEAD, HEAD), lambda b, g: (b, 0, 0, 0, 0))],
        out_shape=[tok, tok, tok, tok, jax.ShapeDtypeStruct((nseq, 2, N_HEADS, HEAD, HEAD), F32)],
        scratch_shapes=[pltpu.VMEM((2, N_PAIRS, HEAD, 128), F32),
                        pltpu.VMEM((2 * G, N_PAIRS, CHUNK, 128), BF16),
                        pltpu.VMEM((2 * G, N_PAIRS, CHUNK, 128), F32),
                        pltpu.VMEM((2 * G, N_PAIRS, CHUNK, 256), BF16),
                        pltpu.VMEM((2 * G, N_PAIRS, 2 * CHUNK, 128), BF16),
                        pltpu.VMEM((2 * G, N_PAIRS, 2 * CHUNK, 128), BF16),
                        pltpu.VMEM((2 * G, 1, D_A), F32)],
        compiler_params=_cparams(("arbitrary", "arbitrary")),
        name="rwkv_scan",
    )(*([proj] * 10), s0_pairs, wdec, bdec, wic, bic, kkw, ka, rk, hs)


def _gelu_tanh(x):
    return 0.5 * x * (1.0 + jnp.tanh(math.sqrt(2.0 / math.pi) * (x + 0.044715 * (x * x * x))))


def _rglru_kernel(xr_ref, xg_ref, h0_ref, cw_ref, cb_ref, wa_ref, ba_ref, wi_ref, bi_ref, lam_ref,
                  y_ref, hfin_ref, a_scr, u_scr, *, T, lrow):
    RB = min(T, 256)
    nblk = T // RB
    rowi = lax.broadcasted_iota(jnp.int32, (RB, D_B), 0)
    pos = jnp.bitwise_and(rowi, lrow - 1)
    cw = cw_ref[...]
    lam = lam_ref[...]
    sp = jnp.maximum(-lam, 0.0) + jnp.log1p(jnp.exp(-jnp.abs(lam)))

    def gates(blk, carry):
        r0 = pl.multiple_of(blk * RB, RB)
        x = xr_ref[pl.ds(r0, RB), :]
        xm1 = jnp.where(pos >= 1, pltpu.roll(x, 1, 0), 0.0)
        xm2 = jnp.where(pos >= 2, pltpu.roll(x, 2, 0), 0.0)
        xp1 = jnp.where(pos <= lrow - 2, pltpu.roll(x, RB - 1, 0), 0.0)
        xc = xm2 * cw[0:1] + xm1 * cw[1:2] + x * cw[2:3] + xp1 * cw[3:4] + cb_ref[...]
        xc_b = xc.astype(BF16)
        for dd in range(2):
            gr = jax.nn.sigmoid(_mm(xc_b, wa_ref[dd]) + ba_ref[dd:dd + 1, :])
            gi = jax.nn.sigmoid(_mm(xc_b, wi_ref[dd]) + bi_ref[dd:dd + 1, :])
            log_a = (-RGLRU_C) * gr * sp[dd:dd + 1, :]
            a = jnp.exp(log_a)
            a_scr[dd, pl.ds(r0, RB), :] = a
            u_scr[dd, pl.ds(r0, RB), :] = jnp.sqrt((1.0 - a) * (1.0 + a)) * gi * xc
        return carry

    lax.fori_loop(0, nblk, gates, 0)

    SUB = 8
    rowt = lax.broadcasted_iota(jnp.int32, (SUB, D_B), 0)

    def tile_scan(a, u, h_in, rev):
        for s_ in (1, 2, 4):
            keep = (rowt < SUB - s_) if rev else (rowt >= s_)
            sh = SUB - s_ if rev else s_
            a_sh = jnp.where(keep, pltpu.roll(a, sh, 0), 1.0)
            u_sh = jnp.where(keep, pltpu.roll(u, sh, 0), 0.0)
            u = a * u_sh + u
            a = a * a_sh
        return a * h_in + u

    def step(i, carry):
        hf, hb = carry
        rf = pl.ds(pl.multiple_of(i * SUB, SUB), SUB)
        hf_tile = tile_scan(a_scr[0, rf, :], u_scr[0, rf, :], hf, False)
        u_scr[0, rf, :] = hf_tile
        rb = pl.ds(pl.multiple_of(T - SUB - i * SUB, SUB), SUB)
        hb_tile = tile_scan(a_scr[1, rb, :], u_scr[1, rb, :], hb, True)
        u_scr[1, rb, :] = hb_tile
        return hf_tile[SUB - 1:SUB], hb_tile[0:1]

    h0 = h0_ref[0]
    hf, hb = lax.fori_loop(0, T // SUB, step, (h0[0:1], h0[1:2]), unroll=2)
    hfin_ref[0] = jnp.concatenate([hf, hb], axis=0)

    def outp(blk, carry):
        r0 = pl.multiple_of(blk * RB, RB)
        h = u_scr[0, pl.ds(r0, RB), :] + u_scr[1, pl.ds(r0, RB), :]
        y_ref[pl.ds(r0, RB), :] = h * _gelu_tanh(xg_ref[pl.ds(r0, RB), :])
        return carry

    lax.fori_loop(0, nblk, outp, 0)


def _rglru(proj, h0, conv_w, conv_b, wa_bd, ba, wi_bd, bi, lam, nseq, T, lrow):
    ntok = proj.shape[0]

    def wspec(shape):
        return pl.BlockSpec(shape, lambda b: (0,) * len(shape))

    return pl.pallas_call(
        functools.partial(_rglru_kernel, T=T, lrow=lrow),
        grid=(nseq,),
        in_specs=[pl.BlockSpec((T, D_B), lambda b: (b, COL_XR)),
                  pl.BlockSpec((T, D_B), lambda b: (b, COL_XGB)),
                  pl.BlockSpec((1, 2, D_B), lambda b: (b, 0, 0)),
                  wspec((4, D_B)), wspec((1, D_B)),
                  wspec((2, D_B, D_B)), wspec((2, D_B)), wspec((2, D_B, D_B)), wspec((2, D_B)),
                  wspec((2, D_B))],
        out_specs=[pl.BlockSpec((T, D_B), lambda b: (b, 0)),
                   pl.BlockSpec((1, 2, D_B), lambda b: (b, 0, 0))],
        out_shape=[jax.ShapeDtypeStruct((ntok, D_B), F32),
                   jax.ShapeDtypeStruct((nseq, 2, D_B), F32)],
        scratch_shapes=[pltpu.VMEM((2, T, D_B), F32), pltpu.VMEM((2, T, D_B), F32)],
        compiler_params=_cparams(("arbitrary",)),
        name="rglru",
    )(proj, proj, h0, conv_w, conv_b, wa_bd, ba, wi_bd, bi, lam)


def _mixout_kernel(of_ref, ob_ref, bf_ref, bb_ref, yb_ref, lg_ref, x_ref, mod_ref, wgu_ref, avg_ref,
                   gnw_ref, gnb_ref, wout_ref, l1w_ref, l1b_ref, wr_ref, br_ref,
                   x1_ref, u2_ref, rank_ref, prob_ref):
    m = mod_ref[0]
    avg = avg_ref[...]
    wkv = of_ref[...] + ob_ref[...]
    mu = _dot_xb2(wkv, avg)
    dv = wkv - mu
    var = _dot_xb2(dv * dv, avg)
    gn = dv * lax.rsqrt(var + GN_EPS) * gnw_ref[...] + gnb_ref[...]
    g = _dot1(jax.nn.sigmoid(lg_ref[...]), wgu_ref[...])
    ya = (gn + (bf_ref[...] + bb_ref[...])) * g
    mix = _dot1(ya, wout_ref[0:D_A, :]) + _dot1(yb_ref[...], wout_ref[D_A:, :])
    x1 = _layer_norm(DEEPNORM_ALPHA * x_ref[...] + m[2:3] * mix) * l1w_ref[...] + l1b_ref[...]
    x1_ref[...] = x1
    u2 = _layer_norm(x1) * (1.0 + m[4:5]) + m[3:4]
    _store_tok_tiles(u2_ref, u2)
    logits = _dot3(u2, wr_ref[...]) + br_ref[...]
    tm = logits.shape[0]
    lane = lax.broadcasted_iota(jnp.int32, (tm, N_EXPERTS), 1).astype(F32)
    lane4 = lax.broadcasted_iota(jnp.int32, (tm, TOP_K), 1)
    work = logits
    rank = jnp.zeros((tm, N_EXPERTS), F32)
    tops = []
    for j in range(TOP_K):
        mx = jnp.max(work, axis=1, keepdims=True)
        idx = jnp.min(jnp.where(work == mx, lane, float(N_EXPERTS)), axis=1, keepdims=True)
        hot = lane == idx
        tops.append(mx)
        rank = jnp.where(hot, j + 1.0, rank)
        work = jnp.where(hot, -jnp.inf, work)
    es = [jnp.exp(t - tops[0]) for t in tops]
    den = es[0] + es[1] + es[2] + es[3]
    prob = jnp.zeros((tm, TOP_K), F32)
    for j in range(TOP_K):
        prob = jnp.where(lane4 == j, es[j] / den, prob)
    rank_ref[...] = rank
    prob_ref[...] = prob


def _mixout_alias_kernel(*refs):
    _mixout_kernel(*refs[:17], *refs[18:])


def _mixout_first_kernel(*refs, ntiles):
    i = pl.program_id(0)

    @pl.when(i < ntiles)
    def _():
        _mixout_kernel(*refs)

    @pl.when(i >= ntiles)
    def _():
        refs[18][...] = jnp.zeros_like(refs[18])


def _mixout(o_f, o_b, bon_f, bon_b, yb, proj, x, mod_rows, tok_per_row, wgu, avg, gnw, gnb, wout, l1w, l1b,
            wr, br, u2_all, tok_base, ntok_all):
    ntok = x.shape[0]
    tm = 256
    per = tok_per_row // tm
    base_tiles = tok_base // tm
    alias = u2_all is not None
    ntiles = ntok // tm

    def wspec(shape):
        return pl.BlockSpec(shape, lambda i: (0,) * len(shape))

    def tl(i):
        return jnp.minimum(i, ntiles - 1)

    return pl.pallas_call(
        _mixout_alias_kernel if alias else functools.partial(_mixout_first_kernel, ntiles=ntiles),
        grid=(ntiles if alias else ntok_all // tm,),
        input_output_aliases={17: 1} if alias else {},
        in_specs=[pl.BlockSpec((tm, D_A), lambda i: (tl(i), 0)),
                  pl.BlockSpec((tm, D_A), lambda i: (tl(i), 0)),
                  pl.BlockSpec((tm, D_A), lambda i: (tl(i), 0)),
                  pl.BlockSpec((tm, D_A), lambda i: (tl(i), 0)),
                  pl.BlockSpec((tm, D_B), lambda i: (tl(i), 0)),
                  pl.BlockSpec((tm, LORA_G), lambda i: (tl(i), COL_LG)),
                  pl.BlockSpec((tm, D_MODEL), lambda i: (tl(i), 0)),
                  pl.BlockSpec((1, 6, D_MODEL), lambda i: (tl(i) // per, 0, 0)),
                  wspec((LORA_G, D_A)), wspec((D_A, D_A)), wspec((1, D_A)), wspec((1, D_A)),
                  wspec((D_MODEL, D_MODEL)), wspec((1, D_MODEL)), wspec((1, D_MODEL)),
                  wspec((D_MODEL, N_EXPERTS)), wspec((1, N_EXPERTS))]
                 + ([pl.BlockSpec(memory_space=pl.ANY)] if alias else []),
        out_specs=[pl.BlockSpec((tm, D_MODEL), lambda i: (tl(i), 0)),
                   pl.BlockSpec((tm * ROW_TILE, 128), lambda i: (i + base_tiles, 0)),
                   pl.BlockSpec((tm, N_EXPERTS), lambda i: (tl(i), 0)),
                   pl.BlockSpec((tm, TOP_K), lambda i: (tl(i), 0))],
        out_shape=[jax.ShapeDtypeStruct((ntok, D_MODEL), F32),
                   jax.ShapeDtypeStruct((ntok_all * ROW_TILE, 128), F32),
                   jax.ShapeDtypeStruct((ntok, N_EXPERTS), F32),
                   jax.ShapeDtypeStruct((ntok, TOP_K), F32)],
        compiler_params=_cparams(("arbitrary",)),
        name="mixout",
    )(o_f, o_b, bon_f, bon_b, yb, proj, x, mod_rows, wgu, avg, gnw, gnb, wout, l1w, l1b, wr, br,
      *([u2_all] if alias else []))


MOE_TM = 256
ROUTE_BLK = 256
INVERT_CHUNK = 2048


def _route_kernel(rank_ref, slot_ref, te_ref, nv_ref, c_scr, *, ntok, nt_pad):
    nblk = ntok // ROUTE_BLK
    ones = jnp.ones((N_EXPERTS, 128), BF16)
    lane4 = lax.broadcasted_iota(jnp.int32, (ROUTE_BLK, TOP_K), 1)
    rowb = lax.broadcasted_iota(jnp.int32, (ROUTE_BLK, ROUTE_BLK), 0)
    colb = lax.broadcasted_iota(jnp.int32, (ROUTE_BLK, ROUTE_BLK), 1)
    tri = jnp.where(rowb > colb, 1.0, 0.0).astype(BF16)

    def count(b, carry):
        rows = pl.ds(pl.multiple_of(b * ROUTE_BLK, ROUTE_BLK), ROUTE_BLK)
        hot = jnp.where(rank_ref[rows, :] > 0.0, 1.0, 0.0)
        c_scr[rows, :] = _mm(tri, hot.astype(BF16)) + carry
        return carry + jnp.sum(hot, axis=0, keepdims=True)

    n = lax.fori_loop(0, nblk, count, jnp.zeros((1, N_EXPERTS), F32))
    padded = jnp.floor((n + (MOE_TM - 1)) * (1.0 / MOE_TM)) * MOE_TM
    r32 = lax.broadcasted_iota(jnp.int32, (N_EXPERTS, N_EXPERTS), 0)
    c32 = lax.broadcasted_iota(jnp.int32, (N_EXPERTS, N_EXPERTS), 1)
    upper = jnp.where(r32 < c32, 1.0, 0.0).astype(BF16)
    off = _dot_xb(jnp.broadcast_to(padded, (8, N_EXPERTS)), upper)[0:1]
    gend = off + padded

    def place(b, carry):
        rows = pl.ds(pl.multiple_of(b * ROUTE_BLK, ROUTE_BLK), ROUTE_BLK)
        rk = rank_ref[rows, :]
        base = off + c_scr[rows, :]
        s = jnp.zeros((ROUTE_BLK, TOP_K), F32)
        for j in range(TOP_K):
            sj = _dot_xb2(jnp.where(rk == j + 1.0, base, 0.0), ones)
            s = jnp.where(lane4 == j, sj[:, :TOP_K], s)
        slot_ref[rows, :] = s.astype(jnp.int32)
        return carry

    lax.fori_loop(0, nblk, place, 0)

    gcol = jnp.sum(jnp.where(r32 == c32, jnp.broadcast_to(gend, (N_EXPERTS, N_EXPERTS)), 0.0),
                   axis=1, keepdims=True)
    tstart = (lax.broadcasted_iota(jnp.int32, (N_EXPERTS, nt_pad), 1) * MOE_TM).astype(F32)
    te = jnp.sum(jnp.where(gcol <= tstart, 1.0, 0.0), axis=0, keepdims=True)
    te_ref[...] = jnp.minimum(te, N_EXPERTS - 1.0).astype(jnp.int32)
    total = jnp.sum(padded, axis=1, keepdims=True)
    nv_ref[...] = jnp.broadcast_to(total * (1.0 / MOE_TM), (1, 128)).astype(jnp.int32)


def _route(rank):
    ntok = rank.shape[0]
    nt_max = ntok * TOP_K // MOE_TM + N_EXPERTS
    nt_pad = -(-(nt_max + 1) // 128) * 128
    full = lambda shape: pl.BlockSpec(shape, lambda i: (0,) * len(shape))
    slot, te, nv = pl.pallas_call(
        functools.partial(_route_kernel, ntok=ntok, nt_pad=nt_pad),
        grid=(1,),
        in_specs=[full((ntok, N_EXPERTS))],
        out_specs=[full((ntok, TOP_K)), full((1, nt_pad)), full((1, 128))],
        out_shape=[jax.ShapeDtypeStruct((ntok, TOP_K), jnp.int32),
                   jax.ShapeDtypeStruct((1, nt_pad), jnp.int32),
                   jax.ShapeDtypeStruct((1, 128), jnp.int32)],
        scratch_shapes=[pltpu.VMEM((ntok, N_EXPERTS), F32)],
        compiler_params=_cparams(("arbitrary",)),
        name="moe_route",
    )(rank)
    return slot.reshape(ntok * TOP_K), te.reshape(nt_pad), nv.reshape(128), nt_max


def _invert_kernel(slot_ref, inv_ref, fill_vmem, slot_smem, inv_smem, sem_in, sem_out, *, nassign, plane):
    slot_id = lax.broadcasted_iota(jnp.int32, fill_vmem.shape, 0)
    fill_vmem[...] = jnp.bitwise_and(slot_id, MOE_TM - 1) * ROW_TILE + TOP_K * plane
    fill = pltpu.make_async_copy(fill_vmem, inv_smem, sem_out)
    fill.start()
    fill.wait()

    def chunk(ci, c):
        a0 = pl.multiple_of(ci * INVERT_CHUNK, INVERT_CHUNK)
        cp = pltpu.make_async_copy(slot_ref.at[pl.ds(a0, INVERT_CHUNK)], slot_smem, sem_in)
        cp.start()
        cp.wait()

        def put(tt, c2):
            base = (lax.shift_right_logical(a0, 2) + tt) * ROW_TILE
            for k in range(TOP_K):
                inv_smem[slot_smem[tt * TOP_K + k]] = base + k * plane
            return c2

        lax.fori_loop(0, INVERT_CHUNK // TOP_K, put, 0, unroll=4)
        return c

    lax.fori_loop(0, nassign // INVERT_CHUNK, chunk, 0)
    out = pltpu.make_async_copy(inv_smem, inv_ref, sem_out)
    out.start()
    out.wait()


def _invert(slot, ntiles, ntok):
    nassign = slot.shape[0]
    plane = ntok * ROW_TILE
    assert plane & (plane - 1) == 0, "token count must be a power of two"
    anyspec = pl.BlockSpec(memory_space=pl.ANY)
    return pl.pallas_call(
        functools.partial(_invert_kernel, nassign=nassign, plane=plane),
        grid=(1,),
        in_specs=[anyspec], out_specs=anyspec,
        out_shape=jax.ShapeDtypeStruct((ntiles * MOE_TM,), jnp.int32),
        scratch_shapes=[pltpu.VMEM((ntiles * MOE_TM,), jnp.int32),
                        pltpu.SMEM((INVERT_CHUNK,), jnp.int32), pltpu.SMEM((ntiles * MOE_TM,), jnp.int32),
                        pltpu.SemaphoreType.DMA(()), pltpu.SemaphoreType.DMA(())],
        compiler_params=_cparams(("arbitrary",)),
        name="moe_invert",
    )(slot)


def _expert_kernel(te_ref, nv_ref, inv_ref, u2_ref, wg_ref, bg_ref, wu_ref, bu_ref, wd_ref, bd_ref, y4_ref,
                   xbuf, ybuf, wg_b, wu_b, wd_b, sem_g, sem_s, *, ntok):
    i = pl.program_id(0)
    nvalid = nv_ref[0]
    valid = i < nvalid
    trash = TOP_K * ntok
    plane = ntok * ROW_TILE

    def rows8(row):
        return pl.ds(pl.multiple_of(row, ROW_TILE), ROW_TILE)

    def gather_copy(tile, r, buf):
        src = jnp.bitwise_and(inv_ref[tile * MOE_TM + r], plane - 1)
        return pltpu.make_async_copy(u2_ref.at[rows8(src)], xbuf.at[buf, _tok_rows(r)], sem_g.at[buf])

    def scatter_copy(tile, r):
        return pltpu.make_async_copy(ybuf.at[_tok_rows(r)], y4_ref.at[rows8(inv_ref[tile * MOE_TM + r])], sem_s)

    def wait_gather(buf):
        pltpu.make_async_copy(xbuf.at[buf], xbuf.at[buf], sem_g.at[buf]).wait()

    def wait_scatter():
        pltpu.make_async_copy(ybuf, ybuf, sem_s).wait()

    @pl.when(i == 0)
    def _():
        def first(r2, c):
            gather_copy(0, 2 * r2, 0).start(priority=0)
            gather_copy(0, 2 * r2 + 1, 0).start(priority=1)
            return c

        lax.fori_loop(0, MOE_TM // 2, first, 0)
        ybuf[...] = jnp.zeros_like(ybuf)
        spare = pltpu.make_async_copy(ybuf, y4_ref.at[_tok_rows(trash, MOE_TM)], sem_s)
        spare.start()
        spare.wait()

    fresh = jnp.logical_or(i == 0, te_ref[i] != te_ref[jnp.maximum(i - 1, 0)])

    @pl.when(jnp.logical_and(valid, fresh))
    def _():
        wg_b[...] = wg_ref[0].astype(BF16)
        wu_b[...] = wu_ref[0].astype(BF16)
        wd_b[...] = wd_ref[0].astype(BF16)

    @pl.when(valid)
    def _():
        buf = jnp.bitwise_and(i, 1)
        nxt = jnp.minimum(i + 1, nvalid - 1)
        wait_gather(buf)
        x = _load_tok_tiles(xbuf, MOE_TM, (buf,)).astype(BF16)
        FC = 512
        nfc = D_FF // FC
        per = MOE_TM // nfc
        y = jnp.zeros((MOE_TM, D_MODEL), F32)
        for j in range(nfc):
            for r in range(j * per, (j + 1) * per):
                gather_copy(nxt, r, 1 - buf).start(priority=r % 2)
            cs = slice(j * FC, (j + 1) * FC)
            gate = jnp.minimum(_mm(x, wg_b[:, cs]) + bg_ref[0, :, cs], SWIGLU_LIMIT)
            up = jnp.clip(_mm(x, wu_b[:, cs]) + bu_ref[0, :, cs], -SWIGLU_LIMIT, SWIGLU_LIMIT)
            h = (up + 1.0) * gate * jax.nn.sigmoid(SWIGLU_ALPHA * gate)
            y = y + _mm(h.astype(BF16), wd_b[cs, :])

        @pl.when(i > 0)
        def _():
            wait_scatter()

        _store_tok_tiles(ybuf, y + bd_ref[0])
        for r in range(MOE_TM):
            scatter_copy(i, r).start(priority=r % 2)

    @pl.when(i == nvalid)
    def _():
        wait_gather(jnp.bitwise_and(nvalid, 1))
        wait_scatter()


def _experts(te, nv, inv, u2_all, nt_max, wg, bg, wu, bu, wd, bd):
    ntok = u2_all.shape[0] // ROW_TILE

    def tile(i, te, nv, inv):
        return te[jnp.minimum(i, nv[0] - 1)]

    wspec = pl.BlockSpec((1, D_MODEL, D_FF), lambda i, te, nv, inv: (tile(i, te, nv, inv), 0, 0))
    bspec = pl.BlockSpec((1, 1, D_FF), lambda i, te, nv, inv: (tile(i, te, nv, inv), 0, 0))
    anyspec = pl.BlockSpec(memory_space=pl.ANY)
    return pl.pallas_call(
        functools.partial(_expert_kernel, ntok=ntok),
        grid_spec=pltpu.PrefetchScalarGridSpec(
            num_scalar_prefetch=3, grid=(nt_max + 1,),
            in_specs=[anyspec, wspec, bspec, wspec, bspec, wspec, bspec], out_specs=anyspec,
            scratch_shapes=[pltpu.VMEM((2, MOE_TM * ROW_TILE, 128), F32),
                            pltpu.VMEM((MOE_TM * ROW_TILE, 128), F32),
                            pltpu.VMEM((D_MODEL, D_FF), BF16), pltpu.VMEM((D_MODEL, D_FF), BF16),
                            pltpu.VMEM((D_MODEL, D_FF), BF16),
                            pltpu.SemaphoreType.DMA((2,)), pltpu.SemaphoreType.DMA(())]),
        out_shape=jax.ShapeDtypeStruct(((TOP_K * ntok + MOE_TM) * ROW_TILE, 128), F32),
        compiler_params=_cparams(("arbitrary",)),
        name="moe_experts",
    )(te, nv, inv, u2_all, wg, bg.reshape(N_EXPERTS, 1, D_FF), wu, bu.reshape(N_EXPERTS, 1, D_FF),
      wd, bd.reshape(N_EXPERTS, 1, D_MODEL))


def _combine_kernel(y0_ref, y1_ref, y2_ref, y3_ref, prob_ref, x1_ref, mod_ref, w_ref, b_ref, o_ref):
    p = prob_ref[...]
    tm = p.shape[0]
    moe = p[:, 0:1] * _load_tok_tiles(y0_ref, tm)
    for j, y_ref in enumerate((y1_ref, y2_ref, y3_ref), start=1):
        moe = moe + p[:, j:j + 1] * _load_tok_tiles(y_ref, tm)
    m = mod_ref[0]
    o_ref[...] = _layer_norm(DEEPNORM_ALPHA * x1_ref[...] + m[5:6] * moe) * w_ref[...] + b_ref[...]


def _combine(y4, prob, x1, tok_base, ntok_all, mod_rows, tok_per_row, w, b):
    ntok = x1.shape[0]
    tm = 1024
    per = tok_per_row // tm

    def yspec(k):
        return pl.BlockSpec((tm * ROW_TILE, 128), lambda i: ((k * ntok_all + tok_base) // tm + i, 0))

    return pl.pallas_call(
        _combine_kernel,
        grid=(ntok // tm,),
        in_specs=[yspec(0), yspec(1), yspec(2), yspec(3),
                  pl.BlockSpec((tm, TOP_K), lambda i: (i, 0)),
                  pl.BlockSpec((tm, D_MODEL), lambda i: (i, 0)),
                  pl.BlockSpec((1, 6, D_MODEL), lambda i: (i // per, 0, 0)),
                  pl.BlockSpec((1, D_MODEL), lambda i: (0, 0)),
                  pl.BlockSpec((1, D_MODEL), lambda i: (0, 0))],
        out_specs=pl.BlockSpec((tm, D_MODEL), lambda i: (i, 0)),
        out_shape=jax.ShapeDtypeStruct((ntok, D_MODEL), F32),
        compiler_params=_cparams(("arbitrary",)),
        name="moe_combine",
    )(y4, y4, y4, y4, prob, x1, mod_rows, w, b)


def _block_diag(w):
    nb, bb, _ = w.shape
    eye = jnp.eye(nb, dtype=w.dtype)
    return jnp.einsum('nij,nm->nimj', w, eye).reshape(nb * bb, nb * bb)


def kernel(x_prompt, x_sample, state_rwkv, state_rglru, c, c_ctx, w_mod, b_mod, w_in, w_decay_up, b_decay, w_iclr_up, b_iclr, w_gate_up, k_k, k_a, r_k, gn_w, gn_b, conv_w, conv_b, w_rg_a, b_rg_a, w_rg_i, b_rg_i, lam, w_out, ln1_w, ln1_b, w_router, b_router, w_e_gate, b_e_gate, w_e_up, b_e_up, w_e_down, b_e_down, ln2_w, ln2_b):
    Bp, Tp, D = x_prompt.shape
    Bs, Ts, _ = x_sample.shape
    l = 0
    row = lambda a: a[l].reshape(1, -1)

    cond8 = jnp.concatenate([c_ctx[None, :], c, jnp.zeros((8 - 1 - Bs, D), F32)], axis=0)
    mod = _modulation(cond8, w_mod[l], b_mod[l]).reshape(8, 6, D)
    mod_p, mod_s = mod[0:1], mod[1:1 + Bs]

    w_in_b = w_in[l].astype(BF16)
    w_out_b = w_out[l].astype(BF16)
    head_id = jnp.arange(D_A) // HEAD
    hs = (head_id[:, None] == head_id[None, :]).astype(BF16)
    avg = (hs.astype(F32) / HEAD).astype(BF16)
    wdec = w_decay_up[l].reshape(2 * LORA, D_A)
    wic = w_iclr_up[l].reshape(2 * LORA, D_A)
    wa_bd = jnp.stack([_block_diag(w_rg_a[l, 0]), _block_diag(w_rg_a[l, 1])]).astype(BF16)
    wi_bd = jnp.stack([_block_diag(w_rg_i[l, 0]), _block_diag(w_rg_i[l, 1])]).astype(BF16)

    xp = x_prompt.reshape(Bp * Tp, D)
    xs = x_sample.reshape(Bs * Ts, D)

    s0_s = state_rwkv[:, l].reshape(Bs, 2, N_PAIRS, 2, HEAD, HEAD).transpose(0, 1, 2, 4, 3, 5)
    s0_s = s0_s.reshape(Bs, 2, N_PAIRS, HEAD, 2 * HEAD)
    s0_p = jnp.zeros((Bp, 2, N_PAIRS, HEAD, 2 * HEAD), F32)
    h0_p = jnp.zeros((Bp, 2, D_B), F32)
    h0_s = state_rglru[:, l]

    ntok_all = Bp * Tp + Bs * Ts
    outs = []
    u2_all = None
    for x, mod_rows, tok_per_row, nseq, T, s0, h0, lrow, tok_base in (
            (xp, mod_p, Bp * Tp, Bp, Tp, s0_p, h0_p, Tp, 0),
            (xs, mod_s, Ts, Bs, Ts, s0_s, h0_s, GRID_W, Bp * Tp)):
        proj = _inproj(x, mod_rows, w_in_b, tok_per_row)
        o_f, o_b, bon_f, bon_b, s_fin = _rwkv(proj, s0, wdec, b_decay[l], wic, b_iclr[l], row(k_k), row(k_a),
                                              r_k[l].reshape(1, D_A), hs, nseq, T)
        yb, h_fin = _rglru(proj, h0, conv_w[l], row(conv_b), wa_bd, b_rg_a[l], wi_bd, b_rg_i[l], lam[l],
                           nseq, T, lrow)
        x1, u2_all, rank, prob = _mixout(o_f, o_b, bon_f, bon_b, yb, proj, x, mod_rows, tok_per_row,
                                        w_gate_up[l], avg, row(gn_w), row(gn_b), w_out_b, row(ln1_w),
                                        row(ln1_b), w_router[l], row(b_router), u2_all, tok_base, ntok_all)
        outs.append((x1, rank, prob, s_fin, h_fin))

    (x1_p, rank_p, prob_p, sfin_p, hfin_p), (x1_s, rank_s, prob_s, _, _) = outs
    slot, te, nv, nt_max = _route(jnp.concatenate([rank_p, rank_s], axis=0))
    inv = _invert(slot, nt_max, ntok_all)
    y4 = _experts(te, nv, inv, u2_all, nt_max, w_e_gate[l], b_e_gate[l], w_e_up[l], b_e_up[l], w_e_down[l],
                  b_e_down[l])
    y_p = _combine(y4, prob_p, x1_p, 0, ntok_all, mod_p, Bp * Tp, row(ln2_w), row(ln2_b))
    y_s = _combine(y4, prob_s, x1_s, Bp * Tp, ntok_all, mod_s, Ts, row(ln2_w), row(ln2_b))
    return (y_p.reshape(Bp, Tp, D), y_s.reshape(Bs, Ts, D),
            sfin_p[:, None], hfin_p[:, None])
```
